```python
import functools
import jax, jax.numpy as jnp
from jax import lax
import numpy as np

D_MODEL = 1024
BATCH = 8
SEQ = 2048
DEPTH = 1
DEC_BATCH = 128
DEC_SEQ = 8
PAST_LEN = 16384
PAGE_SIZE = 128

PLE_DIM = 256
NORM_EPS = 1e-6
GLA_HEADS = 4
GLA_DK = 64
GLA_DV = 128
GLA_KEY = GLA_HEADS * GLA_DK
GLA_VAL = GLA_HEADS * GLA_DV
GLA_GATE_RANK = 16
GLA_GATE_TEMP = 16.0
GLA_CHUNK = 16
GDN_HEADS = 4
GDN_DK = 128
GDN_DV = 128
GDN_KEY = GDN_HEADS * GDN_DK
GDN_VAL = GDN_HEADS * GDN_DV
GDN_CHUNK = 64
CONV_WIDTH = 4
CONV_CH = 2 * GDN_KEY + GDN_VAL
IN_SPLITS = (GLA_KEY, GLA_KEY, GLA_VAL, GLA_GATE_RANK, GLA_VAL,
             CONV_CH, GDN_HEADS, GDN_HEADS, GDN_VAL,
             D_MODEL, D_MODEL)
IN_COLS = sum(IN_SPLITS)

kernel_name = "hybrid_gla_gated_deltanet_decode_step"


def _split_offsets():
    return [int(v) for v in np.cumsum(IN_SPLITS)[:-1]]


def rms_norm(x, w):
    xf = x.astype(jnp.float32)
    y = xf * lax.rsqrt(jnp.mean(xf * xf, axis=-1, keepdims=True) + NORM_EPS)
    return (y * w.astype(jnp.float32)).astype(x.dtype)


def _l2norm(x):
    return x * lax.rsqrt(jnp.sum(x * x, axis=-1, keepdims=True) + NORM_EPS)


def _heads(x, n_heads):
    b, t, _ = x.shape
    return x.reshape(b, t, n_heads, -1).transpose(0, 2, 1, 3)


def _chunk(x, c):
    t = x.shape[2]
    n = -(-t // c)
    pad = n * c - t
    if pad:
        x = jnp.pad(x, [(0, 0), (0, 0), (0, pad)] + [(0, 0)] * (x.ndim - 3))
    return x.reshape(x.shape[:2] + (n, c) + x.shape[3:])


def gla_recurrence(q, k, v, gk, s0):
    t_len = q.shape[2]
    c = min(GLA_CHUNK, t_len)
    q, k, v, gk = (_chunk(a_.astype(jnp.float32), c) for a_ in (q, k, v, gk))
    b = jnp.cumsum(gk, axis=3)
    b_last = b[:, :, :, -1:, :]
    q_e = q * jnp.exp(b)
    k_e = k * jnp.exp(-b)
    k_end = k * jnp.exp(b_last - b)
    causal = jnp.tril(jnp.ones((c, c), dtype=bool))
    att = jnp.where(causal, jnp.einsum('bhncd,bhnsd->bhncs', q_e, k_e), 0.0)
    o_intra = jnp.einsum('bhncs,bhnsv->bhncv', att, v)
    chunk_decay = jnp.exp(b_last[:, :, :, 0, :])

    def step(s, inp):
        q_c, k_c, v_c, d_c = inp
        o_c = jnp.einsum('bhcd,bhdv->bhcv', q_c, s)
        s = s * d_c[..., None] + jnp.einsum('bhcd,bhcv->bhdv', k_c, v_c)
        return s, o_c

    xs = tuple(jnp.moveaxis(a_, 2, 0) for a_ in (q_e, k_end, v, chunk_decay))
    s_fin, o_inter = lax.scan(step, s0.astype(jnp.float32), xs)
    o = o_intra + jnp.moveaxis(o_inter, 0, 2)
    bsz, nh, n, _, dv = o.shape
    return o.reshape(bsz, nh, n * c, dv)[:, :, :t_len], s_fin


def gdn_recurrence(q, k, v, g, beta, s0):
    t_len = q.shape[2]
    c = min(GDN_CHUNK, t_len)
    q, k, v, g, beta = (_chunk(a_.astype(jnp.float32), c) for a_ in (q, k, v, g, beta))
    decay = jnp.cumsum(g, axis=-1)
    diff = decay[..., :, None] - decay[..., None, :]
    causal = jnp.tril(jnp.ones((c, c), dtype=bool))
    strict = jnp.tril(jnp.ones((c, c), dtype=bool), -1)
    gamma = jnp.where(causal, jnp.exp(jnp.where(causal, diff, 0.0)), 0.0)
    k_beta = k * beta[..., None]
    v_beta = v * beta[..., None]
    a_mat = jnp.where(strict, jnp.einsum('bhnid,bhnjd->bhnij', k_beta, k) * gamma, 0.0)
    t_mat = a_mat + jnp.eye(c, dtype=jnp.float32)
    solve = functools.partial(lax.linalg.triangular_solve, left_side=True, lower=True,
                              unit_diagonal=True)
    u = solve(t_mat, v_beta)
    w = solve(t_mat, k_beta * jnp.exp(decay)[..., None])
    qk = jnp.einsum('bhnid,bhnjd->bhnij', q, k) * gamma
    q_e = q * jnp.exp(decay)[..., None]
    d_last = decay[..., -1:]
    k_end = k * jnp.exp(d_last - decay)[..., None]
    chunk_decay = jnp.exp(d_last[..., 0])

    def step(s, inp):
        w_c, u_c, q_c, qk_c, k_c, d_c = inp
        v_new = u_c - jnp.einsum('bhcd,bhdv->bhcv', w_c, s)
        o_c = jnp.einsum('bhcd,bhdv->bhcv', q_c, s) + jnp.einsum('bhcs,bhsv->bhcv', qk_c, v_new)
        s = s * d_c[..., None, None] + jnp.einsum('bhcd,bhcv->bhdv', k_c, v_new)
        return s, o_c

    xs = tuple(jnp.moveaxis(a_, 2, 0) for a_ in (w, u, q_e, qk, k_end, chunk_decay))
    s_fin, o = lax.scan(step, s0.astype(jnp.float32), xs)
    o = jnp.moveaxis(o, 0, 2)
    bsz, nh, n, _, dv = o.shape
    return o.reshape(bsz, nh, n * c, dv)[:, :, :t_len], s_fin


def mixer_layer(h, p, s_gla, s_gdn, conv_buf, norm_w, w_in, w_gla_gate, b_gla_gate, gla_norm_w,
                conv_w, gdn_a_log, gdn_dt_bias, gdn_norm_w, w_up_gla, w_up_gdn, w_out,
                w_ple_gate, w_ple):
    bsz, t_len, _ = h.shape
    f32 = jnp.float32
    xn = rms_norm(h, norm_w)
    proj = xn @ w_in
    (q_a, k_a, v_a, g_a, z_a, qkv_b, a_b, b_b, z_b, gate_a, gate_b) = jnp.split(
        proj, _split_offsets(), axis=-1)

    gk = jax.nn.log_sigmoid((g_a @ w_gla_gate + b_gla_gate).astype(f32)) / GLA_GATE_TEMP
    o_a, s_gla_new = gla_recurrence(_heads(q_a, GLA_HEADS) * GLA_DK ** -0.5,
                                    _heads(k_a, GLA_HEADS), _heads(v_a, GLA_HEADS),
                                    _heads(gk, GLA_HEADS), s_gla)
    o_a = rms_norm(o_a.transpose(0, 2, 1, 3), gla_norm_w).reshape(bsz, t_len, GLA_VAL)
    y_a = (o_a.astype(h.dtype) * jax.nn.silu(z_a)) @ w_up_gla

    xpad = jnp.concatenate([conv_buf.astype(qkv_b.dtype), qkv_b], axis=1)
    conv = lax.conv_general_dilated(xpad.astype(f32), conv_w.astype(f32)[:, None, :], (1,), 'VALID',
                                    dimension_numbers=('NWC', 'WIO', 'NWC'),
                                    feature_group_count=CONV_CH)
    conv = jax.nn.silu(conv)
    conv_new = xpad[:, -(CONV_WIDTH - 1):, :]
    q_b, k_b, v_b = jnp.split(conv, [GDN_KEY, 2 * GDN_KEY], axis=-1)
    q_b = _l2norm(_heads(q_b, GDN_HEADS)) * GDN_DK ** -0.5
    k_b = _l2norm(_heads(k_b, GDN_HEADS))
    v_b = _heads(v_b, GDN_HEADS)
    beta = jax.nn.sigmoid(b_b.astype(f32)).transpose(0, 2, 1)
    g = (-jnp.exp(gdn_a_log.astype(f32)) *
         jax.nn.softplus((a_b + gdn_dt_bias).astype(f32))).transpose(0, 2, 1)
    o_b, s_gdn_new = gdn_recurrence(q_b, k_b, v_b, g, beta, s_gdn)
    o_b = rms_norm(o_b.transpose(0, 2, 1, 3), gdn_norm_w).reshape(bsz, t_len, GDN_VAL)
    y_b = (o_b.astype(h.dtype) * jax.nn.silu(z_b)) @ w_up_gdn

    merged = jax.nn.sigmoid(gate_a) * y_a + jax.nn.sigmoid(gate_b) * y_b
    h = h + merged @ w_out
    h = h + jax.nn.sigmoid(h @ w_ple_gate) * (p @ w_ple)
    return (h, s_gla_new.astype(h.dtype), s_gdn_new.astype(h.dtype), conv_new.astype(h.dtype))


def trunk(h, p, s_gla, s_gdn, conv_buf, norm_w, w_in, w_gla_gate, b_gla_gate, gla_norm_w, conv_w,
          gdn_a_log, gdn_dt_bias, gdn_norm_w, w_up_gla, w_up_gdn, w_out, w_ple_gate, w_ple,
          final_norm_w):
    new_gla, new_gdn, new_conv = [], [], []
    for i in range(DEPTH):
        h, sg, sd, cb = mixer_layer(h, p[i], s_gla[i], s_gdn[i], conv_buf[i], norm_w[i], w_in[i],
                                    w_gla_gate[i], b_gla_gate[i], gla_norm_w[i], conv_w[i],
                                    gdn_a_log[i], gdn_dt_bias[i], gdn_norm_w[i], w_up_gla[i],
                                    w_up_gdn[i], w_out[i], w_ple_gate[i], w_ple[i])
        new_gla.append(sg)
        new_gdn.append(sd)
        new_conv.append(cb)
    y = rms_norm(h, final_norm_w)
    return y, jnp.stack(new_gla), jnp.stack(new_gdn), jnp.stack(new_conv)


def setup_inputs(seed: int = 0) -> dict:
    key = jax.random.key(seed)
    ks = jax.random.split(key, 24)
    f32 = jnp.float32
    nrm = lambda k, shape, scale: jax.random.normal(k, shape, f32) * scale
    dt = jnp.exp(jax.random.uniform(ks[12], (DEPTH, GDN_HEADS), f32,
                                    minval=float(np.log(1e-3)), maxval=float(np.log(1e-1))))
    return {
        "x_prompt": nrm(ks[0], (BATCH, SEQ, D_MODEL), 1.0),
        "x_sample": nrm(ks[1], (DEC_BATCH, DEC_SEQ, D_MODEL), 1.0),
        "state_gla": nrm(ks[2], (DEPTH, DEC_BATCH, GLA_HEADS, GLA_DK, GLA_DV), 0.5),
        "state_gdn": nrm(ks[3], (DEPTH, DEC_BATCH, GDN_HEADS, GDN_DK, GDN_DV), 0.5),
        "state_conv": nrm(ks[4], (DEPTH, DEC_BATCH, CONV_WIDTH - 1, CONV_CH), 1.0),
        "p_prompt": nrm(ks[5], (DEPTH, BATCH, SEQ, PLE_DIM), 1.0),
        "p_sample": nrm(ks[6], (DEPTH, DEC_BATCH, DEC_SEQ, PLE_DIM), 1.0),
        "norm_w": 1.0 + nrm(ks[7], (DEPTH, D_MODEL), 0.02),
        "w_in": nrm(ks[8], (DEPTH, D_MODEL, IN_COLS), D_MODEL ** -0.5),
        "w_gla_gate": nrm(ks[9], (DEPTH, GLA_GATE_RANK, GLA_KEY), GLA_GATE_RANK ** -0.5),
        "b_gla_gate": nrm(ks[10], (DEPTH, GLA_KEY), 0.1),
        "gla_norm_w": 1.0 + nrm(ks[11], (DEPTH, GLA_DV), 0.02),
        "conv_w": nrm(ks[13], (DEPTH, CONV_WIDTH, CONV_CH), CONV_WIDTH ** -0.5),
        "gdn_a_log": jnp.log(jax.random.uniform(ks[14], (DEPTH, GDN_HEADS), f32, minval=1.0, maxval=16.0)),
        "gdn_dt_bias": dt + jnp.log(-jnp.expm1(-dt)),
        "gdn_norm_w": 1.0 + nrm(ks[15], (DEPTH, GDN_DV), 0.02),
        "w_up_gla": nrm(ks[16], (DEPTH, GLA_VAL, D_MODEL), GLA_VAL ** -0.5),
        "w_up_gdn": nrm(ks[17], (DEPTH, GDN_VAL, D_MODEL), GDN_VAL ** -0.5),
        "w_out": nrm(ks[18], (DEPTH, D_MODEL, D_MODEL), D_MODEL ** -0.5),
        "w_ple_gate": nrm(ks[19], (DEPTH, D_MODEL, D_MODEL), D_MODEL ** -0.5),
        "w_ple": nrm(ks[20], (DEPTH, PLE_DIM, D_MODEL), PLE_DIM ** -0.5),
        "final_norm_w": 1.0 + nrm(ks[21], (D_MODEL,), 0.02),
    }


def reference(x_prompt, x_sample, state_gla, state_gdn, state_conv, p_prompt, p_sample, norm_w, w_in,
              w_gla_gate, b_gla_gate, gla_norm_w, conv_w, gdn_a_log, gdn_dt_bias, gdn_norm_w,
              w_up_gla, w_up_gdn, w_out, w_ple_gate, w_ple, final_norm_w):
    bsz = x_prompt.shape[0]
    dt_ = x_prompt.dtype
    zero_gla = jnp.zeros((DEPTH, bsz, GLA_HEADS, GLA_DK, GLA_DV), dt_)
    zero_gdn = jnp.zeros((DEPTH, bsz, GDN_HEADS, GDN_DK, GDN_DV), dt_)
    zero_conv = jnp.zeros((DEPTH, bsz, CONV_WIDTH - 1, CONV_CH), dt_)
    y_prompt, gla_p, gdn_p, conv_p = trunk(
        x_prompt, p_prompt, zero_gla, zero_gdn, zero_conv, norm_w, w_in, w_gla_gate, b_gla_gate,
        gla_norm_w, conv_w, gdn_a_log, gdn_dt_bias, gdn_norm_w, w_up_gla, w_up_gdn, w_out,
        w_ple_gate, w_ple, final_norm_w)
    y_sample, gla_s, gdn_s, conv_s = trunk(
        x_sample, p_sample, state_gla, state_gdn, state_conv, norm_w, w_in, w_gla_gate, b_gla_gate,
        gla_norm_w, conv_w, gdn_a_log, gdn_dt_bias, gdn_norm_w, w_up_gla, w_up_gdn, w_out,
        w_ple_gate, w_ple, final_norm_w)
    return (y_prompt, y_sample, gla_p, gdn_p, conv_p, gla_s, gdn_s, conv_s)
```

```python
import functools

import jax
import jax.numpy as jnp
from jax import lax
from jax.experimental import pallas as pl
from jax.experimental.pallas import tpu as pltpu

F32 = jnp.float32
BF16 = jnp.bfloat16

D_MODEL = 1024
PLE_DIM = 256
NORM_EPS = 1e-6
GLA_HEADS = 4
GLA_DK = 64
GLA_DV = 128
GLA_KEY = GLA_HEADS * GLA_DK
GLA_VAL = GLA_HEADS * GLA_DV
GLA_GATE_RANK = 16
GLA_GATE_TEMP = 16.0
GDN_HEADS = 4
GDN_DK = 128
GDN_DV = 128
GDN_KEY = GDN_HEADS * GDN_DK
GDN_VAL = GDN_HEADS * GDN_DV
CONV_WIDTH = 4
CONV_CH = 2 * GDN_KEY + GDN_VAL
IN_SPLITS = (GLA_KEY, GLA_KEY, GLA_VAL, GLA_GATE_RANK, GLA_VAL, CONV_CH, GDN_HEADS, GDN_HEADS,
             GDN_VAL, D_MODEL, D_MODEL)

LANES = 128
SUBLANES = 8

P_GLA_QKV = 0
P_GATE_A = 1024
P_GATE_B = 2048
P_QKV_B = 3072
P_Z_A = 4608
P_Z_B = 5120
P_SMALL = 5632
P_COLS = P_SMALL + LANES
SM_G = 0
SM_A = GLA_GATE_RANK
SM_B = GLA_GATE_RANK + GDN_HEADS

PROMPT_CHUNK = 64
ROW_TILE = 256
VMEM_LIMIT = 56 * 1024 * 1024


def _dot(a, b):
    return jnp.dot(a.astype(BF16), b.astype(BF16), preferred_element_type=F32)


def _dot_nt(a, b):
    return lax.dot_general(a.astype(BF16), b.astype(BF16), (((1,), (1,)), ((), ())),
                           preferred_element_type=F32)


def _split3(x):
    h1 = x.astype(BF16)
    r1 = x - h1.astype(F32)
    h2 = r1.astype(BF16)
    h3 = (r1 - h2.astype(F32)).astype(BF16)
    return h1, h2, h3


def _dot_f32(a, b):
    a1, a2, _ = _split3(a)
    b1, b2, _ = _split3(b)
    d = functools.partial(jnp.dot, preferred_element_type=F32)
    return d(a1, b1) + (d(a1, b2) + d(a2, b1))


def _cumsum_rows(tri, x):
    x1, x2, x3 = _split3(x)
    d = functools.partial(jnp.dot, preferred_element_type=F32)
    return d(tri, x1) + (d(tri, x2) + d(tri, x3))


def _softplus(x):
    return jnp.maximum(x, 0.0) + jnp.log1p(jnp.exp(-jnp.abs(x)))


def _sigmoid(x):
    return 1.0 / (1.0 + jnp.exp(-x))


def _silu(x):
    return x * _sigmoid(x)


def _unit_lower_inverse(a, c):
    row = lax.broadcasted_iota(jnp.int32, (c, c), 0)
    col = lax.broadcasted_iota(jnp.int32, (c, c), 1)
    x = jnp.where(row == col, 1.0, 0.0) - a
    p = a
    n = 1
    while 2 * n < c:
        p = _dot_f32(p, p)
        n *= 2
        x = x + _dot_f32(x, p)
    return x


def _inproj_kernel(x_ref, nw_ref, w_ref, o_ref):
    x = x_ref[...]
    xn = x * lax.rsqrt(jnp.mean(x * x, axis=-1, keepdims=True) + NORM_EPS) * nw_ref[...]
    xb = xn.astype(BF16)
    step = 512
    for c0 in range(0, P_SMALL, step):
        o_ref[:, c0:c0 + step] = jnp.dot(xb, w_ref[:, c0:c0 + step], preferred_element_type=F32)
    o_ref[:, P_SMALL:P_COLS] = jnp.dot(xb, w_ref[:, P_SMALL:P_COLS], preferred_element_type=F32)


def _inproj(x2d, norm_w, w_in_r):
    n = x2d.shape[0]
    tm = min(ROW_TILE, n)
    return pl.pallas_call(
        _inproj_kernel,
        grid=(n // tm,),
        in_specs=[
            pl.BlockSpec((tm, D_MODEL), lambda i: (i, 0)),
            pl.BlockSpec((1, D_MODEL), lambda i: (0, 0)),
            pl.BlockSpec((D_MODEL, P_COLS), lambda i: (0, 0), pipeline_mode=pl.Buffered(1)),
        ],
        out_specs=pl.BlockSpec((tm, P_COLS), lambda i: (i, 0)),
        out_shape=jax.ShapeDtypeStruct((n, P_COLS), F32),
        compiler_params=pltpu.CompilerParams(dimension_semantics=("arbitrary",),
                                             vmem_limit_bytes=VMEM_LIMIT),
        name="inproj",
    )(x2d, norm_w, w_in_r)


def _gla_kernel(qkv_ref, small_ref, s0_ref, wgg_ref, bgg_ref, o_ref, sfin_ref, s_scr, *, c, n_chunks):
    t = pl.program_id(1)

    @pl.when(t == 0)
    def _():
        s_scr[...] = s0_ref[0]

    row = lax.broadcasted_iota(jnp.int32, (c, c), 0)
    col = lax.broadcasted_iota(jnp.int32, (c, c), 1)
    causal = row >= col
    tri = jnp.where(causal, 1.0, 0.0).astype(BF16)
    lane = lax.broadcasted_iota(jnp.int32, (c, LANES), 1)

    for ci in range(n_chunks):
        r0 = ci * c
        q = qkv_ref[r0:r0 + c, 0:GLA_KEY]
        k = qkv_ref[r0:r0 + c, GLA_KEY:2 * GLA_KEY]
        sm = small_ref[r0:r0 + c, :]
        pre = _dot(sm, wgg_ref[...]) + bgg_ref[...]
        gk = (jnp.minimum(pre, 0.0) - jnp.log1p(jnp.exp(-jnp.abs(pre)))) * (1.0 / GLA_GATE_TEMP)
        bcum = _cumsum_rows(tri, gk)
        btot = bcum[c - 1:c, :]
        q_e = q * jnp.exp(bcum) * (GLA_DK ** -0.5)
        k_e = k * jnp.exp(-bcum)
        k_end = k * jnp.exp(btot - bcum)
        for pair in range(GLA_HEADS // 2):
            ls = slice(pair * LANES, (pair + 1) * LANES)
            qe_p = q_e[:, ls]
            ke_p = k_e[:, ls]
            kend_t = k_end[:, ls].T
            dcol = jnp.exp(jnp.sum(gk[:, ls].T, axis=1, keepdims=True))
            s_pair = s_scr[pair * LANES:(pair + 1) * LANES, :]
            for hh in range(2):
                h = pair * 2 + hh
                in_head = (lane >= hh * GLA_DK) & (lane < (hh + 1) * GLA_DK)
                qm = jnp.where(in_head, qe_p, 0.0)
                att = jnp.where(causal, _dot_nt(qm, ke_p), 0.0)
                v = qkv_ref[r0:r0 + c, 2 * GLA_KEY + h * GLA_DV:2 * GLA_KEY + (h + 1) * GLA_DV]
                o_ref[r0:r0 + c, h * GLA_DV:(h + 1) * GLA_DV] = _dot(att, v) + _dot(qm, s_pair)
                rs = slice(hh * GLA_DK, (hh + 1) * GLA_DK)
                s_scr[h * GLA_DK:(h + 1) * GLA_DK, :] = (
                    s_pair[rs, :] * dcol[rs, :] + _dot(kend_t[rs, :], v))

    @pl.when(t == pl.num_programs(1) - 1)
    def _():
        sfin_ref[0] = s_scr[...]


def _gla(proj, s0, wgg, bgg, bsz, t_len, c, tb):
    n_t = t_len // tb
    rows = GLA_HEADS * GLA_DK
    return pl.pallas_call(
        functools.partial(_gla_kernel, c=c, n_chunks=tb // c),
        grid=(bsz, n_t),
        in_specs=[
            pl.BlockSpec((tb, 2 * GLA_KEY + GLA_VAL), lambda b, t: (b * n_t + t, P_GLA_QKV // 1024)),
            pl.BlockSpec((tb, LANES), lambda b, t: (b * n_t + t, P_SMALL // LANES)),
            pl.BlockSpec((1, rows, GLA_DV), lambda b, t: (b, 0, 0)),
            pl.BlockSpec((LANES, GLA_KEY), lambda b, t: (0, 0)),
            pl.BlockSpec((1, GLA_KEY), lambda b, t: (0, 0)),
        ],
        out_specs=[
            pl.BlockSpec((tb, GLA_VAL), lambda b, t: (b * n_t + t, 0)),
            pl.BlockSpec((1, rows, GLA_DV), lambda b, t: (b, 0, 0)),
        ],
        out_shape=[
            jax.ShapeDtypeStruct((bsz * t_len, GLA_VAL), F32),
            jax.ShapeDtypeStruct((bsz, rows, GLA_DV), F32),
        ],
        scratch_shapes=[pltpu.VMEM((rows, GLA_DV), F32)],
        compiler_params=pltpu.CompilerParams(dimension_semantics=("arbitrary", "arbitrary"),
                                             vmem_limit_bytes=VMEM_LIMIT),
        name="gla",
    )(proj, proj, s0, wgg, bgg)


def _gdn_kernel(qkv_ref, small_ref, cs_ref, s0_ref, cw_ref, alog_ref, dtb_ref,
                o_ref, sfin_ref, cnew_ref, xbuf, s_scr, *, c, n_chunks):
    t = pl.program_id(1)
    tb = c * n_chunks
    keep = CONV_WIDTH - 1
    base = SUBLANES

    @pl.when(t == 0)
    def _():
        s_scr[...] = s0_ref[0]
        xbuf[base - keep:base, :] = cs_ref[0]

    xbuf[base:base + tb, :] = qkv_ref[...]

    row = lax.broadcasted_iota(jnp.int32, (c, c), 0)
    col = lax.broadcasted_iota(jnp.int32, (c, c), 1)
    causal = row >= col
    strict = row > col
    tri = jnp.where(causal, 1.0, 0.0).astype(BF16)

    for ci in range(n_chunks):
        r0 = base + ci * c
        conv = xbuf[r0:r0 + c, :] * cw_ref[CONV_WIDTH - 1:CONV_WIDTH, :]
        for j in range(1, CONV_WIDTH):
            conv = conv + xbuf[r0 - j:r0 - j + c, :] * cw_ref[CONV_WIDTH - 1 - j:CONV_WIDTH - j, :]
        conv = _silu(conv)

        sm = small_ref[ci * c:(ci + 1) * c, :]
        g_all = -jnp.exp(alog_ref[...]) * _softplus(sm + dtb_ref[...])
        beta_all = _sigmoid(sm)
        dec_all = _cumsum_rows(tri, g_all)
        dec_t = dec_all.T

        for h in range(GDN_HEADS):
            dcol = dec_all[:, SM_A + h:SM_A + h + 1]
            drow = dec_t[SM_A + h:SM_A + h + 1, :]
            beta = beta_all[:, SM_B + h:SM_B + h + 1]
            gamma = jnp.where(causal, jnp.exp(jnp.where(causal, dcol - drow, 0.0)), 0.0)
            qh = conv[:, h * GDN_DK:(h + 1) * GDN_DK]
            kh = conv[:, GDN_KEY + h * GDN_DK:GDN_KEY + (h + 1) * GDN_DK]
            vh = conv[:, 2 * GDN_KEY + h * GDN_DV:2 * GDN_KEY + (h + 1) * GDN_DV]
            qh = qh * lax.rsqrt(jnp.sum(qh * qh, axis=-1, keepdims=True) + NORM_EPS) * (GDN_DK ** -0.5)
            kh = kh * lax.rsqrt(jnp.sum(kh * kh, axis=-1, keepdims=True) + NORM_EPS)
            kb = kh * beta
            vb = vh * beta
            a_mat = jnp.where(strict, _dot_nt(kb, kh) * gamma, 0.0)
            t_inv = _unit_lower_inverse(a_mat, c)
            edec = jnp.exp(dcol)
            u = _dot(t_inv, vb)
            w = _dot(t_inv, kb * edec)
            qk = _dot_nt(qh, kh) * gamma
            q_e = qh * edec
            dlast = dcol[c - 1:c, :]
            k_end = kh * jnp.exp(dlast - dcol)
            s = s_scr[h * GDN_DK:(h + 1) * GDN_DK, :]
            v_new = u - _dot(w, s)
            o_ref[ci * c:(ci + 1) * c, h * GDN_DV:(h + 1) * GDN_DV] = _dot(q_e, s) + _dot(qk, v_new)
            s_scr[h * GDN_DK:(h + 1) * GDN_DK, :] = s * jnp.exp(dlast) + _dot(k_end.T, v_new)

    tail = xbuf[base + tb - keep:base + tb, :]
    cnew_ref[0] = tail
    xbuf[base - keep:base, :] = tail

    @pl.when(t == pl.num_programs(1) - 1)
    def _():
        sfin_ref[0] = s_scr[...]


def _gdn(proj, conv_state, s0, conv_w, alog_v, dtb_v, bsz, t_len, c, tb):
    n_t = t_len // tb
    rows = GDN_HEADS * GDN_DK
    keep = CONV_WIDTH - 1
    return pl.pallas_call(
        functools.partial(_gdn_kernel, c=c, n_chunks=tb // c),
        grid=(bsz, n_t),
        in_specs=[
            pl.BlockSpec((tb, CONV_CH), lambda b, t: (b * n_t + t, P_QKV_B // CONV_CH)),
            pl.BlockSpec((tb, LANES), lambda b, t: (b * n_t + t, P_SMALL // LANES)),
            pl.BlockSpec((1, keep, CONV_CH), lambda b, t: (b, 0, 0)),
            pl.BlockSpec((1, rows, GDN_DV), lambda b, t: (b, 0, 0)),
            pl.BlockSpec((CONV_WIDTH, CONV_CH), lambda b, t: (0, 0)),
            pl.BlockSpec((1, LANES), lambda b, t: (0, 0)),
            pl.BlockSpec((1, LANES), lambda b, t: (0, 0)),
        ],
        out_specs=[
            pl.BlockSpec((tb, GDN_VAL), lambda b, t: (b * n_t + t, 0)),
            pl.BlockSpec((1, rows, GDN_DV), lambda b, t: (b, 0, 0)),
            pl.BlockSpec((1, keep, CONV_CH), lambda b, t: (b, 0, 0)),
        ],
        out_shape=[
            jax.ShapeDtypeStruct((bsz * t_len, GDN_VAL), F32),
            jax.ShapeDtypeStruct((bsz, rows, GDN_DV), F32),
            jax.ShapeDtypeStruct((bsz, keep, CONV_CH), F32),
        ],
        scratch_shapes=[pltpu.VMEM((SUBLANES + tb, CONV_CH), F32), pltpu.VMEM((rows, GDN_DV), F32)],
        compiler_params=pltpu.CompilerParams(dimension_semantics=("arbitrary", "arbitrary"),
                                             vmem_limit_bytes=VMEM_LIMIT),
        name="gdn",
    )(proj, proj, conv_state, s0, conv_w, alog_v, dtb_v)


def _head_norm_gate(o, z, w):
    parts = []
    for h in range(o.shape[-1] // LANES):
        oh = o[:, h * LANES:(h + 1) * LANES]
        parts.append(oh * lax.rsqrt(jnp.mean(oh * oh, axis=-1, keepdims=True) + NORM_EPS) * w)
    return jnp.concatenate(parts, axis=-1) * _silu(z)


def _out_kernel(oa_ref, ob_ref, ga_ref, gb_ref, za_ref, zb_ref, x_ref, p_ref, anw_ref, bnw_ref,
                wua_ref, wub_ref, wout_ref, wpg_ref, wp_ref, fnw_ref, y_ref):
    ya = _dot(_head_norm_gate(oa_ref[...], za_ref[...], anw_ref[...]), wua_ref[...])
    yb = _dot(_head_norm_gate(ob_ref[...], zb_ref[...], bnw_ref[...]), wub_ref[...])
    merged = _sigmoid(ga_ref[...]) * ya + _sigmoid(gb_ref[...]) * yb
    h1 = x_ref[...] + _dot(merged, wout_ref[...])
    h2 = h1 + _sigmoid(_dot(h1, wpg_ref[...])) * _dot(p_ref[...], wp_ref[...])
    y_ref[...] = h2 * lax.rsqrt(jnp.mean(h2 * h2, axis=-1, keepdims=True) + NORM_EPS) * fnw_ref[...]


def _out_stage(o_a, o_b, proj, x2d, p2d, anw, bnw, wua, wub, wout, wpg, wp, fnw):
    n = x2d.shape[0]
    tm = min(ROW_TILE, n)
    const = lambda i: (0, 0)
    return pl.pallas_call(
        _out_kernel,
        grid=(n // tm,),
        in_specs=[
            pl.BlockSpec((tm, GLA_VAL), lambda i: (i, 0)),
            pl.BlockSpec((tm, GDN_VAL), lambda i: (i, 0)),
            pl.BlockSpec((tm, D_MODEL), lambda i: (i, P_GATE_A // D_MODEL)),
            pl.BlockSpec((tm, D_MODEL), lambda i: (i, P_GATE_B // D_MODEL)),
            pl.BlockSpec((tm, GLA_VAL), lambda i: (i, P_Z_A // GLA_VAL)),
            pl.BlockSpec((tm, GDN_VAL), lambda i: (i, P_Z_B // GDN_VAL)),
            pl.BlockSpec((tm, D_MODEL), lambda i: (i, 0)),
            pl.BlockSpec((tm, PLE_DIM), lambda i: (i, 0)),
            pl.BlockSpec((1, GLA_DV), const),
            pl.BlockSpec((1, GDN_DV), const),
            pl.BlockSpec((GLA_VAL, D_MODEL), const),
            pl.BlockSpec((GDN_VAL, D_MODEL), const),
            pl.BlockSpec((D_MODEL, D_MODEL), const),
            pl.BlockSpec((D_MODEL, D_MODEL), const),
            pl.BlockSpec((PLE_DIM, D_MODEL), const),
            pl.BlockSpec((1, D_MODEL), const),
        ],
        out_specs=pl.BlockSpec((tm, D_MODEL), lambda i: (i, 0)),
        out_shape=jax.ShapeDtypeStruct((n, D_MODEL), F32),
        compiler_params=pltpu.CompilerParams(dimension_semantics=("arbitrary",),
                                             vmem_limit_bytes=VMEM_LIMIT),
        name="out_stage",
    )(o_a, o_b, proj, proj, proj, proj, x2d, p2d, anw, bnw, wua, wub, wout, wpg, wp, fnw)


def _regroup_w_in(w_in):
    offs = [0]
    for s in IN_SPLITS:
        offs.append(offs[-1] + s)
    part = lambda i: w_in[:, offs[i]:offs[i + 1]]
    q_a, k_a, v_a, g_a, z_a, qkv_b, a_b, b_b, z_b, gate_a, gate_b = (part(i) for i in range(11))
    pad = jnp.zeros((D_MODEL, LANES - GLA_GATE_RANK - 2 * GDN_HEADS), w_in.dtype)
    return jnp.concatenate([q_a, k_a, v_a, gate_a, gate_b, qkv_b, z_a, z_b, g_a, a_b, b_b, pad],
                           axis=1).astype(BF16)


def _small_lane_vec(v, off):
    return jnp.zeros((1, LANES), F32).at[0, off:off + v.shape[0]].set(v.astype(F32))


def _trunk(x, p, s_gla, s_gdn, conv_state, wts):
    bsz, t_len, _ = x.shape
    c = min(PROMPT_CHUNK, t_len)
    tb = c
    x2d = x.reshape(bsz * t_len, D_MODEL)
    p2d = p.reshape(bsz * t_len, PLE_DIM)
    proj = _inproj(x2d, wts["norm_w"], wts["w_in_r"])
    o_a, gla_fin = _gla(proj, s_gla.reshape(bsz, GLA_HEADS * GLA_DK, GLA_DV), wts["wgg"], wts["bgg"],
                        bsz, t_len, c, tb)
    o_b, gdn_fin, conv_new = _gdn(proj, conv_state, s_gdn.reshape(bsz, GDN_HEADS * GDN_DK, GDN_DV),
                                  wts["conv_w"], wts["alog_v"], wts["dtb_v"], bsz, t_len, c, tb)
    y = _out_stage(o_a, o_b, proj, x2d, p2d, wts["anw"], wts["bnw"], wts["wua"], wts["wub"],
                   wts["wout"], wts["wpg"], wts["wp"], wts["fnw"])
    return (y.reshape(bsz, t_len, D_MODEL),
            gla_fin.reshape(1, bsz, GLA_HEADS, GLA_DK, GLA_DV),
            gdn_fin.reshape(1, bsz, GDN_HEADS, GDN_DK, GDN_DV),
            conv_new.reshape(1, bsz, CONV_WIDTH - 1, CONV_CH))


def kernel(x_prompt, x_sample, state_gla, state_gdn, state_conv, p_prompt, p_sample, norm_w, w_in,
           w_gla_gate, b_gla_gate, gla_norm_w, conv_w, gdn_a_log, gdn_dt_bias, gdn_norm_w,
           w_up_gla, w_up_gdn, w_out, w_ple_gate, w_ple, final_norm_w):
    wgg = jnp.zeros((LANES, GLA_KEY), F32).at[SM_G:SM_G + GLA_GATE_RANK].set(w_gla_gate[0]).astype(BF16)
    wts = {
        "norm_w": norm_w[0].reshape(1, D_MODEL),
        "w_in_r": _regroup_w_in(w_in[0]),
        "wgg": wgg,
        "bgg": b_gla_gate[0].reshape(1, GLA_KEY),
        "conv_w": conv_w[0],
        "alog_v": _small_lane_vec(gdn_a_log[0], SM_A),
        "dtb_v": _small_lane_vec(gdn_dt_bias[0], SM_A),
        "anw": gla_norm_w[0].reshape(1, GLA_DV),
        "bnw": gdn_norm_w[0].reshape(1, GDN_DV),
        "wua": w_up_gla[0].astype(BF16),
        "wub": w_up_gdn[0].astype(BF16),
        "wout": w_out[0].astype(BF16),
        "wpg": w_ple_gate[0].astype(BF16),
        "wp": w_ple[0].astype(BF16),
        "fnw": final_norm_w.reshape(1, D_MODEL),
    }
    bsz = x_prompt.shape[0]
    dt = x_prompt.dtype
    y_p, gla_p, gdn_p, conv_p = _trunk(
        x_prompt, p_prompt[0],
        jnp.zeros((bsz, GLA_HEADS, GLA_DK, GLA_DV), dt), jnp.zeros((bsz, GDN_HEADS, GDN_DK, GDN_DV), dt),
        jnp.zeros((bsz, CONV_WIDTH - 1, CONV_CH), dt), wts)
    y_s, gla_s, gdn_s, conv_s = _trunk(x_sample, p_sample[0], state_gla[0], state_gdn[0], state_conv[0], wts)
    return (y_p, y_s, gla_p, gdn_p, conv_p, gla_s, gdn_s, conv_s)
```

```python
import functools

import jax
import jax.numpy as jnp
from jax import lax
from jax.experimental import pallas as pl
from jax.experimental.pallas import tpu as pltpu

F32 = jnp.float32
BF16 = jnp.bfloat16

D_MODEL = 1024
PLE_DIM = 256
NORM_EPS = 1e-6
GLA_HEADS = 4
GLA_DK = 64
GLA_DV = 128
GLA_KEY = GLA_HEADS * GLA_DK
GLA_VAL = GLA_HEADS * GLA_DV
GLA_GATE_RANK = 16
GLA_GATE_TEMP = 16.0
GDN_HEADS = 4
GDN_DK = 128
GDN_DV = 128
GDN_KEY = GDN_HEADS * GDN_DK
GDN_VAL = GDN_HEADS * GDN_DV
CONV_WIDTH = 4
CONV_CH = 2 * GDN_KEY + GDN_VAL
IN_SPLITS = (GLA_KEY, GLA_KEY, GLA_VAL, GLA_GATE_RANK, GLA_VAL, CONV_CH, GDN_HEADS, GDN_HEADS,
             GDN_VAL, D_MODEL, D_MODEL)

LANES = 128
SUBLANES = 8

P_GLA_QKV = 0
P_GATE_A = 1024
P_GATE_B = 2048
P_QKV_B = 3072
P_Z_A = 4608
P_Z_B = 5120
P_SMALL = 5632
P_COLS = P_SMALL + LANES
SM_G = 0
SM_A = GLA_GATE_RANK
SM_B = GLA_GATE_RANK + GDN_HEADS

PROMPT_CHUNK = 64
GROUP_ROWS = 128
ROW_TILE = 256
VMEM_LIMIT = 56 * 1024 * 1024


def _dot(a, b):
    return jnp.dot(a.astype(BF16), b.astype(BF16), preferred_element_type=F32)


def _dot_nt(a, b):
    return lax.dot_general(a.astype(BF16), b.astype(BF16), (((1,), (1,)), ((), ())),
                           preferred_element_type=F32)


def _split2(x):
    h1 = x.astype(BF16)
    return h1, (x - h1.astype(F32)).astype(BF16)


def _split3(x):
    h1 = x.astype(BF16)
    r1 = x - h1.astype(F32)
    h2 = r1.astype(BF16)
    h3 = (r1 - h2.astype(F32)).astype(BF16)
    return h1, h2, h3


def _dot_f32(a, b):
    a1, a2 = _split2(a)
    b1, b2 = _split2(b)
    d = functools.partial(jnp.dot, preferred_element_type=F32)
    return d(a1, b1) + (d(a1, b2) + d(a2, b1))


def _cumsum_rows(tri, x):
    x1, x2, x3 = _split3(x)
    d = functools.partial(jnp.dot, preferred_element_type=F32)
    return d(tri, x1) + (d(tri, x2) + d(tri, x3))


def _softplus(x):
    return jnp.maximum(x, 0.0) + jnp.log1p(jnp.exp(-jnp.abs(x)))


def _sigmoid(x):
    return 1.0 / (1.0 + jnp.exp(-x))


def _silu(x):
    return x * _sigmoid(x)


def _inproj_kernel(x_ref, nw_ref, w_ref, o_ref):
    x = x_ref[...]
    xn = x * lax.rsqrt(jnp.mean(x * x, axis=-1, keepdims=True) + NORM_EPS) * nw_ref[...]
    xb = xn.astype(BF16)
    step = 512
    for c0 in range(0, P_SMALL, step):
        o_ref[:, c0:c0 + step] = jnp.dot(xb, w_ref[:, c0:c0 + step], preferred_element_type=F32)
    o_ref[:, P_SMALL:P_COLS] = jnp.dot(xb, w_ref[:, P_SMALL:P_COLS], preferred_element_type=F32)


def _inproj(x2d, norm_w, w_in_r):
    n = x2d.shape[0]
    tm = min(ROW_TILE, n)
    return pl.pallas_call(
        _inproj_kernel,
        grid=(n // tm,),
        in_specs=[
            pl.BlockSpec((tm, D_MODEL), lambda i: (i, 0)),
            pl.BlockSpec((1, D_MODEL), lambda i: (0, 0)),
            pl.BlockSpec((D_MODEL, P_COLS), lambda i: (0, 0), pipeline_mode=pl.Buffered(1)),
        ],
        out_specs=pl.BlockSpec((tm, P_COLS), lambda i: (i, 0)),
        out_shape=jax.ShapeDtypeStruct((n, P_COLS), F32),
        compiler_params=pltpu.CompilerParams(dimension_semantics=("arbitrary",),
                                             vmem_limit_bytes=VMEM_LIMIT),
        name="inproj",
    )(x2d, norm_w, w_in_r)


def _gla_kernel(qkv_ref, small_ref, s0_ref, wgg_ref, bgg_ref, o_ref, sfin_ref, s_scr, *, c, n_chunks):
    t = pl.program_id(1)

    @pl.when(t == 0)
    def _():
        s_scr[...] = s0_ref[0]

    row = lax.broadcasted_iota(jnp.int32, (c, c), 0)
    col = lax.broadcasted_iota(jnp.int32, (c, c), 1)
    causal = row >= col
    tri = jnp.where(causal, 1.0, 0.0).astype(BF16)
    lane = lax.broadcasted_iota(jnp.int32, (c, LANES), 1)

    for ci in range(n_chunks):
        r0 = ci * c
        q = qkv_ref[r0:r0 + c, 0:GLA_KEY]
        k = qkv_ref[r0:r0 + c, GLA_KEY:2 * GLA_KEY]
        sm = small_ref[r0:r0 + c, :]
        pre = _dot(sm, wgg_ref[...]) + bgg_ref[...]
        gk = (jnp.minimum(pre, 0.0) - jnp.log1p(jnp.exp(-jnp.abs(pre)))) * (1.0 / GLA_GATE_TEMP)
        bcum = _cumsum_rows(tri, gk)
        btot = bcum[c - 1:c, :]
        q_e = q * jnp.exp(bcum) * (GLA_DK ** -0.5)
        k_e = k * jnp.exp(-bcum)
        k_end = k * jnp.exp(btot - bcum)
        for pair in range(GLA_HEADS // 2):
            ls = slice(pair * LANES, (pair + 1) * LANES)
            qe_p = q_e[:, ls]
            ke_p = k_e[:, ls]
            kend_t = k_end[:, ls].T
            dcol = jnp.exp(jnp.sum(gk[:, ls].T, axis=1, keepdims=True))
            s_pair = s_scr[pair * LANES:(pair + 1) * LANES, :]
            for hh in range(2):
                h = pair * 2 + hh
                in_head = (lane >= hh * GLA_DK) & (lane < (hh + 1) * GLA_DK)
                qm = jnp.where(in_head, qe_p, 0.0)
                att = jnp.where(causal, _dot_nt(qm, ke_p), 0.0)
                v = qkv_ref[r0:r0 + c, 2 * GLA_KEY + h * GLA_DV:2 * GLA_KEY + (h + 1) * GLA_DV]
                o_ref[r0:r0 + c, h * GLA_DV:(h + 1) * GLA_DV] = _dot(att, v) + _dot(qm, s_pair)
                rs = slice(hh * GLA_DK, (hh + 1) * GLA_DK)
                s_scr[h * GLA_DK:(h + 1) * GLA_DK, :] = (
                    s_pair[rs, :] * dcol[rs, :] + _dot(kend_t[rs, :], v))

    @pl.when(t == pl.num_programs(1) - 1)
    def _():
        sfin_ref[0] = s_scr[...]


def _gla(proj, s0, wgg, bgg, bsz, t_len, c, tb):
    n_t = t_len // tb
    rows = GLA_HEADS * GLA_DK
    return pl.pallas_call(
        functools.partial(_gla_kernel, c=c, n_chunks=tb // c),
        grid=(bsz, n_t),
        in_specs=[
            pl.BlockSpec((tb, 2 * GLA_KEY + GLA_VAL), lambda b, t: (b * n_t + t, P_GLA_QKV // 1024)),
            pl.BlockSpec((tb, LANES), lambda b, t: (b * n_t + t, P_SMALL // LANES)),
            pl.BlockSpec((1, rows, GLA_DV), lambda b, t: (b, 0, 0)),
            pl.BlockSpec((LANES, GLA_KEY), lambda b, t: (0, 0)),
            pl.BlockSpec((1, GLA_KEY), lambda b, t: (0, 0)),
        ],
        out_specs=[
            pl.BlockSpec((tb, GLA_VAL), lambda b, t: (b * n_t + t, 0)),
            pl.BlockSpec((1, rows, GLA_DV), lambda b, t: (b, 0, 0)),
        ],
        out_shape=[
            jax.ShapeDtypeStruct((bsz * t_len, GLA_VAL), F32),
            jax.ShapeDtypeStruct((bsz, rows, GLA_DV), F32),
        ],
        scratch_shapes=[pltpu.VMEM((rows, GLA_DV), F32)],
        compiler_params=pltpu.CompilerParams(dimension_semantics=("arbitrary", "arbitrary"),
                                             vmem_limit_bytes=VMEM_LIMIT),
        name="gla",
    )(proj, proj, s0, wgg, bgg)


def _block_unit_lower_inverse(a_list, block):
    r = a_list[0].shape[0]
    row = lax.broadcasted_iota(jnp.int32, (r, r), 0)
    col = lax.broadcasted_iota(jnp.int32, (r, r), 1)
    eye = jnp.where(row == col, 1.0, 0.0)
    xs = [eye - a for a in a_list]
    ps = [_dot_f32(a, a) for a in a_list]
    n = 2
    d = functools.partial(jnp.dot, preferred_element_type=F32)
    while True:
        p_parts = [_split2(p) for p in ps]
        x_parts = [_split2(x) for x in xs]
        if 2 * n >= block:
            out = []
            for x, (p1, p2), (x1, x2) in zip(xs, p_parts, x_parts):
                xp = d(x1, jnp.concatenate([p1, p2], axis=1))
                out.append(x + (xp[:, :r] + (xp[:, r:] + d(x2, p1))))
            return out
        ms = [d(jnp.concatenate([p1, x1], axis=0), jnp.concatenate([p1, p2], axis=1))
              for (p1, p2), (x1, _) in zip(p_parts, x_parts)]
        qs = [d(jnp.concatenate([p2, x2], axis=0), p1)
              for (p1, p2), (_, x2) in zip(p_parts, x_parts)]
        ss = [m[:, :r] + (m[:, r:] + q) for m, q in zip(ms, qs)]
        ps = [s[:r] for s in ss]
        xs = [x + s[r:] for x, s in zip(xs, ss)]
        n *= 2


def _gdn_masks(r, block):
    row = lax.broadcasted_iota(jnp.int32, (r, r), 0)
    col = lax.broadcasted_iota(jnp.int32, (r, r), 1)
    same = (row // block) == (col // block)
    return same & (row >= col), same & (row > col), same


def _gdn_decays(sm, alog, dtb, same_le, same):
    g_all = -jnp.exp(alog) * _softplus(sm + dtb)
    beta_all = _sigmoid(sm)
    r = sm.shape[0]
    sums = _cumsum_rows(jnp.concatenate([jnp.where(same_le, 1.0, 0.0).astype(BF16),
                                         jnp.where(same, 1.0, 0.0).astype(BF16)], axis=0), g_all)
    return sums[:r], sums[r:], beta_all


def _gdn_prepass(conv, dec_all, dec_t, dend_all, beta_all, same_le, same_lt, block):
    r = conv.shape[0]
    heads = range(GDN_HEADS)
    dcol = [dec_all[:, SM_A + h:SM_A + h + 1] for h in heads]
    drow = [dec_t[SM_A + h:SM_A + h + 1, :] for h in heads]
    dend = [dend_all[:, SM_A + h:SM_A + h + 1] for h in heads]
    beta = [beta_all[:, SM_B + h:SM_B + h + 1] for h in heads]
    gamma = [jnp.where(same_le, jnp.exp(jnp.where(same_le, dcol[h] - drow[h], 0.0)), 0.0) for h in heads]
    q = [conv[:, h * GDN_DK:(h + 1) * GDN_DK] for h in heads]
    k = [conv[:, GDN_KEY + h * GDN_DK:GDN_KEY + (h + 1) * GDN_DK] for h in heads]
    v = [conv[:, 2 * GDN_KEY + h * GDN_DV:2 * GDN_KEY + (h + 1) * GDN_DV] for h in heads]
    q = [x * lax.rsqrt(jnp.sum(x * x, axis=-1, keepdims=True) + NORM_EPS) * (GDN_DK ** -0.5) for x in q]
    k = [x * lax.rsqrt(jnp.sum(x * x, axis=-1, keepdims=True) + NORM_EPS) for x in k]
    kb = [k[h] * beta[h] for h in heads]
    vb = [v[h] * beta[h] for h in heads]
    kq = [_dot_nt(jnp.concatenate([kb[h], q[h]], axis=0), k[h]) for h in heads]
    a_mat = [jnp.where(same_lt, kq[h][:r] * gamma[h], 0.0) for h in heads]
    qk = [kq[h][r:] * gamma[h] for h in heads]
    t_inv = _block_unit_lower_inverse(a_mat, block)
    edec = [jnp.exp(dcol[h]) for h in heads]
    uw = [_dot(t_inv[h], jnp.concatenate([vb[h], kb[h] * edec[h]], axis=1)) for h in heads]
    u = [x[:, :GDN_DV] for x in uw]
    w = [x[:, GDN_DV:] for x in uw]
    q_e = [q[h] * edec[h] for h in heads]
    k_end_t = [(k[h] * jnp.exp(dend[h] - dcol[h])).T for h in heads]
    return u, w, q_e, qk, k_end_t, dend


def _conv_silu(win, cw_ref):
    conv = win(0) * cw_ref[CONV_WIDTH - 1:CONV_WIDTH, :]
    for j in range(1, CONV_WIDTH):
        conv = conv + win(j) * cw_ref[CONV_WIDTH - 1 - j:CONV_WIDTH - j, :]
    return _silu(conv)


def _gdn_prompt_kernel(qkv_ref, small_ref, cs_ref, s0_ref, cw_ref, alog_ref, dtb_ref,
                       o_ref, sfin_ref, cnew_ref, xbuf, prev, s_scr, *, c):
    t = pl.program_id(0)
    nb, r, _ = qkv_ref.shape
    keep = CONV_WIDTH - 1
    base = SUBLANES

    @pl.when(t == 0)
    def _():
        s_scr[...] = s0_ref[...]
        prev[:, base - keep:base, :] = cs_ref[...]

    same_le, same_lt, same = _gdn_masks(r, c)
    zeros = jnp.zeros((c, GDN_DV), F32)
    n_sub = r // c

    def per_batch(b, carry):
        xbuf[base - keep:base, :] = prev[b, base - keep:base, :]
        xbuf[base:base + r, :] = qkv_ref[b]
        conv = _conv_silu(lambda j: xbuf[base - j:base - j + r, :], cw_ref)
        prev[b, base - keep:base, :] = xbuf[base + r - keep:base + r, :]
        dec_all, dend_all, beta_all = _gdn_decays(small_ref[b], alog_ref[...], dtb_ref[...], same_le, same)
        dec_t = dec_all.T
        u, w, q_e, qk, k_end_t, dend = _gdn_prepass(
            conv, dec_all, dec_t, dend_all, beta_all, same_le, same_lt, c)
        heads = range(GDN_HEADS)
        hs = [slice(h * GDN_DK, (h + 1) * GDN_DK) for h in heads]
        s = [s_scr[b, hs[h], :] for h in heads]
        for i in range(n_sub):
            rows = slice(i * c, (i + 1) * c)
            ws = [_dot(jnp.concatenate([w[h][rows], q_e[h][rows]], axis=0), s[h]) for h in heads]
            v_new = [u[h][rows] - ws[h][:c] for h in heads]
            padded = [jnp.concatenate([zeros] * i + [v_new[h]] + [zeros] * (n_sub - 1 - i), axis=0)
                      for h in heads]
            upd = [_dot(jnp.concatenate([qk[h][rows], k_end_t[h]], axis=0), padded[h]) for h in heads]
            for h in heads:
                o_ref[b, rows, h * GDN_DV:(h + 1) * GDN_DV] = ws[h][c:] + upd[h][:c]
            s = [s[h] * jnp.exp(dend[h][i * c:i * c + 1, :]) + upd[h][c:] for h in heads]
        for h in heads:
            s_scr[b, hs[h], :] = s[h]
        return carry

    lax.fori_loop(0, nb, per_batch, None)

    @pl.when(t == pl.num_programs(0) - 1)
    def _():
        sfin_ref[...] = s_scr[...]
        cnew_ref[...] = prev[:, base - keep:base, :]


def _gdn_sample_kernel(qkv_ref, small_ref, cs_ref, s0_ref, cw_ref, alog_ref, dtb_ref,
                       o_ref, sfin_ref, cnew_ref, xbuf):
    ns, c, _ = qkv_ref.shape
    r = ns * c
    keep = CONV_WIDTH - 1
    base = SUBLANES
    xbuf[:, base - keep:base, :] = cs_ref[...]
    xbuf[:, base:base + c, :] = qkv_ref[...]
    conv = _conv_silu(lambda j: xbuf[:, base - j:base - j + c, :], cw_ref).reshape(r, CONV_CH)
    cnew_ref[...] = xbuf[:, base + c - keep:base + c, :]

    same_le, same_lt, same = _gdn_masks(r, c)
    dec_all, dend_all, beta_all = _gdn_decays(small_ref[...].reshape(r, LANES), alog_ref[...], dtb_ref[...],
                                              same_le, same)
    dec_t = dec_all.T
    seq_of_row = lax.broadcasted_iota(jnp.int32, (r, GDN_DV), 0) // c
    u, w, q_e, qk, k_end_t, dend = _gdn_prepass(conv, dec_all, dec_t, dend_all, beta_all, same_le, same_lt, c)
    for h in range(GDN_HEADS):
        hs = slice(h * GDN_DK, (h + 1) * GDN_DK)
        v_parts, o_parts = [], []
        for s in range(ns):
            rows = slice(s * c, (s + 1) * c)
            ws = _dot(jnp.concatenate([w[h][rows], q_e[h][rows]], axis=0), s0_ref[s, hs, :])
            v_parts.append(u[h][rows] - ws[:c])
            o_parts.append(ws[c:])
        v_new = jnp.concatenate(v_parts, axis=0)
        o = jnp.concatenate(o_parts, axis=0) + _dot(qk[h], v_new)
        o_ref[:, :, h * GDN_DV:(h + 1) * GDN_DV] = o.reshape(ns, c, GDN_DV)
        for s in range(ns):
            upd = _dot(k_end_t[h], jnp.where(seq_of_row == s, v_new, 0.0))
            sfin_ref[s, hs, :] = s0_ref[s, hs, :] * jnp.exp(dend[h][s * c:s * c + 1, :]) + upd


def _gdn_prompt(proj3, conv_state, s0, conv_w, alog_v, dtb_v, c):
    bsz, t_len, _ = proj3.shape
    rows = GDN_HEADS * GDN_DK
    keep = CONV_WIDTH - 1
    r = GROUP_ROWS
    full3 = lambda t: (0, 0, 0)
    const2 = lambda t: (0, 0)
    return pl.pallas_call(
        functools.partial(_gdn_prompt_kernel, c=c),
        grid=(t_len // r,),
        in_specs=[
            pl.BlockSpec((bsz, r, CONV_CH), lambda t: (0, t, P_QKV_B // CONV_CH)),
            pl.BlockSpec((bsz, r, LANES), lambda t: (0, t, P_SMALL // LANES)),
            pl.BlockSpec((bsz, keep, CONV_CH), full3),
            pl.BlockSpec((bsz, rows, GDN_DV), full3),
            pl.BlockSpec((CONV_WIDTH, CONV_CH), const2),
            pl.BlockSpec((1, LANES), const2),
            pl.BlockSpec((1, LANES), const2),
        ],
        out_specs=[
            pl.BlockSpec((bsz, r, GDN_VAL), lambda t: (0, t, 0)),
            pl.BlockSpec((bsz, rows, GDN_DV), full3),
            pl.BlockSpec((bsz, keep, CONV_CH), full3),
        ],
        out_shape=[
            jax.ShapeDtypeStruct((bsz, t_len, GDN_VAL), F32),
            jax.ShapeDtypeStruct((bsz, rows, GDN_DV), F32),
            jax.ShapeDtypeStruct((bsz, keep, CONV_CH), F32),
        ],
        scratch_shapes=[pltpu.VMEM((SUBLANES + r, CONV_CH), F32),
                        pltpu.VMEM((bsz, SUBLANES, CONV_CH), F32),
                        pltpu.VMEM((bsz, rows, GDN_DV), F32)],
        compiler_params=pltpu.CompilerParams(dimension_semantics=("arbitrary",),
                                             vmem_limit_bytes=VMEM_LIMIT),
        name="gdn_prompt",
    )(proj3, proj3, conv_state, s0, conv_w, alog_v, dtb_v)


def _gdn_sample(proj3, conv_state, s0, conv_w, alog_v, dtb_v):
    bsz, c, _ = proj3.shape
    rows = GDN_HEADS * GDN_DK
    keep = CONV_WIDTH - 1
    ns = GROUP_ROWS // c
    grp = lambda g: (g, 0, 0)
    const2 = lambda g: (0, 0)
    return pl.pallas_call(
        _gdn_sample_kernel,
        grid=(bsz // ns,),
        in_specs=[
            pl.BlockSpec((ns, c, CONV_CH), lambda g: (g, 0, P_QKV_B // CONV_CH)),
            pl.BlockSpec((ns, c, LANES), lambda g: (g, 0, P_SMALL // LANES)),
            pl.BlockSpec((ns, keep, CONV_CH), grp),
            pl.BlockSpec((ns, rows, GDN_DV), grp),
            pl.BlockSpec((CONV_WIDTH, CONV_CH), const2),
            pl.BlockSpec((1, LANES), const2),
            pl.BlockSpec((1, LANES), const2),
        ],
        out_specs=[
            pl.BlockSpec((ns, c, GDN_VAL), grp),
            pl.BlockSpec((ns, rows, GDN_DV), grp),
            pl.BlockSpec((ns, keep, CONV_CH), grp),
        ],
        out_shape=[
            jax.ShapeDtypeStruct((bsz, c, GDN_VAL), F32),
            jax.ShapeDtypeStruct((bsz, rows, GDN_DV), F32),
            jax.ShapeDtypeStruct((bsz, keep, CONV_CH), F32),
        ],
        scratch_shapes=[pltpu.VMEM((ns, 2 * SUBLANES, CONV_CH), F32)],
        compiler_params=pltpu.CompilerParams(dimension_semantics=("arbitrary",),
                                             vmem_limit_bytes=VMEM_LIMIT),
        name="gdn_sample",
    )(proj3, proj3, conv_state, s0, conv_w, alog_v, dtb_v)


def _head_norm_gate(o, z, w):
    parts = []
    for h in range(o.shape[-1] // LANES):
        oh = o[:, h * LANES:(h + 1) * LANES]
        parts.append(oh * lax.rsqrt(jnp.mean(oh * oh, axis=-1, keepdims=True) + NORM_EPS) * w)
    return jnp.concatenate(parts, axis=-1) * _silu(z)


def _out_kernel(oa_ref, ob_ref, ga_ref, gb_ref, za_ref, zb_ref, x_ref, p_ref, anw_ref, bnw_ref,
                wua_ref, wub_ref, wout_ref, wpg_ref, wp_ref, fnw_ref, y_ref):
    ya = _dot(_head_norm_gate(oa_ref[...], za_ref[...], anw_ref[...]), wua_ref[...])
    yb = _dot(_head_norm_gate(ob_ref[...], zb_ref[...], bnw_ref[...]), wub_ref[...])
    merged = _sigmoid(ga_ref[...]) * ya + _sigmoid(gb_ref[...]) * yb
    h1 = x_ref[...] + _dot(merged, wout_ref[...])
    h2 = h1 + _sigmoid(_dot(h1, wpg_ref[...])) * _dot(p_ref[...], wp_ref[...])
    y_ref[...] = h2 * lax.rsqrt(jnp.mean(h2 * h2, axis=-1, keepdims=True) + NORM_EPS) * fnw_ref[...]


def _out_stage(o_a, o_b, proj, x2d, p2d, anw, bnw, wua, wub, wout, wpg, wp, fnw):
    n = x2d.shape[0]
    tm = min(ROW_TILE, n)
    const = lambda i: (0, 0)
    return pl.pallas_call(
        _out_kernel,
        grid=(n // tm,),
        in_specs=[
            pl.BlockSpec((tm, GLA_VAL), lambda i: (i, 0)),
            pl.BlockSpec((tm, GDN_VAL), lambda i: (i, 0)),
            pl.BlockSpec((tm, D_MODEL), lambda i: (i, P_GATE_A // D_MODEL)),
            pl.BlockSpec((tm, D_MODEL), lambda i: (i, P_GATE_B // D_MODEL)),
            pl.BlockSpec((tm, GLA_VAL), lambda i: (i, P_Z_A // GLA_VAL)),
            pl.BlockSpec((tm, GDN_VAL), lambda i: (i, P_Z_B // GDN_VAL)),
            pl.BlockSpec((tm, D_MODEL), lambda i: (i, 0)),
            pl.BlockSpec((tm, PLE_DIM), lambda i: (i, 0)),
            pl.BlockSpec((1, GLA_DV), const),
            pl.BlockSpec((1, GDN_DV), const),
            pl.BlockSpec((GLA_VAL, D_MODEL), const),
            pl.BlockSpec((GDN_VAL, D_MODEL), const),
            pl.BlockSpec((D_MODEL, D_MODEL), const),
            pl.BlockSpec((D_MODEL, D_MODEL), const),
            pl.BlockSpec((PLE_DIM, D_MODEL), const),
            pl.BlockSpec((1, D_MODEL), const),
        ],
        out_specs=pl.BlockSpec((tm, D_MODEL), lambda i: (i, 0)),
        out_shape=jax.ShapeDtypeStruct((n, D_MODEL), F32),
        compiler_params=pltpu.CompilerParams(dimension_semantics=("arbitrary",),
                                             vmem_limit_bytes=VMEM_LIMIT),
        name="out_stage",
    )(o_a, o_b, proj, proj, proj, proj, x2d, p2d, anw, bnw, wua, wub, wout, wpg, wp, fnw)


def _regroup_w_in(w_in):
    offs = [0]
    for s in IN_SPLITS:
        offs.append(offs[-1] + s)
    part = lambda i: w_in[:, offs[i]:offs[i + 1]]
    q_a, k_a, v_a, g_a, z_a, qkv_b, a_b, b_b, z_b, gate_a, gate_b = (part(i) for i in range(11))
    pad = jnp.zeros((D_MODEL, LANES - GLA_GATE_RANK - 2 * GDN_HEADS), w_in.dtype)
    return jnp.concatenate([q_a, k_a, v_a, gate_a, gate_b, qkv_b, z_a, z_b, g_a, a_b, b_b, pad],
                           axis=1).astype(BF16)


def _small_lane_vec(v, off):
    return jnp.zeros((1, LANES), F32).at[0, off:off + v.shape[0]].set(v.astype(F32))


def _trunk(x, p, s_gla, s_gdn, conv_state, wts):
    bsz, t_len, _ = x.shape
    c = min(PROMPT_CHUNK, t_len)
    tb = c
    n = bsz * t_len
    x2d = x.reshape(n, D_MODEL)
    p2d = p.reshape(n, PLE_DIM)
    proj = _inproj(x2d, wts["norm_w"], wts["w_in_r"])
    proj3 = proj.reshape(bsz, t_len, P_COLS)
    o_a, gla_fin = _gla(proj, s_gla.reshape(bsz, GLA_HEADS * GLA_DK, GLA_DV), wts["wgg"], wts["bgg"],
                        bsz, t_len, c, tb)
    s_gdn2 = s_gdn.reshape(bsz, GDN_HEADS * GDN_DK, GDN_DV)
    if t_len % GROUP_ROWS == 0:
        o_b, gdn_fin, conv_new = _gdn_prompt(proj3, conv_state, s_gdn2, wts["conv_w"], wts["alog_v"],
                                             wts["dtb_v"], c)
    else:
        assert GROUP_ROWS % t_len == 0 and bsz % (GROUP_ROWS // t_len) == 0 and t_len >= CONV_WIDTH - 1
        o_b, gdn_fin, conv_new = _gdn_sample(proj3, conv_state, s_gdn2, wts["conv_w"], wts["alog_v"],
                                             wts["dtb_v"])
    y = _out_stage(o_a, o_b.reshape(n, GDN_VAL), proj, x2d, p2d, wts["anw"], wts["bnw"], wts["wua"],
                   wts["wub"], wts["wout"], wts["wpg"], wts["wp"], wts["fnw"])
    return (y.reshape(bsz, t_len, D_MODEL),
            gla_fin.reshape(1, bsz, GLA_HEADS, GLA_DK, GLA_DV),
            gdn_fin.reshape(1, bsz, GDN_HEADS, GDN_DK, GDN_DV),
            conv_new.reshape(1, bsz, CONV_WIDTH - 1, CONV_CH))


def kernel(x_prompt, x_sample, state_gla, state_gdn, state_conv, p_prompt, p_sample, norm_w, w_in,
           w_gla_gate, b_gla_gate, gla_norm_w, conv_w, gdn_a_log, gdn_dt_bias, gdn_norm_w,
           w_up_gla, w_up_gdn, w_out, w_ple_gate, w_ple, final_norm_w):
    wgg = jnp.zeros((LANES, GLA_KEY), F32).at[SM_G:SM_G + GLA_GATE_RANK].set(w_gla_gate[0]).astype(BF16)
    wts = {
        "norm_w": norm_w[0].reshape(1, D_MODEL),
        "w_in_r": _regroup_w_in(w_in[0]),
        "wgg": wgg,
        "bgg": b_gla_gate[0].reshape(1, GLA_KEY),
        "conv_w": conv_w[0],
        "alog_v": _small_lane_vec(gdn_a_log[0], SM_A),
        "dtb_v": _small_lane_vec(gdn_dt_bias[0], SM_A),
        "anw": gla_norm_w[0].reshape(1, GLA_DV),
        "bnw": gdn_norm_w[0].reshape(1, GDN_DV),
        "wua": w_up_gla[0].astype(BF16),
        "wub": w_up_gdn[0].astype(BF16),
        "wout": w_out[0].astype(BF16),
        "wpg": w_ple_gate[0].astype(BF16),
        "wp": w_ple[0].astype(BF16),
        "fnw": final_norm_w.reshape(1, D_MODEL),
    }
    bsz = x_prompt.shape[0]
    dt = x_prompt.dtype
    y_p, gla_p, gdn_p, conv_p = _trunk(
        x_prompt, p_prompt[0],
        jnp.zeros((bsz, GLA_HEADS, GLA_DK, GLA_DV), dt), jnp.zeros((bsz, GDN_HEADS, GDN_DK, GDN_DV), dt),
        jnp.zeros((bsz, CONV_WIDTH - 1, CONV_CH), dt), wts)
    y_s, gla_s, gdn_s, conv_s = _trunk(x_sample, p_sample[0], state_gla[0], state_gdn[0], state_conv[0], wts)
    return (y_p, y_s, gla_p, gdn_p, conv_p, gla_s, gdn_s, conv_s)
```

```python
import functools

import jax
import jax.numpy as jnp
from jax import lax
from jax.experimental import pallas as pl
from jax.experimental.pallas import tpu as pltpu

F32 = jnp.float32
BF16 = jnp.bfloat16

D_MODEL = 1024
PLE_DIM = 256
NORM_EPS = 1e-6
GLA_HEADS = 4
GLA_DK = 64
GLA_DV = 128
GLA_KEY = GLA_HEADS * GLA_DK
GLA_VAL = GLA_HEADS * GLA_DV
GLA_GATE_RANK = 16
GLA_GATE_TEMP = 16.0
GDN_HEADS = 4
GDN_DK = 128
GDN_DV = 128
GDN_KEY = GDN_HEADS * GDN_DK
GDN_VAL = GDN_HEADS * GDN_DV
CONV_WIDTH = 4
CONV_CH = 2 * GDN_KEY + GDN_VAL
IN_SPLITS = (GLA_KEY, GLA_KEY, GLA_VAL, GLA_GATE_RANK, GLA_VAL, CONV_CH, GDN_HEADS, GDN_HEADS,
             GDN_VAL, D_MODEL, D_MODEL)

LANES = 128
SUBLANES = 8

P_GLA_QKV = 0
P_GATE_A = 1024
P_GATE_B = 2048
P_QKV_B = 3072
P_Z_A = 4608
P_Z_B = 5120
P_SMALL = 5632
P_COLS = P_SMALL + LANES
SM_G = 0
SM_A = GLA_GATE_RANK
SM_B = GLA_GATE_RANK + GDN_HEADS

PROMPT_CHUNK = 64
GROUP_ROWS = 128
BATCH_INTERLEAVE = 4
GDN_BATCH_INTERLEAVE = 2
ROW_TILE = 256
VMEM_LIMIT = 56 * 1024 * 1024


def _dot(a, b):
    return jnp.dot(a.astype(BF16), b.astype(BF16), preferred_element_type=F32)


def _dot_nt(a, b):
    return lax.dot_general(a.astype(BF16), b.astype(BF16), (((1,), (1,)), ((), ())),
                           preferred_element_type=F32)


def _split2(x):
    h1 = x.astype(BF16)
    return h1, (x - h1.astype(F32)).astype(BF16)


def _dot_f32(a, b):
    a1, a2 = _split2(a)
    b1, b2 = _split2(b)
    d = functools.partial(jnp.dot, preferred_element_type=F32)
    return d(a1, b1) + (d(a1, b2) + d(a2, b1))


def _cumsum_rows(tri, x):
    x1, x2 = _split2(x)
    d = functools.partial(jnp.dot, preferred_element_type=F32)
    return d(tri, x1) + d(tri, x2)


def _softplus(x):
    return jnp.maximum(x, 0.0) + jnp.log(1.0 + jnp.exp(-jnp.abs(x)))


def _sigmoid(x):
    return 1.0 / (1.0 + jnp.exp(-x))


def _silu(x):
    return x * _sigmoid(x)


def _interleave(emitters):
    results = [None] * len(emitters)
    live = list(range(len(emitters)))
    while live:
        for i in list(live):
            try:
                next(emitters[i])
            except StopIteration as stop:
                results[i] = stop.value
                live.remove(i)
    return results


def _block_masks(r, block):
    row = lax.broadcasted_iota(jnp.int32, (r, r), 0)
    col = lax.broadcasted_iota(jnp.int32, (r, r), 1)
    same = (row // block) == (col // block)
    return same & (row >= col), same & (row > col), same


def _inproj_kernel(x_ref, nw_ref, w_ref, o_ref):
    x = x_ref[...]
    xn = x * lax.rsqrt(jnp.mean(x * x, axis=-1, keepdims=True) + NORM_EPS) * nw_ref[...]
    xb = xn.astype(BF16)
    step = 512
    for c0 in range(0, P_SMALL, step):
        o_ref[:, c0:c0 + step] = jnp.dot(xb, w_ref[:, c0:c0 + step], preferred_element_type=F32)
    o_ref[:, P_SMALL:P_COLS] = jnp.dot(xb, w_ref[:, P_SMALL:P_COLS], preferred_element_type=F32)


def _inproj(x2d, norm_w, w_in_r):
    n = x2d.shape[0]
    tm = min(ROW_TILE, n)
    return pl.pallas_call(
        _inproj_kernel,
        grid=(n // tm,),
        in_specs=[
            pl.BlockSpec((tm, D_MODEL), lambda i: (i, 0)),
            pl.BlockSpec((1, D_MODEL), lambda i: (0, 0)),
            pl.BlockSpec((D_MODEL, P_COLS), lambda i: (0, 0), pipeline_mode=pl.Buffered(1)),
        ],
        out_specs=pl.BlockSpec((tm, P_COLS), lambda i: (i, 0)),
        out_shape=jax.ShapeDtypeStruct((n, P_COLS), F32),
        compiler_params=pltpu.CompilerParams(dimension_semantics=("arbitrary",),
                                             vmem_limit_bytes=VMEM_LIMIT),
        name="inproj",
    )(x2d, norm_w, w_in_r)


GLA_PAIRS = GLA_HEADS // 2
GLA_QKV = 2 * GLA_KEY + GLA_VAL


def _gla_prepass(q, k, v, sm, wgg, bgg, same_le, same):
    r = q.shape[0]
    heads = range(GLA_HEADS)
    pairs = range(GLA_PAIRS)
    pre = _dot(sm, wgg) + bgg
    yield
    gk = (jnp.minimum(pre, 0.0) - jnp.log(1.0 + jnp.exp(-jnp.abs(pre)))) * (1.0 / GLA_GATE_TEMP)
    sums = _cumsum_rows(jnp.concatenate([jnp.where(same_le, 1.0, 0.0).astype(BF16),
                                         jnp.where(same, 1.0, 0.0).astype(BF16)], axis=0), gk)
    yield
    bcum = sums[:r]
    bend = sums[r:]
    q_e = q * jnp.exp(bcum) * (GLA_DK ** -0.5)
    k_e = k * jnp.exp(-bcum)
    lane = lax.broadcasted_iota(jnp.int32, (r, LANES), 1)
    in_head = [lane < GLA_DK, lane >= GLA_DK]
    pl_ = [slice(p * LANES, (p + 1) * LANES) for p in pairs]
    qm = [jnp.where(in_head[h % 2], q_e[:, pl_[h // 2]], 0.0) for h in heads]
    att = [jnp.where(same_le, _dot_nt(qm[h], k_e[:, pl_[h // 2]]), 0.0) for h in heads]
    yield
    o_intra = [_dot(att[h], v[h]) for h in heads]
    k_end = k * jnp.exp(bend - bcum)
    k_end_t = [k_end[:, pl_[p]].T for p in pairs]
    bend_t = [bend[:, pl_[p]].T for p in pairs]
    yield
    return qm, o_intra, k_end_t, bend_t


def _gla_prompt_kernel(qkv_ref, small_ref, s0_ref, wgg_ref, bgg_ref, o_ref, sfin_ref, s_scr, *, c):
    t = pl.program_id(0)
    nb, r, _ = qkv_ref.shape

    @pl.when(t == 0)
    def _():
        s_scr[...] = s0_ref[...]

    same_le, _, same = _block_masks(r, c)
    zeros = jnp.zeros((c, GLA_DV), F32)
    n_sub = r // c
    heads = range(GLA_HEADS)
    pairs = range(GLA_PAIRS)

    def one_batch(b):
        q = qkv_ref[b, :, 0:GLA_KEY]
        k = qkv_ref[b, :, GLA_KEY:2 * GLA_KEY]
        v = [qkv_ref[b, :, 2 * GLA_KEY + h * GLA_DV:2 * GLA_KEY + (h + 1) * GLA_DV] for h in heads]
        qm, o_intra, k_end_t, bend_t = yield from _gla_prepass(
            q, k, v, small_ref[b], wgg_ref[...], bgg_ref[...], same_le, same)
        s = [s_scr[b, p * LANES:(p + 1) * LANES, :] for p in pairs]
        for i in range(n_sub):
            rows = slice(i * c, (i + 1) * c)
            ws = [_dot(jnp.concatenate([qm[2 * p][rows], qm[2 * p + 1][rows]], axis=0), s[p]) for p in pairs]
            padded = [jnp.concatenate([zeros] * i + [v[h][rows]] + [zeros] * (n_sub - 1 - i), axis=0)
                      for h in heads]
            upd = [_dot(k_end_t[h // 2][(h % 2) * GLA_DK:(h % 2 + 1) * GLA_DK, :], padded[h]) for h in heads]
            yield
            for h in heads:
                o_ref[b, rows, h * GLA_DV:(h + 1) * GLA_DV] = (
                    o_intra[h][rows] + ws[h // 2][(h % 2) * c:(h % 2 + 1) * c])
            s = [s[p] * jnp.exp(bend_t[p][:, i * c:i * c + 1])
                 + jnp.concatenate([upd[2 * p], upd[2 * p + 1]], axis=0) for p in pairs]
        for p in pairs:
            s_scr[b, p * LANES:(p + 1) * LANES, :] = s[p]

    def per_step(i, carry):
        _interleave([one_batch(i * BATCH_INTERLEAVE + j) for j in range(BATCH_INTERLEAVE)])
        return carry

    lax.fori_loop(0, nb // BATCH_INTERLEAVE, per_step, None)

    @pl.when(t == pl.num_programs(0) - 1)
    def _():
        sfin_ref[...] = s_scr[...]


def _gla_sample_kernel(qkv_ref, small_ref, s0_ref, wgg_ref, bgg_ref, o_ref, sfin_ref):
    ns, c, _ = qkv_ref.shape
    r = ns * c
    heads = range(GLA_HEADS)
    same_le, _, same = _block_masks(r, c)
    q = qkv_ref[:, :, 0:GLA_KEY].reshape(r, GLA_KEY)
    k = qkv_ref[:, :, GLA_KEY:2 * GLA_KEY].reshape(r, GLA_KEY)
    v = [qkv_ref[:, :, 2 * GLA_KEY + h * GLA_DV:2 * GLA_KEY + (h + 1) * GLA_DV].reshape(r, GLA_DV)
         for h in heads]
    (qm, o_intra, k_end_t, bend_t), = _interleave([_gla_prepass(
        q, k, v, small_ref[...].reshape(r, LANES), wgg_ref[...], bgg_ref[...], same_le, same)])
    seq_of_row = lax.broadcasted_iota(jnp.int32, (r, GLA_DV), 0) // c
    for p in range(GLA_PAIRS):
        ps = slice(p * LANES, (p + 1) * LANES)
        inter = [[], []]
        for s in range(ns):
            rows = slice(s * c, (s + 1) * c)
            ws = _dot(jnp.concatenate([qm[2 * p][rows], qm[2 * p + 1][rows]], axis=0), s0_ref[s, ps, :])
            inter[0].append(ws[:c])
            inter[1].append(ws[c:])
        for hh in range(2):
            h = 2 * p + hh
            o = o_intra[h] + jnp.concatenate(inter[hh], axis=0)
            o_ref[:, :, h * GLA_DV:(h + 1) * GLA_DV] = o.reshape(ns, c, GLA_DV)
        for s in range(ns):
            upd = [_dot(k_end_t[p][hh * GLA_DK:(hh + 1) * GLA_DK, :],
                        jnp.where(seq_of_row == s, v[2 * p + hh], 0.0)) for hh in range(2)]
            sfin_ref[s, ps, :] = (s0_ref[s, ps, :] * jnp.exp(bend_t[p][:, s * c:s * c + 1])
                                  + jnp.concatenate(upd, axis=0))


def _gla_prompt(proj3, s0, wgg, bgg, c):
    bsz, t_len, _ = proj3.shape
    rows = GLA_HEADS * GLA_DK
    r = GROUP_ROWS
    full3 = lambda t: (0, 0, 0)
    const2 = lambda t: (0, 0)
    return pl.pallas_call(
        functools.partial(_gla_prompt_kernel, c=c),
        grid=(t_len // r,),
        in_specs=[
            pl.BlockSpec((bsz, r, GLA_QKV), lambda t: (0, t, P_GLA_QKV // GLA_QKV)),
            pl.BlockSpec((bsz, r, LANES), lambda t: (0, t, P_SMALL // LANES)),
            pl.BlockSpec((bsz, rows, GLA_DV), full3),
            pl.BlockSpec((LANES, GLA_KEY), const2),
            pl.BlockSpec((1, GLA_KEY), const2),
        ],
        out_specs=[
            pl.BlockSpec((bsz, r, GLA_VAL), lambda t: (0, t, 0)),
            pl.BlockSpec((bsz, rows, GLA_DV), full3),
        ],
        out_shape=[
            jax.ShapeDtypeStruct((bsz, t_len, GLA_VAL), F32),
            jax.ShapeDtypeStruct((bsz, rows, GLA_DV), F32),
        ],
        scratch_shapes=[pltpu.VMEM((bsz, rows, GLA_DV), F32)],
        compiler_params=pltpu.CompilerParams(dimension_semantics=("arbitrary",),
                                             vmem_limit_bytes=VMEM_LIMIT),
        name="gla_prompt",
    )(proj3, proj3, s0, wgg, bgg)


def _gla_sample(proj3, s0, wgg, bgg):
    bsz, c, _ = proj3.shape
    rows = GLA_HEADS * GLA_DK
    ns = GROUP_ROWS // c
    grp = lambda g: (g, 0, 0)
    const2 = lambda g: (0, 0)
    return pl.pallas_call(
        _gla_sample_kernel,
        grid=(bsz // ns,),
        in_specs=[
            pl.BlockSpec((ns, c, GLA_QKV), lambda g: (g, 0, P_GLA_QKV // GLA_QKV)),
            pl.BlockSpec((ns, c, LANES), lambda g: (g, 0, P_SMALL // LANES)),
            pl.BlockSpec((ns, rows, GLA_DV), grp),
            pl.BlockSpec((LANES, GLA_KEY), const2),
            pl.BlockSpec((1, GLA_KEY), const2),
        ],
        out_specs=[
            pl.BlockSpec((ns, c, GLA_VAL), grp),
            pl.BlockSpec((ns, rows, GLA_DV), grp),
        ],
        out_shape=[
            jax.ShapeDtypeStruct((bsz, c, GLA_VAL), F32),
            jax.ShapeDtypeStruct((bsz, rows, GLA_DV), F32),
        ],
        compiler_params=pltpu.CompilerParams(dimension_semantics=("arbitrary",),
                                             vmem_limit_bytes=VMEM_LIMIT),
        name="gla_sample",
    )(proj3, proj3, s0, wgg, bgg)


def _block_unit_lower_inverse(a_list, block):
    r = a_list[0].shape[0]
    row = lax.broadcasted_iota(jnp.int32, (r, r), 0)
    col = lax.broadcasted_iota(jnp.int32, (r, r), 1)
    eye = jnp.where(row == col, 1.0, 0.0)
    xs = [eye - a for a in a_list]
    ps = [_dot_f32(a, a) for a in a_list]
    n = 2
    d = functools.partial(jnp.dot, preferred_element_type=F32)
    while True:
        yield
        p_parts = [_split2(p) for p in ps]
        x_parts = [_split2(x) for x in xs]
        if 2 * n >= block:
            out = []
            for x, (p1, p2), (x1, x2) in zip(xs, p_parts, x_parts):
                xp = d(x1, jnp.concatenate([p1, p2], axis=1))
                out.append(x + (xp[:, :r] + (xp[:, r:] + d(x2, p1))))
            return out
        ms = [d(jnp.concatenate([p1, x1], axis=0), jnp.concatenate([p1, p2], axis=1))
              for (p1, p2), (x1, _) in zip(p_parts, x_parts)]
        qs = [d(jnp.concatenate([p2, x2], axis=0), p1)
              for (p1, p2), (_, x2) in zip(p_parts, x_parts)]
        ss = [m[:, :r] + (m[:, r:] + q) for m, q in zip(ms, qs)]
        ps = [s[:r] for s in ss]
        xs = [x + s[r:] for x, s in zip(xs, ss)]
        n *= 2


def _gdn_decays(sm, alog_c, dtb_c, same_lt, same):
    r = sm.shape[0]
    heads = range(GDN_HEADS)
    ab = sm.T[SM_A:SM_A + 2 * GDN_HEADS, :]
    g8 = -jnp.exp(alog_c) * _softplus(ab + dtb_c)
    beta_c = _sigmoid(ab).T
    g1, g2 = _split2(g8)
    same_ge = same & jnp.logical_not(same_lt)
    sel = jnp.concatenate([jnp.where(same_ge, 1.0, 0.0).astype(BF16),
                           jnp.where(same, 1.0, 0.0).astype(BF16)], axis=1)
    sums = jnp.dot(jnp.concatenate([g1, g2], axis=0), sel, preferred_element_type=F32)
    sums = sums[:2 * GDN_HEADS] + sums[2 * GDN_HEADS:]
    dec_rows = sums[:, :r]
    dec_c = dec_rows.T
    dend_c = sums[:, r:].T
    return ([dec_c[:, h:h + 1] for h in heads], [dec_rows[h:h + 1, :] for h in heads],
            [dend_c[:, h:h + 1] for h in heads], [beta_c[:, GDN_HEADS + h:GDN_HEADS + h + 1] for h in heads])


def _gdn_prepass(conv, sm, alog_c, dtb_c, same_le, same_lt, same, block):
    r = conv.shape[0]
    heads = range(GDN_HEADS)
    dcol, drow, dend, beta = _gdn_decays(sm, alog_c, dtb_c, same_lt, same)
    gamma = [jnp.where(same_le, jnp.exp(jnp.where(same_le, dcol[h] - drow[h], 0.0)), 0.0) for h in heads]
    q = [conv[:, h * GDN_DK:(h + 1) * GDN_DK] for h in heads]
    k = [conv[:, GDN_KEY + h * GDN_DK:GDN_KEY + (h + 1) * GDN_DK] for h in heads]
    v = [conv[:, 2 * GDN_KEY + h * GDN_DV:2 * GDN_KEY + (h + 1) * GDN_DV] for h in heads]
    q = [x * lax.rsqrt(jnp.sum(x * x, axis=-1, keepdims=True) + NORM_EPS) * (GDN_DK ** -0.5) for x in q]
    k = [x * lax.rsqrt(jnp.sum(x * x, axis=-1, keepdims=True) + NORM_EPS) for x in k]
    yield
    kb = [k[h] * beta[h] for h in heads]
    vb = [v[h] * beta[h] for h in heads]
    kq = [_dot_nt(jnp.concatenate([kb[h], q[h]], axis=0), k[h]) for h in heads]
    yield
    a_mat = [jnp.where(same_lt, kq[h][:r] * gamma[h], 0.0) for h in heads]
    qk = [kq[h][r:] * gamma[h] for h in heads]
    t_inv = yield from _block_unit_lower_inverse(a_mat, block)
    edec = [jnp.exp(dcol[h]) for h in heads]
    uw = [_dot(t_inv[h], jnp.concatenate([vb[h], kb[h] * edec[h]], axis=1)) for h in heads]
    q_e = [q[h] * edec[h] for h in heads]
    k_end_t = [(k[h] * jnp.exp(dend[h] - dcol[h])).T for h in heads]
    yield
    u = [x[:, :GDN_DV] for x in uw]
    w = [x[:, GDN_DV:] for x in uw]
    return u, w, q_e, qk, k_end_t, dend


def _conv_silu(win, cw_ref):
    conv = win(0) * cw_ref[CONV_WIDTH - 1:CONV_WIDTH, :]
    for j in range(1, CONV_WIDTH):
        conv = conv + win(j) * cw_ref[CONV_WIDTH - 1 - j:CONV_WIDTH - j, :]
    return _silu(conv)


def _gdn_prompt_kernel(qkv_ref, small_ref, cs_ref, s0_ref, cw_ref, alog_ref, dtb_ref,
                       o_ref, sfin_ref, cnew_ref, xbuf, prev, s_scr, *, c):
    t = pl.program_id(0)
    nb, r, _ = qkv_ref.shape
    keep = CONV_WIDTH - 1
    base = SUBLANES

    @pl.when(t == 0)
    def _():
        s_scr[...] = s0_ref[...]
        prev[:, base - keep:base, :] = cs_ref[...]

    same_le, same_lt, same = _block_masks(r, c)
    zeros = jnp.zeros((c, GDN_DV), F32)
    n_sub = r // c

    def one_batch(b, slot):
        xb = xbuf.at[slot]
        xb[base - keep:base, :] = prev[b, base - keep:base, :]
        xb[base:base + r, :] = qkv_ref[b]
        conv = _conv_silu(lambda j: xb[base - j:base - j + r, :], cw_ref)
        prev[b, base - keep:base, :] = xb[base + r - keep:base + r, :]
        yield
        u, w, q_e, qk, k_end_t, dend = yield from _gdn_prepass(
            conv, small_ref[b], alog_ref[...], dtb_ref[...], same_le, same_lt, same, c)
        heads = range(GDN_HEADS)
        hs = [slice(h * GDN_DK, (h + 1) * GDN_DK) for h in heads]
        s = [s_scr[b, hs[h], :] for h in heads]
        for i in range(n_sub):
            rows = slice(i * c, (i + 1) * c)
            ws = [_dot(jnp.concatenate([w[h][rows], q_e[h][rows]], axis=0), s[h]) for h in heads]
            yield
            v_new = [u[h][rows] - ws[h][:c] for h in heads]
            padded = [jnp.concatenate([zeros] * i + [v_new[h]] + [zeros] * (n_sub - 1 - i), axis=0)
                      for h in heads]
            upd = [_dot(jnp.concatenate([qk[h][rows], k_end_t[h]], axis=0), padded[h]) for h in heads]
            yield
            for h in heads:
                o_ref[b, rows, h * GDN_DV:(h + 1) * GDN_DV] = ws[h][c:] + upd[h][:c]
            s = [s[h] * jnp.exp(dend[h][i * c:i * c + 1, :]) + upd[h][c:] for h in heads]
        for h in heads:
            s_scr[b, hs[h], :] = s[h]

    def per_step(i, carry):
        _interleave([one_batch(i * GDN_BATCH_INTERLEAVE + j, j) for j in range(GDN_BATCH_INTERLEAVE)])
        return carry

    lax.fori_loop(0, nb // GDN_BATCH_INTERLEAVE, per_step, None)

    @pl.when(t == pl.num_programs(0) - 1)
    def _():
        sfin_ref[...] = s_scr[...]
        cnew_ref[...] = prev[:, base - keep:base, :]


def _gdn_sample_kernel(qkv_ref, small_ref, cs_ref, s0_ref, cw_ref, alog_ref, dtb_ref,
                       o_ref, sfin_ref, cnew_ref, xbuf):
    ns, c, _ = qkv_ref.shape
    r = ns * c
    keep = CONV_WIDTH - 1
    base = SUBLANES
    xbuf[:, base - keep:base, :] = cs_ref[...]
    xbuf[:, base:base + c, :] = qkv_ref[...]
    conv = _conv_silu(lambda j: xbuf[:, base - j:base - j + c, :], cw_ref).reshape(r, CONV_CH)
    cnew_ref[...] = xbuf[:, base + c - keep:base + c, :]

    same_le, same_lt, same = _block_masks(r, c)
    seq_of_row = lax.broadcasted_iota(jnp.int32, (r, GDN_DV), 0) // c
    (u, w, q_e, qk, k_end_t, dend), = _interleave([_gdn_prepass(
        conv, small_ref[...].reshape(r, LANES), alog_ref[...], dtb_ref[...], same_le, same_lt, same, c)])
    for h in range(GDN_HEADS):
        hs = slice(h * GDN_DK, (h + 1) * GDN_DK)
        v_parts, o_parts = [], []
        for s in range(ns):
            rows = slice(s * c, (s + 1) * c)
            ws = _dot(jnp.concatenate([w[h][rows], q_e[h][rows]], axis=0), s0_ref[s, hs, :])
            v_parts.append(u[h][rows] - ws[:c])
            o_parts.append(ws[c:])
        v_new = jnp.concatenate(v_parts, axis=0)
        o = jnp.concatenate(o_parts, axis=0) + _dot(qk[h], v_new)
        o_ref[:, :, h * GDN_DV:(h + 1) * GDN_DV] = o.reshape(ns, c, GDN_DV)
        for s in range(ns):
            upd = _dot(k_end_t[h], jnp.where(seq_of_row == s, v_new, 0.0))
            sfin_ref[s, hs, :] = s0_ref[s, hs, :] * jnp.exp(dend[h][s * c:s * c + 1, :]) + upd


def _gdn_prompt(proj3, conv_state, s0, conv_w, alog_v, dtb_v, c):
    bsz, t_len, _ = proj3.shape
    rows = GDN_HEADS * GDN_DK
    keep = CONV_WIDTH - 1
    r = GROUP_ROWS
    full3 = lambda t: (0, 0, 0)
    const2 = lambda t: (0, 0)
    return pl.pallas_call(
        functools.partial(_gdn_prompt_kernel, c=c),
        grid=(t_len // r,),
        in_specs=[
            pl.BlockSpec((bsz, r, CONV_CH), lambda t: (0, t, P_QKV_B // CONV_CH)),
            pl.BlockSpec((bsz, r, LANES), lambda t: (0, t, P_SMALL // LANES)),
            pl.BlockSpec((bsz, keep, CONV_CH), full3),
            pl.BlockSpec((bsz, rows, GDN_DV), full3),
            pl.BlockSpec((CONV_WIDTH, CONV_CH), const2),
            pl.BlockSpec((2 * GDN_HEADS, 1), const2),
            pl.BlockSpec((2 * GDN_HEADS, 1), const2),
        ],
        out_specs=[
            pl.BlockSpec((bsz, r, GDN_VAL), lambda t: (0, t, 0)),
            pl.BlockSpec((bsz, rows, GDN_DV), full3),
            pl.BlockSpec((bsz, keep, CONV_CH), full3),
        ],
        out_shape=[
            jax.ShapeDtypeStruct((bsz, t_len, GDN_VAL), F32),
            jax.ShapeDtypeStruct((bsz, rows, GDN_DV), F32),
            jax.ShapeDtypeStruct((bsz, keep, CONV_CH), F32),
        ],
        scratch_shapes=[pltpu.VMEM((GDN_BATCH_INTERLEAVE, SUBLANES + r, CONV_CH), F32),
                        pltpu.VMEM((bsz, SUBLANES, CONV_CH), F32),
                        pltpu.VMEM((bsz, rows, GDN_DV), F32)],
        compiler_params=pltpu.CompilerParams(dimension_semantics=("arbitrary",),
                                             vmem_limit_bytes=VMEM_LIMIT),
        name="gdn_prompt",
    )(proj3, proj3, conv_state, s0, conv_w, alog_v, dtb_v)


def _gdn_sample(proj3, conv_state, s0, conv_w, alog_v, dtb_v):
    bsz, c, _ = proj3.shape
    rows = GDN_HEADS * GDN_DK
    keep = CONV_WIDTH - 1
    ns = GROUP_ROWS // c
    grp = lambda g: (g, 0, 0)
    const2 = lambda g: (0, 0)
    return pl.pallas_call(
        _gdn_sample_kernel,
        grid=(bsz // ns,),
        in_specs=[
            pl.BlockSpec((ns, c, CONV_CH), lambda g: (g, 0, P_QKV_B // CONV_CH)),
            pl.BlockSpec((ns, c, LANES), lambda g: (g, 0, P_SMALL // LANES)),
            pl.BlockSpec((ns, keep, CONV_CH), grp),
            pl.BlockSpec((ns, rows, GDN_DV), grp),
            pl.BlockSpec((CONV_WIDTH, CONV_CH), const2),
            pl.BlockSpec((2 * GDN_HEADS, 1), const2),
            pl.BlockSpec((2 * GDN_HEADS, 1), const2),
        ],
        out_specs=[
            pl.BlockSpec((ns, c, GDN_VAL), grp),
            pl.BlockSpec((ns, rows, GDN_DV), grp),
            pl.BlockSpec((ns, keep, CONV_CH), grp),
        ],
        out_shape=[
            jax.ShapeDtypeStruct((bsz, c, GDN_VAL), F32),
            jax.ShapeDtypeStruct((bsz, rows, GDN_DV), F32),
            jax.ShapeDtypeStruct((bsz, keep, CONV_CH), F32),
        ],
        scratch_shapes=[pltpu.VMEM((ns, 2 * SUBLANES, CONV_CH), F32)],
        compiler_params=pltpu.CompilerParams(dimension_semantics=("arbitrary",),
                                             vmem_limit_bytes=VMEM_LIMIT),
        name="gdn_sample",
    )(proj3, proj3, conv_state, s0, conv_w, alog_v, dtb_v)


def _head_norm_gate(o, z, w):
    parts = []
    for h in range(o.shape[-1] // LANES):
        oh = o[:, h * LANES:(h + 1) * LANES]
        parts.append(oh * lax.rsqrt(jnp.mean(oh * oh, axis=-1, keepdims=True) + NORM_EPS) * w)
    return jnp.concatenate(parts, axis=-1) * _silu(z)


def _out_kernel(oa_ref, ob_ref, ga_ref, gb_ref, za_ref, zb_ref, x_ref, p_ref, anw_ref, bnw_ref,
                wua_ref, wub_ref, wout_ref, wpg_ref, wp_ref, fnw_ref, y_ref):
    ya = _dot(_head_norm_gate(oa_ref[...], za_ref[...], anw_ref[...]), wua_ref[...])
    yb = _dot(_head_norm_gate(ob_ref[...], zb_ref[...], bnw_ref[...]), wub_ref[...])
    merged = _sigmoid(ga_ref[...]) * ya + _sigmoid(gb_ref[...]) * yb
    h1 = x_ref[...] + _dot(merged, wout_ref[...])
    h2 = h1 + _sigmoid(_dot(h1, wpg_ref[...])) * _dot(p_ref[...], wp_ref[...])
    y_ref[...] = h2 * lax.rsqrt(jnp.mean(h2 * h2, axis=-1, keepdims=True) + NORM_EPS) * fnw_ref[...]


def _out_stage(o_a, o_b, proj, x2d, p2d, anw, bnw, wua, wub, wout, wpg, wp, fnw):
    n = x2d.shape[0]
    tm = min(ROW_TILE, n)
    const = lambda i: (0, 0)
    return pl.pallas_call(
        _out_kernel,
        grid=(n // tm,),
        in_specs=[
            pl.BlockSpec((tm, GLA_VAL), lambda i: (i, 0)),
            pl.BlockSpec((tm, GDN_VAL), lambda i: (i, 0)),
            pl.BlockSpec((tm, D_MODEL), lambda i: (i, P_GATE_A // D_MODEL)),
            pl.BlockSpec((tm, D_MODEL), lambda i: (i, P_GATE_B // D_MODEL)),
            pl.BlockSpec((tm, GLA_VAL), lambda i: (i, P_Z_A // GLA_VAL)),
            pl.BlockSpec((tm, GDN_VAL), lambda i: (i, P_Z_B // GDN_VAL)),
            pl.BlockSpec((tm, D_MODEL), lambda i: (i, 0)),
            pl.BlockSpec((tm, PLE_DIM), lambda i: (i, 0)),
            pl.BlockSpec((1, GLA_DV), const),
            pl.BlockSpec((1, GDN_DV), const),
            pl.BlockSpec((GLA_VAL, D_MODEL), const),
            pl.BlockSpec((GDN_VAL, D_MODEL), const),
            pl.BlockSpec((D_MODEL, D_MODEL), const),
            pl.BlockSpec((D_MODEL, D_MODEL), const),
            pl.BlockSpec((PLE_DIM, D_MODEL), const),
            pl.BlockSpec((1, D_MODEL), const),
        ],
        out_specs=pl.BlockSpec((tm, D_MODEL), lambda i: (i, 0)),
        out_shape=jax.ShapeDtypeStruct((n, D_MODEL), F32),
        compiler_params=pltpu.CompilerParams(dimension_semantics=("arbitrary",),
                                             vmem_limit_bytes=VMEM_LIMIT),
        name="out_stage",
    )(o_a, o_b, proj, proj, proj, proj, x2d, p2d, anw, bnw, wua, wub, wout, wpg, wp, fnw)


def _regroup_w_in(w_in):
    offs = [0]
    for s in IN_SPLITS:
        offs.append(offs[-1] + s)
    part = lambda i: w_in[:, offs[i]:offs[i + 1]]
    q_a, k_a, v_a, g_a, z_a, qkv_b, a_b, b_b, z_b, gate_a, gate_b = (part(i) for i in range(11))
    pad = jnp.zeros((D_MODEL, LANES - GLA_GATE_RANK - 2 * GDN_HEADS), w_in.dtype)
    return jnp.concatenate([q_a, k_a, v_a, gate_a, gate_b, qkv_b, z_a, z_b, g_a, a_b, b_b, pad],
                           axis=1).astype(BF16)


def _head_param_col(v):
    return jnp.zeros((2 * GDN_HEADS, 1), F32).at[:GDN_HEADS, 0].set(v.astype(F32))


def _trunk(x, p, s_gla, s_gdn, conv_state, wts):
    bsz, t_len, _ = x.shape
    c = min(PROMPT_CHUNK, t_len)
    n = bsz * t_len
    x2d = x.reshape(n, D_MODEL)
    p2d = p.reshape(n, PLE_DIM)
    proj = _inproj(x2d, wts["norm_w"], wts["w_in_r"])
    proj3 = proj.reshape(bsz, t_len, P_COLS)
    s_gla2 = s_gla.reshape(bsz, GLA_HEADS * GLA_DK, GLA_DV)
    s_gdn2 = s_gdn.reshape(bsz, GDN_HEADS * GDN_DK, GDN_DV)
    if t_len % GROUP_ROWS == 0:
        o_a, gla_fin = _gla_prompt(proj3, s_gla2, wts["wgg"], wts["bgg"], c)
        o_b, gdn_fin, conv_new = _gdn_prompt(proj3, conv_state, s_gdn2, wts["conv_w"], wts["alog_v"],
                                             wts["dtb_v"], c)
    else:
        assert GROUP_ROWS % t_len == 0 and bsz % (GROUP_ROWS // t_len) == 0 and t_len >= CONV_WIDTH - 1
        o_a, gla_fin = _gla_sample(proj3, s_gla2, wts["wgg"], wts["bgg"])
        o_b, gdn_fin, conv_new = _gdn_sample(proj3, conv_state, s_gdn2, wts["conv_w"], wts["alog_v"],
                                             wts["dtb_v"])
    y = _out_stage(o_a.reshape(n, GLA_VAL), o_b.reshape(n, GDN_VAL), proj, x2d, p2d, wts["anw"], wts["bnw"],
                   wts["wua"], wts["wub"], wts["wout"], wts["wpg"], wts["wp"], wts["fnw"])
    return (y.reshape(bsz, t_len, D_MODEL),
            gla_fin.reshape(1, bsz, GLA_HEADS, GLA_DK, GLA_DV),
            gdn_fin.reshape(1, bsz, GDN_HEADS, GDN_DK, GDN_DV),
            conv_new.reshape(1, bsz, CONV_WIDTH - 1, CONV_CH))


def kernel(x_prompt, x_sample, state_gla, state_gdn, state_conv, p_prompt, p_sample, norm_w, w_in,
           w_gla_gate, b_gla_gate, gla_norm_w, conv_w, gdn_a_log, gdn_dt_bias, gdn_norm_w,
           w_up_gla, w_up_gdn, w_out, w_ple_gate, w_ple, final_norm_w):
    wgg = jnp.zeros((LANES, GLA_KEY), F32).at[SM_G:SM_G + GLA_GATE_RANK].set(w_gla_gate[0]).astype(BF16)
    wts = {
        "norm_w": norm_w[0].reshape(1, D_MODEL),
        "w_in_r": _regroup_w_in(w_in[0]),
        "wgg": wgg,
        "bgg": b_gla_gate[0].reshape(1, GLA_KEY),
        "conv_w": conv_w[0],
        "alog_v": _head_param_col(gdn_a_log[0]),
        "dtb_v": _head_param_col(gdn_dt_bias[0]),
        "anw": gla_norm_w[0].reshape(1, GLA_DV),
        "bnw": gdn_norm_w[0].reshape(1, GDN_DV),
        "wua": w_up_gla[0].astype(BF16),
        "wub": w_up_gdn[0].astype(BF16),
        "wout": w_out[0].astype(BF16),
        "wpg": w_ple_gate[0].astype(BF16),
        "wp": w_ple[0].astype(BF16),
        "fnw": final_norm_w.reshape(1, D_MODEL),
    }
    bsz = x_prompt.shape[0]
    dt = x_prompt.dtype
    y_p, gla_p, gdn_p, conv_p = _trunk(
        x_prompt, p_prompt[0],
        jnp.zeros((bsz, GLA_HEADS, GLA_DK, GLA_DV), dt), jnp.zeros((bsz, GDN_HEADS, GDN_DK, GDN_DV), dt),
        jnp.zeros((bsz, CONV_WIDTH - 1, CONV_CH), dt), wts)
    y_s, gla_s, gdn_s, conv_s = _trunk(x_sample, p_sample[0], state_gla[0], state_gdn[0], state_conv[0], wts)
    return (y_p, y_s, gla_p, gdn_p, conv_p, gla_s, gdn_s, conv_s)
```

```python
import functools

import jax
import jax.numpy as jnp
from jax import lax
from jax.experimental import pallas as pl
from jax.experimental.pallas import tpu as pltpu

F32 = jnp.float32
BF16 = jnp.bfloat16

D_MODEL = 1024
PLE_DIM = 256
NORM_EPS = 1e-6
GLA_HEADS = 4
GLA_DK = 64
GLA_DV = 128
GLA_KEY = GLA_HEADS * GLA_DK
GLA_VAL = GLA_HEADS * GLA_DV
GLA_GATE_RANK = 16
GLA_GATE_TEMP = 16.0
GDN_HEADS = 4
GDN_DK = 128
GDN_DV = 128
GDN_KEY = GDN_HEADS * GDN_DK
GDN_VAL = GDN_HEADS * GDN_DV
CONV_WIDTH = 4
CONV_CH = 2 * GDN_KEY + GDN_VAL
IN_SPLITS = (GLA_KEY, GLA_KEY, GLA_VAL, GLA_GATE_RANK, GLA_VAL, CONV_CH, GDN_HEADS, GDN_HEADS,
             GDN_VAL, D_MODEL, D_MODEL)

LANES = 128
SUBLANES = 8

GLA_QKV = 2 * GLA_KEY + GLA_VAL
P_GLA_QKV = 0
P_QKV_B = P_GLA_QKV + GLA_QKV
P_SMALL = P_QKV_B + CONV_CH
P_GATES = P_SMALL + LANES
GZ_COLS = 2 * D_MODEL + GLA_VAL + GDN_VAL
P_COLS = P_GATES + GZ_COLS
GZ_GATE_A = 0
GZ_GATE_B = D_MODEL
GZ_Z_A = 2 * D_MODEL
GZ_Z_B = 2 * D_MODEL + GLA_VAL
SM_G = 0
SM_A = GLA_GATE_RANK
SM_B = GLA_GATE_RANK + GDN_HEADS

PROMPT_CHUNK = 64
GROUP_ROWS = 128
BATCH_INTERLEAVE = 4
GDN_BATCH_INTERLEAVE = 2
ROW_TILE = 512
VMEM_LIMIT = 56 * 1024 * 1024


def _dot(a, b):
    return jnp.dot(a.astype(BF16), b.astype(BF16), preferred_element_type=F32)


def _dot_nt(a, b):
    return lax.dot_general(a.astype(BF16), b.astype(BF16), (((1,), (1,)), ((), ())),
                           preferred_element_type=F32)


def _split2(x):
    h1 = x.astype(BF16)
    return h1, (x - h1.astype(F32)).astype(BF16)


def _dot_f32(a, b):
    a1, a2 = _split2(a)
    b1, b2 = _split2(b)
    d = functools.partial(jnp.dot, preferred_element_type=F32)
    return d(a1, b1) + (d(a1, b2) + d(a2, b1))


def _cumsum_rows(tri, x):
    x1, x2 = _split2(x)
    d = functools.partial(jnp.dot, preferred_element_type=F32)
    return d(tri, x1) + d(tri, x2)


def _softplus(x):
    return jnp.maximum(x, 0.0) + jnp.log(1.0 + jnp.exp(-jnp.abs(x)))


def _sigmoid(x):
    return 1.0 / (1.0 + jnp.exp(-x))


def _silu(x):
    return x * _sigmoid(x)


def _interleave(emitters):
    results = [None] * len(emitters)
    live = list(range(len(emitters)))
    while live:
        for i in list(live):
            try:
                next(emitters[i])
            except StopIteration as stop:
                results[i] = stop.value
                live.remove(i)
    return results


def _block_masks(r, block):
    row = lax.broadcasted_iota(jnp.int32, (r, r), 0)
    col = lax.broadcasted_iota(jnp.int32, (r, r), 1)
    same = (row // block) == (col // block)
    return same & (row >= col), same & (row > col), same


INPROJ_COL_STEP = 512


def _inproj_kernel(x_ref, nw_ref, w_ref, gla_ref, gdn_ref, small_ref, gz_ref):
    x = x_ref[...]
    xn = x * lax.rsqrt(jnp.mean(x * x, axis=-1, keepdims=True) + NORM_EPS) * nw_ref[...]
    xb = xn.astype(BF16)

    def emit(o_ref, w0, width):
        for c0 in range(0, width, INPROJ_COL_STEP):
            c1 = min(c0 + INPROJ_COL_STEP, width)
            o_ref[:, c0:c1] = jnp.dot(xb, w_ref[:, w0 + c0:w0 + c1],
                                      preferred_element_type=F32).astype(o_ref.dtype)

    emit(gla_ref, P_GLA_QKV, GLA_QKV)
    emit(gdn_ref, P_QKV_B, CONV_CH)
    emit(small_ref, P_SMALL, LANES)
    emit(gz_ref, P_GATES, GZ_COLS)


def _inproj(x2d, norm_w, w_in_r):
    n = x2d.shape[0]
    tm = min(ROW_TILE, n)
    rows = lambda i: (i, 0)
    return pl.pallas_call(
        _inproj_kernel,
        grid=(n // tm,),
        in_specs=[
            pl.BlockSpec((tm, D_MODEL), rows),
            pl.BlockSpec((1, D_MODEL), lambda i: (0, 0)),
            pl.BlockSpec((D_MODEL, P_COLS), lambda i: (0, 0), pipeline_mode=pl.Buffered(1)),
        ],
        out_specs=[
            pl.BlockSpec((tm, GLA_QKV), rows),
            pl.BlockSpec((tm, CONV_CH), rows),
            pl.BlockSpec((tm, LANES), rows),
            pl.BlockSpec((tm, GZ_COLS), rows),
        ],
        out_shape=[
            jax.ShapeDtypeStruct((n, GLA_QKV), F32),
            jax.ShapeDtypeStruct((n, CONV_CH), F32),
            jax.ShapeDtypeStruct((n, LANES), F32),
            jax.ShapeDtypeStruct((n, GZ_COLS), BF16),
        ],
        compiler_params=pltpu.CompilerParams(dimension_semantics=("arbitrary",),
                                             vmem_limit_bytes=VMEM_LIMIT),
        name="inproj",
    )(x2d, norm_w, w_in_r)


GLA_PAIRS = GLA_HEADS // 2


def _gla_prepass(q, k, v, sm, wgg, bgg, same_le, same):
    r = q.shape[0]
    heads = range(GLA_HEADS)
    pairs = range(GLA_PAIRS)
    pre = _dot(sm, wgg) + bgg
    yield
    gk = (jnp.minimum(pre, 0.0) - jnp.log(1.0 + jnp.exp(-jnp.abs(pre)))) * (1.0 / GLA_GATE_TEMP)
    sums = _cumsum_rows(jnp.concatenate([jnp.where(same_le, 1.0, 0.0).astype(BF16),
                                         jnp.where(same, 1.0, 0.0).astype(BF16)], axis=0), gk)
    yield
    bcum = sums[:r]
    bend = sums[r:]
    q_e = q * jnp.exp(bcum) * (GLA_DK ** -0.5)
    k_e = k * jnp.exp(-bcum)
    lane = lax.broadcasted_iota(jnp.int32, (r, LANES), 1)
    in_head = [lane < GLA_DK, lane >= GLA_DK]
    pl_ = [slice(p * LANES, (p + 1) * LANES) for p in pairs]
    qm = [jnp.where(in_head[h % 2], q_e[:, pl_[h // 2]], 0.0) for h in heads]
    att = [jnp.where(same_le, _dot_nt(qm[h], k_e[:, pl_[h // 2]]), 0.0) for h in heads]
    yield
    o_intra = [_dot(att[h], v[h]) for h in heads]
    k_end = k * jnp.exp(bend - bcum)
    k_end_t = [k_end[:, pl_[p]].T for p in pairs]
    bend_t = [bend[:, pl_[p]].T for p in pairs]
    yield
    return qm, o_intra, k_end_t, bend_t


def _gla_prompt_kernel(qkv_ref, small_ref, s0_ref, wgg_ref, bgg_ref, o_ref, sfin_ref, s_scr, *, c):
    t = pl.program_id(0)
    nb, r, _ = qkv_ref.shape

    @pl.when(t == 0)
    def _():
        s_scr[...] = s0_ref[...]

    same_le, _, same = _block_masks(r, c)
    zeros = jnp.zeros((c, GLA_DV), F32)
    n_sub = r // c
    heads = range(GLA_HEADS)
    pairs = range(GLA_PAIRS)

    def one_batch(b):
        q = qkv_ref[b, :, 0:GLA_KEY]
        k = qkv_ref[b, :, GLA_KEY:2 * GLA_KEY]
        v = [qkv_ref[b, :, 2 * GLA_KEY + h * GLA_DV:2 * GLA_KEY + (h + 1) * GLA_DV] for h in heads]
        qm, o_intra, k_end_t, bend_t = yield from _gla_prepass(
            q, k, v, small_ref[b], wgg_ref[...], bgg_ref[...], same_le, same)
        s = [s_scr[b, p * LANES:(p + 1) * LANES, :] for p in pairs]
        for i in range(n_sub):
            rows = slice(i * c, (i + 1) * c)
            ws = [_dot(jnp.concatenate([qm[2 * p][rows], qm[2 * p + 1][rows]], axis=0), s[p]) for p in pairs]
            padded = [jnp.concatenate([zeros] * i + [v[h][rows]] + [zeros] * (n_sub - 1 - i), axis=0)
                      for h in heads]
            upd = [_dot(k_end_t[h // 2][(h % 2) * GLA_DK:(h % 2 + 1) * GLA_DK, :], padded[h]) for h in heads]
            yield
            for h in heads:
                o_ref[b, rows, h * GLA_DV:(h + 1) * GLA_DV] = (
                    o_intra[h][rows] + ws[h // 2][(h % 2) * c:(h % 2 + 1) * c]).astype(o_ref.dtype)
            s = [s[p] * jnp.exp(bend_t[p][:, i * c:i * c + 1])
                 + jnp.concatenate([upd[2 * p], upd[2 * p + 1]], axis=0) for p in pairs]
        for p in pairs:
            s_scr[b, p * LANES:(p + 1) * LANES, :] = s[p]

    def per_step(i, carry):
        _interleave([one_batch(i * BATCH_INTERLEAVE + j) for j in range(BATCH_INTERLEAVE)])
        return carry

    lax.fori_loop(0, nb // BATCH_INTERLEAVE, per_step, None)

    @pl.when(t == pl.num_programs(0) - 1)
    def _():
        sfin_ref[...] = s_scr[...]


def _gla_sample_kernel(qkv_ref, small_ref, s0_ref, wgg_ref, bgg_ref, o_ref, sfin_ref):
    ns, c, _ = qkv_ref.shape
    r = ns * c
    heads = range(GLA_HEADS)
    same_le, _, same = _block_masks(r, c)
    q = qkv_ref[:, :, 0:GLA_KEY].reshape(r, GLA_KEY)
    k = qkv_ref[:, :, GLA_KEY:2 * GLA_KEY].reshape(r, GLA_KEY)
    v = [qkv_ref[:, :, 2 * GLA_KEY + h * GLA_DV:2 * GLA_KEY + (h + 1) * GLA_DV].reshape(r, GLA_DV)
         for h in heads]
    (qm, o_intra, k_end_t, bend_t), = _interleave([_gla_prepass(
        q, k, v, small_ref[...].reshape(r, LANES), wgg_ref[...], bgg_ref[...], same_le, same)])
    seq_of_row = lax.broadcasted_iota(jnp.int32, (r, GLA_DV), 0) // c
    for p in range(GLA_PAIRS):
        ps = slice(p * LANES, (p + 1) * LANES)
        inter = [[], []]
        for s in range(ns):
            rows = slice(s * c, (s + 1) * c)
            ws = _dot(jnp.concatenate([qm[2 * p][rows], qm[2 * p + 1][rows]], axis=0), s0_ref[s, ps, :])
            inter[0].append(ws[:c])
            inter[1].append(ws[c:])
        for hh in range(2):
            h = 2 * p + hh
            o = o_intra[h] + jnp.concatenate(inter[hh], axis=0)
            o_ref[:, h * GLA_DV:(h + 1) * GLA_DV] = o.astype(o_ref.dtype)
        for s in range(ns):
            upd = [_dot(k_end_t[p][hh * GLA_DK:(hh + 1) * GLA_DK, :],
                        jnp.where(seq_of_row == s, v[2 * p + hh], 0.0)) for hh in range(2)]
            sfin_ref[s, ps, :] = (s0_ref[s, ps, :] * jnp.exp(bend_t[p][:, s * c:s * c + 1])
                                  + jnp.concatenate(upd, axis=0))


def _gla_prompt(qkv3, small3, s0, wgg, bgg, c):
    bsz, t_len, _ = qkv3.shape
    rows = GLA_HEADS * GLA_DK
    r = GROUP_ROWS
    full3 = lambda t: (0, 0, 0)
    const2 = lambda t: (0, 0)
    return pl.pallas_call(
        functools.partial(_gla_prompt_kernel, c=c),
        grid=(t_len // r,),
        in_specs=[
            pl.BlockSpec((bsz, r, GLA_QKV), lambda t: (0, t, 0)),
            pl.BlockSpec((bsz, r, LANES), lambda t: (0, t, 0)),
            pl.BlockSpec((bsz, rows, GLA_DV), full3),
            pl.BlockSpec((LANES, GLA_KEY), const2),
            pl.BlockSpec((1, GLA_KEY), const2),
        ],
        out_specs=[
            pl.BlockSpec((bsz, r, GLA_VAL), lambda t: (0, t, 0)),
            pl.BlockSpec((bsz, rows, GLA_DV), full3),
        ],
        out_shape=[
            jax.ShapeDtypeStruct((bsz, t_len, GLA_VAL), BF16),
            jax.ShapeDtypeStruct((bsz, rows, GLA_DV), F32),
        ],
        scratch_shapes=[pltpu.VMEM((bsz, rows, GLA_DV), F32)],
        compiler_params=pltpu.CompilerParams(dimension_semantics=("arbitrary",),
                                             vmem_limit_bytes=VMEM_LIMIT),
        name="gla_prompt",
    )(qkv3, small3, s0, wgg, bgg)


def _gla_sample(qkv3, small3, s0, wgg, bgg):
    bsz, c, _ = qkv3.shape
    rows = GLA_HEADS * GLA_DK
    ns = GROUP_ROWS // c
    grp = lambda g: (g, 0, 0)
    const2 = lambda g: (0, 0)
    return pl.pallas_call(
        _gla_sample_kernel,
        grid=(bsz // ns,),
        in_specs=[
            pl.BlockSpec((ns, c, GLA_QKV), grp),
            pl.BlockSpec((ns, c, LANES), grp),
            pl.BlockSpec((ns, rows, GLA_DV), grp),
            pl.BlockSpec((LANES, GLA_KEY), const2),
            pl.BlockSpec((1, GLA_KEY), const2),
        ],
        out_specs=[
            pl.BlockSpec((ns * c, GLA_VAL), lambda g: (g, 0)),
            pl.BlockSpec((ns, rows, GLA_DV), grp),
        ],
        out_shape=[
            jax.ShapeDtypeStruct((bsz * c, GLA_VAL), BF16),
            jax.ShapeDtypeStruct((bsz, rows, GLA_DV), F32),
        ],
        compiler_params=pltpu.CompilerParams(dimension_semantics=("arbitrary",),
                                             vmem_limit_bytes=VMEM_LIMIT),
        name="gla_sample",
    )(qkv3, small3, s0, wgg, bgg)


def _block_unit_lower_inverse(a_list, block):
    r = a_list[0].shape[0]
    row = lax.broadcasted_iota(jnp.int32, (r, r), 0)
    col = lax.broadcasted_iota(jnp.int32, (r, r), 1)
    eye = jnp.where(row == col, 1.0, 0.0)
    xs = [eye - a for a in a_list]
    ps = [_dot_f32(a, a) for a in a_list]
    n = 2
    d = functools.partial(jnp.dot, preferred_element_type=F32)
    while True:
        yield
        p_parts = [_split2(p) for p in ps]
        x_parts = [_split2(x) for x in xs]
        if 2 * n >= block:
            out = []
            for x, (p1, p2), (x1, x2) in zip(xs, p_parts, x_parts):
                xp = d(x1, jnp.concatenate([p1, p2], axis=1))
                out.append(x + (xp[:, :r] + (xp[:, r:] + d(x2, p1))))
            return out
        ms = [d(jnp.concatenate([p1, x1], axis=0), jnp.concatenate([p1, p2], axis=1))
              for (p1, p2), (x1, _) in zip(p_parts, x_parts)]
        qs = [d(jnp.concatenate([p2, x2], axis=0), p1)
              for (p1, p2), (_, x2) in zip(p_parts, x_parts)]
        ss = [m[:, :r] + (m[:, r:] + q) for m, q in zip(ms, qs)]
        ps = [s[:r] for s in ss]
        xs = [x + s[r:] for x, s in zip(xs, ss)]
        n *= 2


def _gdn_decays(sm, alog_c, dtb_c, same_lt, same):
    r = sm.shape[0]
    heads = range(GDN_HEADS)
    ab = sm.T[SM_A:SM_A + 2 * GDN_HEADS, :]
    g8 = -jnp.exp(alog_c) * _softplus(ab + dtb_c)
    beta_c = _sigmoid(ab).T
    g1, g2 = _split2(g8)
    same_ge = same & jnp.logical_not(same_lt)
    sel = jnp.concatenate([jnp.where(same_ge, 1.0, 0.0).astype(BF16),
                           jnp.where(same, 1.0, 0.0).astype(BF16)], axis=1)
    sums = jnp.dot(jnp.concatenate([g1, g2], axis=0), sel, preferred_element_type=F32)
    sums = sums[:2 * GDN_HEADS] + sums[2 * GDN_HEADS:]
    dec_rows = sums[:, :r]
    dec_c = dec_rows.T
    dend_c = sums[:, r:].T
    return ([dec_c[:, h:h + 1] for h in heads], [dec_rows[h:h + 1, :] for h in heads],
            [dend_c[:, h:h + 1] for h in heads], [beta_c[:, GDN_HEADS + h:GDN_HEADS + h + 1] for h in heads])


def _gdn_prepass(conv, sm, alog_c, dtb_c, same_le, same_lt, same, block):
    r = conv.shape[0]
    heads = range(GDN_HEADS)
    dcol, drow, dend, beta = _gdn_decays(sm, alog_c, dtb_c, same_lt, same)
    gamma = [jnp.where(same_le, jnp.exp(jnp.where(same_le, dcol[h] - drow[h], 0.0)), 0.0) for h in heads]
    q = [conv[:, h * GDN_DK:(h + 1) * GDN_DK] for h in heads]
    k = [conv[:, GDN_KEY + h * GDN_DK:GDN_KEY + (h + 1) * GDN_DK] for h in heads]
    v = [conv[:, 2 * GDN_KEY + h * GDN_DV:2 * GDN_KEY + (h + 1) * GDN_DV] for h in heads]
    q = [x * lax.rsqrt(jnp.sum(x * x, axis=-1, keepdims=True) + NORM_EPS) * (GDN_DK ** -0.5) for x in q]
    k = [x * lax.rsqrt(jnp.sum(x * x, axis=-1, keepdims=True) + NORM_EPS) for x in k]
    yield
    kb = [k[h] * beta[h] for h in heads]
    vb = [v[h] * beta[h] for h in heads]
    kq = [_dot_nt(jnp.concatenate([kb[h], q[h]], axis=0), k[h]) for h in heads]
    yield
    a_mat = [jnp.where(same_lt, kq[h][:r] * gamma[h], 0.0) for h in heads]
    qk = [kq[h][r:] * gamma[h] for h in heads]
    t_inv = yield from _block_unit_lower_inverse(a_mat, block)
    edec = [jnp.exp(dcol[h]) for h in heads]
    uw = [_dot(t_inv[h], jnp.concatenate([vb[h], kb[h] * edec[h]], axis=1)) for h in heads]
    q_e = [q[h] * edec[h] for h in heads]
    k_end_t = [(k[h] * jnp.exp(dend[h] - dcol[h])).T for h in heads]
    yield
    u = [x[:, :GDN_DV] for x in uw]
    w = [x[:, GDN_DV:] for x in uw]
    return u, w, q_e, qk, k_end_t, dend


def _conv_silu(win, cw_ref):
    conv = win(0) * cw_ref[CONV_WIDTH - 1:CONV_WIDTH, :]
    for j in range(1, CONV_WIDTH):
        conv = conv + win(j) * cw_ref[CONV_WIDTH - 1 - j:CONV_WIDTH - j, :]
    return _silu(conv)


def _gdn_prompt_kernel(qkv_ref, small_ref, cs_ref, s0_ref, cw_ref, alog_ref, dtb_ref,
                       o_ref, sfin_ref, cnew_ref, xbuf, prev, s_scr, *, c):
    t = pl.program_id(0)
    nb, r, _ = qkv_ref.shape
    keep = CONV_WIDTH - 1
    base = SUBLANES

    @pl.when(t == 0)
    def _():
        s_scr[...] = s0_ref[...]
        prev[:, base - keep:base, :] = cs_ref[...]

    same_le, same_lt, same = _block_masks(r, c)
    zeros = jnp.zeros((c, GDN_DV), F32)
    n_sub = r // c

    def one_batch(b, slot):
        xb = xbuf.at[slot]
        xb[base - keep:base, :] = prev[b, base - keep:base, :]
        xb[base:base + r, :] = qkv_ref[b]
        conv = _conv_silu(lambda j: xb[base - j:base - j + r, :], cw_ref)
        prev[b, base - keep:base, :] = xb[base + r - keep:base + r, :]
        yield
        u, w, q_e, qk, k_end_t, dend = yield from _gdn_prepass(
            conv, small_ref[b], alog_ref[...], dtb_ref[...], same_le, same_lt, same, c)
        heads = range(GDN_HEADS)
        hs = [slice(h * GDN_DK, (h + 1) * GDN_DK) for h in heads]
        s = [s_scr[b, hs[h], :] for h in heads]
        for i in range(n_sub):
            rows = slice(i * c, (i + 1) * c)
            ws = [_dot(jnp.concatenate([w[h][rows], q_e[h][rows]], axis=0), s[h]) for h in heads]
            yield
            v_new = [u[h][rows] - ws[h][:c] for h in heads]
            padded = [jnp.concatenate([zeros] * i + [v_new[h]] + [zeros] * (n_sub - 1 - i), axis=0)
                      for h in heads]
            upd = [_dot(jnp.concatenate([qk[h][rows], k_end_t[h]], axis=0), padded[h]) for h in heads]
            yield
            for h in heads:
                o_ref[b, rows, h * GDN_DV:(h + 1) * GDN_DV] = (ws[h][c:] + upd[h][:c]).astype(o_ref.dtype)
            s = [s[h] * jnp.exp(dend[h][i * c:i * c + 1, :]) + upd[h][c:] for h in heads]
        for h in heads:
            s_scr[b, hs[h], :] = s[h]

    def per_step(i, carry):
        _interleave([one_batch(i * GDN_BATCH_INTERLEAVE + j, j) for j in range(GDN_BATCH_INTERLEAVE)])
        return carry

    lax.fori_loop(0, nb // GDN_BATCH_INTERLEAVE, per_step, None)

    @pl.when(t == pl.num_programs(0) - 1)
    def _():
        sfin_ref[...] = s_scr[...]
        cnew_ref[...] = prev[:, base - keep:base, :]


def _gdn_sample_kernel(qkv_ref, small_ref, cs_ref, s0_ref, cw_ref, alog_ref, dtb_ref,
                       o_ref, sfin_ref, cnew_ref, xbuf):
    ns, c, _ = qkv_ref.shape
    r = ns * c
    keep = CONV_WIDTH - 1
    base = SUBLANES
    xbuf[:, base - keep:base, :] = cs_ref[...]
    xbuf[:, base:base + c, :] = qkv_ref[...]
    conv = _conv_silu(lambda j: xbuf[:, base - j:base - j + c, :], cw_ref).reshape(r, CONV_CH)
    cnew_ref[...] = xbuf[:, base + c - keep:base + c, :]

    same_le, same_lt, same = _block_masks(r, c)
    seq_of_row = lax.broadcasted_iota(jnp.int32, (r, GDN_DV), 0) // c
    (u, w, q_e, qk, k_end_t, dend), = _interleave([_gdn_prepass(
        conv, small_ref[...].reshape(r, LANES), alog_ref[...], dtb_ref[...], same_le, same_lt, same, c)])
    for h in range(GDN_HEADS):
        hs = slice(h * GDN_DK, (h + 1) * GDN_DK)
        v_parts, o_parts = [], []
        for s in range(ns):
            rows = slice(s * c, (s + 1) * c)
            ws = _dot(jnp.concatenate([w[h][rows], q_e[h][rows]], axis=0), s0_ref[s, hs, :])
            v_parts.append(u[h][rows] - ws[:c])
            o_parts.append(ws[c:])
        v_new = jnp.concatenate(v_parts, axis=0)
        o = jnp.concatenate(o_parts, axis=0) + _dot(qk[h], v_new)
        o_ref[:, h * GDN_DV:(h + 1) * GDN_DV] = o.astype(o_ref.dtype)
        for s in range(ns):
            upd = _dot(k_end_t[h], jnp.where(seq_of_row == s, v_new, 0.0))
            sfin_ref[s, hs, :] = s0_ref[s, hs, :] * jnp.exp(dend[h][s * c:s * c + 1, :]) + upd


def _gdn_prompt(qkv3, small3, conv_state, s0, conv_w, alog_v, dtb_v, c):
    bsz, t_len, _ = qkv3.shape
    rows = GDN_HEADS * GDN_DK
    keep = CONV_WIDTH - 1
    r = GROUP_ROWS
    full3 = lambda t: (0, 0, 0)
    const2 = lambda t: (0, 0)
    return pl.pallas_call(
        functools.partial(_gdn_prompt_kernel, c=c),
        grid=(t_len // r,),
        in_specs=[
            pl.BlockSpec((bsz, r, CONV_CH), lambda t: (0, t, 0)),
            pl.BlockSpec((bsz, r, LANES), lambda t: (0, t, 0)),
            pl.BlockSpec((bsz, keep, CONV_CH), full3),
            pl.BlockSpec((bsz, rows, GDN_DV), full3),
            pl.BlockSpec((CONV_WIDTH, CONV_CH), const2),
            pl.BlockSpec((2 * GDN_HEADS, 1), const2),
            pl.BlockSpec((2 * GDN_HEADS, 1), const2),
        ],
        out_specs=[
            pl.BlockSpec((bsz, r, GDN_VAL), lambda t: (0, t, 0)),
            pl.BlockSpec((bsz, rows, GDN_DV), full3),
            pl.BlockSpec((bsz, keep, CONV_CH), full3),
        ],
        out_shape=[
            jax.ShapeDtypeStruct((bsz, t_len, GDN_VAL), BF16),
            jax.ShapeDtypeStruct((bsz, rows, GDN_DV), F32),
            jax.ShapeDtypeStruct((bsz, keep, CONV_CH), F32),
        ],
        scratch_shapes=[pltpu.VMEM((GDN_BATCH_INTERLEAVE, SUBLANES + r, CONV_CH), F32),
                        pltpu.VMEM((bsz, SUBLANES, CONV_CH), F32),
                        pltpu.VMEM((bsz, rows, GDN_DV), F32)],
        compiler_params=pltpu.CompilerParams(dimension_semantics=("arbitrary",),
                                             vmem_limit_bytes=VMEM_LIMIT),
        name="gdn_prompt",
    )(qkv3, small3, conv_state, s0, conv_w, alog_v, dtb_v)


def _gdn_sample(qkv3, small3, conv_state, s0, conv_w, alog_v, dtb_v):
    bsz, c, _ = qkv3.shape
    rows = GDN_HEADS * GDN_DK
    keep = CONV_WIDTH - 1
    ns = GROUP_ROWS // c
    grp = lambda g: (g, 0, 0)
    const2 = lambda g: (0, 0)
    return pl.pallas_call(
        _gdn_sample_kernel,
        grid=(bsz // ns,),
        in_specs=[
            pl.BlockSpec((ns, c, CONV_CH), grp),
            pl.BlockSpec((ns, c, LANES), grp),
            pl.BlockSpec((ns, keep, CONV_CH), grp),
            pl.BlockSpec((ns, rows, GDN_DV), grp),
            pl.BlockSpec((CONV_WIDTH, CONV_CH), const2),
            pl.BlockSpec((2 * GDN_HEADS, 1), const2),
            pl.BlockSpec((2 * GDN_HEADS, 1), const2),
        ],
        out_specs=[
            pl.BlockSpec((ns * c, GDN_VAL), lambda g: (g, 0)),
            pl.BlockSpec((ns, rows, GDN_DV), grp),
            pl.BlockSpec((ns, keep, CONV_CH), grp),
        ],
        out_shape=[
            jax.ShapeDtypeStruct((bsz * c, GDN_VAL), BF16),
            jax.ShapeDtypeStruct((bsz, rows, GDN_DV), F32),
            jax.ShapeDtypeStruct((bsz, keep, CONV_CH), F32),
        ],
        scratch_shapes=[pltpu.VMEM((ns, 2 * SUBLANES, CONV_CH), F32)],
        compiler_params=pltpu.CompilerParams(dimension_semantics=("arbitrary",),
                                             vmem_limit_bytes=VMEM_LIMIT),
        name="gdn_sample",
    )(qkv3, small3, conv_state, s0, conv_w, alog_v, dtb_v)


def _head_norm_gate(o, z, w):
    parts = []
    for h in range(o.shape[-1] // LANES):
        oh = o[:, h * LANES:(h + 1) * LANES]
        parts.append(oh * lax.rsqrt(jnp.mean(oh * oh, axis=-1, keepdims=True) + NORM_EPS) * w)
    return jnp.concatenate(parts, axis=-1) * _silu(z)


def _out_kernel(oa_ref, ob_ref, ga_ref, gb_ref, za_ref, zb_ref, x_ref, p_ref, anw_ref, bnw_ref,
                wua_ref, wub_ref, wout_ref, wpg_ref, wp_ref, fnw_ref, y_ref):
    f32 = lambda ref: ref[...].astype(F32)
    ya = _dot(_head_norm_gate(f32(oa_ref), f32(za_ref), anw_ref[...]), wua_ref[...])
    yb = _dot(_head_norm_gate(f32(ob_ref), f32(zb_ref), bnw_ref[...]), wub_ref[...])
    merged = _sigmoid(f32(ga_ref)) * ya + _sigmoid(f32(gb_ref)) * yb
    h1 = x_ref[...] + _dot(merged, wout_ref[...])
    h2 = h1 + _sigmoid(_dot(h1, wpg_ref[...])) * _dot(p_ref[...], wp_ref[...])
    y_ref[...] = h2 * lax.rsqrt(jnp.mean(h2 * h2, axis=-1, keepdims=True) + NORM_EPS) * fnw_ref[...]


def _out_stage(o_a, o_b, gz, x2d, p2d, anw, bnw, wua, wub, wout, wpg, wp, fnw):
    n = x2d.shape[0]
    tm = min(ROW_TILE, n)
    const = lambda i: (0, 0)
    return pl.pallas_call(
        _out_kernel,
        grid=(n // tm,),
        in_specs=[
            pl.BlockSpec((tm, GLA_VAL), lambda i: (i, 0)),
            pl.BlockSpec((tm, GDN_VAL), lambda i: (i, 0)),
            pl.BlockSpec((tm, D_MODEL), lambda i: (i, GZ_GATE_A // D_MODEL)),
            pl.BlockSpec((tm, D_MODEL), lambda i: (i, GZ_GATE_B // D_MODEL)),
            pl.BlockSpec((tm, GLA_VAL), lambda i: (i, GZ_Z_A // GLA_VAL)),
            pl.BlockSpec((tm, GDN_VAL), lambda i: (i, GZ_Z_B // GDN_VAL)),
            pl.BlockSpec((tm, D_MODEL), lambda i: (i, 0)),
            pl.BlockSpec((tm, PLE_DIM), lambda i: (i, 0)),
            pl.BlockSpec((1, GLA_DV), const),
            pl.BlockSpec((1, GDN_DV), const),
            pl.BlockSpec((GLA_VAL, D_MODEL), const),
            pl.BlockSpec((GDN_VAL, D_MODEL), const),
            pl.BlockSpec((D_MODEL, D_MODEL), const),
            pl.BlockSpec((D_MODEL, D_MODEL), const),
            pl.BlockSpec((PLE_DIM, D_MODEL), const),
            pl.BlockSpec((1, D_MODEL), const),
        ],
        out_specs=pl.BlockSpec((tm, D_MODEL), lambda i: (i, 0)),
        out_shape=jax.ShapeDtypeStruct((n, D_MODEL), F32),
        compiler_params=pltpu.CompilerParams(dimension_semantics=("arbitrary",),
                                             vmem_limit_bytes=VMEM_LIMIT),
        name="out_stage",
    )(o_a, o_b, gz, gz, gz, gz, x2d, p2d, anw, bnw, wua, wub, wout, wpg, wp, fnw)


def _regroup_w_in(w_in):
    offs = [0]
    for s in IN_SPLITS:
        offs.append(offs[-1] + s)
    w_bf = w_in.astype(BF16)
    part = lambda i: w_bf[:, offs[i]:offs[i + 1]]
    q_a, k_a, v_a, g_a, z_a, qkv_b, a_b, b_b, z_b, gate_a, gate_b = (part(i) for i in range(11))
    pad = jnp.zeros((D_MODEL, LANES - GLA_GATE_RANK - 2 * GDN_HEADS), BF16)
    return jnp.concatenate([q_a, k_a, v_a, qkv_b, g_a, a_b, b_b, pad, gate_a, gate_b, z_a, z_b], axis=1)


def _head_param_col(v):
    return jnp.zeros((2 * GDN_HEADS, 1), F32).at[:GDN_HEADS, 0].set(v.astype(F32))


def _trunk(x, p, s_gla, s_gdn, conv_state, wts):
    bsz, t_len, _ = x.shape
    c = min(PROMPT_CHUNK, t_len)
    n = bsz * t_len
    x2d = x.reshape(n, D_MODEL)
    p2d = p.reshape(n, PLE_DIM)
    gla_in, gdn_in, small, gz = _inproj(x2d, wts["norm_w"], wts["w_in_r"])
    gla_in3 = gla_in.reshape(bsz, t_len, GLA_QKV)
    gdn_in3 = gdn_in.reshape(bsz, t_len, CONV_CH)
    small3 = small.reshape(bsz, t_len, LANES)
    s_gla2 = s_gla.reshape(bsz, GLA_HEADS * GLA_DK, GLA_DV)
    s_gdn2 = s_gdn.reshape(bsz, GDN_HEADS * GDN_DK, GDN_DV)
    if t_len % GROUP_ROWS == 0:
        o_a, gla_fin = _gla_prompt(gla_in3, small3, s_gla2, wts["wgg"], wts["bgg"], c)
        o_b, gdn_fin, conv_new = _gdn_prompt(gdn_in3, small3, conv_state, s_gdn2, wts["conv_w"],
                                             wts["alog_v"], wts["dtb_v"], c)
    else:
        assert GROUP_ROWS % t_len == 0 and bsz % (GROUP_ROWS // t_len) == 0 and t_len >= CONV_WIDTH - 1
        o_a, gla_fin = _gla_sample(gla_in3, small3, s_gla2, wts["wgg"], wts["bgg"])
        o_b, gdn_fin, conv_new = _gdn_sample(gdn_in3, small3, conv_state, s_gdn2, wts["conv_w"],
                                             wts["alog_v"], wts["dtb_v"])
    y = _out_stage(o_a.reshape(n, GLA_VAL), o_b.reshape(n, GDN_VAL), gz, x2d, p2d, wts["anw"], wts["bnw"],
                   wts["wua"], wts["wub"], wts["wout"], wts["wpg"], wts["wp"], wts["fnw"])
    return (y.reshape(bsz, t_len, D_MODEL),
            gla_fin.reshape(1, bsz, GLA_HEADS, GLA_DK, GLA_DV),
            gdn_fin.reshape(1, bsz, GDN_HEADS, GDN_DK, GDN_DV),
            conv_new.reshape(1, bsz, CONV_WIDTH - 1, CONV_CH))


def kernel(x_prompt, x_sample, state_gla, state_gdn, state_conv, p_prompt, p_sample, norm_w, w_in,
           w_gla_gate, b_gla_gate, gla_norm_w, conv_w, gdn_a_log, gdn_dt_bias, gdn_norm_w,
           w_up_gla, w_up_gdn, w_out, w_ple_gate, w_ple, final_norm_w):
    wgg = jnp.zeros((LANES, GLA_KEY), F32).at[SM_G:SM_G + GLA_GATE_RANK].set(w_gla_gate[0]).astype(BF16)
    wts = {
        "norm_w": norm_w[0].reshape(1, D_MODEL),
        "w_in_r": _regroup_w_in(w_in[0]),
        "wgg": wgg,
        "bgg": b_gla_gate[0].reshape(1, GLA_KEY),
        "conv_w": conv_w[0],
        "alog_v": _head_param_col(gdn_a_log[0]),
        "dtb_v": _head_param_col(gdn_dt_bias[0]),
        "anw": gla_norm_w[0].reshape(1, GLA_DV),
        "bnw": gdn_norm_w[0].reshape(1, GDN_DV),
        "wua": w_up_gla[0].astype(BF16),
        "wub": w_up_gdn[0].astype(BF16),
        "wout": w_out[0].astype(BF16),
        "wpg": w_ple_gate[0].astype(BF16),
        "wp": w_ple[0].astype(BF16),
        "fnw": final_norm_w.reshape(1, D_MODEL),
    }
    bsz = x_prompt.shape[0]
    dt = x_prompt.dtype
    y_p, gla_p, gdn_p, conv_p = _trunk(
        x_prompt, p_prompt[0],
        jnp.zeros((bsz, GLA_HEADS, GLA_DK, GLA_DV), dt), jnp.zeros((bsz, GDN_HEADS, GDN_DK, GDN_DV), dt),
        jnp.zeros((bsz, CONV_WIDTH - 1, CONV_CH), dt), wts)
    y_s, gla_s, gdn_s, conv_s = _trunk(x_sample, p_sample[0], state_gla[0], state_gdn[0], state_conv[0], wts)
    return (y_p, y_s, gla_p, gdn_p, conv_p, gla_s, gdn_s, conv_s)
```

```python
import functools

import jax
import jax.numpy as jnp
from jax import lax
from jax.experimental import pallas as pl
from jax.experimental.pallas import tpu as pltpu

F32 = jnp.float32
BF16 = jnp.bfloat16

D_MODEL = 1024
PLE_DIM = 256
NORM_EPS = 1e-6
GLA_HEADS = 4
GLA_DK = 64
GLA_DV = 128
GLA_KEY = GLA_HEADS * GLA_DK
GLA_VAL = GLA_HEADS * GLA_DV
GLA_GATE_RANK = 16
GLA_GATE_TEMP = 16.0
GDN_HEADS = 4
GDN_DK = 128
GDN_DV = 128
GDN_KEY = GDN_HEADS * GDN_DK
GDN_VAL = GDN_HEADS * GDN_DV
CONV_WIDTH = 4
CONV_CH = 2 * GDN_KEY + GDN_VAL
IN_SPLITS = (GLA_KEY, GLA_KEY, GLA_VAL, GLA_GATE_RANK, GLA_VAL, CONV_CH, GDN_HEADS, GDN_HEADS,
             GDN_VAL, D_MODEL, D_MODEL)

LANES = 128
SUBLANES = 8

GLA_QKV = 2 * GLA_KEY + GLA_VAL
P_GLA_QKV = 0
P_QKV_B = P_GLA_QKV + GLA_QKV
P_SMALL = P_QKV_B + CONV_CH
P_GATES = P_SMALL + LANES
GZ_COLS = 2 * D_MODEL + GLA_VAL + GDN_VAL
P_COLS = P_GATES + GZ_COLS
GZ_GATE_A = 0
GZ_GATE_B = D_MODEL
GZ_Z_A = 2 * D_MODEL
GZ_Z_B = 2 * D_MODEL + GLA_VAL
SM_G = 0
SM_A = GLA_GATE_RANK
SM_B = GLA_GATE_RANK + GDN_HEADS

PROMPT_CHUNK = 64
GROUP_ROWS = 128
BATCH_INTERLEAVE = 4
GDN_BATCH_INTERLEAVE = 2
INVERSE_F32_POWERS = 2
ROW_TILE = 512
VMEM_LIMIT = 56 * 1024 * 1024


def _dot(a, b):
    return jnp.dot(a.astype(BF16), b.astype(BF16), preferred_element_type=F32)


def _dot_nt(a, b):
    return lax.dot_general(a.astype(BF16), b.astype(BF16), (((1,), (1,)), ((), ())),
                           preferred_element_type=F32)


def _split2(x):
    h1 = x.astype(BF16)
    return h1, (x - h1.astype(F32)).astype(BF16)


def _dot_f32(a, b):
    a1, a2 = _split2(a)
    b1, b2 = _split2(b)
    d = functools.partial(jnp.dot, preferred_element_type=F32)
    return d(a1, b1) + (d(a1, b2) + d(a2, b1))


def _cumsum_rows(tri, x):
    x1, x2 = _split2(x)
    d = functools.partial(jnp.dot, preferred_element_type=F32)
    return d(tri, x1) + d(tri, x2)


def _softplus(x):
    return jnp.maximum(x, 0.0) + jnp.log(1.0 + jnp.exp(-jnp.abs(x)))


def _sigmoid(x):
    return 1.0 / (1.0 + jnp.exp(-x))


def _silu(x):
    return x * _sigmoid(x)


def _interleave(emitters):
    results = [None] * len(emitters)
    live = list(range(len(emitters)))
    while live:
        for i in list(live):
            try:
                next(emitters[i])
            except StopIteration as stop:
                results[i] = stop.value
                live.remove(i)
    return results


def _block_masks(r, block):
    row = lax.broadcasted_iota(jnp.int32, (r, r), 0)
    col = lax.broadcasted_iota(jnp.int32, (r, r), 1)
    same = (row // block) == (col // block)
    return same & (row >= col), same & (row > col), same


INPROJ_COL_STEP = 512


def _inproj_kernel(x_ref, nw_ref, w_ref, gla_ref, gdn_ref, small_ref, gz_ref):
    x = x_ref[...]
    xn = x * lax.rsqrt(jnp.mean(x * x, axis=-1, keepdims=True) + NORM_EPS) * nw_ref[...]
    xb = xn.astype(BF16)

    def emit(o_ref, w0, width):
        for c0 in range(0, width, INPROJ_COL_STEP):
            c1 = min(c0 + INPROJ_COL_STEP, width)
            o_ref[:, c0:c1] = jnp.dot(xb, w_ref[:, w0 + c0:w0 + c1],
                                      preferred_element_type=F32).astype(o_ref.dtype)

    emit(gla_ref, P_GLA_QKV, GLA_QKV)
    emit(gdn_ref, P_QKV_B, CONV_CH)
    emit(small_ref, P_SMALL, LANES)
    emit(gz_ref, P_GATES, GZ_COLS)


def _inproj(x2d, norm_w, w_in_r):
    n = x2d.shape[0]
    tm = min(ROW_TILE, n)
    rows = lambda i: (i, 0)
    return pl.pallas_call(
        _inproj_kernel,
        grid=(n // tm,),
        in_specs=[
            pl.BlockSpec((tm, D_MODEL), rows),
            pl.BlockSpec((1, D_MODEL), lambda i: (0, 0)),
            pl.BlockSpec((D_MODEL, P_COLS), lambda i: (0, 0), pipeline_mode=pl.Buffered(1)),
        ],
        out_specs=[
            pl.BlockSpec((tm, GLA_QKV), rows),
            pl.BlockSpec((tm, CONV_CH), rows),
            pl.BlockSpec((tm, LANES), rows),
            pl.BlockSpec((tm, GZ_COLS), rows),
        ],
        out_shape=[
            jax.ShapeDtypeStruct((n, GLA_QKV), F32),
            jax.ShapeDtypeStruct((n, CONV_CH), F32),
            jax.ShapeDtypeStruct((n, LANES), F32),
            jax.ShapeDtypeStruct((n, GZ_COLS), BF16),
        ],
        compiler_params=pltpu.CompilerParams(dimension_semantics=("arbitrary",),
                                             vmem_limit_bytes=VMEM_LIMIT),
        name="inproj",
    )(x2d, norm_w, w_in_r)


GLA_PAIRS = GLA_HEADS // 2


def _gla_prepass(q, k, v, sm, wgg, bgg, same_le, same):
    r = q.shape[0]
    heads = range(GLA_HEADS)
    pairs = range(GLA_PAIRS)
    pre = _dot(sm, wgg) + bgg
    yield
    gk = (jnp.minimum(pre, 0.0) - jnp.log(1.0 + jnp.exp(-jnp.abs(pre)))) * (1.0 / GLA_GATE_TEMP)
    sums = _cumsum_rows(jnp.concatenate([jnp.where(same_le, 1.0, 0.0).astype(BF16),
                                         jnp.where(same, 1.0, 0.0).astype(BF16)], axis=0), gk)
    yield
    bcum = sums[:r]
    bend = sums[r:]
    q_e = q * jnp.exp(bcum) * (GLA_DK ** -0.5)
    k_e = k * jnp.exp(-bcum)
    lane = lax.broadcasted_iota(jnp.int32, (r, LANES), 1)
    in_head = [lane < GLA_DK, lane >= GLA_DK]
    pl_ = [slice(p * LANES, (p + 1) * LANES) for p in pairs]
    qm = [jnp.where(in_head[h % 2], q_e[:, pl_[h // 2]], 0.0) for h in heads]
    att = [jnp.where(same_le, _dot_nt(qm[h], k_e[:, pl_[h // 2]]), 0.0) for h in heads]
    yield
    o_intra = [_dot(att[h], v[h]) for h in heads]
    k_end = k * jnp.exp(bend - bcum)
    k_end_t = [k_end[:, pl_[p]].T for p in pairs]
    bend_t = [bend[:, pl_[p]].T for p in pairs]
    yield
    return qm, o_intra, k_end_t, bend_t


def _gla_prompt_kernel(qkv_ref, small_ref, s0_ref, wgg_ref, bgg_ref, o_ref, sfin_ref, s_scr, *, c):
    t = pl.program_id(0)
    nb, r, _ = qkv_ref.shape

    @pl.when(t == 0)
    def _():
        s_scr[...] = s0_ref[...]

    same_le, _, same = _block_masks(r, c)
    zeros = jnp.zeros((c, GLA_DV), F32)
    n_sub = r // c
    heads = range(GLA_HEADS)
    pairs = range(GLA_PAIRS)

    def one_batch(b):
        q = qkv_ref[b, :, 0:GLA_KEY]
        k = qkv_ref[b, :, GLA_KEY:2 * GLA_KEY]
        v = [qkv_ref[b, :, 2 * GLA_KEY + h * GLA_DV:2 * GLA_KEY + (h + 1) * GLA_DV] for h in heads]
        qm, o_intra, k_end_t, bend_t = yield from _gla_prepass(
            q, k, v, small_ref[b], wgg_ref[...], bgg_ref[...], same_le, same)
        s = [s_scr[b, p * LANES:(p + 1) * LANES, :] for p in pairs]
        for i in range(n_sub):
            rows = slice(i * c, (i + 1) * c)
            ws = [_dot(jnp.concatenate([qm[2 * p][rows], qm[2 * p + 1][rows]], axis=0), s[p]) for p in pairs]
            padded = [jnp.concatenate([zeros] * i + [v[h][rows]] + [zeros] * (n_sub - 1 - i), axis=0)
                      for h in heads]
            upd = [_dot(k_end_t[h // 2][(h % 2) * GLA_DK:(h % 2 + 1) * GLA_DK, :], padded[h]) for h in heads]
            yield
            for h in heads:
                o_ref[b, rows, h * GLA_DV:(h + 1) * GLA_DV] = (
                    o_intra[h][rows] + ws[h // 2][(h % 2) * c:(h % 2 + 1) * c]).astype(o_ref.dtype)
            s = [s[p] * jnp.exp(bend_t[p][:, i * c:i * c + 1])
                 + jnp.concatenate([upd[2 * p], upd[2 * p + 1]], axis=0) for p in pairs]
        for p in pairs:
            s_scr[b, p * LANES:(p + 1) * LANES, :] = s[p]

    def per_step(i, carry):
        _interleave([one_batch(i * BATCH_INTERLEAVE + j) for j in range(BATCH_INTERLEAVE)])
        return carry

    lax.fori_loop(0, nb // BATCH_INTERLEAVE, per_step, None)

    @pl.when(t == pl.num_programs(0) - 1)
    def _():
        sfin_ref[...] = s_scr[...]


def _gla_sample_kernel(qkv_ref, small_ref, s0_ref, wgg_ref, bgg_ref, o_ref, sfin_ref):
    ns, c, _ = qkv_ref.shape
    r = ns * c
    heads = range(GLA_HEADS)
    same_le, _, same = _block_masks(r, c)
    q = qkv_ref[:, :, 0:GLA_KEY].reshape(r, GLA_KEY)
    k = qkv_ref[:, :, GLA_KEY:2 * GLA_KEY].reshape(r, GLA_KEY)
    v = [qkv_ref[:, :, 2 * GLA_KEY + h * GLA_DV:2 * GLA_KEY + (h + 1) * GLA_DV].reshape(r, GLA_DV)
         for h in heads]
    (qm, o_intra, k_end_t, bend_t), = _interleave([_gla_prepass(
        q, k, v, small_ref[...].reshape(r, LANES), wgg_ref[...], bgg_ref[...], same_le, same)])
    seq_of_row = lax.broadcasted_iota(jnp.int32, (r, GLA_DV), 0) // c
    for p in range(GLA_PAIRS):
        ps = slice(p * LANES, (p + 1) * LANES)
        inter = [[], []]
        for s in range(ns):
            rows = slice(s * c, (s + 1) * c)
            ws = _dot(jnp.concatenate([qm[2 * p][rows], qm[2 * p + 1][rows]], axis=0), s0_ref[s, ps, :])
            inter[0].append(ws[:c])
            inter[1].append(ws[c:])
        for hh in range(2):
            h = 2 * p + hh
            o = o_intra[h] + jnp.concatenate(inter[hh], axis=0)
            o_ref[:, h * GLA_DV:(h + 1) * GLA_DV] = o.astype(o_ref.dtype)
        for s in range(ns):
            upd = [_dot(k_end_t[p][hh * GLA_DK:(hh + 1) * GLA_DK, :],
                        jnp.where(seq_of_row == s, v[2 * p + hh], 0.0)) for hh in range(2)]
            sfin_ref[s, ps, :] = (s0_ref[s, ps, :] * jnp.exp(bend_t[p][:, s * c:s * c + 1])
                                  + jnp.concatenate(upd, axis=0))


def _gla_prompt(qkv3, small3, s0, wgg, bgg, c):
    bsz, t_len, _ = qkv3.shape
    rows = GLA_HEADS * GLA_DK
    r = GROUP_ROWS
    full3 = lambda t: (0, 0, 0)
    const2 = lambda t: (0, 0)
    return pl.pallas_call(
        functools.partial(_gla_prompt_kernel, c=c),
        grid=(t_len // r,),
        in_specs=[
            pl.BlockSpec((bsz, r, GLA_QKV), lambda t: (0, t, 0)),
            pl.BlockSpec((bsz, r, LANES), lambda t: (0, t, 0)),
            pl.BlockSpec((bsz, rows, GLA_DV), full3),
            pl.BlockSpec((LANES, GLA_KEY), const2),
            pl.BlockSpec((1, GLA_KEY), const2),
        ],
        out_specs=[
            pl.BlockSpec((bsz, r, GLA_VAL), lambda t: (0, t, 0)),
            pl.BlockSpec((bsz, rows, GLA_DV), full3),
        ],
        out_shape=[
            jax.ShapeDtypeStruct((bsz, t_len, GLA_VAL), BF16),
            jax.ShapeDtypeStruct((bsz, rows, GLA_DV), F32),
        ],
        scratch_shapes=[pltpu.VMEM((bsz, rows, GLA_DV), F32)],
        compiler_params=pltpu.CompilerParams(dimension_semantics=("arbitrary",),
                                             vmem_limit_bytes=VMEM_LIMIT),
        name="gla_prompt",
    )(qkv3, small3, s0, wgg, bgg)


def _gla_sample(qkv3, small3, s0, wgg, bgg):
    bsz, c, _ = qkv3.shape
    rows = GLA_HEADS * GLA_DK
    ns = GROUP_ROWS // c
    grp = lambda g: (g, 0, 0)
    const2 = lambda g: (0, 0)
    return pl.pallas_call(
        _gla_sample_kernel,
        grid=(bsz // ns,),
        in_specs=[
            pl.BlockSpec((ns, c, GLA_QKV), grp),
            pl.BlockSpec((ns, c, LANES), grp),
            pl.BlockSpec((ns, rows, GLA_DV), grp),
            pl.BlockSpec((LANES, GLA_KEY), const2),
            pl.BlockSpec((1, GLA_KEY), const2),
        ],
        out_specs=[
            pl.BlockSpec((ns * c, GLA_VAL), lambda g: (g, 0)),
            pl.BlockSpec((ns, rows, GLA_DV), grp),
        ],
        out_shape=[
            jax.ShapeDtypeStruct((bsz * c, GLA_VAL), BF16),
            jax.ShapeDtypeStruct((bsz, rows, GLA_DV), F32),
        ],
        compiler_params=pltpu.CompilerParams(dimension_semantics=("arbitrary",),
                                             vmem_limit_bytes=VMEM_LIMIT),
        name="gla_sample",
    )(qkv3, small3, s0, wgg, bgg)


def _block_unit_lower_inverse(a_list, block):
    r = a_list[0].shape[0]
    row = lax.broadcasted_iota(jnp.int32, (r, r), 0)
    col = lax.broadcasted_iota(jnp.int32, (r, r), 1)
    eye = jnp.where(row == col, 1.0, 0.0)
    xs = [eye - a for a in a_list]
    ps = [_dot_f32(a, a) if INVERSE_F32_POWERS > 2 else _dot(a, a) for a in a_list]
    n = 2
    d = functools.partial(jnp.dot, preferred_element_type=F32)
    while True:
        yield
        last = 2 * n >= block
        if n >= INVERSE_F32_POWERS:
            ms = [d(x.astype(BF16) if last else jnp.concatenate([p.astype(BF16), x.astype(BF16)], axis=0),
                    p.astype(BF16)) for p, x in zip(ps, xs)]
            if last:
                return [x + m for x, m in zip(xs, ms)]
            ps = [m[:r] for m in ms]
            xs = [x + m[r:] for x, m in zip(xs, ms)]
            n *= 2
            continue
        p_parts = [_split2(p) for p in ps]
        x_parts = [_split2(x) for x in xs]
        if last:
            out = []
            for x, (p1, p2), (x1, x2) in zip(xs, p_parts, x_parts):
                xp = d(x1, jnp.concatenate([p1, p2], axis=1))
                out.append(x + (xp[:, :r] + (xp[:, r:] + d(x2, p1))))
            return out
        ms = [d(jnp.concatenate([p1, x1], axis=0), jnp.concatenate([p1, p2], axis=1))
              for (p1, p2), (x1, _) in zip(p_parts, x_parts)]
        qs = [d(jnp.concatenate([p2, x2], axis=0), p1)
              for (p1, p2), (_, x2) in zip(p_parts, x_parts)]
        ss = [m[:, :r] + (m[:, r:] + q) for m, q in zip(ms, qs)]
        ps = [s[:r] for s in ss]
        xs = [x + s[r:] for x, s in zip(xs, ss)]
        n *= 2


def _gdn_decays(sm, alog_c, dtb_c, same_lt, same):
    r = sm.shape[0]
    heads = range(GDN_HEADS)
    ab = sm.T[SM_A:SM_A + 2 * GDN_HEADS, :]
    g8 = -jnp.exp(alog_c) * _softplus(ab + dtb_c)
    beta_c = _sigmoid(ab).T
    g1, g2 = _split2(g8)
    same_ge = same & jnp.logical_not(same_lt)
    sel = jnp.concatenate([jnp.where(same_ge, 1.0, 0.0).astype(BF16),
                           jnp.where(same, 1.0, 0.0).astype(BF16)], axis=1)
    sums = jnp.dot(jnp.concatenate([g1, g2], axis=0), sel, preferred_element_type=F32)
    sums = sums[:2 * GDN_HEADS] + sums[2 * GDN_HEADS:]
    dec_rows = sums[:, :r]
    dec_c = dec_rows.T
    dend_c = sums[:, r:].T
    return ([dec_c[:, h:h + 1] for h in heads], [dec_rows[h:h + 1, :] for h in heads],
            [dend_c[:, h:h + 1] for h in heads], [beta_c[:, GDN_HEADS + h:GDN_HEADS + h + 1] for h in heads])


def _gdn_prepass(conv, sm, alog_c, dtb_c, same_le, same_lt, same, block):
    r = conv.shape[0]
    heads = range(GDN_HEADS)
    dcol, drow, dend, beta = _gdn_decays(sm, alog_c, dtb_c, same_lt, same)
    gamma = [jnp.where(same_le, jnp.exp(jnp.where(same_le, dcol[h] - drow[h], 0.0)), 0.0) for h in heads]
    q = [conv[:, h * GDN_DK:(h + 1) * GDN_DK] for h in heads]
    k = [conv[:, GDN_KEY + h * GDN_DK:GDN_KEY + (h + 1) * GDN_DK] for h in heads]
    v = [conv[:, 2 * GDN_KEY + h * GDN_DV:2 * GDN_KEY + (h + 1) * GDN_DV] for h in heads]
    q = [x * lax.rsqrt(jnp.sum(x * x, axis=-1, keepdims=True) + NORM_EPS) * (GDN_DK ** -0.5) for x in q]
    k = [x * lax.rsqrt(jnp.sum(x * x, axis=-1, keepdims=True) + NORM_EPS) for x in k]
    yield
    kb = [k[h] * beta[h] for h in heads]
    vb = [v[h] * beta[h] for h in heads]
    kq = [_dot_nt(jnp.concatenate([kb[h], q[h]], axis=0), k[h]) for h in heads]
    yield
    a_mat = [jnp.where(same_lt, kq[h][:r] * gamma[h], 0.0) for h in heads]
    qk = [(kq[h][r:] * gamma[h]).astype(BF16) for h in heads]
    t_inv = yield from _block_unit_lower_inverse(a_mat, block)
    edec = [jnp.exp(dcol[h]) for h in heads]
    uw = [_dot(t_inv[h], jnp.concatenate([vb[h], kb[h] * edec[h]], axis=1)) for h in heads]
    q_e = [(q[h] * edec[h]).astype(BF16) for h in heads]
    k_end_t = [(k[h] * jnp.exp(dend[h] - dcol[h])).T.astype(BF16) for h in heads]
    yield
    u = [x[:, :GDN_DV] for x in uw]
    w = [x[:, GDN_DV:].astype(BF16) for x in uw]
    return u, w, q_e, qk, k_end_t, dend


def _conv_silu(win, cw_ref):
    conv = win(0) * cw_ref[CONV_WIDTH - 1:CONV_WIDTH, :]
    for j in range(1, CONV_WIDTH):
        conv = conv + win(j) * cw_ref[CONV_WIDTH - 1 - j:CONV_WIDTH - j, :]
    return _silu(conv)


def _gdn_prompt_kernel(qkv_ref, small_ref, cs_ref, s0_ref, cw_ref, alog_ref, dtb_ref,
                       o_ref, sfin_ref, cnew_ref, xbuf, prev, s_scr, *, c):
    t = pl.program_id(0)
    nb, r, _ = qkv_ref.shape
    keep = CONV_WIDTH - 1
    base = SUBLANES

    @pl.when(t == 0)
    def _():
        s_scr[...] = s0_ref[...]
        prev[:, base - keep:base, :] = cs_ref[...]

    same_le, same_lt, same = _block_masks(r, c)
    zeros = jnp.zeros((c, GDN_DV), F32)
    n_sub = r // c

    def one_batch(b, slot):
        xb = xbuf.at[slot]
        xb[base - keep:base, :] = prev[b, base - keep:base, :]
        xb[base:base + r, :] = qkv_ref[b]
        conv = _conv_silu(lambda j: xb[base - j:base - j + r, :], cw_ref)
        prev[b, base - keep:base, :] = xb[base + r - keep:base + r, :]
        yield
        u, w, q_e, qk, k_end_t, dend = yield from _gdn_prepass(
            conv, small_ref[b], alog_ref[...], dtb_ref[...], same_le, same_lt, same, c)
        heads = range(GDN_HEADS)
        hs = [slice(h * GDN_DK, (h + 1) * GDN_DK) for h in heads]
        s = [s_scr[b, hs[h], :] for h in heads]
        for i in range(n_sub):
            rows = slice(i * c, (i + 1) * c)
            ws = [_dot(jnp.concatenate([w[h][rows], q_e[h][rows]], axis=0), s[h]) for h in heads]
            yield
            v_new = [u[h][rows] - ws[h][:c] for h in heads]
            padded = [jnp.concatenate([zeros] * i + [v_new[h]] + [zeros] * (n_sub - 1 - i), axis=0)
                      for h in heads]
            upd = [_dot(jnp.concatenate([qk[h][rows], k_end_t[h]], axis=0), padded[h]) for h in heads]
            yield
            for h in heads:
                o_ref[b, rows, h * GDN_DV:(h + 1) * GDN_DV] = (ws[h][c:] + upd[h][:c]).astype(o_ref.dtype)
            s = [s[h] * jnp.exp(dend[h][i * c:i * c + 1, :]) + upd[h][c:] for h in heads]
        for h in heads:
            s_scr[b, hs[h], :] = s[h]

    def per_step(i, carry):
        _interleave([one_batch(i * GDN_BATCH_INTERLEAVE + j, j) for j in range(GDN_BATCH_INTERLEAVE)])
        return carry

    lax.fori_loop(0, nb // GDN_BATCH_INTERLEAVE, per_step, None)

    @pl.when(t == pl.num_programs(0) - 1)
    def _():
        sfin_ref[...] = s_scr[...]
        cnew_ref[...] = prev[:, base - keep:base, :]


def _gdn_sample_kernel(qkv_ref, small_ref, cs_ref, s0_ref, cw_ref, alog_ref, dtb_ref,
                       o_ref, sfin_ref, cnew_ref, xbuf):
    ns, c, _ = qkv_ref.shape
    r = ns * c
    keep = CONV_WIDTH - 1
    base = SUBLANES
    xbuf[:, base - keep:base, :] = cs_ref[...]
    xbuf[:, base:base + c, :] = qkv_ref[...]
    conv = _conv_silu(lambda j: xbuf[:, base - j:base - j + c, :], cw_ref).reshape(r, CONV_CH)
    cnew_ref[...] = xbuf[:, base + c - keep:base + c, :]

    same_le, same_lt, same = _block_masks(r, c)
    seq_of_row = lax.broadcasted_iota(jnp.int32, (r, GDN_DV), 0) // c
    (u, w, q_e, qk, k_end_t, dend), = _interleave([_gdn_prepass(
        conv, small_ref[...].reshape(r, LANES), alog_ref[...], dtb_ref[...], same_le, same_lt, same, c)])
    for h in range(GDN_HEADS):
        hs = slice(h * GDN_DK, (h + 1) * GDN_DK)
        v_parts, o_parts = [], []
        for s in range(ns):
            rows = slice(s * c, (s + 1) * c)
            ws = _dot(jnp.concatenate([w[h][rows], q_e[h][rows]], axis=0), s0_ref[s, hs, :])
            v_parts.append(u[h][rows] - ws[:c])
            o_parts.append(ws[c:])
        v_new = jnp.concatenate(v_parts, axis=0)
        o = jnp.concatenate(o_parts, axis=0) + _dot(qk[h], v_new)
        o_ref[:, h * GDN_DV:(h + 1) * GDN_DV] = o.astype(o_ref.dtype)
        for s in range(ns):
            upd = _dot(k_end_t[h], jnp.where(seq_of_row == s, v_new, 0.0))
            sfin_ref[s, hs, :] = s0_ref[s, hs, :] * jnp.exp(dend[h][s * c:s * c + 1, :]) + upd


def _gdn_prompt(qkv3, small3, conv_state, s0, conv_w, alog_v, dtb_v, c):
    bsz, t_len, _ = qkv3.shape
    rows = GDN_HEADS * GDN_DK
    keep = CONV_WIDTH - 1
    r = GROUP_ROWS
    full3 = lambda t: (0, 0, 0)
    const2 = lambda t: (0, 0)
    return pl.pallas_call(
        functools.partial(_gdn_prompt_kernel, c=c),
        grid=(t_len // r,),
        in_specs=[
            pl.BlockSpec((bsz, r, CONV_CH), lambda t: (0, t, 0)),
            pl.BlockSpec((bsz, r, LANES), lambda t: (0, t, 0)),
            pl.BlockSpec((bsz, keep, CONV_CH), full3),
            pl.BlockSpec((bsz, rows, GDN_DV), full3),
            pl.BlockSpec((CONV_WIDTH, CONV_CH), const2),
            pl.BlockSpec((2 * GDN_HEADS, 1), const2),
            pl.BlockSpec((2 * GDN_HEADS, 1), const2),
        ],
        out_specs=[
            pl.BlockSpec((bsz, r, GDN_VAL), lambda t: (0, t, 0)),
            pl.BlockSpec((bsz, rows, GDN_DV), full3),
            pl.BlockSpec((bsz, keep, CONV_CH), full3),
        ],
        out_shape=[
            jax.ShapeDtypeStruct((bsz, t_len, GDN_VAL), BF16),
            jax.ShapeDtypeStruct((bsz, rows, GDN_DV), F32),
            jax.ShapeDtypeStruct((bsz, keep, CONV_CH), F32),
        ],
        scratch_shapes=[pltpu.VMEM((GDN_BATCH_INTERLEAVE, SUBLANES + r, CONV_CH), F32),
                        pltpu.VMEM((bsz, SUBLANES, CONV_CH), F32),
                        pltpu.VMEM((bsz, rows, GDN_DV), F32)],
        compiler_params=pltpu.CompilerParams(dimension_semantics=("arbitrary",),
                                             vmem_limit_bytes=VMEM_LIMIT),
        name="gdn_prompt",
    )(qkv3, small3, conv_state, s0, conv_w, alog_v, dtb_v)


def _gdn_sample(qkv3, small3, conv_state, s0, conv_w, alog_v, dtb_v):
    bsz, c, _ = qkv3.shape
    rows = GDN_HEADS * GDN_DK
    keep = CONV_WIDTH - 1
    ns = GROUP_ROWS // c
    grp = lambda g: (g, 0, 0)
    const2 = lambda g: (0, 0)
    return pl.pallas_call(
        _gdn_sample_kernel,
        grid=(bsz // ns,),
        in_specs=[
            pl.BlockSpec((ns, c, CONV_CH), grp),
            pl.BlockSpec((ns, c, LANES), grp),
            pl.BlockSpec((ns, keep, CONV_CH), grp),
            pl.BlockSpec((ns, rows, GDN_DV), grp),
            pl.BlockSpec((CONV_WIDTH, CONV_CH), const2),
            pl.BlockSpec((2 * GDN_HEADS, 1), const2),
            pl.BlockSpec((2 * GDN_HEADS, 1), const2),
        ],
        out_specs=[
            pl.BlockSpec((ns * c, GDN_VAL), lambda g: (g, 0)),
            pl.BlockSpec((ns, rows, GDN_DV), grp),
            pl.BlockSpec((ns, keep, CONV_CH), grp),
        ],
        out_shape=[
            jax.ShapeDtypeStruct((bsz * c, GDN_VAL), BF16),
            jax.ShapeDtypeStruct((bsz, rows, GDN_DV), F32),
            jax.ShapeDtypeStruct((bsz, keep, CONV_CH), F32),
        ],
        scratch_shapes=[pltpu.VMEM((ns, 2 * SUBLANES, CONV_CH), F32)],
        compiler_params=pltpu.CompilerParams(dimension_semantics=("arbitrary",),
                                             vmem_limit_bytes=VMEM_LIMIT),
        name="gdn_sample",
    )(qkv3, small3, conv_state, s0, conv_w, alog_v, dtb_v)


def _head_norm_gate(o, z, w):
    parts = []
    for h in range(o.shape[-1] // LANES):
        oh = o[:, h * LANES:(h + 1) * LANES]
        parts.append(oh * lax.rsqrt(jnp.mean(oh * oh, axis=-1, keepdims=True) + NORM_EPS) * w)
    return jnp.concatenate(parts, axis=-1) * _silu(z)


def _out_kernel(oa_ref, ob_ref, ga_ref, gb_ref, za_ref, zb_ref, x_ref, p_ref, anw_ref, bnw_ref,
                wua_ref, wub_ref, wout_ref, wpg_ref, wp_ref, fnw_ref, y_ref):
    f32 = lambda ref: ref[...].astype(F32)
    ya = _dot(_head_norm_gate(f32(oa_ref), f32(za_ref), anw_ref[...]), wua_ref[...])
    yb = _dot(_head_norm_gate(f32(ob_ref), f32(zb_ref), bnw_ref[...]), wub_ref[...])
    merged = _sigmoid(f32(ga_ref)) * ya + _sigmoid(f32(gb_ref)) * yb
    h1 = x_ref[...] + _dot(merged, wout_ref[...])
    h2 = h1 + _sigmoid(_dot(h1, wpg_ref[...])) * _dot(p_ref[...], wp_ref[...])
    y_ref[...] = h2 * lax.rsqrt(jnp.mean(h2 * h2, axis=-1, keepdims=True) + NORM_EPS) * fnw_ref[...]


def _out_stage(o_a, o_b, gz, x2d, p2d, anw, bnw, wua, wub, wout, wpg, wp, fnw):
    n = x2d.shape[0]
    tm = min(ROW_TILE, n)
    const = lambda i: (0, 0)
    return pl.pallas_call(
        _out_kernel,
        grid=(n // tm,),
        in_specs=[
            pl.BlockSpec((tm, GLA_VAL), lambda i: (i, 0)),
            pl.BlockSpec((tm, GDN_VAL), lambda i: (i, 0)),
            pl.BlockSpec((tm, D_MODEL), lambda i: (i, GZ_GATE_A // D_MODEL)),
            pl.BlockSpec((tm, D_MODEL), lambda i: (i, GZ_GATE_B // D_MODEL)),
            pl.BlockSpec((tm, GLA_VAL), lambda i: (i, GZ_Z_A // GLA_VAL)),
            pl.BlockSpec((tm, GDN_VAL), lambda i: (i, GZ_Z_B // GDN_VAL)),
            pl.BlockSpec((tm, D_MODEL), lambda i: (i, 0)),
            pl.BlockSpec((tm, PLE_DIM), lambda i: (i, 0)),
            pl.BlockSpec((1, GLA_DV), const),
            pl.BlockSpec((1, GDN_DV), const),
            pl.BlockSpec((GLA_VAL, D_MODEL), const),
            pl.BlockSpec((GDN_VAL, D_MODEL), const),
            pl.BlockSpec((D_MODEL, D_MODEL), const),
            pl.BlockSpec((D_MODEL, D_MODEL), const),
            pl.BlockSpec((PLE_DIM, D_MODEL), const),
            pl.BlockSpec((1, D_MODEL), const),
        ],
        out_specs=pl.BlockSpec((tm, D_MODEL), lambda i: (i, 0)),
        out_shape=jax.ShapeDtypeStruct((n, D_MODEL), F32),
        compiler_params=pltpu.CompilerParams(dimension_semantics=("arbitrary",),
                                             vmem_limit_bytes=VMEM_LIMIT),
        name="out_stage",
    )(o_a, o_b, gz, gz, gz, gz, x2d, p2d, anw, bnw, wua, wub, wout, wpg, wp, fnw)


def _in_offsets():
    offs = [0]
    for s in IN_SPLITS:
        offs.append(offs[-1] + s)
    return offs


def _regroup_kernel(w_ref, o_ref):
    offs = _in_offsets()
    (q_a, _, _, g_a, z_a, qkv_b, a_b, _, z_b, gate_a, gate_b, end) = offs
    piece = lambda lo, hi: w_ref[:, lo:hi].astype(BF16)
    o_ref[:, P_GLA_QKV:P_GLA_QKV + GLA_QKV] = piece(q_a, g_a)
    o_ref[:, P_QKV_B:P_QKV_B + CONV_CH] = piece(qkv_b, a_b)
    assert g_a % LANES == SM_G and a_b % LANES == SM_A
    lane = lax.broadcasted_iota(jnp.int32, (w_ref.shape[0], LANES), 1)
    tile = lambda col: w_ref[:, col - col % LANES:col - col % LANES + LANES]
    o_ref[:, P_SMALL:P_SMALL + LANES] = jnp.where(lane < SM_A, tile(g_a), tile(a_b)).astype(BF16)
    o_ref[:, P_GATES + GZ_GATE_A:P_GATES + GZ_GATE_A + D_MODEL] = piece(gate_a, gate_b)
    o_ref[:, P_GATES + GZ_GATE_B:P_GATES + GZ_GATE_B + D_MODEL] = piece(gate_b, end)
    o_ref[:, P_GATES + GZ_Z_A:P_GATES + GZ_Z_A + GLA_VAL] = piece(z_a, qkv_b)
    o_ref[:, P_GATES + GZ_Z_B:P_GATES + GZ_Z_B + GDN_VAL] = piece(z_b, gate_a)


def _regroup_w_in(w_in):
    rows = 256
    return pl.pallas_call(
        _regroup_kernel,
        grid=(D_MODEL // rows,),
        in_specs=[pl.BlockSpec((None, rows, w_in.shape[2]), lambda i: (0, i, 0))],
        out_specs=pl.BlockSpec((rows, P_COLS), lambda i: (i, 0)),
        out_shape=jax.ShapeDtypeStruct((D_MODEL, P_COLS), BF16),
        compiler_params=pltpu.CompilerParams(dimension_semantics=("arbitrary",),
                                             vmem_limit_bytes=VMEM_LIMIT),
        name="regroup_w_in",
    )(w_in)


def _head_param_col(v):
    return jnp.zeros((2 * GDN_HEADS, 1), F32).at[:GDN_HEADS, 0].set(v.astype(F32))


def _trunk(x, p, s_gla, s_gdn, conv_state, wts):
    bsz, t_len, _ = x.shape
    c = min(PROMPT_CHUNK, t_len)
    n = bsz * t_len
    x2d = x.reshape(n, D_MODEL)
    p2d = p.reshape(n, PLE_DIM)
    gla_in, gdn_in, small, gz = _inproj(x2d, wts["norm_w"], wts["w_in_r"])
    gla_in3 = gla_in.reshape(bsz, t_len, GLA_QKV)
    gdn_in3 = gdn_in.reshape(bsz, t_len, CONV_CH)
    small3 = small.reshape(bsz, t_len, LANES)
    s_gla2 = s_gla.reshape(bsz, GLA_HEADS * GLA_DK, GLA_DV)
    s_gdn2 = s_gdn.reshape(bsz, GDN_HEADS * GDN_DK, GDN_DV)
    if t_len % GROUP_ROWS == 0:
        o_a, gla_fin = _gla_prompt(gla_in3, small3, s_gla2, wts["wgg"], wts["bgg"], c)
        o_b, gdn_fin, conv_new = _gdn_prompt(gdn_in3, small3, conv_state, s_gdn2, wts["conv_w"],
                                             wts["alog_v"], wts["dtb_v"], c)
    else:
        assert GROUP_ROWS % t_len == 0 and bsz % (GROUP_ROWS // t_len) == 0 and t_len >= CONV_WIDTH - 1
        o_a, gla_fin = _gla_sample(gla_in3, small3, s_gla2, wts["wgg"], wts["bgg"])
        o_b, gdn_fin, conv_new = _gdn_sample(gdn_in3, small3, conv_state, s_gdn2, wts["conv_w"],
                                             wts["alog_v"], wts["dtb_v"])
    y = _out_stage(o_a.reshape(n, GLA_VAL), o_b.reshape(n, GDN_VAL), gz, x2d, p2d, wts["anw"], wts["bnw"],
                   wts["wua"], wts["wub"], wts["wout"], wts["wpg"], wts["wp"], wts["fnw"])
    return (y.reshape(bsz, t_len, D_MODEL),
            gla_fin.reshape(1, bsz, GLA_HEADS, GLA_DK, GLA_DV),
            gdn_fin.reshape(1, bsz, GDN_HEADS, GDN_DK, GDN_DV),
            conv_new.reshape(1, bsz, CONV_WIDTH - 1, CONV_CH))


def kernel(x_prompt, x_sample, state_gla, state_gdn, state_conv, p_prompt, p_sample, norm_w, w_in,
           w_gla_gate, b_gla_gate, gla_norm_w, conv_w, gdn_a_log, gdn_dt_bias, gdn_norm_w,
           w_up_gla, w_up_gdn, w_out, w_ple_gate, w_ple, final_norm_w):
    wgg = jnp.zeros((LANES, GLA_KEY), F32).at[SM_G:SM_G + GLA_GATE_RANK].set(w_gla_gate[0]).astype(BF16)
    wts = {
        "norm_w": norm_w[0].reshape(1, D_MODEL),
        "w_in_r": _regroup_w_in(w_in),
        "wgg": wgg,
        "bgg": b_gla_gate[0].reshape(1, GLA_KEY),
        "conv_w": conv_w[0],
        "alog_v": _head_param_col(gdn_a_log[0]),
        "dtb_v": _head_param_col(gdn_dt_bias[0]),
        "anw": gla_norm_w[0].reshape(1, GLA_DV),
        "bnw": gdn_norm_w[0].reshape(1, GDN_DV),
        "wua": w_up_gla[0].astype(BF16),
        "wub": w_up_gdn[0].astype(BF16),
        "wout": w_out[0].astype(BF16),
        "wpg": w_ple_gate[0].astype(BF16),
        "wp": w_ple[0].astype(BF16),
        "fnw": final_norm_w.reshape(1, D_MODEL),
    }
    bsz = x_prompt.shape[0]
    dt = x_prompt.dtype
    y_p, gla_p, gdn_p, conv_p = _trunk(
        x_prompt, p_prompt[0],
        jnp.zeros((bsz, GLA_HEADS, GLA_DK, GLA_DV), dt), jnp.zeros((bsz, GDN_HEADS, GDN_DK, GDN_DV), dt),
        jnp.zeros((bsz, CONV_WIDTH - 1, CONV_CH), dt), wts)
    y_s, gla_s, gdn_s, conv_s = _trunk(x_sample, p_sample[0], state_gla[0], state_gdn[0], state_conv[0], wts)
    return (y_p, y_s, gla_p, gdn_p, conv_p, gla_s, gdn_s, conv_s)
```

```python
import functools

import jax
import jax.numpy as jnp
from jax import lax
from jax.experimental import pallas as pl
from jax.experimental.pallas import tpu as pltpu

F32 = jnp.float32
BF16 = jnp.bfloat16

D_MODEL = 1024
PLE_DIM = 256
NORM_EPS = 1e-6
GLA_HEADS = 4
GLA_DK = 64
GLA_DV = 128
GLA_KEY = GLA_HEADS * GLA_DK
GLA_VAL = GLA_HEADS * GLA_DV
GLA_GATE_RANK = 16
GLA_GATE_TEMP = 16.0
GDN_HEADS = 4
GDN_DK = 128
GDN_DV = 128
GDN_KEY = GDN_HEADS * GDN_DK
GDN_VAL = GDN_HEADS * GDN_DV
CONV_WIDTH = 4
CONV_CH = 2 * GDN_KEY + GDN_VAL
IN_SPLITS = (GLA_KEY, GLA_KEY, GLA_VAL, GLA_GATE_RANK, GLA_VAL, CONV_CH, GDN_HEADS, GDN_HEADS,
             GDN_VAL, D_MODEL, D_MODEL)

LANES = 128
SUBLANES = 8

GLA_QKV = 2 * GLA_KEY + GLA_VAL
P_GLA_QKV = 0
P_QKV_B = P_GLA_QKV + GLA_QKV
P_SMALL = P_QKV_B + CONV_CH
P_GATES = P_SMALL + LANES
GZ_COLS = 2 * D_MODEL + GLA_VAL + GDN_VAL
P_COLS = P_GATES + GZ_COLS
GZ_GATE_A = 0
GZ_GATE_B = D_MODEL
GZ_Z_A = 2 * D_MODEL
GZ_Z_B = 2 * D_MODEL + GLA_VAL
SM_G = 0
SM_A = GLA_GATE_RANK
SM_B = GLA_GATE_RANK + GDN_HEADS

PROMPT_CHUNK = 64
GROUP_ROWS = 128
BATCH_INTERLEAVE = 4
GDN_BATCH_INTERLEAVE = 2
INVERSE_F32_POWERS = 2
ROW_TILE = 512
VMEM_LIMIT = 56 * 1024 * 1024


def _dot(a, b):
    return jnp.dot(a.astype(BF16), b.astype(BF16), preferred_element_type=F32)


def _dot_nt(a, b):
    return lax.dot_general(a.astype(BF16), b.astype(BF16), (((1,), (1,)), ((), ())),
                           preferred_element_type=F32)


def _split2(x):
    h1 = x.astype(BF16)
    return h1, (x - h1.astype(F32)).astype(BF16)


def _dot_f32(a, b):
    a1, a2 = _split2(a)
    b1, b2 = _split2(b)
    d = functools.partial(jnp.dot, preferred_element_type=F32)
    return d(a1, b1) + (d(a1, b2) + d(a2, b1))


def _cumsum_rows(tri, x):
    x1, x2 = _split2(x)
    d = functools.partial(jnp.dot, preferred_element_type=F32)
    return d(tri, x1) + d(tri, x2)


def _softplus(x):
    return jnp.maximum(x, 0.0) + jnp.log(1.0 + jnp.exp(-jnp.abs(x)))


def _sigmoid(x):
    return 1.0 / (1.0 + jnp.exp(-x))


def _silu(x):
    return x * _sigmoid(x)


def _interleave(emitters):
    results = [None] * len(emitters)
    live = list(range(len(emitters)))
    while live:
        for i in list(live):
            try:
                next(emitters[i])
            except StopIteration as stop:
                results[i] = stop.value
                live.remove(i)
    return results


def _block_masks(r, block):
    row = lax.broadcasted_iota(jnp.int32, (r, r), 0)
    col = lax.broadcasted_iota(jnp.int32, (r, r), 1)
    same = (row // block) == (col // block)
    return same & (row >= col), same & (row > col), same


INPROJ_COL_STEP = 512


def _inproj_kernel(x_ref, nw_ref, w_ref, gla_ref, gdn_ref, small_ref, gz_ref):
    x = x_ref[...]
    xn = x * lax.rsqrt(jnp.mean(x * x, axis=-1, keepdims=True) + NORM_EPS) * nw_ref[...]
    xb = xn.astype(BF16)

    def emit(o_ref, w0, width):
        for c0 in range(0, width, INPROJ_COL_STEP):
            c1 = min(c0 + INPROJ_COL_STEP, width)
            o_ref[:, c0:c1] = _dot_nt(xb, w_ref[w0 + c0:w0 + c1, :]).astype(o_ref.dtype)

    emit(gla_ref, P_GLA_QKV, GLA_QKV)
    emit(gdn_ref, P_QKV_B, CONV_CH)
    emit(small_ref, P_SMALL, LANES)
    emit(gz_ref, P_GATES, GZ_COLS)


def _inproj(x2d, norm_w, w_in_r):
    n = x2d.shape[0]
    tm = min(ROW_TILE, n)
    rows = lambda i: (i, 0)
    return pl.pallas_call(
        _inproj_kernel,
        grid=(n // tm,),
        in_specs=[
            pl.BlockSpec((tm, D_MODEL), rows),
            pl.BlockSpec((1, D_MODEL), lambda i: (0, 0)),
            pl.BlockSpec((P_COLS, D_MODEL), lambda i: (0, 0), pipeline_mode=pl.Buffered(1)),
        ],
        out_specs=[
            pl.BlockSpec((tm, GLA_QKV), rows),
            pl.BlockSpec((tm, CONV_CH), rows),
            pl.BlockSpec((tm, LANES), rows),
            pl.BlockSpec((tm, GZ_COLS), rows),
        ],
        out_shape=[
            jax.ShapeDtypeStruct((n, GLA_QKV), F32),
            jax.ShapeDtypeStruct((n, CONV_CH), F32),
            jax.ShapeDtypeStruct((n, LANES), F32),
            jax.ShapeDtypeStruct((n, GZ_COLS), BF16),
        ],
        compiler_params=pltpu.CompilerParams(dimension_semantics=("arbitrary",),
                                             vmem_limit_bytes=VMEM_LIMIT),
        name="inproj",
    )(x2d, norm_w, w_in_r)


GLA_PAIRS = GLA_HEADS // 2


def _gla_prepass(q, k, v, sm, wgg, bgg, same_le, same):
    r = q.shape[0]
    heads = range(GLA_HEADS)
    pairs = range(GLA_PAIRS)
    pre = _dot(sm, wgg) + bgg
    yield
    gk = (jnp.minimum(pre, 0.0) - jnp.log(1.0 + jnp.exp(-jnp.abs(pre)))) * (1.0 / GLA_GATE_TEMP)
    sums = _cumsum_rows(jnp.concatenate([jnp.where(same_le, 1.0, 0.0).astype(BF16),
                                         jnp.where(same, 1.0, 0.0).astype(BF16)], axis=0), gk)
    yield
    bcum = sums[:r]
    bend = sums[r:]
    q_e = q * jnp.exp(bcum) * (GLA_DK ** -0.5)
    k_e = k * jnp.exp(-bcum)
    lane = lax.broadcasted_iota(jnp.int32, (r, LANES), 1)
    in_head = [lane < GLA_DK, lane >= GLA_DK]
    pl_ = [slice(p * LANES, (p + 1) * LANES) for p in pairs]
    qm = [jnp.where(in_head[h % 2], q_e[:, pl_[h // 2]], 0.0) for h in heads]
    att = [jnp.where(same_le, _dot_nt(qm[h], k_e[:, pl_[h // 2]]), 0.0) for h in heads]
    yield
    o_intra = [_dot(att[h], v[h]) for h in heads]
    k_end = k * jnp.exp(bend - bcum)
    k_end_t = [k_end[:, pl_[p]].T for p in pairs]
    bend_t = [bend[:, pl_[p]].T for p in pairs]
    yield
    return qm, o_intra, k_end_t, bend_t


def _gla_prompt_kernel(qkv_ref, small_ref, s0_ref, wgg_ref, bgg_ref, o_ref, sfin_ref, s_scr, *, c):
    t = pl.program_id(0)
    nb, r, _ = qkv_ref.shape

    @pl.when(t == 0)
    def _():
        s_scr[...] = s0_ref[...]

    same_le, _, same = _block_masks(r, c)
    zeros = jnp.zeros((c, GLA_DV), F32)
    n_sub = r // c
    heads = range(GLA_HEADS)
    pairs = range(GLA_PAIRS)

    def one_batch(b):
        q = qkv_ref[b, :, 0:GLA_KEY]
        k = qkv_ref[b, :, GLA_KEY:2 * GLA_KEY]
        v = [qkv_ref[b, :, 2 * GLA_KEY + h * GLA_DV:2 * GLA_KEY + (h + 1) * GLA_DV] for h in heads]
        qm, o_intra, k_end_t, bend_t = yield from _gla_prepass(
            q, k, v, small_ref[b], wgg_ref[...], bgg_ref[...], same_le, same)
        s = [s_scr[b, p * LANES:(p + 1) * LANES, :] for p in pairs]
        for i in range(n_sub):
            rows = slice(i * c, (i + 1) * c)
            ws = [_dot(jnp.concatenate([qm[2 * p][rows], qm[2 * p + 1][rows]], axis=0), s[p]) for p in pairs]
            padded = [jnp.concatenate([zeros] * i + [v[h][rows]] + [zeros] * (n_sub - 1 - i), axis=0)
                      for h in heads]
            upd = [_dot(k_end_t[h // 2][(h % 2) * GLA_DK:(h % 2 + 1) * GLA_DK, :], padded[h]) for h in heads]
            yield
            for h in heads:
                o_ref[b, rows, h * GLA_DV:(h + 1) * GLA_DV] = (
                    o_intra[h][rows] + ws[h // 2][(h % 2) * c:(h % 2 + 1) * c]).astype(o_ref.dtype)
            s = [s[p] * jnp.exp(bend_t[p][:, i * c:i * c + 1])
                 + jnp.concatenate([upd[2 * p], upd[2 * p + 1]], axis=0) for p in pairs]
        for p in pairs:
            s_scr[b, p * LANES:(p + 1) * LANES, :] = s[p]

    def per_step(i, carry):
        _interleave([one_batch(i * BATCH_INTERLEAVE + j) for j in range(BATCH_INTERLEAVE)])
        return carry

    lax.fori_loop(0, nb // BATCH_INTERLEAVE, per_step, None)

    @pl.when(t == pl.num_programs(0) - 1)
    def _():
        sfin_ref[...] = s_scr[...]


def _gla_sample_kernel(qkv_ref, small_ref, s0_ref, wgg_ref, bgg_ref, o_ref, sfin_ref):
    ns, c, _ = qkv_ref.shape
    r = ns * c
    heads = range(GLA_HEADS)
    same_le, _, same = _block_masks(r, c)
    q = qkv_ref[:, :, 0:GLA_KEY].reshape(r, GLA_KEY)
    k = qkv_ref[:, :, GLA_KEY:2 * GLA_KEY].reshape(r, GLA_KEY)
    v = [qkv_ref[:, :, 2 * GLA_KEY + h * GLA_DV:2 * GLA_KEY + (h + 1) * GLA_DV].reshape(r, GLA_DV)
         for h in heads]
    (qm, o_intra, k_end_t, bend_t), = _interleave([_gla_prepass(
        q, k, v, small_ref[...].reshape(r, LANES), wgg_ref[...], bgg_ref[...], same_le, same)])
    seq_of_row = lax.broadcasted_iota(jnp.int32, (r, GLA_DV), 0) // c
    for p in range(GLA_PAIRS):
        ps = slice(p * LANES, (p + 1) * LANES)
        inter = [[], []]
        for s in range(ns):
            rows = slice(s * c, (s + 1) * c)
            ws = _dot(jnp.concatenate([qm[2 * p][rows], qm[2 * p + 1][rows]], axis=0), s0_ref[s, ps, :])
            inter[0].append(ws[:c])
            inter[1].append(ws[c:])
        for hh in range(2):
            h = 2 * p + hh
            o = o_intra[h] + jnp.concatenate(inter[hh], axis=0)
            o_ref[:, h * GLA_DV:(h + 1) * GLA_DV] = o.astype(o_ref.dtype)
        for s in range(ns):
            upd = [_dot(k_end_t[p][hh * GLA_DK:(hh + 1) * GLA_DK, :],
                        jnp.where(seq_of_row == s, v[2 * p + hh], 0.0)) for hh in range(2)]
            sfin_ref[s, ps, :] = (s0_ref[s, ps, :] * jnp.exp(bend_t[p][:, s * c:s * c + 1])
                                  + jnp.concatenate(upd, axis=0))


def _gla_prompt(qkv3, small3, s0, wgg, bgg, c):
    bsz, t_len, _ = qkv3.shape
    rows = GLA_HEADS * GLA_DK
    r = GROUP_ROWS
    full3 = lambda t: (0, 0, 0)
    const2 = lambda t: (0, 0)
    return pl.pallas_call(
        functools.partial(_gla_prompt_kernel, c=c),
        grid=(t_len // r,),
        in_specs=[
            pl.BlockSpec((bsz, r, GLA_QKV), lambda t: (0, t, 0)),
            pl.BlockSpec((bsz, r, LANES), lambda t: (0, t, 0)),
            pl.BlockSpec((bsz, rows, GLA_DV), full3),
            pl.BlockSpec((LANES, GLA_KEY), const2),
            pl.BlockSpec((1, GLA_KEY), const2),
        ],
        out_specs=[
            pl.BlockSpec((bsz, r, GLA_VAL), lambda t: (0, t, 0)),
            pl.BlockSpec((bsz, rows, GLA_DV), full3),
        ],
        out_shape=[
            jax.ShapeDtypeStruct((bsz, t_len, GLA_VAL), BF16),
            jax.ShapeDtypeStruct((bsz, rows, GLA_DV), F32),
        ],
        scratch_shapes=[pltpu.VMEM((bsz, rows, GLA_DV), F32)],
        compiler_params=pltpu.CompilerParams(dimension_semantics=("arbitrary",),
                                             vmem_limit_bytes=VMEM_LIMIT),
        name="gla_prompt",
    )(qkv3, small3, s0, wgg, bgg)


def _gla_sample(qkv3, small3, s0, wgg, bgg):
    bsz, c, _ = qkv3.shape
    rows = GLA_HEADS * GLA_DK
    ns = GROUP_ROWS // c
    grp = lambda g: (g, 0, 0)
    const2 = lambda g: (0, 0)
    return pl.pallas_call(
        _gla_sample_kernel,
        grid=(bsz // ns,),
        in_specs=[
            pl.BlockSpec((ns, c, GLA_QKV), grp),
            pl.BlockSpec((ns, c, LANES), grp),
            pl.BlockSpec((ns, rows, GLA_DV), grp),
            pl.BlockSpec((LANES, GLA_KEY), const2),
            pl.BlockSpec((1, GLA_KEY), const2),
        ],
        out_specs=[
            pl.BlockSpec((ns * c, GLA_VAL), lambda g: (g, 0)),
            pl.BlockSpec((ns, rows, GLA_DV), grp),
        ],
        out_shape=[
            jax.ShapeDtypeStruct((bsz * c, GLA_VAL), BF16),
            jax.ShapeDtypeStruct((bsz, rows, GLA_DV), F32),
        ],
        compiler_params=pltpu.CompilerParams(dimension_semantics=("arbitrary",),
                                             vmem_limit_bytes=VMEM_LIMIT),
        name="gla_sample",
    )(qkv3, small3, s0, wgg, bgg)


def _block_unit_lower_inverse(a_list, block):
    r = a_list[0].shape[0]
    row = lax.broadcasted_iota(jnp.int32, (r, r), 0)
    col = lax.broadcasted_iota(jnp.int32, (r, r), 1)
    eye = jnp.where(row == col, 1.0, 0.0)
    xs = [eye - a for a in a_list]
    ps = [_dot_f32(a, a) if INVERSE_F32_POWERS > 2 else _dot(a, a) for a in a_list]
    n = 2
    d = functools.partial(jnp.dot, preferred_element_type=F32)
    while True:
        yield
        last = 2 * n >= block
        if n >= INVERSE_F32_POWERS:
            ms = [d(x.astype(BF16) if last else jnp.concatenate([p.astype(BF16), x.astype(BF16)], axis=0),
                    p.astype(BF16)) for p, x in zip(ps, xs)]
            if last:
                return [x + m for x, m in zip(xs, ms)]
            ps = [m[:r] for m in ms]
            xs = [x + m[r:] for x, m in zip(xs, ms)]
            n *= 2
            continue
        p_parts = [_split2(p) for p in ps]
        x_parts = [_split2(x) for x in xs]
        if last:
            out = []
            for x, (p1, p2), (x1, x2) in zip(xs, p_parts, x_parts):
                xp = d(x1, jnp.concatenate([p1, p2], axis=1))
                out.append(x + (xp[:, :r] + (xp[:, r:] + d(x2, p1))))
            return out
        ms = [d(jnp.concatenate([p1, x1], axis=0), jnp.concatenate([p1, p2], axis=1))
              for (p1, p2), (x1, _) in zip(p_parts, x_parts)]
        qs = [d(jnp.concatenate([p2, x2], axis=0), p1)
              for (p1, p2), (_, x2) in zip(p_parts, x_parts)]
        ss = [m[:, :r] + (m[:, r:] + q) for m, q in zip(ms, qs)]
        ps = [s[:r] for s in ss]
        xs = [x + s[r:] for x, s in zip(xs, ss)]
        n *= 2


def _gdn_decays(sm, alog_c, dtb_c, same_lt, same):
    r = sm.shape[0]
    heads = range(GDN_HEADS)
    ab = sm.T[SM_A:SM_A + 2 * GDN_HEADS, :]
    g8 = -jnp.exp(alog_c) * _softplus(ab + dtb_c)
    beta_c = _sigmoid(ab).T
    g1, g2 = _split2(g8)
    same_ge = same & jnp.logical_not(same_lt)
    sel = jnp.concatenate([jnp.where(same_ge, 1.0, 0.0).astype(BF16),
                           jnp.where(same, 1.0, 0.0).astype(BF16)], axis=1)
    sums = jnp.dot(jnp.concatenate([g1, g2], axis=0), sel, preferred_element_type=F32)
    sums = sums[:2 * GDN_HEADS] + sums[2 * GDN_HEADS:]
    dec_rows = sums[:, :r]
    dec_c = dec_rows.T
    dend_c = sums[:, r:].T
    return ([dec_c[:, h:h + 1] for h in heads], [dec_rows[h:h + 1, :] for h in heads],
            [dend_c[:, h:h + 1] for h in heads], [beta_c[:, GDN_HEADS + h:GDN_HEADS + h + 1] for h in heads])


def _gdn_prepass(conv, sm, alog_c, dtb_c, same_le, same_lt, same, block):
    r = conv.shape[0]
    heads = range(GDN_HEADS)
    dcol, drow, dend, beta = _gdn_decays(sm, alog_c, dtb_c, same_lt, same)
    gamma = [jnp.where(same_le, jnp.exp(jnp.where(same_le, dcol[h] - drow[h], 0.0)), 0.0) for h in heads]
    q = [conv[:, h * GDN_DK:(h + 1) * GDN_DK] for h in heads]
    k = [conv[:, GDN_KEY + h * GDN_DK:GDN_KEY + (h + 1) * GDN_DK] for h in heads]
    v = [conv[:, 2 * GDN_KEY + h * GDN_DV:2 * GDN_KEY + (h + 1) * GDN_DV] for h in heads]
    q = [x * lax.rsqrt(jnp.sum(x * x, axis=-1, keepdims=True) + NORM_EPS) * (GDN_DK ** -0.5) for x in q]
    k = [x * lax.rsqrt(jnp.sum(x * x, axis=-1, keepdims=True) + NORM_EPS) for x in k]
    yield
    kb = [k[h] * beta[h] for h in heads]
    vb = [v[h] * beta[h] for h in heads]
    kq = [_dot_nt(jnp.concatenate([kb[h], q[h]], axis=0), k[h]) for h in heads]
    yield
    a_mat = [jnp.where(same_lt, kq[h][:r] * gamma[h], 0.0) for h in heads]
    qk = [(kq[h][r:] * gamma[h]).astype(BF16) for h in heads]
    t_inv = yield from _block_unit_lower_inverse(a_mat, block)
    edec = [jnp.exp(dcol[h]) for h in heads]
    uw = [_dot(t_inv[h], jnp.concatenate([vb[h], kb[h] * edec[h]], axis=1)) for h in heads]
    q_e = [(q[h] * edec[h]).astype(BF16) for h in heads]
    k_end_t = [(k[h] * jnp.exp(dend[h] - dcol[h])).T.astype(BF16) for h in heads]
    yield
    u = [x[:, :GDN_DV] for x in uw]
    w = [x[:, GDN_DV:].astype(BF16) for x in uw]
    return u, w, q_e, qk, k_end_t, dend


def _conv_silu(win, cw_ref):
    conv = win(0) * cw_ref[CONV_WIDTH - 1:CONV_WIDTH, :]
    for j in range(1, CONV_WIDTH):
        conv = conv + win(j) * cw_ref[CONV_WIDTH - 1 - j:CONV_WIDTH - j, :]
    return _silu(conv)


def _gdn_prompt_kernel(qkv_ref, small_ref, cs_ref, s0_ref, cw_ref, alog_ref, dtb_ref,
                       o_ref, sfin_ref, cnew_ref, xbuf, prev, s_scr, *, c):
    t = pl.program_id(0)
    nb, r, _ = qkv_ref.shape
    keep = CONV_WIDTH - 1
    base = SUBLANES

    @pl.when(t == 0)
    def _():
        s_scr[...] = s0_ref[...]
        prev[:, base - keep:base, :] = cs_ref[...]

    same_le, same_lt, same = _block_masks(r, c)
    zeros = jnp.zeros((c, GDN_DV), F32)
    n_sub = r // c

    def one_batch(b, slot):
        xb = xbuf.at[slot]
        xb[base - keep:base, :] = prev[b, base - keep:base, :]
        xb[base:base + r, :] = qkv_ref[b]
        conv = _conv_silu(lambda j: xb[base - j:base - j + r, :], cw_ref)
        prev[b, base - keep:base, :] = xb[base + r - keep:base + r, :]
        yield
        u, w, q_e, qk, k_end_t, dend = yield from _gdn_prepass(
            conv, small_ref[b], alog_ref[...], dtb_ref[...], same_le, same_lt, same, c)
        heads = range(GDN_HEADS)
        hs = [slice(h * GDN_DK, (h + 1) * GDN_DK) for h in heads]
        s = [s_scr[b, hs[h], :] for h in heads]
        for i in range(n_sub):
            rows = slice(i * c, (i + 1) * c)
            ws = [_dot(jnp.concatenate([w[h][rows], q_e[h][rows]], axis=0), s[h]) for h in heads]
            yield
            v_new = [u[h][rows] - ws[h][:c] for h in heads]
            padded = [jnp.concatenate([zeros] * i + [v_new[h]] + [zeros] * (n_sub - 1 - i), axis=0)
                      for h in heads]
            upd = [_dot(jnp.concatenate([qk[h][rows], k_end_t[h]], axis=0), padded[h]) for h in heads]
            yield
            for h in heads:
                o_ref[b, rows, h * GDN_DV:(h + 1) * GDN_DV] = (ws[h][c:] + upd[h][:c]).astype(o_ref.dtype)
            s = [s[h] * jnp.exp(dend[h][i * c:i * c + 1, :]) + upd[h][c:] for h in heads]
        for h in heads:
            s_scr[b, hs[h], :] = s[h]

    def per_step(i, carry):
        _interleave([one_batch(i * GDN_BATCH_INTERLEAVE + j, j) for j in range(GDN_BATCH_INTERLEAVE)])
        return carry

    lax.fori_loop(0, nb // GDN_BATCH_INTERLEAVE, per_step, None)

    @pl.when(t == pl.num_programs(0) - 1)
    def _():
        sfin_ref[...] = s_scr[...]
        cnew_ref[...] = prev[:, base - keep:base, :]


def _gdn_sample_kernel(qkv_ref, small_ref, cs_ref, s0_ref, cw_ref, alog_ref, dtb_ref,
                       o_ref, sfin_ref, cnew_ref, xbuf):
    ns, c, _ = qkv_ref.shape
    r = ns * c
    keep = CONV_WIDTH - 1
    base = SUBLANES
    xbuf[:, base - keep:base, :] = cs_ref[...]
    xbuf[:, base:base + c, :] = qkv_ref[...]
    conv = _conv_silu(lambda j: xbuf[:, base - j:base - j + c, :], cw_ref).reshape(r, CONV_CH)
    cnew_ref[...] = xbuf[:, base + c - keep:base + c, :]

    same_le, same_lt, same = _block_masks(r, c)
    seq_of_row = lax.broadcasted_iota(jnp.int32, (r, GDN_DV), 0) // c
    (u, w, q_e, qk, k_end_t, dend), = _interleave([_gdn_prepass(
        conv, small_ref[...].reshape(r, LANES), alog_ref[...], dtb_ref[...], same_le, same_lt, same, c)])
    for h in range(GDN_HEADS):
        hs = slice(h * GDN_DK, (h + 1) * GDN_DK)
        v_parts, o_parts = [], []
        for s in range(ns):
            rows = slice(s * c, (s + 1) * c)
            ws = _dot(jnp.concatenate([w[h][rows], q_e[h][rows]], axis=0), s0_ref[s, hs, :])
            v_parts.append(u[h][rows] - ws[:c])
            o_parts.append(ws[c:])
        v_new = jnp.concatenate(v_parts, axis=0)
        o = jnp.concatenate(o_parts, axis=0) + _dot(qk[h], v_new)
        o_ref[:, h * GDN_DV:(h + 1) * GDN_DV] = o.astype(o_ref.dtype)
        for s in range(ns):
            upd = _dot(k_end_t[h], jnp.where(seq_of_row == s, v_new, 0.0))
            sfin_ref[s, hs, :] = s0_ref[s, hs, :] * jnp.exp(dend[h][s * c:s * c + 1, :]) + upd


def _gdn_prompt(qkv3, small3, conv_state, s0, conv_w, alog_v, dtb_v, c):
    bsz, t_len, _ = qkv3.shape
    rows = GDN_HEADS * GDN_DK
    keep = CONV_WIDTH - 1
    r = GROUP_ROWS
    full3 = lambda t: (0, 0, 0)
    const2 = lambda t: (0, 0)
    return pl.pallas_call(
        functools.partial(_gdn_prompt_kernel, c=c),
        grid=(t_len // r,),
        in_specs=[
            pl.BlockSpec((bsz, r, CONV_CH), lambda t: (0, t, 0)),
            pl.BlockSpec((bsz, r, LANES), lambda t: (0, t, 0)),
            pl.BlockSpec((bsz, keep, CONV_CH), full3),
            pl.BlockSpec((bsz, rows, GDN_DV), full3),
            pl.BlockSpec((CONV_WIDTH, CONV_CH), const2),
            pl.BlockSpec((2 * GDN_HEADS, 1), const2),
            pl.BlockSpec((2 * GDN_HEADS, 1), const2),
        ],
        out_specs=[
            pl.BlockSpec((bsz, r, GDN_VAL), lambda t: (0, t, 0)),
            pl.BlockSpec((bsz, rows, GDN_DV), full3),
            pl.BlockSpec((bsz, keep, CONV_CH), full3),
        ],
        out_shape=[
            jax.ShapeDtypeStruct((bsz, t_len, GDN_VAL), BF16),
            jax.ShapeDtypeStruct((bsz, rows, GDN_DV), F32),
            jax.ShapeDtypeStruct((bsz, keep, CONV_CH), F32),
        ],
        scratch_shapes=[pltpu.VMEM((GDN_BATCH_INTERLEAVE, SUBLANES + r, CONV_CH), F32),
                        pltpu.VMEM((bsz, SUBLANES, CONV_CH), F32),
                        pltpu.VMEM((bsz, rows, GDN_DV), F32)],
        compiler_params=pltpu.CompilerParams(dimension_semantics=("arbitrary",),
                                             vmem_limit_bytes=VMEM_LIMIT),
        name="gdn_prompt",
    )(qkv3, small3, conv_state, s0, conv_w, alog_v, dtb_v)


def _gdn_sample(qkv3, small3, conv_state, s0, conv_w, alog_v, dtb_v):
    bsz, c, _ = qkv3.shape
    rows = GDN_HEADS * GDN_DK
    keep = CONV_WIDTH - 1
    ns = GROUP_ROWS // c
    grp = lambda g: (g, 0, 0)
    const2 = lambda g: (0, 0)
    return pl.pallas_call(
        _gdn_sample_kernel,
        grid=(bsz // ns,),
        in_specs=[
            pl.BlockSpec((ns, c, CONV_CH), grp),
            pl.BlockSpec((ns, c, LANES), grp),
            pl.BlockSpec((ns, keep, CONV_CH), grp),
            pl.BlockSpec((ns, rows, GDN_DV), grp),
            pl.BlockSpec((CONV_WIDTH, CONV_CH), const2),
            pl.BlockSpec((2 * GDN_HEADS, 1), const2),
            pl.BlockSpec((2 * GDN_HEADS, 1), const2),
        ],
        out_specs=[
            pl.BlockSpec((ns * c, GDN_VAL), lambda g: (g, 0)),
            pl.BlockSpec((ns, rows, GDN_DV), grp),
            pl.BlockSpec((ns, keep, CONV_CH), grp),
        ],
        out_shape=[
            jax.ShapeDtypeStruct((bsz * c, GDN_VAL), BF16),
            jax.ShapeDtypeStruct((bsz, rows, GDN_DV), F32),
            jax.ShapeDtypeStruct((bsz, keep, CONV_CH), F32),
        ],
        scratch_shapes=[pltpu.VMEM((ns, 2 * SUBLANES, CONV_CH), F32)],
        compiler_params=pltpu.CompilerParams(dimension_semantics=("arbitrary",),
                                             vmem_limit_bytes=VMEM_LIMIT),
        name="gdn_sample",
    )(qkv3, small3, conv_state, s0, conv_w, alog_v, dtb_v)


def _head_norm_gate(o, z, w):
    parts = []
    for h in range(o.shape[-1] // LANES):
        oh = o[:, h * LANES:(h + 1) * LANES]
        parts.append(oh * lax.rsqrt(jnp.mean(oh * oh, axis=-1, keepdims=True) + NORM_EPS) * w)
    return jnp.concatenate(parts, axis=-1) * _silu(z)


def _out_kernel(oa_ref, ob_ref, ga_ref, gb_ref, za_ref, zb_ref, x_ref, p_ref, anw_ref, bnw_ref,
                wua_ref, wub_ref, wout_ref, wpg_ref, wp_ref, fnw_ref, y_ref):
    f32 = lambda ref: ref[...].astype(F32)
    ya = _dot(_head_norm_gate(f32(oa_ref), f32(za_ref), anw_ref[...]), wua_ref[...])
    yb = _dot(_head_norm_gate(f32(ob_ref), f32(zb_ref), bnw_ref[...]), wub_ref[...])
    merged = _sigmoid(f32(ga_ref)) * ya + _sigmoid(f32(gb_ref)) * yb
    h1 = x_ref[...] + _dot(merged, wout_ref[...])
    h2 = h1 + _sigmoid(_dot(h1, wpg_ref[...])) * _dot(p_ref[...], wp_ref[...])
    y_ref[...] = h2 * lax.rsqrt(jnp.mean(h2 * h2, axis=-1, keepdims=True) + NORM_EPS) * fnw_ref[...]


def _out_stage(o_a, o_b, gz, x2d, p2d, anw, bnw, wua, wub, wout, wpg, wp, fnw):
    n = x2d.shape[0]
    tm = min(ROW_TILE, n)
    const = lambda i: (0, 0)
    return pl.pallas_call(
        _out_kernel,
        grid=(n // tm,),
        in_specs=[
            pl.BlockSpec((tm, GLA_VAL), lambda i: (i, 0)),
            pl.BlockSpec((tm, GDN_VAL), lambda i: (i, 0)),
            pl.BlockSpec((tm, D_MODEL), lambda i: (i, GZ_GATE_A // D_MODEL)),
            pl.BlockSpec((tm, D_MODEL), lambda i: (i, GZ_GATE_B // D_MODEL)),
            pl.BlockSpec((tm, GLA_VAL), lambda i: (i, GZ_Z_A // GLA_VAL)),
            pl.BlockSpec((tm, GDN_VAL), lambda i: (i, GZ_Z_B // GDN_VAL)),
            pl.BlockSpec((tm, D_MODEL), lambda i: (i, 0)),
            pl.BlockSpec((tm, PLE_DIM), lambda i: (i, 0)),
            pl.BlockSpec((1, GLA_DV), const),
            pl.BlockSpec((1, GDN_DV), const),
            pl.BlockSpec((GLA_VAL, D_MODEL), const),
            pl.BlockSpec((GDN_VAL, D_MODEL), const),
            pl.BlockSpec((D_MODEL, D_MODEL), const),
            pl.BlockSpec((D_MODEL, D_MODEL), const),
            pl.BlockSpec((PLE_DIM, D_MODEL), const),
            pl.BlockSpec((1, D_MODEL), const),
        ],
        out_specs=pl.BlockSpec((tm, D_MODEL), lambda i: (i, 0)),
        out_shape=jax.ShapeDtypeStruct((n, D_MODEL), F32),
        compiler_params=pltpu.CompilerParams(dimension_semantics=("arbitrary",),
                                             vmem_limit_bytes=VMEM_LIMIT),
        name="out_stage",
    )(o_a, o_b, gz, gz, gz, gz, x2d, p2d, anw, bnw, wua, wub, wout, wpg, wp, fnw)


def _in_offsets():
    offs = [0]
    for s in IN_SPLITS:
        offs.append(offs[-1] + s)
    return offs


def _regroup_kernel(wt_ref, o_ref):
    offs = _in_offsets()
    (q_a, _, _, g_a, z_a, qkv_b, a_b, _, z_b, gate_a, gate_b, end) = offs
    piece = lambda lo, hi: wt_ref[lo:hi, :].astype(BF16)
    o_ref[P_GLA_QKV:P_GLA_QKV + GLA_QKV, :] = piece(q_a, g_a)
    o_ref[P_QKV_B:P_QKV_B + CONV_CH, :] = piece(qkv_b, a_b)
    small = jnp.concatenate([wt_ref[g_a:z_a, :], wt_ref[a_b:z_b, :],
                             jnp.zeros((LANES - (z_a - g_a) - (z_b - a_b), wt_ref.shape[1]), F32)], axis=0)
    o_ref[P_SMALL:P_SMALL + LANES, :] = small.astype(BF16)
    o_ref[P_GATES + GZ_GATE_A:P_GATES + GZ_GATE_A + D_MODEL, :] = piece(gate_a, gate_b)
    o_ref[P_GATES + GZ_GATE_B:P_GATES + GZ_GATE_B + D_MODEL, :] = piece(gate_b, end)
    o_ref[P_GATES + GZ_Z_A:P_GATES + GZ_Z_A + GLA_VAL, :] = piece(z_a, qkv_b)
    o_ref[P_GATES + GZ_Z_B:P_GATES + GZ_Z_B + GDN_VAL, :] = piece(z_b, gate_a)


def _regroup_w_in(w_in_t):
    cols = 256
    return pl.pallas_call(
        _regroup_kernel,
        grid=(D_MODEL // cols,),
        in_specs=[pl.BlockSpec((None, w_in_t.shape[1], cols), lambda i: (0, 0, i))],
        out_specs=pl.BlockSpec((P_COLS, cols), lambda i: (0, i)),
        out_shape=jax.ShapeDtypeStruct((P_COLS, D_MODEL), BF16),
        compiler_params=pltpu.CompilerParams(dimension_semantics=("arbitrary",),
                                             vmem_limit_bytes=VMEM_LIMIT),
        name="regroup_w_in",
    )(w_in_t)


def _head_param_col(v):
    return jnp.zeros((2 * GDN_HEADS, 1), F32).at[:GDN_HEADS, 0].set(v.astype(F32))


def _trunk(x, p, s_gla, s_gdn, conv_state, wts):
    bsz, t_len, _ = x.shape
    c = min(PROMPT_CHUNK, t_len)
    n = bsz * t_len
    x2d = x.reshape(n, D_MODEL)
    p2d = p.reshape(n, PLE_DIM)
    gla_in, gdn_in, small, gz = _inproj(x2d, wts["norm_w"], wts["w_in_r"])
    gla_in3 = gla_in.reshape(bsz, t_len, GLA_QKV)
    gdn_in3 = gdn_in.reshape(bsz, t_len, CONV_CH)
    small3 = small.reshape(bsz, t_len, LANES)
    s_gla2 = s_gla.reshape(bsz, GLA_HEADS * GLA_DK, GLA_DV)
    s_gdn2 = s_gdn.reshape(bsz, GDN_HEADS * GDN_DK, GDN_DV)
    if t_len % GROUP_ROWS == 0:
        o_a, gla_fin = _gla_prompt(gla_in3, small3, s_gla2, wts["wgg"], wts["bgg"], c)
        o_b, gdn_fin, conv_new = _gdn_prompt(gdn_in3, small3, conv_state, s_gdn2, wts["conv_w"],
                                             wts["alog_v"], wts["dtb_v"], c)
    else:
        assert GROUP_ROWS % t_len == 0 and bsz % (GROUP_ROWS // t_len) == 0 and t_len >= CONV_WIDTH - 1
        o_a, gla_fin = _gla_sample(gla_in3, small3, s_gla2, wts["wgg"], wts["bgg"])
        o_b, gdn_fin, conv_new = _gdn_sample(gdn_in3, small3, conv_state, s_gdn2, wts["conv_w"],
                                             wts["alog_v"], wts["dtb_v"])
    y = _out_stage(o_a.reshape(n, GLA_VAL), o_b.reshape(n, GDN_VAL), gz, x2d, p2d, wts["anw"], wts["bnw"],
                   wts["wua"], wts["wub"], wts["wout"], wts["wpg"], wts["wp"], wts["fnw"])
    return (y.reshape(bsz, t_len, D_MODEL),
            gla_fin.reshape(1, bsz, GLA_HEADS, GLA_DK, GLA_DV),
            gdn_fin.reshape(1, bsz, GDN_HEADS, GDN_DK, GDN_DV),
            conv_new.reshape(1, bsz, CONV_WIDTH - 1, CONV_CH))


def kernel(x_prompt, x_sample, state_gla, state_gdn, state_conv, p_prompt, p_sample, norm_w, w_in,
           w_gla_gate, b_gla_gate, gla_norm_w, conv_w, gdn_a_log, gdn_dt_bias, gdn_norm_w,
           w_up_gla, w_up_gdn, w_out, w_ple_gate, w_ple, final_norm_w):
    wgg = jnp.zeros((LANES, GLA_KEY), F32).at[SM_G:SM_G + GLA_GATE_RANK].set(w_gla_gate[0]).astype(BF16)
    wts = {
        "norm_w": norm_w[0].reshape(1, D_MODEL),
        "w_in_r": _regroup_w_in(jnp.swapaxes(w_in, 1, 2)),
        "wgg": wgg,
        "bgg": b_gla_gate[0].reshape(1, GLA_KEY),
        "conv_w": conv_w[0],
        "alog_v": _head_param_col(gdn_a_log[0]),
        "dtb_v": _head_param_col(gdn_dt_bias[0]),
        "anw": gla_norm_w[0].reshape(1, GLA_DV),
        "bnw": gdn_norm_w[0].reshape(1, GDN_DV),
        "wua": w_up_gla[0].astype(BF16),
        "wub": w_up_gdn[0].astype(BF16),
        "wout": w_out[0].astype(BF16),
        "wpg": w_ple_gate[0].astype(BF16),
        "wp": w_ple[0].astype(BF16),
        "fnw": final_norm_w.reshape(1, D_MODEL),
    }
    bsz = x_prompt.shape[0]
    dt = x_prompt.dtype
    y_p, gla_p, gdn_p, conv_p = _trunk(
        x_prompt, p_prompt[0],
        jnp.zeros((bsz, GLA_HEADS, GLA_DK, GLA_DV), dt), jnp.zeros((bsz, GDN_HEADS, GDN_DK, GDN_DV), dt),
        jnp.zeros((bsz, CONV_WIDTH - 1, CONV_CH), dt), wts)
    y_s, gla_s, gdn_s, conv_s = _trunk(x_sample, p_sample[0], state_gla[0], state_gdn[0], state_conv[0], wts)
    return (y_p, y_s, gla_p, gdn_p, conv_p, gla_s, gdn_s, conv_s)
```

```python
import functools

import jax
import jax.numpy as jnp
from jax import lax
from jax.experimental import pallas as pl
from jax.experimental.pallas import tpu as pltpu

F32 = jnp.float32
BF16 = jnp.bfloat16

D_MODEL = 1024
PLE_DIM = 256
NORM_EPS = 1e-6
GLA_HEADS = 4
GLA_DK = 64
GLA_DV = 128
GLA_KEY = GLA_HEADS * GLA_DK
GLA_VAL = GLA_HEADS * GLA_DV
GLA_GATE_RANK = 16
GLA_GATE_TEMP = 16.0
GDN_HEADS = 4
GDN_DK = 128
GDN_DV = 128
GDN_KEY = GDN_HEADS * GDN_DK
GDN_VAL = GDN_HEADS * GDN_DV
CONV_WIDTH = 4
CONV_CH = 2 * GDN_KEY + GDN_VAL
IN_SPLITS = (GLA_KEY, GLA_KEY, GLA_VAL, GLA_GATE_RANK, GLA_VAL, CONV_CH, GDN_HEADS, GDN_HEADS,
             GDN_VAL, D_MODEL, D_MODEL)

LANES = 128
SUBLANES = 8

GLA_QKV = 2 * GLA_KEY + GLA_VAL
P_GLA_QKV = 0
P_QKV_B = P_GLA_QKV + GLA_QKV
P_SMALL = P_QKV_B + CONV_CH
P_GATES = P_SMALL + LANES
GZ_COLS = 2 * D_MODEL + GLA_VAL + GDN_VAL
P_COLS = P_GATES + GZ_COLS
GZ_GATE_A = 0
GZ_GATE_B = D_MODEL
GZ_Z_A = 2 * D_MODEL
GZ_Z_B = 2 * D_MODEL + GLA_VAL
SM_G = 0
SM_A = GLA_GATE_RANK
SM_B = GLA_GATE_RANK + GDN_HEADS

PROMPT_CHUNK = 64
GROUP_ROWS = 128
BATCH_INTERLEAVE = 4
GDN_BATCH_INTERLEAVE = 4
INVERSE_F32_POWERS = 2
ROW_TILE = 512
VMEM_LIMIT = 56 * 1024 * 1024


def _dot(a, b):
    return jnp.dot(a.astype(BF16), b.astype(BF16), preferred_element_type=F32)


def _dot_nt(a, b):
    return lax.dot_general(a.astype(BF16), b.astype(BF16), (((1,), (1,)), ((), ())),
                           preferred_element_type=F32)


def _split2(x):
    h1 = x.astype(BF16)
    return h1, (x - h1.astype(F32)).astype(BF16)


def _dot_f32(a, b):
    a1, a2 = _split2(a)
    b1, b2 = _split2(b)
    d = functools.partial(jnp.dot, preferred_element_type=F32)
    return d(a1, b1) + (d(a1, b2) + d(a2, b1))


def _cumsum_rows(tri, x):
    x1, x2 = _split2(x)
    d = functools.partial(jnp.dot, preferred_element_type=F32)
    return d(tri, x1) + d(tri, x2)


def _softplus(x):
    return jnp.maximum(x, 0.0) + jnp.log(1.0 + jnp.exp(-jnp.abs(x)))


def _sigmoid(x):
    return 1.0 / (1.0 + jnp.exp(-x))


def _silu(x):
    return x * _sigmoid(x)


def _interleave(emitters):
    results = [None] * len(emitters)
    live = list(range(len(emitters)))
    while live:
        for i in list(live):
            try:
                next(emitters[i])
            except StopIteration as stop:
                results[i] = stop.value
                live.remove(i)
    return results


def _block_masks(r, block):
    row = lax.broadcasted_iota(jnp.int32, (r, r), 0)
    col = lax.broadcasted_iota(jnp.int32, (r, r), 1)
    same = (row // block) == (col // block)
    return same & (row >= col), same & (row > col), same


INPROJ_COL_STEP = 512


def _inproj_kernel(x_ref, nw_ref, w_ref, gla_ref, gdn_ref, small_ref, gz_ref):
    x = x_ref[...]
    xn = x * lax.rsqrt(jnp.mean(x * x, axis=-1, keepdims=True) + NORM_EPS) * nw_ref[...]
    xb = xn.astype(BF16)

    def emit(o_ref, w0, width, act=None, o0=0):
        for c0 in range(0, width, INPROJ_COL_STEP):
            c1 = min(c0 + INPROJ_COL_STEP, width)
            res = _dot_nt(xb, w_ref[w0 + c0:w0 + c1, :])
            o_ref[:, o0 + c0:o0 + c1] = (res if act is None else act(res)).astype(o_ref.dtype)

    emit(gla_ref, P_GLA_QKV, GLA_QKV)
    emit(gdn_ref, P_QKV_B, CONV_CH)
    emit(small_ref, P_SMALL, LANES)
    emit(gz_ref, P_GATES + GZ_GATE_A, 2 * D_MODEL, act=_sigmoid, o0=GZ_GATE_A)
    emit(gz_ref, P_GATES + GZ_Z_A, GLA_VAL + GDN_VAL, act=_silu, o0=GZ_Z_A)


def _inproj(x2d, norm_w, w_in_r):
    n = x2d.shape[0]
    tm = min(ROW_TILE, n)
    rows = lambda i: (i, 0)
    return pl.pallas_call(
        _inproj_kernel,
        grid=(n // tm,),
        in_specs=[
            pl.BlockSpec((tm, D_MODEL), rows),
            pl.BlockSpec((1, D_MODEL), lambda i: (0, 0)),
            pl.BlockSpec((P_COLS, D_MODEL), lambda i: (0, 0), pipeline_mode=pl.Buffered(1)),
        ],
        out_specs=[
            pl.BlockSpec((tm, GLA_QKV), rows),
            pl.BlockSpec((tm, CONV_CH), rows),
            pl.BlockSpec((tm, LANES), rows),
            pl.BlockSpec((tm, GZ_COLS), rows),
        ],
        out_shape=[
            jax.ShapeDtypeStruct((n, GLA_QKV), F32),
            jax.ShapeDtypeStruct((n, CONV_CH), F32),
            jax.ShapeDtypeStruct((n, LANES), F32),
            jax.ShapeDtypeStruct((n, GZ_COLS), BF16),
        ],
        compiler_params=pltpu.CompilerParams(dimension_semantics=("arbitrary",),
                                             vmem_limit_bytes=VMEM_LIMIT),
        name="inproj",
    )(x2d, norm_w, w_in_r)


GLA_PAIRS = GLA_HEADS // 2


def _gla_prepass(q, k, v, sm, wgg, bgg, same_le, same):
    r = q.shape[0]
    heads = range(GLA_HEADS)
    pairs = range(GLA_PAIRS)
    pre = _dot(sm, wgg) + bgg
    yield
    gk = (jnp.minimum(pre, 0.0) - jnp.log(1.0 + jnp.exp(-jnp.abs(pre)))) * (1.0 / GLA_GATE_TEMP)
    sums = _cumsum_rows(jnp.concatenate([jnp.where(same_le, 1.0, 0.0).astype(BF16),
                                         jnp.where(same, 1.0, 0.0).astype(BF16)], axis=0), gk)
    yield
    bcum = sums[:r]
    bend = sums[r:]
    q_e = q * jnp.exp(bcum) * (GLA_DK ** -0.5)
    k_e = k * jnp.exp(-bcum)
    lane = lax.broadcasted_iota(jnp.int32, (r, LANES), 1)
    in_head = [lane < GLA_DK, lane >= GLA_DK]
    pl_ = [slice(p * LANES, (p + 1) * LANES) for p in pairs]
    qm = [jnp.where(in_head[h % 2], q_e[:, pl_[h // 2]], 0.0) for h in heads]
    att = [jnp.where(same_le, _dot_nt(qm[h], k_e[:, pl_[h // 2]]), 0.0) for h in heads]
    yield
    o_intra = [_dot(att[h], v[h]) for h in heads]
    k_end = k * jnp.exp(bend - bcum)
    k_end_t = [k_end[:, pl_[p]].T for p in pairs]
    bend_t = [bend[:, pl_[p]].T for p in pairs]
    yield
    return qm, o_intra, k_end_t, bend_t


def _gla_prompt_kernel(qkv_ref, small_ref, s0_ref, wgg_ref, bgg_ref, o_ref, sfin_ref, s_scr, *, c):
    t = pl.program_id(0)
    nb, r, _ = qkv_ref.shape

    @pl.when(t == 0)
    def _():
        s_scr[...] = s0_ref[...]

    same_le, _, same = _block_masks(r, c)
    zeros = jnp.zeros((c, GLA_DV), F32)
    n_sub = r // c
    heads = range(GLA_HEADS)
    pairs = range(GLA_PAIRS)

    def one_batch(b):
        q = qkv_ref[b, :, 0:GLA_KEY]
        k = qkv_ref[b, :, GLA_KEY:2 * GLA_KEY]
        v = [qkv_ref[b, :, 2 * GLA_KEY + h * GLA_DV:2 * GLA_KEY + (h + 1) * GLA_DV] for h in heads]
        qm, o_intra, k_end_t, bend_t = yield from _gla_prepass(
            q, k, v, small_ref[b], wgg_ref[...], bgg_ref[...], same_le, same)
        s = [s_scr[b, p * LANES:(p + 1) * LANES, :] for p in pairs]
        for i in range(n_sub):
            rows = slice(i * c, (i + 1) * c)
            ws = [_dot(jnp.concatenate([qm[2 * p][rows], qm[2 * p + 1][rows]], axis=0), s[p]) for p in pairs]
            padded = [jnp.concatenate([zeros] * i + [v[h][rows]] + [zeros] * (n_sub - 1 - i), axis=0)
                      for h in heads]
            upd = [_dot(k_end_t[h // 2][(h % 2) * GLA_DK:(h % 2 + 1) * GLA_DK, :], padded[h]) for h in heads]
            yield
            for h in heads:
                o_ref[b, rows, h * GLA_DV:(h + 1) * GLA_DV] = (
                    o_intra[h][rows] + ws[h // 2][(h % 2) * c:(h % 2 + 1) * c]).astype(o_ref.dtype)
            s = [s[p] * jnp.exp(bend_t[p][:, i * c:i * c + 1])
                 + jnp.concatenate([upd[2 * p], upd[2 * p + 1]], axis=0) for p in pairs]
        for p in pairs:
            s_scr[b, p * LANES:(p + 1) * LANES, :] = s[p]

    def per_step(i, carry):
        _interleave([one_batch(i * BATCH_INTERLEAVE + j) for j in range(BATCH_INTERLEAVE)])
        return carry

    lax.fori_loop(0, nb // BATCH_INTERLEAVE, per_step, None)

    @pl.when(t == pl.num_programs(0) - 1)
    def _():
        sfin_ref[...] = s_scr[...]


def _gla_sample_kernel(qkv_ref, small_ref, s0_ref, wgg_ref, bgg_ref, o_ref, sfin_ref):
    ns, c, _ = qkv_ref.shape
    r = ns * c
    heads = range(GLA_HEADS)
    same_le, _, same = _block_masks(r, c)
    q = qkv_ref[:, :, 0:GLA_KEY].reshape(r, GLA_KEY)
    k = qkv_ref[:, :, GLA_KEY:2 * GLA_KEY].reshape(r, GLA_KEY)
    v = [qkv_ref[:, :, 2 * GLA_KEY + h * GLA_DV:2 * GLA_KEY + (h + 1) * GLA_DV].reshape(r, GLA_DV)
         for h in heads]
    (qm, o_intra, k_end_t, bend_t), = _interleave([_gla_prepass(
        q, k, v, small_ref[...].reshape(r, LANES), wgg_ref[...], bgg_ref[...], same_le, same)])
    seq_of_row = lax.broadcasted_iota(jnp.int32, (r, GLA_DV), 0) // c
    for p in range(GLA_PAIRS):
        ps = slice(p * LANES, (p + 1) * LANES)
        inter = [[], []]
        for s in range(ns):
            rows = slice(s * c, (s + 1) * c)
            ws = _dot(jnp.concatenate([qm[2 * p][rows], qm[2 * p + 1][rows]], axis=0), s0_ref[s, ps, :])
            inter[0].append(ws[:c])
            inter[1].append(ws[c:])
        for hh in range(2):
            h = 2 * p + hh
            o = o_intra[h] + jnp.concatenate(inter[hh], axis=0)
            o_ref[:, h * GLA_DV:(h + 1) * GLA_DV] = o.astype(o_ref.dtype)
        for s in range(ns):
            upd = [_dot(k_end_t[p][hh * GLA_DK:(hh + 1) * GLA_DK, :],
                        jnp.where(seq_of_row == s, v[2 * p + hh], 0.0)) for hh in range(2)]
            sfin_ref[s, ps, :] = (s0_ref[s, ps, :] * jnp.exp(bend_t[p][:, s * c:s * c + 1])
                                  + jnp.concatenate(upd, axis=0))


def _gla_prompt(qkv3, small3, s0, wgg, bgg, c):
    bsz, t_len, _ = qkv3.shape
    rows = GLA_HEADS * GLA_DK
    r = GROUP_ROWS
    full3 = lambda t: (0, 0, 0)
    const2 = lambda t: (0, 0)
    return pl.pallas_call(
        functools.partial(_gla_prompt_kernel, c=c),
        grid=(t_len // r,),
        in_specs=[
            pl.BlockSpec((bsz, r, GLA_QKV), lambda t: (0, t, 0)),
            pl.BlockSpec((bsz, r, LANES), lambda t: (0, t, 0)),
            pl.BlockSpec((bsz, rows, GLA_DV), full3),
            pl.BlockSpec((LANES, GLA_KEY), const2),
            pl.BlockSpec((1, GLA_KEY), const2),
        ],
        out_specs=[
            pl.BlockSpec((bsz, r, GLA_VAL), lambda t: (0, t, 0)),
            pl.BlockSpec((bsz, rows, GLA_DV), full3),
        ],
        out_shape=[
            jax.ShapeDtypeStruct((bsz, t_len, GLA_VAL), BF16),
            jax.ShapeDtypeStruct((bsz, rows, GLA_DV), F32),
        ],
        scratch_shapes=[pltpu.VMEM((bsz, rows, GLA_DV), F32)],
        compiler_params=pltpu.CompilerParams(dimension_semantics=("arbitrary",),
                                             vmem_limit_bytes=VMEM_LIMIT),
        name="gla_prompt",
    )(qkv3, small3, s0, wgg, bgg)


def _gla_sample(qkv3, small3, s0, wgg, bgg):
    bsz, c, _ = qkv3.shape
    rows = GLA_HEADS * GLA_DK
    ns = GROUP_ROWS // c
    grp = lambda g: (g, 0, 0)
    const2 = lambda g: (0, 0)
    return pl.pallas_call(
        _gla_sample_kernel,
        grid=(bsz // ns,),
        in_specs=[
            pl.BlockSpec((ns, c, GLA_QKV), grp),
            pl.BlockSpec((ns, c, LANES), grp),
            pl.BlockSpec((ns, rows, GLA_DV), grp),
            pl.BlockSpec((LANES, GLA_KEY), const2),
            pl.BlockSpec((1, GLA_KEY), const2),
        ],
        out_specs=[
            pl.BlockSpec((ns * c, GLA_VAL), lambda g: (g, 0)),
            pl.BlockSpec((ns, rows, GLA_DV), grp),
        ],
        out_shape=[
            jax.ShapeDtypeStruct((bsz * c, GLA_VAL), BF16),
            jax.ShapeDtypeStruct((bsz, rows, GLA_DV), F32),
        ],
        compiler_params=pltpu.CompilerParams(dimension_semantics=("arbitrary",),
                                             vmem_limit_bytes=VMEM_LIMIT),
        name="gla_sample",
    )(qkv3, small3, s0, wgg, bgg)


def _block_unit_lower_inverse(a_list, block):
    r = a_list[0].shape[0]
    row = lax.broadcasted_iota(jnp.int32, (r, r), 0)
    col = lax.broadcasted_iota(jnp.int32, (r, r), 1)
    eye = jnp.where(row == col, 1.0, 0.0)
    xs = [eye - a for a in a_list]
    ps = [_dot_f32(a, a) if INVERSE_F32_POWERS > 2 else _dot(a, a) for a in a_list]
    n = 2
    d = functools.partial(jnp.dot, preferred_element_type=F32)
    while True:
        yield
        last = 2 * n >= block
        if n >= INVERSE_F32_POWERS:
            ms = [d(x.astype(BF16) if last else jnp.concatenate([p.astype(BF16), x.astype(BF16)], axis=0),
                    p.astype(BF16)) for p, x in zip(ps, xs)]
            if last:
                return [x + m for x, m in zip(xs, ms)]
            ps = [m[:r] for m in ms]
            xs = [x + m[r:] for x, m in zip(xs, ms)]
            n *= 2
            continue
        p_parts = [_split2(p) for p in ps]
        x_parts = [_split2(x) for x in xs]
        if last:
            out = []
            for x, (p1, p2), (x1, x2) in zip(xs, p_parts, x_parts):
                xp = d(x1, jnp.concatenate([p1, p2], axis=1))
                out.append(x + (xp[:, :r] + (xp[:, r:] + d(x2, p1))))
            return out
        ms = [d(jnp.concatenate([p1, x1], axis=0), jnp.concatenate([p1, p2], axis=1))
              for (p1, p2), (x1, _) in zip(p_parts, x_parts)]
        qs = [d(jnp.concatenate([p2, x2], axis=0), p1)
              for (p1, p2), (_, x2) in zip(p_parts, x_parts)]
        ss = [m[:, :r] + (m[:, r:] + q) for m, q in zip(ms, qs)]
        ps = [s[:r] for s in ss]
        xs = [x + s[r:] for x, s in zip(xs, ss)]
        n *= 2


def _gdn_decays(sm, alog_c, dtb_c, same_lt, same):
    r = sm.shape[0]
    heads = range(GDN_HEADS)
    ab = sm.T[SM_A:SM_A + 2 * GDN_HEADS, :]
    g8 = -jnp.exp(alog_c) * _softplus(ab + dtb_c)
    beta_c = _sigmoid(ab).T
    g1, g2 = _split2(g8)
    same_ge = same & jnp.logical_not(same_lt)
    sel = jnp.concatenate([jnp.where(same_ge, 1.0, 0.0).astype(BF16),
                           jnp.where(same, 1.0, 0.0).astype(BF16)], axis=1)
    sums = jnp.dot(jnp.concatenate([g1, g2], axis=0), sel, preferred_element_type=F32)
    sums = sums[:2 * GDN_HEADS] + sums[2 * GDN_HEADS:]
    dec_rows = sums[:, :r]
    dec_c = dec_rows.T
    dend_c = sums[:, r:].T
    return ([dec_c[:, h:h + 1] for h in heads], [dec_rows[h:h + 1, :] for h in heads],
            [dend_c[:, h:h + 1] for h in heads], [beta_c[:, GDN_HEADS + h:GDN_HEADS + h + 1] for h in heads])


def _gdn_prepass(conv, sm, alog_c, dtb_c, same_le, same_lt, same, block):
    r = conv.shape[0]
    heads = range(GDN_HEADS)
    dcol, drow, dend, beta = _gdn_decays(sm, alog_c, dtb_c, same_lt, same)
    gamma = [jnp.where(same_le, jnp.exp(jnp.where(same_le, dcol[h] - drow[h], 0.0)), 0.0) for h in heads]
    q = [conv[:, h * GDN_DK:(h + 1) * GDN_DK] for h in heads]
    k = [conv[:, GDN_KEY + h * GDN_DK:GDN_KEY + (h + 1) * GDN_DK] for h in heads]
    v = [conv[:, 2 * GDN_KEY + h * GDN_DV:2 * GDN_KEY + (h + 1) * GDN_DV] for h in heads]
    q = [x * lax.rsqrt(jnp.sum(x * x, axis=-1, keepdims=True) + NORM_EPS) * (GDN_DK ** -0.5) for x in q]
    k = [x * lax.rsqrt(jnp.sum(x * x, axis=-1, keepdims=True) + NORM_EPS) for x in k]
    yield
    kb = [k[h] * beta[h] for h in heads]
    vb = [v[h] * beta[h] for h in heads]
    kq = [_dot_nt(jnp.concatenate([kb[h], q[h]], axis=0), k[h]) for h in heads]
    yield
    a_mat = [jnp.where(same_lt, kq[h][:r] * gamma[h], 0.0) for h in heads]
    qk = [(kq[h][r:] * gamma[h]).astype(BF16) for h in heads]
    t_inv = yield from _block_unit_lower_inverse(a_mat, block)
    edec = [jnp.exp(dcol[h]) for h in heads]
    uw = [_dot(t_inv[h], jnp.concatenate([vb[h], kb[h] * edec[h]], axis=1)) for h in heads]
    q_e = [(q[h] * edec[h]).astype(BF16) for h in heads]
    k_end_t = [(k[h] * jnp.exp(dend[h] - dcol[h])).T.astype(BF16) for h in heads]
    yield
    u = [x[:, :GDN_DV] for x in uw]
    w = [x[:, GDN_DV:].astype(BF16) for x in uw]
    return u, w, q_e, qk, k_end_t, dend


def _conv_silu(win, cw_ref):
    conv = win(0) * cw_ref[CONV_WIDTH - 1:CONV_WIDTH, :]
    for j in range(1, CONV_WIDTH):
        conv = conv + win(j) * cw_ref[CONV_WIDTH - 1 - j:CONV_WIDTH - j, :]
    return _silu(conv)

def _gdn_prompt_kernel(qkv_ref, small_ref, cs_ref, s0_ref, cw_ref, alog_ref, dtb_ref,
                       o_ref, sfin_ref, cnew_ref, xbuf, prev, s_scr, *, c):
    t = pl.program_id(0)
    nb, r, _ = qkv_ref.shape
    keep = CONV_WIDTH - 1
    base = SUBLANES

    @pl.when(t == 0)
    def _():
        s_scr[...] = s0_ref[...]
        prev[:, base - keep:base, :] = cs_ref[...]

    same_le, same_lt, same = _block_masks(r, c)
    zeros = jnp.zeros((c, GDN_DV), F32)
    n_sub = r // c

    def one_batch(b, slot):
        xb = xbuf.at[slot]
        xb[base - keep:base, :] = prev[b, base - keep:base, :]
        xb[base:base + r, :] = qkv_ref[b]
        conv = _conv_silu(lambda j: xb[base - j:base - j + r, :], cw_ref)
        prev[b, base - keep:base, :] = xb[base + r - keep:base + r, :]
        yield
        u, w, q_e, qk, k_end_t, dend = yield from _gdn_prepass(
            conv, small_ref[b], alog_ref[...], dtb_ref[...], same_le, same_lt, same, c)
        heads = range(GDN_HEADS)
        hs = [slice(h * GDN_DK, (h + 1) * GDN_DK) for h in heads]
        s = [s_scr[b, hs[h], :] for h in heads]
        for i in range(n_sub):
            rows = slice(i * c, (i + 1) * c)
            ws = [_dot(jnp.concatenate([w[h][rows], q_e[h][rows]], axis=0), s[h]) for h in heads]
            yield
            v_new = [u[h][rows] - ws[h][:c] for h in heads]
            padded = [jnp.concatenate([zeros] * i + [v_new[h]] + [zeros] * (n_sub - 1 - i), axis=0)
                      for h in heads]
            upd = [_dot(jnp.concatenate([qk[h][rows], k_end_t[h]], axis=0), padded[h]) for h in heads]
            yield
            for h in heads:
                o_ref[b, rows, h * GDN_DV:(h + 1) * GDN_DV] = (ws[h][c:] + upd[h][:c]).astype(o_ref.dtype)
            s = [s[h] * jnp.exp(dend[h][i * c:i * c + 1, :]) + upd[h][c:] for h in heads]
        for h in heads:
            s_scr[b, hs[h], :] = s[h]

    def per_step(i, carry):
        _interleave([one_batch(i * GDN_BATCH_INTERLEAVE + j, j) for j in range(GDN_BATCH_INTERLEAVE)])
        return carry

    lax.fori_loop(0, nb // GDN_BATCH_INTERLEAVE, per_step, None)

    @pl.when(t == pl.num_programs(0) - 1)
    def _():
        sfin_ref[...] = s_scr[...]
        cnew_ref[...] = prev[:, base - keep:base, :]


def _gdn_sample_kernel(qkv_ref, small_ref, cs_ref, s0_ref, cw_ref, alog_ref, dtb_ref,
                       o_ref, sfin_ref, cnew_ref, xbuf):
    ns, c, _ = qkv_ref.shape
    r = ns * c
    keep = CONV_WIDTH - 1
    base = SUBLANES
    xbuf[:, base - keep:base, :] = cs_ref[...]
    xbuf[:, base:base + c, :] = qkv_ref[...]
    conv = _conv_silu(lambda j: xbuf[:, base - j:base - j + c, :], cw_ref).reshape(r, CONV_CH)
    cnew_ref[...] = xbuf[:, base + c - keep:base + c, :]

    same_le, same_lt, same = _block_masks(r, c)
    seq_of_row = lax.broadcasted_iota(jnp.int32, (r, GDN_DV), 0) // c
    (u, w, q_e, qk, k_end_t, dend), = _interleave([_gdn_prepass(
        conv, small_ref[...].reshape(r, LANES), alog_ref[...], dtb_ref[...], same_le, same_lt, same, c)])
    for h in range(GDN_HEADS):
        hs = slice(h * GDN_DK, (h + 1) * GDN_DK)
        v_parts, o_parts = [], []
        for s in range(ns):
            rows = slice(s * c, (s + 1) * c)
            ws = _dot(jnp.concatenate([w[h][rows], q_e[h][rows]], axis=0), s0_ref[s, hs, :])
            v_parts.append(u[h][rows] - ws[:c])
            o_parts.append(ws[c:])
        v_new = jnp.concatenate(v_parts, axis=0)
        o = jnp.concatenate(o_parts, axis=0) + _dot(qk[h], v_new)
        o_ref[:, h * GDN_DV:(h + 1) * GDN_DV] = o.astype(o_ref.dtype)
        for s in range(ns):
            upd = _dot(k_end_t[h], jnp.where(seq_of_row == s, v_new, 0.0))
            sfin_ref[s, hs, :] = s0_ref[s, hs, :] * jnp.exp(dend[h][s * c:s * c + 1, :]) + upd


def _gdn_prompt(qkv3, small3, conv_state, s0, conv_w, alog_v, dtb_v, c):
    bsz, t_len, _ = qkv3.shape
    rows = GDN_HEADS * GDN_DK
    keep = CONV_WIDTH - 1
    r = GROUP_ROWS
    full3 = lambda t: (0, 0, 0)
    const2 = lambda t: (0, 0)
    return pl.pallas_call(
        functools.partial(_gdn_prompt_kernel, c=c),
        grid=(t_len // r,),
        in_specs=[
            pl.BlockSpec((bsz, r, CONV_CH), lambda t: (0, t, 0)),
            pl.BlockSpec((bsz, r, LANES), lambda t: (0, t, 0)),
            pl.BlockSpec((bsz, keep, CONV_CH), full3),
            pl.BlockSpec((bsz, rows, GDN_DV), full3),
            pl.BlockSpec((CONV_WIDTH, CONV_CH), const2),
            pl.BlockSpec((2 * GDN_HEADS, 1), const2),
            pl.BlockSpec((2 * GDN_HEADS, 1), const2),
        ],
        out_specs=[
            pl.BlockSpec((bsz, r, GDN_VAL), lambda t: (0, t, 0)),
            pl.BlockSpec((bsz, rows, GDN_DV), full3),
            pl.BlockSpec((bsz, keep, CONV_CH), full3),
        ],
        out_shape=[
            jax.ShapeDtypeStruct((bsz, t_len, GDN_VAL), BF16),
            jax.ShapeDtypeStruct((bsz, rows, GDN_DV), F32),
            jax.ShapeDtypeStruct((bsz, keep, CONV_CH), F32),
        ],
        scratch_shapes=[pltpu.VMEM((GDN_BATCH_INTERLEAVE, SUBLANES + r, CONV_CH), F32),
                        pltpu.VMEM((bsz, SUBLANES, CONV_CH), F32),
                        pltpu.VMEM((bsz, rows, GDN_DV), F32)],
        compiler_params=pltpu.CompilerParams(dimension_semantics=("arbitrary",),
                                             vmem_limit_bytes=VMEM_LIMIT),
        name="gdn_prompt",
    )(qkv3, small3, conv_state, s0, conv_w, alog_v, dtb_v)


def _gdn_sample(qkv3, small3, conv_state, s0, conv_w, alog_v, dtb_v):
    bsz, c, _ = qkv3.shape
    rows = GDN_HEADS * GDN_DK
    keep = CONV_WIDTH - 1
    ns = GROUP_ROWS // c
    grp = lambda g: (g, 0, 0)
    const2 = lambda g: (0, 0)
    return pl.pallas_call(
        _gdn_sample_kernel,
        grid=(bsz // ns,),
        in_specs=[
            pl.BlockSpec((ns, c, CONV_CH), grp),
            pl.BlockSpec((ns, c, LANES), grp),
            pl.BlockSpec((ns, keep, CONV_CH), grp),
            pl.BlockSpec((ns, rows, GDN_DV), grp),
            pl.BlockSpec((CONV_WIDTH, CONV_CH), const2),
            pl.BlockSpec((2 * GDN_HEADS, 1), const2),
            pl.BlockSpec((2 * GDN_HEADS, 1), const2),
        ],
        out_specs=[
            pl.BlockSpec((ns * c, GDN_VAL), lambda g: (g, 0)),
            pl.BlockSpec((ns, rows, GDN_DV), grp),
            pl.BlockSpec((ns, keep, CONV_CH), grp),
        ],
        out_shape=[
            jax.ShapeDtypeStruct((bsz * c, GDN_VAL), BF16),
            jax.ShapeDtypeStruct((bsz, rows, GDN_DV), F32),
            jax.ShapeDtypeStruct((bsz, keep, CONV_CH), F32),
        ],
        scratch_shapes=[pltpu.VMEM((ns, 2 * SUBLANES, CONV_CH), F32)],
        compiler_params=pltpu.CompilerParams(dimension_semantics=("arbitrary",),
                                             vmem_limit_bytes=VMEM_LIMIT),
        name="gdn_sample",
    )(qkv3, small3, conv_state, s0, conv_w, alog_v, dtb_v)


def _head_norm_gate(o, silu_z, w):
    parts = []
    for h in range(o.shape[-1] // LANES):
        oh = o[:, h * LANES:(h + 1) * LANES]
        parts.append(oh * lax.rsqrt(jnp.mean(oh * oh, axis=-1, keepdims=True) + NORM_EPS) * w)
    return jnp.concatenate(parts, axis=-1) * silu_z


def _out_kernel(oa_ref, ob_ref, ga_ref, gb_ref, za_ref, zb_ref, x_ref, p_ref, anw_ref, bnw_ref,
                wua_ref, wub_ref, wout_ref, wpg_ref, wp_ref, fnw_ref, y_ref):
    f32 = lambda ref: ref[...].astype(F32)
    ya = _dot(_head_norm_gate(f32(oa_ref), f32(za_ref), anw_ref[...]), wua_ref[...])
    yb = _dot(_head_norm_gate(f32(ob_ref), f32(zb_ref), bnw_ref[...]), wub_ref[...])
    merged = f32(ga_ref) * ya + f32(gb_ref) * yb
    h1 = x_ref[...] + _dot(merged, wout_ref[...])
    h2 = h1 + _sigmoid(_dot(h1, wpg_ref[...])) * _dot(p_ref[...], wp_ref[...])
    y_ref[...] = h2 * lax.rsqrt(jnp.mean(h2 * h2, axis=-1, keepdims=True) + NORM_EPS) * fnw_ref[...]


def _out_stage(o_a, o_b, gz, x2d, p2d, anw, bnw, wua, wub, wout, wpg, wp, fnw):
    n = x2d.shape[0]
    tm = min(ROW_TILE, n)
    const = lambda i: (0, 0)
    return pl.pallas_call(
        _out_kernel,
        grid=(n // tm,),
        in_specs=[
            pl.BlockSpec((tm, GLA_VAL), lambda i: (i, 0)),
            pl.BlockSpec((tm, GDN_VAL), lambda i: (i, 0)),
            pl.BlockSpec((tm, D_MODEL), lambda i: (i, GZ_GATE_A // D_MODEL)),
            pl.BlockSpec((tm, D_MODEL), lambda i: (i, GZ_GATE_B // D_MODEL)),
            pl.BlockSpec((tm, GLA_VAL), lambda i: (i, GZ_Z_A // GLA_VAL)),
            pl.BlockSpec((tm, GDN_VAL), lambda i: (i, GZ_Z_B // GDN_VAL)),
            pl.BlockSpec((tm, D_MODEL), lambda i: (i, 0)),
            pl.BlockSpec((tm, PLE_DIM), lambda i: (i, 0)),
            pl.BlockSpec((1, GLA_DV), const),
            pl.BlockSpec((1, GDN_DV), const),
            pl.BlockSpec((GLA_VAL, D_MODEL), const),
            pl.BlockSpec((GDN_VAL, D_MODEL), const),
            pl.BlockSpec((D_MODEL, D_MODEL), const),
            pl.BlockSpec((D_MODEL, D_MODEL), const),
            pl.BlockSpec((PLE_DIM, D_MODEL), const),
            pl.BlockSpec((1, D_MODEL), const),
        ],
        out_specs=pl.BlockSpec((tm, D_MODEL), lambda i: (i, 0)),
        out_shape=jax.ShapeDtypeStruct((n, D_MODEL), F32),
        compiler_params=pltpu.CompilerParams(dimension_semantics=("arbitrary",),
                                             vmem_limit_bytes=VMEM_LIMIT),
        name="out_stage",
    )(o_a, o_b, gz, gz, gz, gz, x2d, p2d, anw, bnw, wua, wub, wout, wpg, wp, fnw)


def _in_offsets():
    offs = [0]
    for s in IN_SPLITS:
        offs.append(offs[-1] + s)
    return offs


def _regroup_kernel(wt_ref, o_ref):
    offs = _in_offsets()
    (q_a, _, _, g_a, z_a, qkv_b, a_b, _, z_b, gate_a, gate_b, end) = offs
    piece = lambda lo, hi: wt_ref[lo:hi, :].astype(BF16)
    o_ref[P_GLA_QKV:P_GLA_QKV + GLA_QKV, :] = piece(q_a, g_a)
    o_ref[P_QKV_B:P_QKV_B + CONV_CH, :] = piece(qkv_b, a_b)
    small = jnp.concatenate([wt_ref[g_a:z_a, :], wt_ref[a_b:z_b, :],
                             jnp.zeros((LANES - (z_a - g_a) - (z_b - a_b), wt_ref.shape[1]), F32)], axis=0)
    o_ref[P_SMALL:P_SMALL + LANES, :] = small.astype(BF16)
    o_ref[P_GATES + GZ_GATE_A:P_GATES + GZ_GATE_A + D_MODEL, :] = piece(gate_a, gate_b)
    o_ref[P_GATES + GZ_GATE_B:P_GATES + GZ_GATE_B + D_MODEL, :] = piece(gate_b, end)
    o_ref[P_GATES + GZ_Z_A:P_GATES + GZ_Z_A + GLA_VAL, :] = piece(z_a, qkv_b)
    o_ref[P_GATES + GZ_Z_B:P_GATES + GZ_Z_B + GDN_VAL, :] = piece(z_b, gate_a)


def _regroup_w_in(w_in_t):
    cols = 256
    return pl.pallas_call(
        _regroup_kernel,
        grid=(D_MODEL // cols,),
        in_specs=[pl.BlockSpec((None, w_in_t.shape[1], cols), lambda i: (0, 0, i))],
        out_specs=pl.BlockSpec((P_COLS, cols), lambda i: (0, i)),
        out_shape=jax.ShapeDtypeStruct((P_COLS, D_MODEL), BF16),
        compiler_params=pltpu.CompilerParams(dimension_semantics=("arbitrary",),
                                             vmem_limit_bytes=VMEM_LIMIT),
        name="regroup_w_in",
    )(w_in_t)


def _head_param_col(v):
    return jnp.zeros((2 * GDN_HEADS, 1), F32).at[:GDN_HEADS, 0].set(v.astype(F32))


def _trunk(x, p, s_gla, s_gdn, conv_state, wts):
    bsz, t_len, _ = x.shape
    c = min(PROMPT_CHUNK, t_len)
    n = bsz * t_len
    x2d = x.reshape(n, D_MODEL)
    p2d = p.reshape(n, PLE_DIM)
    gla_in, gdn_in, small, gz = _inproj(x2d, wts["norm_w"], wts["w_in_r"])
    gla_in3 = gla_in.reshape(bsz, t_len, GLA_QKV)
    gdn_in3 = gdn_in.reshape(bsz, t_len, CONV_CH)
    small3 = small.reshape(bsz, t_len, LANES)
    s_gla2 = s_gla.reshape(bsz, GLA_HEADS * GLA_DK, GLA_DV)
    s_gdn2 = s_gdn.reshape(bsz, GDN_HEADS * GDN_DK, GDN_DV)
    if t_len % GROUP_ROWS == 0:
        o_a, gla_fin = _gla_prompt(gla_in3, small3, s_gla2, wts["wgg"], wts["bgg"], c)
        o_b, gdn_fin, conv_new = _gdn_prompt(gdn_in3, small3, conv_state, s_gdn2, wts["conv_w"],
                                             wts["alog_v"], wts["dtb_v"], c)
    else:
        assert GROUP_ROWS % t_len == 0 and bsz % (GROUP_ROWS // t_len) == 0 and t_len >= CONV_WIDTH - 1
        o_a, gla_fin = _gla_sample(gla_in3, small3, s_gla2, wts["wgg"], wts["bgg"])
        o_b, gdn_fin, conv_new = _gdn_sample(gdn_in3, small3, conv_state, s_gdn2, wts["conv_w"],
                                             wts["alog_v"], wts["dtb_v"])
    y = _out_stage(o_a.reshape(n, GLA_VAL), o_b.reshape(n, GDN_VAL), gz, x2d, p2d, wts["anw"], wts["bnw"],
                   wts["wua"], wts["wub"], wts["wout"], wts["wpg"], wts["wp"], wts["fnw"])
    return (y.reshape(bsz, t_len, D_MODEL),
            gla_fin.reshape(1, bsz, GLA_HEADS, GLA_DK, GLA_DV),
            gdn_fin.reshape(1, bsz, GDN_HEADS, GDN_DK, GDN_DV),
            conv_new.reshape(1, bsz, CONV_WIDTH - 1, CONV_CH))


def kernel(x_prompt, x_sample, state_gla, state_gdn, state_conv, p_prompt, p_sample, norm_w, w_in,
           w_gla_gate, b_gla_gate, gla_norm_w, conv_w, gdn_a_log, gdn_dt_bias, gdn_norm_w,
           w_up_gla, w_up_gdn, w_out, w_ple_gate, w_ple, final_norm_w):
    wgg = jnp.zeros((LANES, GLA_KEY), F32).at[SM_G:SM_G + GLA_GATE_RANK].set(w_gla_gate[0]).astype(BF16)
    wts = {
        "norm_w": norm_w[0].reshape(1, D_MODEL),
        "w_in_r": _regroup_w_in(jnp.swapaxes(w_in, 1, 2)),
        "wgg": wgg,
        "bgg": b_gla_gate[0].reshape(1, GLA_KEY),
        "conv_w": conv_w[0],
        "alog_v": _head_param_col(gdn_a_log[0]),
        "dtb_v": _head_param_col(gdn_dt_bias[0]),
        "anw": gla_norm_w[0].reshape(1, GLA_DV),
        "bnw": gdn_norm_w[0].reshape(1, GDN_DV),
        "wua": w_up_gla[0].astype(BF16),
        "wub": w_up_gdn[0].astype(BF16),
        "wout": w_out[0].astype(BF16),
        "wpg": w_ple_gate[0].astype(BF16),
        "wp": w_ple[0].astype(BF16),
        "fnw": final_norm_w.reshape(1, D_MODEL),
    }
    bsz = x_prompt.shape[0]
    dt = x_prompt.dtype
    y_p, gla_p, gdn_p, conv_p = _trunk(
        x_prompt, p_prompt[0],
        jnp.zeros((bsz, GLA_HEADS, GLA_DK, GLA_DV), dt), jnp.zeros((bsz, GDN_HEADS, GDN_DK, GDN_DV), dt),
        jnp.zeros((bsz, CONV_WIDTH - 1, CONV_CH), dt), wts)
    y_s, gla_s, gdn_s, conv_s = _trunk(x_sample, p_sample[0], state_gla[0], state_gdn[0], state_conv[0], wts)
    return (y_p, y_s, gla_p, gdn_p, conv_p, gla_s, gdn_s, conv_s)
```

```python
import functools

import jax
import jax.numpy as jnp
from jax import lax
from jax.experimental import pallas as pl
from jax.experimental.pallas import tpu as pltpu

F32 = jnp.float32
BF16 = jnp.bfloat16

D_MODEL = 1024
PLE_DIM = 256
NORM_EPS = 1e-6
GLA_HEADS = 4
GLA_DK = 64
GLA_DV = 128
GLA_KEY = GLA_HEADS * GLA_DK
GLA_VAL = GLA_HEADS * GLA_DV
GLA_GATE_RANK = 16
GLA_GATE_TEMP = 16.0
GDN_HEADS = 4
GDN_DK = 128
GDN_DV = 128
GDN_KEY = GDN_HEADS * GDN_DK
GDN_VAL = GDN_HEADS * GDN_DV
CONV_WIDTH = 4
CONV_CH = 2 * GDN_KEY + GDN_VAL
IN_SPLITS = (GLA_KEY, GLA_KEY, GLA_VAL, GLA_GATE_RANK, GLA_VAL, CONV_CH, GDN_HEADS, GDN_HEADS,
             GDN_VAL, D_MODEL, D_MODEL)

LANES = 128
SUBLANES = 8

GLA_QKV = 2 * GLA_KEY + GLA_VAL
P_GLA_QKV = 0
P_QKV_B = P_GLA_QKV + GLA_QKV
P_SMALL = P_QKV_B + CONV_CH
P_GATES = P_SMALL + LANES
GZ_COLS = 2 * D_MODEL + GLA_VAL + GDN_VAL
P_COLS = P_GATES + GZ_COLS
GZ_GATE_A = 0
GZ_GATE_B = D_MODEL
GZ_Z_A = 2 * D_MODEL
GZ_Z_B = 2 * D_MODEL + GLA_VAL
SM_G = 0
SM_A = GLA_GATE_RANK
SM_B = GLA_GATE_RANK + GDN_HEADS

PROMPT_CHUNK = 64
GROUP_ROWS = 128
BATCH_INTERLEAVE = 8
GDN_BATCH_INTERLEAVE = 8
INVERSE_F32_POWERS = 2
ROW_TILE = 512
VMEM_LIMIT = 56 * 1024 * 1024


def _dot(a, b):
    return jnp.dot(a.astype(BF16), b.astype(BF16), preferred_element_type=F32)


def _dot_nt(a, b):
    return lax.dot_general(a.astype(BF16), b.astype(BF16), (((1,), (1,)), ((), ())),
                           preferred_element_type=F32)


def _split2(x):
    h1 = x.astype(BF16)
    return h1, (x - h1.astype(F32)).astype(BF16)


def _dot_f32(a, b):
    a1, a2 = _split2(a)
    b1, b2 = _split2(b)
    d = functools.partial(jnp.dot, preferred_element_type=F32)
    return d(a1, b1) + (d(a1, b2) + d(a2, b1))


def _cumsum_rows(tri, x):
    x1, x2 = _split2(x)
    d = functools.partial(jnp.dot, preferred_element_type=F32)
    return d(tri, x1) + d(tri, x2)


def _softplus(x):
    return jnp.maximum(x, 0.0) + jnp.log(1.0 + jnp.exp(-jnp.abs(x)))


def _sigmoid(x):
    return 1.0 / (1.0 + jnp.exp(-x))


def _silu(x):
    return x * _sigmoid(x)


def _interleave(emitters):
    results = [None] * len(emitters)
    live = list(range(len(emitters)))
    while live:
        for i in list(live):
            try:
                next(emitters[i])
            except StopIteration as stop:
                results[i] = stop.value
                live.remove(i)
    return results


def _block_masks(r, block):
    row = lax.broadcasted_iota(jnp.int32, (r, r), 0)
    col = lax.broadcasted_iota(jnp.int32, (r, r), 1)
    same = (row // block) == (col // block)
    return same & (row >= col), same & (row > col), same


INPROJ_COL_STEP = 512


def _inproj_kernel(x_ref, nw_ref, w_ref, gla_ref, gdn_ref, small_ref, gz_ref):
    x = x_ref[...]
    xn = x * lax.rsqrt(jnp.mean(x * x, axis=-1, keepdims=True) + NORM_EPS) * nw_ref[...]
    xb = xn.astype(BF16)

    def emit(o_ref, w0, width, act=None, o0=0):
        for c0 in range(0, width, INPROJ_COL_STEP):
            c1 = min(c0 + INPROJ_COL_STEP, width)
            res = _dot_nt(xb, w_ref[w0 + c0:w0 + c1, :])
            o_ref[:, o0 + c0:o0 + c1] = (res if act is None else act(res)).astype(o_ref.dtype)

    emit(gla_ref, P_GLA_QKV, GLA_QKV)
    emit(gdn_ref, P_QKV_B, CONV_CH)
    emit(small_ref, P_SMALL, LANES)
    emit(gz_ref, P_GATES + GZ_GATE_A, 2 * D_MODEL, act=_sigmoid, o0=GZ_GATE_A)
    emit(gz_ref, P_GATES + GZ_Z_A, GLA_VAL + GDN_VAL, act=_silu, o0=GZ_Z_A)


def _inproj(x2d, norm_w, w_in_r):
    n = x2d.shape[0]
    tm = min(ROW_TILE, n)
    rows = lambda i: (i, 0)
    return pl.pallas_call(
        _inproj_kernel,
        grid=(n // tm,),
        in_specs=[
            pl.BlockSpec((tm, D_MODEL), rows),
            pl.BlockSpec((1, D_MODEL), lambda i: (0, 0)),
            pl.BlockSpec((P_COLS, D_MODEL), lambda i: (0, 0), pipeline_mode=pl.Buffered(1)),
        ],
        out_specs=[
            pl.BlockSpec((tm, GLA_QKV), rows),
            pl.BlockSpec((tm, CONV_CH), rows),
            pl.BlockSpec((tm, LANES), rows),
            pl.BlockSpec((tm, GZ_COLS), rows),
        ],
        out_shape=[
            jax.ShapeDtypeStruct((n, GLA_QKV), F32),
            jax.ShapeDtypeStruct((n, CONV_CH), F32),
            jax.ShapeDtypeStruct((n, LANES), F32),
            jax.ShapeDtypeStruct((n, GZ_COLS), BF16),
        ],
        compiler_params=pltpu.CompilerParams(dimension_semantics=("arbitrary",),
                                             vmem_limit_bytes=VMEM_LIMIT),
        name="inproj",
    )(x2d, norm_w, w_in_r)


GLA_PAIRS = GLA_HEADS // 2


def _gla_prepass(q, k, v, sm, wgg, bgg, same_le, same):
    r = q.shape[0]
    heads = range(GLA_HEADS)
    pairs = range(GLA_PAIRS)
    pre = _dot(sm, wgg) + bgg
    yield
    gk = (jnp.minimum(pre, 0.0) - jnp.log(1.0 + jnp.exp(-jnp.abs(pre)))) * (1.0 / GLA_GATE_TEMP)
    sums = _cumsum_rows(jnp.concatenate([jnp.where(same_le, 1.0, 0.0).astype(BF16),
                                         jnp.where(same, 1.0, 0.0).astype(BF16)], axis=0), gk)
    yield
    bcum = sums[:r]
    bend = sums[r:]
    q_e = q * jnp.exp(bcum) * (GLA_DK ** -0.5)
    k_e = k * jnp.exp(-bcum)
    lane = lax.broadcasted_iota(jnp.int32, (r, LANES), 1)
    in_head = [lane < GLA_DK, lane >= GLA_DK]
    pl_ = [slice(p * LANES, (p + 1) * LANES) for p in pairs]
    qm = [jnp.where(in_head[h % 2], q_e[:, pl_[h // 2]], 0.0) for h in heads]
    att = [jnp.where(same_le, _dot_nt(qm[h], k_e[:, pl_[h // 2]]), 0.0) for h in heads]
    yield
    o_intra = [_dot(att[h], v[h]) for h in heads]
    k_end = k * jnp.exp(bend - bcum)
    k_end_t = [k_end[:, pl_[p]].T for p in pairs]
    bend_t = [bend[:, pl_[p]].T for p in pairs]
    yield
    return qm, o_intra, k_end_t, bend_t


def _gla_prompt_kernel(qkv_ref, small_ref, s0_ref, wgg_ref, bgg_ref, o_ref, sfin_ref, s_scr, *, c):
    t = pl.program_id(0)
    nb, r, _ = qkv_ref.shape

    @pl.when(t == 0)
    def _():
        s_scr[...] = s0_ref[...]

    same_le, _, same = _block_masks(r, c)
    zeros = jnp.zeros((c, GLA_DV), F32)
    n_sub = r // c
    heads = range(GLA_HEADS)
    pairs = range(GLA_PAIRS)

    def one_batch(b):
        q = qkv_ref[b, :, 0:GLA_KEY]
        k = qkv_ref[b, :, GLA_KEY:2 * GLA_KEY]
        v = [qkv_ref[b, :, 2 * GLA_KEY + h * GLA_DV:2 * GLA_KEY + (h + 1) * GLA_DV] for h in heads]
        qm, o_intra, k_end_t, bend_t = yield from _gla_prepass(
            q, k, v, small_ref[b], wgg_ref[...], bgg_ref[...], same_le, same)
        s = [s_scr[b, p * LANES:(p + 1) * LANES, :] for p in pairs]
        for i in range(n_sub):
            rows = slice(i * c, (i + 1) * c)
            ws = [_dot(jnp.concatenate([qm[2 * p][rows], qm[2 * p + 1][rows]], axis=0), s[p]) for p in pairs]
            padded = [jnp.concatenate([zeros] * i + [v[h][rows]] + [zeros] * (n_sub - 1 - i), axis=0)
                      for h in heads]
            upd = [_dot(k_end_t[h // 2][(h % 2) * GLA_DK:(h % 2 + 1) * GLA_DK, :], padded[h]) for h in heads]
            yield
            for h in heads:
                o_ref[b, rows, h * GLA_DV:(h + 1) * GLA_DV] = (
                    o_intra[h][rows] + ws[h // 2][(h % 2) * c:(h % 2 + 1) * c]).astype(o_ref.dtype)
            s = [s[p] * jnp.exp(bend_t[p][:, i * c:i * c + 1])
                 + jnp.concatenate([upd[2 * p], upd[2 * p + 1]], axis=0) for p in pairs]
        for p in pairs:
            s_scr[b, p * LANES:(p + 1) * LANES, :] = s[p]

    def per_step(i, carry):
        _interleave([one_batch(i * BATCH_INTERLEAVE + j) for j in range(BATCH_INTERLEAVE)])
        return carry

    lax.fori_loop(0, nb // BATCH_INTERLEAVE, per_step, None)

    @pl.when(t == pl.num_programs(0) - 1)
    def _():
        sfin_ref[...] = s_scr[...]


def _gla_sample_kernel(qkv_ref, small_ref, s0_ref, wgg_ref, bgg_ref, o_ref, sfin_ref):
    ns, c, _ = qkv_ref.shape
    r = ns * c
    heads = range(GLA_HEADS)
    same_le, _, same = _block_masks(r, c)
    q = qkv_ref[:, :, 0:GLA_KEY].reshape(r, GLA_KEY)
    k = qkv_ref[:, :, GLA_KEY:2 * GLA_KEY].reshape(r, GLA_KEY)
    v = [qkv_ref[:, :, 2 * GLA_KEY + h * GLA_DV:2 * GLA_KEY + (h + 1) * GLA_DV].reshape(r, GLA_DV)
         for h in heads]
    (qm, o_intra, k_end_t, bend_t), = _interleave([_gla_prepass(
        q, k, v, small_ref[...].reshape(r, LANES), wgg_ref[...], bgg_ref[...], same_le, same)])
    seq_of_row = lax.broadcasted_iota(jnp.int32, (r, GLA_DV), 0) // c
    for p in range(GLA_PAIRS):
        ps = slice(p * LANES, (p + 1) * LANES)
        inter = [[], []]
        for s in range(ns):
            rows = slice(s * c, (s + 1) * c)
            ws = _dot(jnp.concatenate([qm[2 * p][rows], qm[2 * p + 1][rows]], axis=0), s0_ref[s, ps, :])
            inter[0].append(ws[:c])
            inter[1].append(ws[c:])
        for hh in range(2):
            h = 2 * p + hh
            o = o_intra[h] + jnp.concatenate(inter[hh], axis=0)
            o_ref[:, h * GLA_DV:(h + 1) * GLA_DV] = o.astype(o_ref.dtype)
        for s in range(ns):
            upd = [_dot(k_end_t[p][hh * GLA_DK:(hh + 1) * GLA_DK, :],
                        jnp.where(seq_of_row == s, v[2 * p + hh], 0.0)) for hh in range(2)]
            sfin_ref[s, ps, :] = (s0_ref[s, ps, :] * jnp.exp(bend_t[p][:, s * c:s * c + 1])
                                  + jnp.concatenate(upd, axis=0))


def _gla_prompt(qkv3, small3, s0, wgg, bgg, c):
    bsz, t_len, _ = qkv3.shape
    rows = GLA_HEADS * GLA_DK
    r = GROUP_ROWS
    full3 = lambda t: (0, 0, 0)
    const2 = lambda t: (0, 0)
    return pl.pallas_call(
        functools.partial(_gla_prompt_kernel, c=c),
        grid=(t_len // r,),
        in_specs=[
            pl.BlockSpec((bsz, r, GLA_QKV), lambda t: (0, t, 0)),
            pl.BlockSpec((bsz, r, LANES), lambda t: (0, t, 0)),
            pl.BlockSpec((bsz, rows, GLA_DV), full3),
            pl.BlockSpec((LANES, GLA_KEY), const2),
            pl.BlockSpec((1, GLA_KEY), const2),
        ],
        out_specs=[
            pl.BlockSpec((bsz, r, GLA_VAL), lambda t: (0, t, 0)),
            pl.BlockSpec((bsz, rows, GLA_DV), full3),
        ],
        out_shape=[
            jax.ShapeDtypeStruct((bsz, t_len, GLA_VAL), BF16),
            jax.ShapeDtypeStruct((bsz, rows, GLA_DV), F32),
        ],
        scratch_shapes=[pltpu.VMEM((bsz, rows, GLA_DV), F32)],
        compiler_params=pltpu.CompilerParams(dimension_semantics=("arbitrary",),
                                             vmem_limit_bytes=VMEM_LIMIT),
        name="gla_prompt",
    )(qkv3, small3, s0, wgg, bgg)


def _gla_sample(qkv3, small3, s0, wgg, bgg):
    bsz, c, _ = qkv3.shape
    rows = GLA_HEADS * GLA_DK
    ns = GROUP_ROWS // c
    grp = lambda g: (g, 0, 0)
    const2 = lambda g: (0, 0)
    return pl.pallas_call(
        _gla_sample_kernel,
        grid=(bsz // ns,),
        in_specs=[
            pl.BlockSpec((ns, c, GLA_QKV), grp),
            pl.BlockSpec((ns, c, LANES), grp),
            pl.BlockSpec((ns, rows, GLA_DV), grp),
            pl.BlockSpec((LANES, GLA_KEY), const2),
            pl.BlockSpec((1, GLA_KEY), const2),
        ],
        out_specs=[
            pl.BlockSpec((ns * c, GLA_VAL), lambda g: (g, 0)),
            pl.BlockSpec((ns, rows, GLA_DV), grp),
        ],
        out_shape=[
            jax.ShapeDtypeStruct((bsz * c, GLA_VAL), BF16),
            jax.ShapeDtypeStruct((bsz, rows, GLA_DV), F32),
        ],
        compiler_params=pltpu.CompilerParams(dimension_semantics=("arbitrary",),
                                             vmem_limit_bytes=VMEM_LIMIT),
        name="gla_sample",
    )(qkv3, small3, s0, wgg, bgg)


def _block_unit_lower_inverse(a_list, block):
    r = a_list[0].shape[0]
    row = lax.broadcasted_iota(jnp.int32, (r, r), 0)
    col = lax.broadcasted_iota(jnp.int32, (r, r), 1)
    eye = jnp.where(row == col, 1.0, 0.0)
    xs = [eye - a for a in a_list]
    ps = [_dot_f32(a, a) if INVERSE_F32_POWERS > 2 else _dot(a, a) for a in a_list]
    n = 2
    d = functools.partial(jnp.dot, preferred_element_type=F32)
    while True:
        yield
        last = 2 * n >= block
        if n >= INVERSE_F32_POWERS:
            ms = [d(x.astype(BF16) if last else jnp.concatenate([p.astype(BF16), x.astype(BF16)], axis=0),
                    p.astype(BF16)) for p, x in zip(ps, xs)]
            if last:
                return [x + m for x, m in zip(xs, ms)]
            ps = [m[:r] for m in ms]
            xs = [x + m[r:] for x, m in zip(xs, ms)]
            n *= 2
            continue
        p_parts = [_split2(p) for p in ps]
        x_parts = [_split2(x) for x in xs]
        if last:
            out = []
            for x, (p1, p2), (x1, x2) in zip(xs, p_parts, x_parts):
                xp = d(x1, jnp.concatenate([p1, p2], axis=1))
                out.append(x + (xp[:, :r] + (xp[:, r:] + d(x2, p1))))
            return out
        ms = [d(jnp.concatenate([p1, x1], axis=0), jnp.concatenate([p1, p2], axis=1))
              for (p1, p2), (x1, _) in zip(p_parts, x_parts)]
        qs = [d(jnp.concatenate([p2, x2], axis=0), p1)
              for (p1, p2), (_, x2) in zip(p_parts, x_parts)]
        ss = [m[:, :r] + (m[:, r:] + q) for m, q in zip(ms, qs)]
        ps = [s[:r] for s in ss]
        xs = [x + s[r:] for x, s in zip(xs, ss)]
        n *= 2


def _gdn_decays(sm, alog_c, dtb_c, same_lt, same):
    r = sm.shape[0]
    heads = range(GDN_HEADS)
    ab = sm.T[SM_A:SM_A + 2 * GDN_HEADS, :]
    g8 = -jnp.exp(alog_c) * _softplus(ab + dtb_c)
    beta_c = _sigmoid(ab).T
    g1, g2 = _split2(g8)
    same_ge = same & jnp.logical_not(same_lt)
    sel = jnp.concatenate([jnp.where(same_ge, 1.0, 0.0).astype(BF16),
                           jnp.where(same, 1.0, 0.0).astype(BF16)], axis=1)
    sums = jnp.dot(jnp.concatenate([g1, g2], axis=0), sel, preferred_element_type=F32)
    sums = sums[:2 * GDN_HEADS] + sums[2 * GDN_HEADS:]
    dec_rows = sums[:, :r]
    dec_c = dec_rows.T
    dend_c = sums[:, r:].T
    return ([dec_c[:, h:h + 1] for h in heads], [dec_rows[h:h + 1, :] for h in heads],
            [dend_c[:, h:h + 1] for h in heads], [beta_c[:, GDN_HEADS + h:GDN_HEADS + h + 1] for h in heads])


def _gdn_prepass(conv, sm, alog_c, dtb_c, same_le, same_lt, same, block):
    r = conv.shape[0]
    heads = range(GDN_HEADS)
    dcol, drow, dend, beta = _gdn_decays(sm, alog_c, dtb_c, same_lt, same)
    gamma = [jnp.where(same_le, jnp.exp(jnp.where(same_le, dcol[h] - drow[h], 0.0)), 0.0) for h in heads]
    q = [conv[:, h * GDN_DK:(h + 1) * GDN_DK] for h in heads]
    k = [conv[:, GDN_KEY + h * GDN_DK:GDN_KEY + (h + 1) * GDN_DK] for h in heads]
    v = [conv[:, 2 * GDN_KEY + h * GDN_DV:2 * GDN_KEY + (h + 1) * GDN_DV] for h in heads]
    q = [x * lax.rsqrt(jnp.sum(x * x, axis=-1, keepdims=True) + NORM_EPS) * (GDN_DK ** -0.5) for x in q]
    k = [x * lax.rsqrt(jnp.sum(x * x, axis=-1, keepdims=True) + NORM_EPS) for x in k]
    yield
    kb = [k[h] * beta[h] for h in heads]
    vb = [v[h] * beta[h] for h in heads]
    kq = [_dot_nt(jnp.concatenate([kb[h], q[h]], axis=0), k[h]) for h in heads]
    yield
    a_mat = [jnp.where(same_lt, kq[h][:r] * gamma[h], 0.0) for h in heads]
    qk = [(kq[h][r:] * gamma[h]).astype(BF16) for h in heads]
    t_inv = yield from _block_unit_lower_inverse(a_mat, block)
    edec = [jnp.exp(dcol[h]) for h in heads]
    uw = [_dot(t_inv[h], jnp.concatenate([vb[h], kb[h] * edec[h]], axis=1)) for h in heads]
    q_e = [(q[h] * edec[h]).astype(BF16) for h in heads]
    k_end_t = [(k[h] * jnp.exp(dend[h] - dcol[h])).T.astype(BF16) for h in heads]
    yield
    u = [x[:, :GDN_DV] for x in uw]
    w = [x[:, GDN_DV:].astype(BF16) for x in uw]
    return u, w, q_e, qk, k_end_t, dend


def _conv_silu(win, cw_ref):
    conv = win(0) * cw_ref[CONV_WIDTH - 1:CONV_WIDTH, :]
    for j in range(1, CONV_WIDTH):
        conv = conv + win(j) * cw_ref[CONV_WIDTH - 1 - j:CONV_WIDTH - j, :]
    return _silu(conv)

def _gdn_prompt_kernel(qkv_ref, small_ref, cs_ref, s0_ref, cw_ref, alog_ref, dtb_ref,
                       o_ref, sfin_ref, cnew_ref, xbuf, prev, s_scr, *, c):
    t = pl.program_id(0)
    nb, r, _ = qkv_ref.shape
    keep = CONV_WIDTH - 1
    base = SUBLANES

    @pl.when(t == 0)
    def _():
        s_scr[...] = s0_ref[...]
        prev[:, base - keep:base, :] = cs_ref[...]

    same_le, same_lt, same = _block_masks(r, c)
    zeros = jnp.zeros((c, GDN_DV), F32)
    n_sub = r // c

    def one_batch(b, slot):
        xb = xbuf.at[slot]
        xb[base - keep:base, :] = prev[b, base - keep:base, :]
        xb[base:base + r, :] = qkv_ref[b]
        conv = _conv_silu(lambda j: xb[base - j:base - j + r, :], cw_ref)
        prev[b, base - keep:base, :] = xb[base + r - keep:base + r, :]
        yield
        u, w, q_e, qk, k_end_t, dend = yield from _gdn_prepass(
            conv, small_ref[b], alog_ref[...], dtb_ref[...], same_le, same_lt, same, c)
        heads = range(GDN_HEADS)
        hs = [slice(h * GDN_DK, (h + 1) * GDN_DK) for h in heads]
        s = [s_scr[b, hs[h], :] for h in heads]
        for i in range(n_sub):
            rows = slice(i * c, (i + 1) * c)
            ws = [_dot(jnp.concatenate([w[h][rows], q_e[h][rows]], axis=0), s[h]) for h in heads]
            yield
            v_new = [u[h][rows] - ws[h][:c] for h in heads]
            padded = [jnp.concatenate([zeros] * i + [v_new[h]] + [zeros] * (n_sub - 1 - i), axis=0)
                      for h in heads]
            upd = [_dot(jnp.concatenate([qk[h][rows], k_end_t[h]], axis=0), padded[h]) for h in heads]
            yield
            for h in heads:
                o_ref[b, rows, h * GDN_DV:(h + 1) * GDN_DV] = (ws[h][c:] + upd[h][:c]).astype(o_ref.dtype)
            s = [s[h] * jnp.exp(dend[h][i * c:i * c + 1, :]) + upd[h][c:] for h in heads]
        for h in heads:
            s_scr[b, hs[h], :] = s[h]

    def per_step(i, carry):
        _interleave([one_batch(i * GDN_BATCH_INTERLEAVE + j, j) for j in range(GDN_BATCH_INTERLEAVE)])
        return carry

    lax.fori_loop(0, nb // GDN_BATCH_INTERLEAVE, per_step, None)

    @pl.when(t == pl.num_programs(0) - 1)
    def _():
        sfin_ref[...] = s_scr[...]
        cnew_ref[...] = prev[:, base - keep:base, :]


def _gdn_sample_kernel(qkv_ref, small_ref, cs_ref, s0_ref, cw_ref, alog_ref, dtb_ref,
                       o_ref, sfin_ref, cnew_ref, xbuf):
    ns, c, _ = qkv_ref.shape
    r = ns * c
    keep = CONV_WIDTH - 1
    base = SUBLANES
    xbuf[:, base - keep:base, :] = cs_ref[...]
    xbuf[:, base:base + c, :] = qkv_ref[...]
    conv = _conv_silu(lambda j: xbuf[:, base - j:base - j + c, :], cw_ref).reshape(r, CONV_CH)
    cnew_ref[...] = xbuf[:, base + c - keep:base + c, :]

    same_le, same_lt, same = _block_masks(r, c)
    seq_of_row = lax.broadcasted_iota(jnp.int32, (r, GDN_DV), 0) // c
    (u, w, q_e, qk, k_end_t, dend), = _interleave([_gdn_prepass(
        conv, small_ref[...].reshape(r, LANES), alog_ref[...], dtb_ref[...], same_le, same_lt, same, c)])
    for h in range(GDN_HEADS):
        hs = slice(h * GDN_DK, (h + 1) * GDN_DK)
        v_parts, o_parts = [], []
        for s in range(ns):
            rows = slice(s * c, (s + 1) * c)
            ws = _dot(jnp.concatenate([w[h][rows], q_e[h][rows]], axis=0), s0_ref[s, hs, :])
            v_parts.append(u[h][rows] - ws[:c])
            o_parts.append(ws[c:])
        v_new = jnp.concatenate(v_parts, axis=0)
        o = jnp.concatenate(o_parts, axis=0) + _dot(qk[h], v_new)
        o_ref[:, h * GDN_DV:(h + 1) * GDN_DV] = o.astype(o_ref.dtype)
        for s in range(ns):
            upd = _dot(k_end_t[h], jnp.where(seq_of_row == s, v_new, 0.0))
            sfin_ref[s, hs, :] = s0_ref[s, hs, :] * jnp.exp(dend[h][s * c:s * c + 1, :]) + upd


def _gdn_prompt(qkv3, small3, conv_state, s0, conv_w, alog_v, dtb_v, c):
    bsz, t_len, _ = qkv3.shape
    rows = GDN_HEADS * GDN_DK
    keep = CONV_WIDTH - 1
    r = GROUP_ROWS
    full3 = lambda t: (0, 0, 0)
    const2 = lambda t: (0, 0)
    return pl.pallas_call(
        functools.partial(_gdn_prompt_kernel, c=c),
        grid=(t_len // r,),
        in_specs=[
            pl.BlockSpec((bsz, r, CONV_CH), lambda t: (0, t, 0)),
            pl.BlockSpec((bsz, r, LANES), lambda t: (0, t, 0)),
            pl.BlockSpec((bsz, keep, CONV_CH), full3),
            pl.BlockSpec((bsz, rows, GDN_DV), full3),
            pl.BlockSpec((CONV_WIDTH, CONV_CH), const2),
            pl.BlockSpec((2 * GDN_HEADS, 1), const2),
            pl.BlockSpec((2 * GDN_HEADS, 1), const2),
        ],
        out_specs=[
            pl.BlockSpec((bsz, r, GDN_VAL), lambda t: (0, t, 0)),
            pl.BlockSpec((bsz, rows, GDN_DV), full3),
            pl.BlockSpec((bsz, keep, CONV_CH), full3),
        ],
        out_shape=[
            jax.ShapeDtypeStruct((bsz, t_len, GDN_VAL), BF16),
            jax.ShapeDtypeStruct((bsz, rows, GDN_DV), F32),
            jax.ShapeDtypeStruct((bsz, keep, CONV_CH), F32),
        ],
        scratch_shapes=[pltpu.VMEM((GDN_BATCH_INTERLEAVE, SUBLANES + r, CONV_CH), F32),
                        pltpu.VMEM((bsz, SUBLANES, CONV_CH), F32),
                        pltpu.VMEM((bsz, rows, GDN_DV), F32)],
        compiler_params=pltpu.CompilerParams(dimension_semantics=("arbitrary",),
                                             vmem_limit_bytes=VMEM_LIMIT),
        name="gdn_prompt",
    )(qkv3, small3, conv_state, s0, conv_w, alog_v, dtb_v)


def _gdn_sample(qkv3, small3, conv_state, s0, conv_w, alog_v, dtb_v):
    bsz, c, _ = qkv3.shape
    rows = GDN_HEADS * GDN_DK
    keep = CONV_WIDTH - 1
    ns = GROUP_ROWS // c
    grp = lambda g: (g, 0, 0)
    const2 = lambda g: (0, 0)
    return pl.pallas_call(
        _gdn_sample_kernel,
        grid=(bsz // ns,),
        in_specs=[
            pl.BlockSpec((ns, c, CONV_CH), grp),
            pl.BlockSpec((ns, c, LANES), grp),
            pl.BlockSpec((ns, keep, CONV_CH), grp),
            pl.BlockSpec((ns, rows, GDN_DV), grp),
            pl.BlockSpec((CONV_WIDTH, CONV_CH), const2),
            pl.BlockSpec((2 * GDN_HEADS, 1), const2),
            pl.BlockSpec((2 * GDN_HEADS, 1), const2),
        ],
        out_specs=[
            pl.BlockSpec((ns * c, GDN_VAL), lambda g: (g, 0)),
            pl.BlockSpec((ns, rows, GDN_DV), grp),
            pl.BlockSpec((ns, keep, CONV_CH), grp),
        ],
        out_shape=[
            jax.ShapeDtypeStruct((bsz * c, GDN_VAL), BF16),
            jax.ShapeDtypeStruct((bsz, rows, GDN_DV), F32),
            jax.ShapeDtypeStruct((bsz, keep, CONV_CH), F32),
        ],
        scratch_shapes=[pltpu.VMEM((ns, 2 * SUBLANES, CONV_CH), F32)],
        compiler_params=pltpu.CompilerParams(dimension_semantics=("arbitrary",),
                                             vmem_limit_bytes=VMEM_LIMIT),
        name="gdn_sample",
    )(qkv3, small3, conv_state, s0, conv_w, alog_v, dtb_v)


def _head_norm_gate(o, silu_z, w):
    parts = []
    for h in range(o.shape[-1] // LANES):
        oh = o[:, h * LANES:(h + 1) * LANES]
        parts.append(oh * lax.rsqrt(jnp.mean(oh * oh, axis=-1, keepdims=True) + NORM_EPS) * w)
    return jnp.concatenate(parts, axis=-1) * silu_z


def _out_kernel(oa_ref, ob_ref, ga_ref, gb_ref, za_ref, zb_ref, x_ref, p_ref, anw_ref, bnw_ref,
                wua_ref, wub_ref, wout_ref, wpg_ref, wp_ref, fnw_ref, y_ref):
    f32 = lambda ref: ref[...].astype(F32)
    ya = _dot(_head_norm_gate(f32(oa_ref), f32(za_ref), anw_ref[...]), wua_ref[...])
    yb = _dot(_head_norm_gate(f32(ob_ref), f32(zb_ref), bnw_ref[...]), wub_ref[...])
    merged = f32(ga_ref) * ya + f32(gb_ref) * yb
    h1 = x_ref[...] + _dot(merged, wout_ref[...])
    h2 = h1 + _sigmoid(_dot(h1, wpg_ref[...])) * _dot(p_ref[...], wp_ref[...])
    y_ref[...] = h2 * lax.rsqrt(jnp.mean(h2 * h2, axis=-1, keepdims=True) + NORM_EPS) * fnw_ref[...]


def _out_stage(o_a, o_b, gz, x2d, p2d, anw, bnw, wua, wub, wout, wpg, wp, fnw):
    n = x2d.shape[0]
    tm = min(ROW_TILE, n)
    const = lambda i: (0, 0)
    return pl.pallas_call(
        _out_kernel,
        grid=(n // tm,),
        in_specs=[
            pl.BlockSpec((tm, GLA_VAL), lambda i: (i, 0)),
            pl.BlockSpec((tm, GDN_VAL), lambda i: (i, 0)),
            pl.BlockSpec((tm, D_MODEL), lambda i: (i, GZ_GATE_A // D_MODEL)),
            pl.BlockSpec((tm, D_MODEL), lambda i: (i, GZ_GATE_B // D_MODEL)),
            pl.BlockSpec((tm, GLA_VAL), lambda i: (i, GZ_Z_A // GLA_VAL)),
            pl.BlockSpec((tm, GDN_VAL), lambda i: (i, GZ_Z_B // GDN_VAL)),
            pl.BlockSpec((tm, D_MODEL), lambda i: (i, 0)),
            pl.BlockSpec((tm, PLE_DIM), lambda i: (i, 0)),
            pl.BlockSpec((1, GLA_DV), const),
            pl.BlockSpec((1, GDN_DV), const),
            pl.BlockSpec((GLA_VAL, D_MODEL), const),
            pl.BlockSpec((GDN_VAL, D_MODEL), const),
            pl.BlockSpec((D_MODEL, D_MODEL), const),
            pl.BlockSpec((D_MODEL, D_MODEL), const),
            pl.BlockSpec((PLE_DIM, D_MODEL), const),
            pl.BlockSpec((1, D_MODEL), const),
        ],
        out_specs=pl.BlockSpec((tm, D_MODEL), lambda i: (i, 0)),
        out_shape=jax.ShapeDtypeStruct((n, D_MODEL), F32),
        compiler_params=pltpu.CompilerParams(dimension_semantics=("arbitrary",),
                                             vmem_limit_bytes=VMEM_LIMIT),
        name="out_stage",
    )(o_a, o_b, gz, gz, gz, gz, x2d, p2d, anw, bnw, wua, wub, wout, wpg, wp, fnw)


def _in_offsets():
    offs = [0]
    for s in IN_SPLITS:
        offs.append(offs[-1] + s)
    return offs


def _regroup_kernel(wt_ref, o_ref):
    offs = _in_offsets()
    (q_a, _, _, g_a, z_a, qkv_b, a_b, _, z_b, gate_a, gate_b, end) = offs
    piece = lambda lo, hi: wt_ref[lo:hi, :].astype(BF16)
    o_ref[P_GLA_QKV:P_GLA_QKV + GLA_QKV, :] = piece(q_a, g_a)
    o_ref[P_QKV_B:P_QKV_B + CONV_CH, :] = piece(qkv_b, a_b)
    small = jnp.concatenate([wt_ref[g_a:z_a, :], wt_ref[a_b:z_b, :],
                             jnp.zeros((LANES - (z_a - g_a) - (z_b - a_b), wt_ref.shape[1]), F32)], axis=0)
    o_ref[P_SMALL:P_SMALL + LANES, :] = small.astype(BF16)
    o_ref[P_GATES + GZ_GATE_A:P_GATES + GZ_GATE_A + D_MODEL, :] = piece(gate_a, gate_b)
    o_ref[P_GATES + GZ_GATE_B:P_GATES + GZ_GATE_B + D_MODEL, :] = piece(gate_b, end)
    o_ref[P_GATES + GZ_Z_A:P_GATES + GZ_Z_A + GLA_VAL, :] = piece(z_a, qkv_b)
    o_ref[P_GATES + GZ_Z_B:P_GATES + GZ_Z_B + GDN_VAL, :] = piece(z_b, gate_a)


def _regroup_w_in(w_in_t):
    cols = 256
    return pl.pallas_call(
        _regroup_kernel,
        grid=(D_MODEL // cols,),
        in_specs=[pl.BlockSpec((None, w_in_t.shape[1], cols), lambda i: (0, 0, i))],
        out_specs=pl.BlockSpec((P_COLS, cols), lambda i: (0, i)),
        out_shape=jax.ShapeDtypeStruct((P_COLS, D_MODEL), BF16),
        compiler_params=pltpu.CompilerParams(dimension_semantics=("arbitrary",),
                                             vmem_limit_bytes=VMEM_LIMIT),
        name="regroup_w_in",
    )(w_in_t)


def _head_param_col(v):
    return jnp.zeros((2 * GDN_HEADS, 1), F32).at[:GDN_HEADS, 0].set(v.astype(F32))


def _trunk(x, p, s_gla, s_gdn, conv_state, wts):
    bsz, t_len, _ = x.shape
    c = min(PROMPT_CHUNK, t_len)
    n = bsz * t_len
    x2d = x.reshape(n, D_MODEL)
    p2d = p.reshape(n, PLE_DIM)
    gla_in, gdn_in, small, gz = _inproj(x2d, wts["norm_w"], wts["w_in_r"])
    gla_in3 = gla_in.reshape(bsz, t_len, GLA_QKV)
    gdn_in3 = gdn_in.reshape(bsz, t_len, CONV_CH)
    small3 = small.reshape(bsz, t_len, LANES)
    s_gla2 = s_gla.reshape(bsz, GLA_HEADS * GLA_DK, GLA_DV)
    s_gdn2 = s_gdn.reshape(bsz, GDN_HEADS * GDN_DK, GDN_DV)
    if t_len % GROUP_ROWS == 0:
        o_a, gla_fin = _gla_prompt(gla_in3, small3, s_gla2, wts["wgg"], wts["bgg"], c)
        o_b, gdn_fin, conv_new = _gdn_prompt(gdn_in3, small3, conv_state, s_gdn2, wts["conv_w"],
                                             wts["alog_v"], wts["dtb_v"], c)
    else:
        assert GROUP_ROWS % t_len == 0 and bsz % (GROUP_ROWS // t_len) == 0 and t_len >= CONV_WIDTH - 1
        o_a, gla_fin = _gla_sample(gla_in3, small3, s_gla2, wts["wgg"], wts["bgg"])
        o_b, gdn_fin, conv_new = _gdn_sample(gdn_in3, small3, conv_state, s_gdn2, wts["conv_w"],
                                             wts["alog_v"], wts["dtb_v"])
    y = _out_stage(o_a.reshape(n, GLA_VAL), o_b.reshape(n, GDN_VAL), gz, x2d, p2d, wts["anw"], wts["bnw"],
                   wts["wua"], wts["wub"], wts["wout"], wts["wpg"], wts["wp"], wts["fnw"])
    return (y.reshape(bsz, t_len, D_MODEL),
            gla_fin.reshape(1, bsz, GLA_HEADS, GLA_DK, GLA_DV),
            gdn_fin.reshape(1, bsz, GDN_HEADS, GDN_DK, GDN_DV),
            conv_new.reshape(1, bsz, CONV_WIDTH - 1, CONV_CH))


def kernel(x_prompt, x_sample, state_gla, state_gdn, state_conv, p_prompt, p_sample, norm_w, w_in,
           w_gla_gate, b_gla_gate, gla_norm_w, conv_w, gdn_a_log, gdn_dt_bias, gdn_norm_w,
           w_up_gla, w_up_gdn, w_out, w_ple_gate, w_ple, final_norm_w):
    wgg = jnp.zeros((LANES, GLA_KEY), F32).at[SM_G:SM_G + GLA_GATE_RANK].set(w_gla_gate[0]).astype(BF16)
    wts = {
        "norm_w": norm_w[0].reshape(1, D_MODEL),
        "w_in_r": _regroup_w_in(jnp.swapaxes(w_in, 1, 2)),
        "wgg": wgg,
        "bgg": b_gla_gate[0].reshape(1, GLA_KEY),
        "conv_w": conv_w[0],
        "alog_v": _head_param_col(gdn_a_log[0]),
        "dtb_v": _head_param_col(gdn_dt_bias[0]),
        "anw": gla_norm_w[0].reshape(1, GLA_DV),
        "bnw": gdn_norm_w[0].reshape(1, GDN_DV),
        "wua": w_up_gla[0].astype(BF16),
        "wub": w_up_gdn[0].astype(BF16),
        "wout": w_out[0].astype(BF16),
        "wpg": w_ple_gate[0].astype(BF16),
        "wp": w_ple[0].astype(BF16),
        "fnw": final_norm_w.reshape(1, D_MODEL),
    }
    bsz = x_prompt.shape[0]
    dt = x_prompt.dtype
    y_p, gla_p, gdn_p, conv_p = _trunk(
        x_prompt, p_prompt[0],
        jnp.zeros((bsz, GLA_HEADS, GLA_DK, GLA_DV), dt), jnp.zeros((bsz, GDN_HEADS, GDN_DK, GDN_DV), dt),
        jnp.zeros((bsz, CONV_WIDTH - 1, CONV_CH), dt), wts)
    y_s, gla_s, gdn_s, conv_s = _trunk(x_sample, p_sample[0], state_gla[0], state_gdn[0], state_conv[0], wts)
    return (y_p, y_s, gla_p, gdn_p, conv_p, gla_s, gdn_s, conv_s)
```

```python
import functools

import jax
import jax.numpy as jnp
from jax import lax
from jax.experimental import pallas as pl
from jax.experimental.pallas import tpu as pltpu

F32 = jnp.float32
BF16 = jnp.bfloat16

D_MODEL = 1024
PLE_DIM = 256
NORM_EPS = 1e-6
GLA_HEADS = 4
GLA_DK = 64
GLA_DV = 128
GLA_KEY = GLA_HEADS * GLA_DK
GLA_VAL = GLA_HEADS * GLA_DV
GLA_GATE_RANK = 16
GLA_GATE_TEMP = 16.0
GDN_HEADS = 4
GDN_DK = 128
GDN_DV = 128
GDN_KEY = GDN_HEADS * GDN_DK
GDN_VAL = GDN_HEADS * GDN_DV
CONV_WIDTH = 4
CONV_CH = 2 * GDN_KEY + GDN_VAL
IN_SPLITS = (GLA_KEY, GLA_KEY, GLA_VAL, GLA_GATE_RANK, GLA_VAL, CONV_CH, GDN_HEADS, GDN_HEADS,
             GDN_VAL, D_MODEL, D_MODEL)

LANES = 128
SUBLANES = 8

GLA_QKV = 2 * GLA_KEY + GLA_VAL
P_GLA_QKV = 0
P_QKV_B = P_GLA_QKV + GLA_QKV
P_SMALL = P_QKV_B + CONV_CH
P_GATES = P_SMALL + LANES
GZ_COLS = 2 * D_MODEL + GLA_VAL + GDN_VAL
P_COLS = P_GATES + GZ_COLS
GZ_GATE_A = 0
GZ_GATE_B = D_MODEL
GZ_Z_A = 2 * D_MODEL
GZ_Z_B = 2 * D_MODEL + GLA_VAL
SM_G = 0
SM_A = GLA_GATE_RANK
SM_B = GLA_GATE_RANK + GDN_HEADS

PROMPT_CHUNK = 64
GROUP_ROWS = 128
BATCH_INTERLEAVE = 8
GDN_BATCH_INTERLEAVE = 8
GDN_HEAD_GROUP = 4
INVERSE_BASE_BLOCK = 16
ROW_TILE = 512
VMEM_LIMIT = 56 * 1024 * 1024


def _dot(a, b):
    return jnp.dot(a.astype(BF16), b.astype(BF16), preferred_element_type=F32)


def _dot_nt(a, b):
    return lax.dot_general(a.astype(BF16), b.astype(BF16), (((1,), (1,)), ((), ())),
                           preferred_element_type=F32)


def _split2(x):
    h1 = x.astype(BF16)
    return h1, (x - h1.astype(F32)).astype(BF16)


def _cumsum_rows(tri, x):
    x1, x2 = _split2(x)
    d = functools.partial(jnp.dot, preferred_element_type=F32)
    return d(tri, x1) + d(tri, x2)


def _softplus(x):
    return jnp.maximum(x, 0.0) + jnp.log(1.0 + jnp.exp(-jnp.abs(x)))


def _sigmoid(x):
    return 1.0 / (1.0 + jnp.exp(-x))


def _silu(x):
    return x * _sigmoid(x)


def _interleave(emitters):
    results = [None] * len(emitters)
    live = list(range(len(emitters)))
    while live:
        for i in list(live):
            try:
                next(emitters[i])
            except StopIteration as stop:
                results[i] = stop.value
                live.remove(i)
    return results


def _block_masks(r, block):
    row = lax.broadcasted_iota(jnp.int32, (r, r), 0)
    col = lax.broadcasted_iota(jnp.int32, (r, r), 1)
    same = (row // block) == (col // block)
    return same & (row >= col), same & (row > col), same


INPROJ_COL_STEP = 512


def _inproj_kernel(x_ref, nw_ref, w_ref, gla_ref, gdn_ref, small_ref, gz_ref):
    x = x_ref[...]
    xn = x * lax.rsqrt(jnp.mean(x * x, axis=-1, keepdims=True) + NORM_EPS) * nw_ref[...]
    xb = xn.astype(BF16)

    def emit(o_ref, w0, width, act=None, o0=0):
        for c0 in range(0, width, INPROJ_COL_STEP):
            c1 = min(c0 + INPROJ_COL_STEP, width)
            res = _dot_nt(xb, w_ref[w0 + c0:w0 + c1, :])
            o_ref[:, o0 + c0:o0 + c1] = (res if act is None else act(res)).astype(o_ref.dtype)

    emit(gla_ref, P_GLA_QKV, GLA_QKV)
    emit(gdn_ref, P_QKV_B, CONV_CH)
    emit(small_ref, P_SMALL, LANES)
    emit(gz_ref, P_GATES + GZ_GATE_A, 2 * D_MODEL, act=_sigmoid, o0=GZ_GATE_A)
    emit(gz_ref, P_GATES + GZ_Z_A, GLA_VAL + GDN_VAL, act=_silu, o0=GZ_Z_A)


def _inproj(x2d, norm_w, w_in_r):
    n = x2d.shape[0]
    tm = min(ROW_TILE, n)
    rows = lambda i: (i, 0)
    return pl.pallas_call(
        _inproj_kernel,
        grid=(n // tm,),
        in_specs=[
            pl.BlockSpec((tm, D_MODEL), rows),
            pl.BlockSpec((1, D_MODEL), lambda i: (0, 0)),
            pl.BlockSpec((P_COLS, D_MODEL), lambda i: (0, 0), pipeline_mode=pl.Buffered(1)),
        ],
        out_specs=[
            pl.BlockSpec((tm, GLA_QKV), rows),
            pl.BlockSpec((tm, CONV_CH), rows),
            pl.BlockSpec((tm, LANES), rows),
            pl.BlockSpec((tm, GZ_COLS), rows),
        ],
        out_shape=[
            jax.ShapeDtypeStruct((n, GLA_QKV), F32),
            jax.ShapeDtypeStruct((n, CONV_CH), F32),
            jax.ShapeDtypeStruct((n, LANES), F32),
            jax.ShapeDtypeStruct((n, GZ_COLS), BF16),
        ],
        compiler_params=pltpu.CompilerParams(dimension_semantics=("arbitrary",),
                                             vmem_limit_bytes=VMEM_LIMIT),
        name="inproj",
    )(x2d, norm_w, w_in_r)


GLA_PAIRS = GLA_HEADS // 2


def _gla_prepass(q, k, v, sm, wgg, bgg, same_le, same):
    r = q.shape[0]
    heads = range(GLA_HEADS)
    pairs = range(GLA_PAIRS)
    pre = _dot(sm, wgg) + bgg
    yield
    gk = (jnp.minimum(pre, 0.0) - jnp.log(1.0 + jnp.exp(-jnp.abs(pre)))) * (1.0 / GLA_GATE_TEMP)
    sums = _cumsum_rows(jnp.concatenate([jnp.where(same_le, 1.0, 0.0).astype(BF16),
                                         jnp.where(same, 1.0, 0.0).astype(BF16)], axis=0), gk)
    yield
    bcum = sums[:r]
    bend = sums[r:]
    q_e = q * jnp.exp(bcum) * (GLA_DK ** -0.5)
    k_e = k * jnp.exp(-bcum)
    lane = lax.broadcasted_iota(jnp.int32, (r, LANES), 1)
    in_head = [lane < GLA_DK, lane >= GLA_DK]
    pl_ = [slice(p * LANES, (p + 1) * LANES) for p in pairs]
    qm = [jnp.where(in_head[h % 2], q_e[:, pl_[h // 2]], 0.0) for h in heads]
    att = [jnp.where(same_le, _dot_nt(qm[h], k_e[:, pl_[h // 2]]), 0.0) for h in heads]
    yield
    o_intra = [_dot(att[h], v[h]) for h in heads]
    k_end = k * jnp.exp(bend - bcum)
    k_end_t = [k_end[:, pl_[p]].T for p in pairs]
    bend_t = [bend[:, pl_[p]].T for p in pairs]
    yield
    return qm, o_intra, k_end_t, bend_t


def _gla_prompt_kernel(qkv_ref, small_ref, s0_ref, wgg_ref, bgg_ref, o_ref, sfin_ref, s_scr, *, c):
    t = pl.program_id(0)
    nb, r, _ = qkv_ref.shape

    @pl.when(t == 0)
    def _():
        s_scr[...] = s0_ref[...]

    same_le, _, same = _block_masks(r, c)
    zeros = jnp.zeros((c, GLA_DV), F32)
    n_sub = r // c
    heads = range(GLA_HEADS)
    pairs = range(GLA_PAIRS)

    def one_batch(b):
        q = qkv_ref[b, :, 0:GLA_KEY]
        k = qkv_ref[b, :, GLA_KEY:2 * GLA_KEY]
        v = [qkv_ref[b, :, 2 * GLA_KEY + h * GLA_DV:2 * GLA_KEY + (h + 1) * GLA_DV] for h in heads]
        qm, o_intra, k_end_t, bend_t = yield from _gla_prepass(
            q, k, v, small_ref[b], wgg_ref[...], bgg_ref[...], same_le, same)
        s = [s_scr[b, p * LANES:(p + 1) * LANES, :] for p in pairs]
        for i in range(n_sub):
            rows = slice(i * c, (i + 1) * c)
            ws = [_dot(jnp.concatenate([qm[2 * p][rows], qm[2 * p + 1][rows]], axis=0), s[p]) for p in pairs]
            padded = [jnp.concatenate([zeros] * i + [v[h][rows]] + [zeros] * (n_sub - 1 - i), axis=0)
                      for h in heads]
            upd = [_dot(k_end_t[h // 2][(h % 2) * GLA_DK:(h % 2 + 1) * GLA_DK, :], padded[h]) for h in heads]
            yield
            for h in heads:
                o_ref[b, rows, h * GLA_DV:(h + 1) * GLA_DV] = (
                    o_intra[h][rows] + ws[h // 2][(h % 2) * c:(h % 2 + 1) * c]).astype(o_ref.dtype)
            s = [s[p] * jnp.exp(bend_t[p][:, i * c:i * c + 1])
                 + jnp.concatenate([upd[2 * p], upd[2 * p + 1]], axis=0) for p in pairs]
        for p in pairs:
            s_scr[b, p * LANES:(p + 1) * LANES, :] = s[p]

    def per_step(i, carry):
        _interleave([one_batch(i * BATCH_INTERLEAVE + j) for j in range(BATCH_INTERLEAVE)])
        return carry

    lax.fori_loop(0, nb // BATCH_INTERLEAVE, per_step, None)

    @pl.when(t == pl.num_programs(0) - 1)
    def _():
        sfin_ref[...] = s_scr[...]


def _gla_sample_kernel(qkv_ref, small_ref, s0_ref, wgg_ref, bgg_ref, o_ref, sfin_ref):
    ns, c, _ = qkv_ref.shape
    r = ns * c
    heads = range(GLA_HEADS)
    same_le, _, same = _block_masks(r, c)
    q = qkv_ref[:, :, 0:GLA_KEY].reshape(r, GLA_KEY)
    k = qkv_ref[:, :, GLA_KEY:2 * GLA_KEY].reshape(r, GLA_KEY)
    v = [qkv_ref[:, :, 2 * GLA_KEY + h * GLA_DV:2 * GLA_KEY + (h + 1) * GLA_DV].reshape(r, GLA_DV)
         for h in heads]
    (qm, o_intra, k_end_t, bend_t), = _interleave([_gla_prepass(
        q, k, v, small_ref[...].reshape(r, LANES), wgg_ref[...], bgg_ref[...], same_le, same)])
    seq_of_row = lax.broadcasted_iota(jnp.int32, (r, GLA_DV), 0) // c
    for p in range(GLA_PAIRS):
        ps = slice(p * LANES, (p + 1) * LANES)
        inter = [[], []]
        for s in range(ns):
            rows = slice(s * c, (s + 1) * c)
            ws = _dot(jnp.concatenate([qm[2 * p][rows], qm[2 * p + 1][rows]], axis=0), s0_ref[s, ps, :])
            inter[0].append(ws[:c])
            inter[1].append(ws[c:])
        for hh in range(2):
            h = 2 * p + hh
            o = o_intra[h] + jnp.concatenate(inter[hh], axis=0)
            o_ref[:, h * GLA_DV:(h + 1) * GLA_DV] = o.astype(o_ref.dtype)
        for s in range(ns):
            upd = [_dot(k_end_t[p][hh * GLA_DK:(hh + 1) * GLA_DK, :],
                        jnp.where(seq_of_row == s, v[2 * p + hh], 0.0)) for hh in range(2)]
            sfin_ref[s, ps, :] = (s0_ref[s, ps, :] * jnp.exp(bend_t[p][:, s * c:s * c + 1])
                                  + jnp.concatenate(upd, axis=0))


def _gla_prompt(qkv3, small3, s0, wgg, bgg, c):
    bsz, t_len, _ = qkv3.shape
    rows = GLA_HEADS * GLA_DK
    r = GROUP_ROWS
    full3 = lambda t: (0, 0, 0)
    const2 = lambda t: (0, 0)
    return pl.pallas_call(
        functools.partial(_gla_prompt_kernel, c=c),
        grid=(t_len // r,),
        in_specs=[
            pl.BlockSpec((bsz, r, GLA_QKV), lambda t: (0, t, 0)),
            pl.BlockSpec((bsz, r, LANES), lambda t: (0, t, 0)),
            pl.BlockSpec((bsz, rows, GLA_DV), full3),
            pl.BlockSpec((LANES, GLA_KEY), const2),
            pl.BlockSpec((1, GLA_KEY), const2),
        ],
        out_specs=[
            pl.BlockSpec((bsz, r, GLA_VAL), lambda t: (0, t, 0)),
            pl.BlockSpec((bsz, rows, GLA_DV), full3),
        ],
        out_shape=[
            jax.ShapeDtypeStruct((bsz, t_len, GLA_VAL), BF16),
            jax.ShapeDtypeStruct((bsz, rows, GLA_DV), F32),
        ],
        scratch_shapes=[pltpu.VMEM((bsz, rows, GLA_DV), F32)],
        compiler_params=pltpu.CompilerParams(dimension_semantics=("arbitrary",),
                                             vmem_limit_bytes=VMEM_LIMIT),
        name="gla_prompt",
    )(qkv3, small3, s0, wgg, bgg)


def _gla_sample(qkv3, small3, s0, wgg, bgg):
    bsz, c, _ = qkv3.shape
    rows = GLA_HEADS * GLA_DK
    ns = GROUP_ROWS // c
    grp = lambda g: (g, 0, 0)
    const2 = lambda g: (0, 0)
    return pl.pallas_call(
        _gla_sample_kernel,
        grid=(bsz // ns,),
        in_specs=[
            pl.BlockSpec((ns, c, GLA_QKV), grp),
            pl.BlockSpec((ns, c, LANES), grp),
            pl.BlockSpec((ns, rows, GLA_DV), grp),
            pl.BlockSpec((LANES, GLA_KEY), const2),
            pl.BlockSpec((1, GLA_KEY), const2),
        ],
        out_specs=[
            pl.BlockSpec((ns * c, GLA_VAL), lambda g: (g, 0)),
            pl.BlockSpec((ns, rows, GLA_DV), grp),
        ],
        out_shape=[
            jax.ShapeDtypeStruct((bsz * c, GLA_VAL), BF16),
            jax.ShapeDtypeStruct((bsz, rows, GLA_DV), F32),
        ],
        compiler_params=pltpu.CompilerParams(dimension_semantics=("arbitrary",),
                                             vmem_limit_bytes=VMEM_LIMIT),
        name="gla_sample",
    )(qkv3, small3, s0, wgg, bgg)


def _block_unit_lower_inverse(a_list, block):
    r = a_list[0].shape[0]
    row = lax.broadcasted_iota(jnp.int32, (r, r), 0)
    col = lax.broadcasted_iota(jnp.int32, (r, r), 1)
    in_block = lambda n: (row // n) == (col // n)
    base = min(block, INVERSE_BASE_BLOCK)
    a_base = a_list if base == block else [jnp.where(in_block(base), a, 0.0) for a in a_list]
    xs = [jnp.where(row == col, 1.0, 0.0) - a for a in a_base]
    ps = [_dot(a, a) for a in a_base]
    n = 2
    while n < base:
        yield
        last = 2 * n >= base
        ms = [_dot(x if last else jnp.concatenate([p.astype(BF16), x.astype(BF16)], axis=0), p)
              for p, x in zip(ps, xs)]
        ps = [m[:r] for m in ms]
        xs = [x + m[-r:] for x, m in zip(xs, ms)]
        n *= 2
    n = base
    while n < block:
        yield
        between = in_block(2 * n) & jnp.logical_not(in_block(n))
        ts = [_dot(jnp.where(between, a, 0.0), x) for a, x in zip(a_list, xs)]
        yield
        xs = [x - _dot(x, t) for x, t in zip(xs, ts)]
        n *= 2
    return xs


def _gdn_decays(sm, alog_c, dtb_c, same_lt, same):
    r = sm.shape[0]
    heads = range(GDN_HEADS)
    ab = sm.T[SM_A:SM_A + 2 * GDN_HEADS, :]
    g8 = -jnp.exp(alog_c) * _softplus(ab + dtb_c)
    beta_c = _sigmoid(ab).T
    g1, g2 = _split2(g8)
    same_ge = same & jnp.logical_not(same_lt)
    sel = jnp.concatenate([jnp.where(same_ge, 1.0, 0.0).astype(BF16),
                           jnp.where(same, 1.0, 0.0).astype(BF16)], axis=1)
    sums = jnp.dot(jnp.concatenate([g1, g2], axis=0), sel, preferred_element_type=F32)
    sums = sums[:2 * GDN_HEADS] + sums[2 * GDN_HEADS:]
    dec_rows = sums[:, :r]
    dec_c = dec_rows.T
    dend_c = sums[:, r:].T
    return ([dec_c[:, h:h + 1] for h in heads], [dec_rows[h:h + 1, :] for h in heads],
            [dend_c[:, h:h + 1] for h in heads], [beta_c[:, GDN_HEADS + h:GDN_HEADS + h + 1] for h in heads])


def _gdn_head_cols(h):
    return [slice(part * GDN_KEY + h * GDN_DK, part * GDN_KEY + (h + 1) * GDN_DK) for part in range(3)]


def _gdn_prepass(qkv_of_head, sm, alog_c, dtb_c, same_le, same_lt, same, block, heads):
    dcol, drow, dend, beta = ([x[h] for h in heads] for x in _gdn_decays(sm, alog_c, dtb_c, same_lt, same))
    idx = range(len(heads))
    gamma = [jnp.where(same_le, jnp.exp(jnp.where(same_le, dcol[i] - drow[i], 0.0)), 0.0) for i in idx]
    q, k, v = (list(x) for x in zip(*[qkv_of_head(h) for h in heads]))
    r = q[0].shape[0]
    q = [x * lax.rsqrt(jnp.sum(x * x, axis=-1, keepdims=True) + NORM_EPS) * (GDN_DK ** -0.5) for x in q]
    k = [x * lax.rsqrt(jnp.sum(x * x, axis=-1, keepdims=True) + NORM_EPS) for x in k]
    yield
    kb = [k[i] * beta[i] for i in idx]
    vb = [v[i] * beta[i] for i in idx]
    kq = [_dot_nt(jnp.concatenate([kb[i], q[i]], axis=0), k[i]) for i in idx]
    yield
    a_mat = [jnp.where(same_lt, kq[i][:r] * gamma[i], 0.0) for i in idx]
    qk = [(kq[i][r:] * gamma[i]).astype(BF16) for i in idx]
    t_inv = yield from _block_unit_lower_inverse(a_mat, block)
    edec = [jnp.exp(dcol[i]) for i in idx]
    uw = [_dot(t_inv[i], jnp.concatenate([vb[i], kb[i] * edec[i]], axis=1)) for i in idx]
    q_e = [(q[i] * edec[i]).astype(BF16) for i in idx]
    k_end_t = [(k[i] * jnp.exp(dend[i] - dcol[i])).T.astype(BF16) for i in idx]
    yield
    u = [x[:, :GDN_DV] for x in uw]
    w = [x[:, GDN_DV:].astype(BF16) for x in uw]
    return u, w, q_e, qk, k_end_t, dend


def _conv_silu(win, cw_ref):
    conv = win(0) * cw_ref[CONV_WIDTH - 1:CONV_WIDTH, :]
    for j in range(1, CONV_WIDTH):
        conv = conv + win(j) * cw_ref[CONV_WIDTH - 1 - j:CONV_WIDTH - j, :]
    return _silu(conv)

def _gdn_prompt_kernel(qkv_ref, small_ref, cs_ref, s0_ref, cw_ref, alog_ref, dtb_ref,
                       o_ref, sfin_ref, cnew_ref, xbuf, prev, s_scr, *, c):
    t = pl.program_id(0)
    nb, r, _ = qkv_ref.shape
    keep = CONV_WIDTH - 1
    base = SUBLANES

    @pl.when(t == 0)
    def _():
        s_scr[...] = s0_ref[...]
        prev[:, base - keep:base, :] = cs_ref[...]

    same_le, same_lt, same = _block_masks(r, c)
    zeros = jnp.zeros((c, GDN_DV), F32)
    n_sub = r // c

    def one_batch(b, slot, heads, first):
        xb = xbuf.at[slot]
        if first:
            xb[base - keep:base, :] = prev[b, base - keep:base, :]
            xb[base:base + r, :] = qkv_ref[b]
            prev[b, base - keep:base, :] = xb[base + r - keep:base + r, :]

        def qkv_of_head(h):
            return [_conv_silu(lambda j: xb[base - j:base - j + r, cols], cw_ref.at[:, cols])
                    for cols in _gdn_head_cols(h)]

        u, w, q_e, qk, k_end_t, dend = yield from _gdn_prepass(
            qkv_of_head, small_ref[b], alog_ref[...], dtb_ref[...], same_le, same_lt, same, c, heads)
        idx = range(len(heads))
        hs = [slice(h * GDN_DK, (h + 1) * GDN_DK) for h in heads]
        s = [s_scr[b, hs[i], :] for i in idx]
        for ci in range(n_sub):
            rows = slice(ci * c, (ci + 1) * c)
            ws = [_dot(jnp.concatenate([w[i][rows], q_e[i][rows]], axis=0), s[i]) for i in idx]
            yield
            v_new = [u[i][rows] - ws[i][:c] for i in idx]
            padded = [jnp.concatenate([zeros] * ci + [v_new[i]] + [zeros] * (n_sub - 1 - ci), axis=0)
                      for i in idx]
            upd = [_dot(jnp.concatenate([qk[i][rows], k_end_t[i]], axis=0), padded[i]) for i in idx]
            yield
            for i in idx:
                o_ref[b, rows, hs[i]] = (ws[i][c:] + upd[i][:c]).astype(o_ref.dtype)
            s = [s[i] * jnp.exp(dend[i][ci * c:ci * c + 1, :]) + upd[i][c:] for i in idx]
        for i in idx:
            s_scr[b, hs[i], :] = s[i]

    def per_step(step, carry):
        for g in range(0, GDN_HEADS, GDN_HEAD_GROUP):
            heads = tuple(range(g, g + GDN_HEAD_GROUP))
            _interleave([one_batch(step * GDN_BATCH_INTERLEAVE + j, j, heads, g == 0)
                         for j in range(GDN_BATCH_INTERLEAVE)])
        return carry

    lax.fori_loop(0, nb // GDN_BATCH_INTERLEAVE, per_step, None)

    @pl.when(t == pl.num_programs(0) - 1)
    def _():
        sfin_ref[...] = s_scr[...]
        cnew_ref[...] = prev[:, base - keep:base, :]


def _gdn_sample_kernel(qkv_ref, small_ref, cs_ref, s0_ref, cw_ref, alog_ref, dtb_ref,
                       o_ref, sfin_ref, cnew_ref, xbuf):
    ns, c, _ = qkv_ref.shape
    r = ns * c
    keep = CONV_WIDTH - 1
    base = SUBLANES
    xbuf[:, base - keep:base, :] = cs_ref[...]
    xbuf[:, base:base + c, :] = qkv_ref[...]
    conv = _conv_silu(lambda j: xbuf[:, base - j:base - j + c, :], cw_ref).reshape(r, CONV_CH)
    cnew_ref[...] = xbuf[:, base + c - keep:base + c, :]

    same_le, same_lt, same = _block_masks(r, c)
    seq_of_row = lax.broadcasted_iota(jnp.int32, (r, GDN_DV), 0) // c
    (u, w, q_e, qk, k_end_t, dend), = _interleave([_gdn_prepass(
        lambda h: [conv[:, cols] for cols in _gdn_head_cols(h)], small_ref[...].reshape(r, LANES),
        alog_ref[...], dtb_ref[...], same_le, same_lt, same, c, tuple(range(GDN_HEADS)))])
    for h in range(GDN_HEADS):
        hs = slice(h * GDN_DK, (h + 1) * GDN_DK)
        v_parts, o_parts = [], []
        for s in range(ns):
            rows = slice(s * c, (s + 1) * c)
            ws = _dot(jnp.concatenate([w[h][rows], q_e[h][rows]], axis=0), s0_ref[s, hs, :])
            v_parts.append(u[h][rows] - ws[:c])
            o_parts.append(ws[c:])
        v_new = jnp.concatenate(v_parts, axis=0)
        o = jnp.concatenate(o_parts, axis=0) + _dot(qk[h], v_new)
        o_ref[:, h * GDN_DV:(h + 1) * GDN_DV] = o.astype(o_ref.dtype)
        for s in range(ns):
            upd = _dot(k_end_t[h], jnp.where(seq_of_row == s, v_new, 0.0))
            sfin_ref[s, hs, :] = s0_ref[s, hs, :] * jnp.exp(dend[h][s * c:s * c + 1, :]) + upd


def _gdn_prompt(qkv3, small3, conv_state, s0, conv_w, alog_v, dtb_v, c):
    bsz, t_len, _ = qkv3.shape
    rows = GDN_HEADS * GDN_DK
    keep = CONV_WIDTH - 1
    r = GROUP_ROWS
    full3 = lambda t: (0, 0, 0)
    const2 = lambda t: (0, 0)
    return pl.pallas_call(
        functools.partial(_gdn_prompt_kernel, c=c),
        grid=(t_len // r,),
        in_specs=[
            pl.BlockSpec((bsz, r, CONV_CH), lambda t: (0, t, 0)),
            pl.BlockSpec((bsz, r, LANES), lambda t: (0, t, 0)),
            pl.BlockSpec((bsz, keep, CONV_CH), full3),
            pl.BlockSpec((bsz, rows, GDN_DV), full3),
            pl.BlockSpec((CONV_WIDTH, CONV_CH), const2),
            pl.BlockSpec((2 * GDN_HEADS, 1), const2),
            pl.BlockSpec((2 * GDN_HEADS, 1), const2),
        ],
        out_specs=[
            pl.BlockSpec((bsz, r, GDN_VAL), lambda t: (0, t, 0)),
            pl.BlockSpec((bsz, rows, GDN_DV), full3),
            pl.BlockSpec((bsz, keep, CONV_CH), full3),
        ],
        out_shape=[
            jax.ShapeDtypeStruct((bsz, t_len, GDN_VAL), BF16),
            jax.ShapeDtypeStruct((bsz, rows, GDN_DV), F32),
            jax.ShapeDtypeStruct((bsz, keep, CONV_CH), F32),
        ],
        scratch_shapes=[pltpu.VMEM((GDN_BATCH_INTERLEAVE, SUBLANES + r, CONV_CH), F32),
                        pltpu.VMEM((bsz, SUBLANES, CONV_CH), F32),
                        pltpu.VMEM((bsz, rows, GDN_DV), F32)],
        compiler_params=pltpu.CompilerParams(dimension_semantics=("arbitrary",),
                                             vmem_limit_bytes=VMEM_LIMIT),
        name="gdn_prompt",
    )(qkv3, small3, conv_state, s0, conv_w, alog_v, dtb_v)


def _gdn_sample(qkv3, small3, conv_state, s0, conv_w, alog_v, dtb_v):
    bsz, c, _ = qkv3.shape
    rows = GDN_HEADS * GDN_DK
    keep = CONV_WIDTH - 1
    ns = GROUP_ROWS // c
    grp = lambda g: (g, 0, 0)
    const2 = lambda g: (0, 0)
    return pl.pallas_call(
        _gdn_sample_kernel,
        grid=(bsz // ns,),
        in_specs=[
            pl.BlockSpec((ns, c, CONV_CH), grp),
            pl.BlockSpec((ns, c, LANES), grp),
            pl.BlockSpec((ns, keep, CONV_CH), grp),
            pl.BlockSpec((ns, rows, GDN_DV), grp),
            pl.BlockSpec((CONV_WIDTH, CONV_CH), const2),
            pl.BlockSpec((2 * GDN_HEADS, 1), const2),
            pl.BlockSpec((2 * GDN_HEADS, 1), const2),
        ],
        out_specs=[
            pl.BlockSpec((ns * c, GDN_VAL), lambda g: (g, 0)),
            pl.BlockSpec((ns, rows, GDN_DV), grp),
            pl.BlockSpec((ns, keep, CONV_CH), grp),
        ],
        out_shape=[
            jax.ShapeDtypeStruct((bsz * c, GDN_VAL), BF16),
            jax.ShapeDtypeStruct((bsz, rows, GDN_DV), F32),
            jax.ShapeDtypeStruct((bsz, keep, CONV_CH), F32),
        ],
        scratch_shapes=[pltpu.VMEM((ns, 2 * SUBLANES, CONV_CH), F32)],
        compiler_params=pltpu.CompilerParams(dimension_semantics=("arbitrary",),
                                             vmem_limit_bytes=VMEM_LIMIT),
        name="gdn_sample",
    )(qkv3, small3, conv_state, s0, conv_w, alog_v, dtb_v)


def _head_norm_gate(o, silu_z, w):
    parts = []
    for h in range(o.shape[-1] // LANES):
        oh = o[:, h * LANES:(h + 1) * LANES]
        parts.append(oh * lax.rsqrt(jnp.mean(oh * oh, axis=-1, keepdims=True) + NORM_EPS) * w)
    return jnp.concatenate(parts, axis=-1) * silu_z


def _out_kernel(oa_ref, ob_ref, ga_ref, gb_ref, za_ref, zb_ref, x_ref, p_ref, anw_ref, bnw_ref,
                wua_ref, wub_ref, wout_ref, wpg_ref, wp_ref, fnw_ref, y_ref):
    f32 = lambda ref: ref[...].astype(F32)
    ya = _dot(_head_norm_gate(f32(oa_ref), f32(za_ref), anw_ref[...]), wua_ref[...])
    yb = _dot(_head_norm_gate(f32(ob_ref), f32(zb_ref), bnw_ref[...]), wub_ref[...])
    merged = f32(ga_ref) * ya + f32(gb_ref) * yb
    h1 = x_ref[...] + _dot(merged, wout_ref[...])
    h2 = h1 + _sigmoid(_dot(h1, wpg_ref[...])) * _dot(p_ref[...], wp_ref[...])
    y_ref[...] = h2 * lax.rsqrt(jnp.mean(h2 * h2, axis=-1, keepdims=True) + NORM_EPS) * fnw_ref[...]


def _out_stage(o_a, o_b, gz, x2d, p2d, anw, bnw, wua, wub, wout, wpg, wp, fnw):
    n = x2d.shape[0]
    tm = min(ROW_TILE, n)
    const = lambda i: (0, 0)
    return pl.pallas_call(
        _out_kernel,
        grid=(n // tm,),
        in_specs=[
            pl.BlockSpec((tm, GLA_VAL), lambda i: (i, 0)),
            pl.BlockSpec((tm, GDN_VAL), lambda i: (i, 0)),
            pl.BlockSpec((tm, D_MODEL), lambda i: (i, GZ_GATE_A // D_MODEL)),
            pl.BlockSpec((tm, D_MODEL), lambda i: (i, GZ_GATE_B // D_MODEL)),
            pl.BlockSpec((tm, GLA_VAL), lambda i: (i, GZ_Z_A // GLA_VAL)),
            pl.BlockSpec((tm, GDN_VAL), lambda i: (i, GZ_Z_B // GDN_VAL)),
            pl.BlockSpec((tm, D_MODEL), lambda i: (i, 0)),
            pl.BlockSpec((tm, PLE_DIM), lambda i: (i, 0)),
            pl.BlockSpec((1, GLA_DV), const),
            pl.BlockSpec((1, GDN_DV), const),
            pl.BlockSpec((GLA_VAL, D_MODEL), const),
            pl.BlockSpec((GDN_VAL, D_MODEL), const),
            pl.BlockSpec((D_MODEL, D_MODEL), const),
            pl.BlockSpec((D_MODEL, D_MODEL), const),
            pl.BlockSpec((PLE_DIM, D_MODEL), const),
            pl.BlockSpec((1, D_MODEL), const),
        ],
        out_specs=pl.BlockSpec((tm, D_MODEL), lambda i: (i, 0)),
        out_shape=jax.ShapeDtypeStruct((n, D_MODEL), F32),
        compiler_params=pltpu.CompilerParams(dimension_semantics=("arbitrary",),
                                             vmem_limit_bytes=VMEM_LIMIT),
        name="out_stage",
    )(o_a, o_b, gz, gz, gz, gz, x2d, p2d, anw, bnw, wua, wub, wout, wpg, wp, fnw)


def _in_offsets():
    offs = [0]
    for s in IN_SPLITS:
        offs.append(offs[-1] + s)
    return offs


def _regroup_kernel(wt_ref, o_ref):
    offs = _in_offsets()
    (q_a, _, _, g_a, z_a, qkv_b, a_b, _, z_b, gate_a, gate_b, end) = offs
    piece = lambda lo, hi: wt_ref[lo:hi, :].astype(BF16)
    o_ref[P_GLA_QKV:P_GLA_QKV + GLA_QKV, :] = piece(q_a, g_a)
    o_ref[P_QKV_B:P_QKV_B + CONV_CH, :] = piece(qkv_b, a_b)
    small = jnp.concatenate([wt_ref[g_a:z_a, :], wt_ref[a_b:z_b, :],
                             jnp.zeros((LANES - (z_a - g_a) - (z_b - a_b), wt_ref.shape[1]), F32)], axis=0)
    o_ref[P_SMALL:P_SMALL + LANES, :] = small.astype(BF16)
    o_ref[P_GATES + GZ_GATE_A:P_GATES + GZ_GATE_A + D_MODEL, :] = piece(gate_a, gate_b)
    o_ref[P_GATES + GZ_GATE_B:P_GATES + GZ_GATE_B + D_MODEL, :] = piece(gate_b, end)
    o_ref[P_GATES + GZ_Z_A:P_GATES + GZ_Z_A + GLA_VAL, :] = piece(z_a, qkv_b)
    o_ref[P_GATES + GZ_Z_B:P_GATES + GZ_Z_B + GDN_VAL, :] = piece(z_b, gate_a)


def _regroup_w_in(w_in_t):
    cols = 256
    return pl.pallas_call(
        _regroup_kernel,
        grid=(D_MODEL // cols,),
        in_specs=[pl.BlockSpec((None, w_in_t.shape[1], cols), lambda i: (0, 0, i))],
        out_specs=pl.BlockSpec((P_COLS, cols), lambda i: (0, i)),
        out_shape=jax.ShapeDtypeStruct((P_COLS, D_MODEL), BF16),
        compiler_params=pltpu.CompilerParams(dimension_semantics=("arbitrary",),
                                             vmem_limit_bytes=VMEM_LIMIT),
        name="regroup_w_in",
    )(w_in_t)


def _head_param_col(v):
    return jnp.zeros((2 * GDN_HEADS, 1), F32).at[:GDN_HEADS, 0].set(v.astype(F32))


def _trunk(x, p, s_gla, s_gdn, conv_state, wts):
    bsz, t_len, _ = x.shape
    c = min(PROMPT_CHUNK, t_len)
    n = bsz * t_len
    x2d = x.reshape(n, D_MODEL)
    p2d = p.reshape(n, PLE_DIM)
    gla_in, gdn_in, small, gz = _inproj(x2d, wts["norm_w"], wts["w_in_r"])
    gla_in3 = gla_in.reshape(bsz, t_len, GLA_QKV)
    gdn_in3 = gdn_in.reshape(bsz, t_len, CONV_CH)
    small3 = small.reshape(bsz, t_len, LANES)
    s_gla2 = s_gla.reshape(bsz, GLA_HEADS * GLA_DK, GLA_DV)
    s_gdn2 = s_gdn.reshape(bsz, GDN_HEADS * GDN_DK, GDN_DV)
    if t_len % GROUP_ROWS == 0:
        o_a, gla_fin = _gla_prompt(gla_in3, small3, s_gla2, wts["wgg"], wts["bgg"], c)
        o_b, gdn_fin, conv_new = _gdn_prompt(gdn_in3, small3, conv_state, s_gdn2, wts["conv_w"],
                                             wts["alog_v"], wts["dtb_v"], c)
    else:
        assert GROUP_ROWS % t_len == 0 and bsz % (GROUP_ROWS // t_len) == 0 and t_len >= CONV_WIDTH - 1
        o_a, gla_fin = _gla_sample(gla_in3, small3, s_gla2, wts["wgg"], wts["bgg"])
        o_b, gdn_fin, conv_new = _gdn_sample(gdn_in3, small3, conv_state, s_gdn2, wts["conv_w"],
                                             wts["alog_v"], wts["dtb_v"])
    y = _out_stage(o_a.reshape(n, GLA_VAL), o_b.reshape(n, GDN_VAL), gz, x2d, p2d, wts["anw"], wts["bnw"],
                   wts["wua"], wts["wub"], wts["wout"], wts["wpg"], wts["wp"], wts["fnw"])
    return (y.reshape(bsz, t_len, D_MODEL),
            gla_fin.reshape(1, bsz, GLA_HEADS, GLA_DK, GLA_DV),
            gdn_fin.reshape(1, bsz, GDN_HEADS, GDN_DK, GDN_DV),
            conv_new.reshape(1, bsz, CONV_WIDTH - 1, CONV_CH))


def kernel(x_prompt, x_sample, state_gla, state_gdn, state_conv, p_prompt, p_sample, norm_w, w_in,
           w_gla_gate, b_gla_gate, gla_norm_w, conv_w, gdn_a_log, gdn_dt_bias, gdn_norm_w,
           w_up_gla, w_up_gdn, w_out, w_ple_gate, w_ple, final_norm_w):
    wgg = jnp.zeros((LANES, GLA_KEY), F32).at[SM_G:SM_G + GLA_GATE_RANK].set(w_gla_gate[0]).astype(BF16)
    wts = {
        "norm_w": norm_w[0].reshape(1, D_MODEL),
        "w_in_r": _regroup_w_in(jnp.swapaxes(w_in, 1, 2)),
        "wgg": wgg,
        "bgg": b_gla_gate[0].reshape(1, GLA_KEY),
        "conv_w": conv_w[0],
        "alog_v": _head_param_col(gdn_a_log[0]),
        "dtb_v": _head_param_col(gdn_dt_bias[0]),
        "anw": gla_norm_w[0].reshape(1, GLA_DV),
        "bnw": gdn_norm_w[0].reshape(1, GDN_DV),
        "wua": w_up_gla[0].astype(BF16),
        "wub": w_up_gdn[0].astype(BF16),
        "wout": w_out[0].astype(BF16),
        "wpg": w_ple_gate[0].astype(BF16),
        "wp": w_ple[0].astype(BF16),
        "fnw": final_norm_w.reshape(1, D_MODEL),
    }
    bsz = x_prompt.shape[0]
    dt = x_prompt.dtype
    y_p, gla_p, gdn_p, conv_p = _trunk(
        x_prompt, p_prompt[0],
        jnp.zeros((bsz, GLA_HEADS, GLA_DK, GLA_DV), dt), jnp.zeros((bsz, GDN_HEADS, GDN_DK, GDN_DV), dt),
        jnp.zeros((bsz, CONV_WIDTH - 1, CONV_CH), dt), wts)
    y_s, gla_s, gdn_s, conv_s = _trunk(x_sample, p_sample[0], state_gla[0], state_gdn[0], state_conv[0], wts)
    return (y_p, y_s, gla_p, gdn_p, conv_p, gla_s, gdn_s, conv_s)
```

```python
import functools

import jax
import jax.numpy as jnp
from jax import lax
from jax.experimental import pallas as pl
from jax.experimental.pallas import tpu as pltpu

F32 = jnp.float32
BF16 = jnp.bfloat16

D_MODEL = 1024
PLE_DIM = 256
NORM_EPS = 1e-6
GLA_HEADS = 4
GLA_DK = 64
GLA_DV = 128
GLA_KEY = GLA_HEADS * GLA_DK
GLA_VAL = GLA_HEADS * GLA_DV
GLA_GATE_RANK = 16
GLA_GATE_TEMP = 16.0
GDN_HEADS = 4
GDN_DK = 128
GDN_DV = 128
GDN_KEY = GDN_HEADS * GDN_DK
GDN_VAL = GDN_HEADS * GDN_DV
CONV_WIDTH = 4
CONV_CH = 2 * GDN_KEY + GDN_VAL
IN_SPLITS = (GLA_KEY, GLA_KEY, GLA_VAL, GLA_GATE_RANK, GLA_VAL, CONV_CH, GDN_HEADS, GDN_HEADS,
             GDN_VAL, D_MODEL, D_MODEL)

LANES = 128
SUBLANES = 8

GLA_QKV = 2 * GLA_KEY + GLA_VAL
P_GLA_QKV = 0
P_QKV_B = P_GLA_QKV + GLA_QKV
P_SMALL = P_QKV_B + CONV_CH
P_GATES = P_SMALL + LANES
GZ_COLS = 2 * D_MODEL + GLA_VAL + GDN_VAL
P_COLS = P_GATES + GZ_COLS
GZ_GATE_A = 0
GZ_GATE_B = D_MODEL
GZ_Z_A = 2 * D_MODEL
GZ_Z_B = 2 * D_MODEL + GLA_VAL
SM_G = 0
SM_A = GLA_GATE_RANK
SM_B = GLA_GATE_RANK + GDN_HEADS

PROMPT_CHUNK = 64
GROUP_ROWS = 128
BATCH_INTERLEAVE = 8
GDN_BATCH_INTERLEAVE = 8
GDN_HEAD_GROUP = 4
INVERSE_BASE_BLOCK = 16
ROW_TILE = 512
VMEM_LIMIT = 56 * 1024 * 1024


def _dot(a, b):
    return jnp.dot(a.astype(BF16), b.astype(BF16), preferred_element_type=F32)


def _dot_nt(a, b):
    return lax.dot_general(a.astype(BF16), b.astype(BF16), (((1,), (1,)), ((), ())),
                           preferred_element_type=F32)


def _split2(x):
    h1 = x.astype(BF16)
    return h1, (x - h1.astype(F32)).astype(BF16)


def _cumsum_rows(tri, x):
    x1, x2 = _split2(x)
    d = functools.partial(jnp.dot, preferred_element_type=F32)
    return d(tri, x1) + d(tri, x2)


def _softplus(x):
    return jnp.maximum(x, 0.0) + jnp.log(1.0 + jnp.exp(-jnp.abs(x)))


def _sigmoid(x):
    return 1.0 / (1.0 + jnp.exp(-x))


def _silu(x):
    return x * _sigmoid(x)


def _interleave(emitters):
    results = [None] * len(emitters)
    live = list(range(len(emitters)))
    while live:
        for i in list(live):
            try:
                next(emitters[i])
            except StopIteration as stop:
                results[i] = stop.value
                live.remove(i)
    return results


def _block_masks(r, block):
    row = lax.broadcasted_iota(jnp.int32, (r, r), 0)
    col = lax.broadcasted_iota(jnp.int32, (r, r), 1)
    same = (row // block) == (col // block)
    return same & (row >= col), same & (row > col), same


INPROJ_COL_STEP = 512


def _inproj_kernel(x_ref, nw_ref, w_ref, gla_ref, gdn_ref, small_ref, gz_ref):
    x = x_ref[...]
    xn = x * lax.rsqrt(jnp.mean(x * x, axis=-1, keepdims=True) + NORM_EPS) * nw_ref[...]
    xb = xn.astype(BF16)

    def emit(o_ref, w0, width, act=None, o0=0):
        for c0 in range(0, width, INPROJ_COL_STEP):
            c1 = min(c0 + INPROJ_COL_STEP, width)
            res = _dot_nt(xb, w_ref[w0 + c0:w0 + c1, :])
            o_ref[:, o0 + c0:o0 + c1] = (res if act is None else act(res)).astype(o_ref.dtype)

    emit(gla_ref, P_GLA_QKV, GLA_QKV)
    emit(gdn_ref, P_QKV_B, CONV_CH)
    emit(small_ref, P_SMALL, LANES)
    emit(gz_ref, P_GATES + GZ_GATE_A, 2 * D_MODEL, act=_sigmoid, o0=GZ_GATE_A)
    emit(gz_ref, P_GATES + GZ_Z_A, GLA_VAL + GDN_VAL, act=_silu, o0=GZ_Z_A)


def _inproj(x2d, norm_w, w_in_r):
    n = x2d.shape[0]
    tm = min(ROW_TILE, n)
    rows = lambda i: (i, 0)
    return pl.pallas_call(
        _inproj_kernel,
        grid=(n // tm,),
        in_specs=[
            pl.BlockSpec((tm, D_MODEL), rows),
            pl.BlockSpec((1, D_MODEL), lambda i: (0, 0)),
            pl.BlockSpec((P_COLS, D_MODEL), lambda i: (0, 0), pipeline_mode=pl.Buffered(1)),
        ],
        out_specs=[
            pl.BlockSpec((tm, GLA_QKV), rows),
            pl.BlockSpec((tm, CONV_CH), rows),
            pl.BlockSpec((tm, LANES), rows),
            pl.BlockSpec((tm, GZ_COLS), rows),
        ],
        out_shape=[
            jax.ShapeDtypeStruct((n, GLA_QKV), F32),
            jax.ShapeDtypeStruct((n, CONV_CH), F32),
            jax.ShapeDtypeStruct((n, LANES), F32),
            jax.ShapeDtypeStruct((n, GZ_COLS), BF16),
        ],
        compiler_params=pltpu.CompilerParams(dimension_semantics=("arbitrary",),
                                             vmem_limit_bytes=VMEM_LIMIT),
        name="inproj",
    )(x2d, norm_w, w_in_r)


GLA_PAIRS = GLA_HEADS // 2
GLA_SUB_BLOCK = 16


def _gla_att_levels(r, block):
    row = lax.broadcasted_iota(jnp.int32, (r, r), 0)
    col = lax.broadcasted_iota(jnp.int32, (r, r), 1)
    sub = min(block, GLA_SUB_BLOCK)
    levels = [(None, ((row // sub) == (col // sub)) & (row >= col))]
    half = sub
    while half < block:
        levels.append((half, ((row // (2 * half)) == (col // (2 * half)))
                       & ((row // half) % 2 == 1) & ((col // half) % 2 == 0)))
        half *= 2
    return levels


def _rows_at(x, n, offset):
    return jnp.concatenate([jnp.broadcast_to(x[i + offset:i + offset + 1, :], (n, x.shape[1]))
                            for i in range(0, x.shape[0], n)], axis=0)


def _gla_prepass(q, k, v, sm, wgg, bgg, same_le, same, levels, block):
    r = q.shape[0]
    heads = range(GLA_HEADS)
    pairs = range(GLA_PAIRS)
    pre = _dot(sm, wgg) + bgg
    yield
    gk = (jnp.minimum(pre, 0.0) - jnp.log(1.0 + jnp.exp(-jnp.abs(pre)))) * (1.0 / GLA_GATE_TEMP)
    sums = _cumsum_rows(jnp.concatenate([jnp.where(same_le, 1.0, 0.0).astype(BF16),
                                         jnp.where(same, 1.0, 0.0).astype(BF16)], axis=0), gk)
    yield
    bcum = sums[:r]
    bend = sums[r:]
    bex = bcum - gk
    scale = GLA_DK ** -0.5
    q_e = q * jnp.exp(bcum) * scale
    lane = lax.broadcasted_iota(jnp.int32, (r, LANES), 1)
    in_head = [lane < GLA_DK, lane >= GLA_DK]
    pl_ = [slice(p * LANES, (p + 1) * LANES) for p in pairs]
    head_only = lambda x, h: jnp.where(in_head[h % 2], x[:, pl_[h // 2]], 0.0)
    qm = [head_only(q_e, h) for h in heads]
    sub = min(block, GLA_SUB_BLOCK)
    start = _rows_at(bex, sub, 0)
    groups = {None: (q * jnp.exp(bcum - start) * scale, [(levels[0][1], k * jnp.exp(start - bcum))])}
    for half, mask in levels[1:]:
        key = None if half == sub else half
        if key not in groups:
            groups[key] = (q * jnp.exp(bcum - _rows_at(bex, half, 0)) * scale, [])
        groups[key][1].append((mask, k * jnp.exp(_rows_at(bcum, half, half - 1) - bcum)))
    att = [jnp.zeros((r, r), F32)] * GLA_HEADS
    for q_l, parts in groups.values():
        for h in heads:
            keys = [k_l[:, pl_[h // 2]].astype(BF16) for _, k_l in parts]
            prod = _dot_nt(head_only(q_l, h), keys[0] if len(keys) == 1 else jnp.concatenate(keys, axis=0))
            for i, (mask, _) in enumerate(parts):
                att[h] = jnp.where(mask, prod[:, i * r:(i + 1) * r], att[h])
    yield
    o_intra = [_dot(att[h], v[h]) for h in heads]
    k_end = k * jnp.exp(bend - bcum)
    k_end_t = [k_end[:, pl_[p]].T for p in pairs]
    bend_t = [bend[:, pl_[p]].T for p in pairs]
    yield
    return qm, o_intra, k_end_t, bend_t


def _gla_prompt_kernel(qkv_ref, small_ref, s0_ref, wgg_ref, bgg_ref, o_ref, sfin_ref, s_scr, *, c):
    t = pl.program_id(0)
    nb, r, _ = qkv_ref.shape

    @pl.when(t == 0)
    def _():
        s_scr[...] = s0_ref[...]

    same_le, _, same = _block_masks(r, c)
    levels = _gla_att_levels(r, c)
    zeros = jnp.zeros((c, GLA_DV), F32)
    n_sub = r // c
    heads = range(GLA_HEADS)
    pairs = range(GLA_PAIRS)

    def one_batch(b):
        q = qkv_ref[b, :, 0:GLA_KEY]
        k = qkv_ref[b, :, GLA_KEY:2 * GLA_KEY]
        v = [qkv_ref[b, :, 2 * GLA_KEY + h * GLA_DV:2 * GLA_KEY + (h + 1) * GLA_DV] for h in heads]
        qm, o_intra, k_end_t, bend_t = yield from _gla_prepass(
            q, k, v, small_ref[b], wgg_ref[...], bgg_ref[...], same_le, same, levels, c)
        s = [s_scr[b, p * LANES:(p + 1) * LANES, :] for p in pairs]
        for i in range(n_sub):
            rows = slice(i * c, (i + 1) * c)
            ws = [_dot(jnp.concatenate([qm[2 * p][rows], qm[2 * p + 1][rows]], axis=0), s[p]) for p in pairs]
            padded = [jnp.concatenate([zeros] * i + [v[h][rows]] + [zeros] * (n_sub - 1 - i), axis=0)
                      for h in heads]
            upd = [_dot(k_end_t[h // 2][(h % 2) * GLA_DK:(h % 2 + 1) * GLA_DK, :], padded[h]) for h in heads]
            yield
            for h in heads:
                o_ref[b, rows, h * GLA_DV:(h + 1) * GLA_DV] = (
                    o_intra[h][rows] + ws[h // 2][(h % 2) * c:(h % 2 + 1) * c]).astype(o_ref.dtype)
            s = [s[p] * jnp.exp(bend_t[p][:, i * c:i * c + 1])
                 + jnp.concatenate([upd[2 * p], upd[2 * p + 1]], axis=0) for p in pairs]
        for p in pairs:
            s_scr[b, p * LANES:(p + 1) * LANES, :] = s[p]

    def per_step(i, carry):
        _interleave([one_batch(i * BATCH_INTERLEAVE + j) for j in range(BATCH_INTERLEAVE)])
        return carry

    lax.fori_loop(0, nb // BATCH_INTERLEAVE, per_step, None)

    @pl.when(t == pl.num_programs(0) - 1)
    def _():
        sfin_ref[...] = s_scr[...]


def _gla_sample_kernel(qkv_ref, small_ref, s0_ref, wgg_ref, bgg_ref, o_ref, sfin_ref):
    ns, c, _ = qkv_ref.shape
    r = ns * c
    heads = range(GLA_HEADS)
    same_le, _, same = _block_masks(r, c)
    levels = _gla_att_levels(r, c)
    q = qkv_ref[:, :, 0:GLA_KEY].reshape(r, GLA_KEY)
    k = qkv_ref[:, :, GLA_KEY:2 * GLA_KEY].reshape(r, GLA_KEY)
    v = [qkv_ref[:, :, 2 * GLA_KEY + h * GLA_DV:2 * GLA_KEY + (h + 1) * GLA_DV].reshape(r, GLA_DV)
         for h in heads]
    (qm, o_intra, k_end_t, bend_t), = _interleave([_gla_prepass(
        q, k, v, small_ref[...].reshape(r, LANES), wgg_ref[...], bgg_ref[...], same_le, same, levels, c)])
    seq_of_row = lax.broadcasted_iota(jnp.int32, (r, GLA_DV), 0) // c
    for p in range(GLA_PAIRS):
        ps = slice(p * LANES, (p + 1) * LANES)
        inter = [[], []]
        for s in range(ns):
            rows = slice(s * c, (s + 1) * c)
            ws = _dot(jnp.concatenate([qm[2 * p][rows], qm[2 * p + 1][rows]], axis=0), s0_ref[s, ps, :])
            inter[0].append(ws[:c])
            inter[1].append(ws[c:])
        for hh in range(2):
            h = 2 * p + hh
            o = o_intra[h] + jnp.concatenate(inter[hh], axis=0)
            o_ref[:, h * GLA_DV:(h + 1) * GLA_DV] = o.astype(o_ref.dtype)
        for s in range(ns):
            upd = [_dot(k_end_t[p][hh * GLA_DK:(hh + 1) * GLA_DK, :],
                        jnp.where(seq_of_row == s, v[2 * p + hh], 0.0)) for hh in range(2)]
            sfin_ref[s, ps, :] = (s0_ref[s, ps, :] * jnp.exp(bend_t[p][:, s * c:s * c + 1])
                                  + jnp.concatenate(upd, axis=0))


def _gla_prompt(qkv3, small3, s0, wgg, bgg, c):
    bsz, t_len, _ = qkv3.shape
    rows = GLA_HEADS * GLA_DK
    r = GROUP_ROWS
    full3 = lambda t: (0, 0, 0)
    const2 = lambda t: (0, 0)
    return pl.pallas_call(
        functools.partial(_gla_prompt_kernel, c=c),
        grid=(t_len // r,),
        in_specs=[
            pl.BlockSpec((bsz, r, GLA_QKV), lambda t: (0, t, 0)),
            pl.BlockSpec((bsz, r, LANES), lambda t: (0, t, 0)),
            pl.BlockSpec((bsz, rows, GLA_DV), full3),
            pl.BlockSpec((LANES, GLA_KEY), const2),
            pl.BlockSpec((1, GLA_KEY), const2),
        ],
        out_specs=[
            pl.BlockSpec((bsz, r, GLA_VAL), lambda t: (0, t, 0)),
            pl.BlockSpec((bsz, rows, GLA_DV), full3),
        ],
        out_shape=[
            jax.ShapeDtypeStruct((bsz, t_len, GLA_VAL), BF16),
            jax.ShapeDtypeStruct((bsz, rows, GLA_DV), F32),
        ],
        scratch_shapes=[pltpu.VMEM((bsz, rows, GLA_DV), F32)],
        compiler_params=pltpu.CompilerParams(dimension_semantics=("arbitrary",),
                                             vmem_limit_bytes=VMEM_LIMIT),
        name="gla_prompt",
    )(qkv3, small3, s0, wgg, bgg)


def _gla_sample(qkv3, small3, s0, wgg, bgg):
    bsz, c, _ = qkv3.shape
    rows = GLA_HEADS * GLA_DK
    ns = GROUP_ROWS // c
    grp = lambda g: (g, 0, 0)
    const2 = lambda g: (0, 0)
    return pl.pallas_call(
        _gla_sample_kernel,
        grid=(bsz // ns,),
        in_specs=[
            pl.BlockSpec((ns, c, GLA_QKV), grp),
            pl.BlockSpec((ns, c, LANES), grp),
            pl.BlockSpec((ns, rows, GLA_DV), grp),
            pl.BlockSpec((LANES, GLA_KEY), const2),
            pl.BlockSpec((1, GLA_KEY), const2),
        ],
        out_specs=[
            pl.BlockSpec((ns * c, GLA_VAL), lambda g: (g, 0)),
            pl.BlockSpec((ns, rows, GLA_DV), grp),
        ],
        out_shape=[
            jax.ShapeDtypeStruct((bsz * c, GLA_VAL), BF16),
            jax.ShapeDtypeStruct((bsz, rows, GLA_DV), F32),
        ],
        compiler_params=pltpu.CompilerParams(dimension_semantics=("arbitrary",),
                                             vmem_limit_bytes=VMEM_LIMIT),
        name="gla_sample",
    )(qkv3, small3, s0, wgg, bgg)


def _block_unit_lower_inverse(a_list, block):
    r = a_list[0].shape[0]
    row = lax.broadcasted_iota(jnp.int32, (r, r), 0)
    col = lax.broadcasted_iota(jnp.int32, (r, r), 1)
    in_block = lambda n: (row // n) == (col // n)
    base = min(block, INVERSE_BASE_BLOCK)
    a_base = a_list if base == block else [jnp.where(in_block(base), a, 0.0) for a in a_list]
    xs = [jnp.where(row == col, 1.0, 0.0) - a for a in a_base]
    ps = [_dot(a, a) for a in a_base]
    n = 2
    while n < base:
        yield
        last = 2 * n >= base
        ms = [_dot(x if last else jnp.concatenate([p.astype(BF16), x.astype(BF16)], axis=0), p)
              for p, x in zip(ps, xs)]
        ps = [m[:r] for m in ms]
        xs = [x + m[-r:] for x, m in zip(xs, ms)]
        n *= 2
    n = base
    while n < block:
        yield
        between = in_block(2 * n) & jnp.logical_not(in_block(n))
        ts = [_dot(jnp.where(between, a, 0.0), x) for a, x in zip(a_list, xs)]
        yield
        xs = [x - _dot(x, t) for x, t in zip(xs, ts)]
        n *= 2
    return xs


def _gdn_decays(sm, alog_c, dtb_c, same_lt, same):
    r = sm.shape[0]
    heads = range(GDN_HEADS)
    ab = sm.T[SM_A:SM_A + 2 * GDN_HEADS, :]
    g8 = -jnp.exp(alog_c) * _softplus(ab + dtb_c)
    beta_c = _sigmoid(ab).T
    g1, g2 = _split2(g8)
    same_ge = same & jnp.logical_not(same_lt)
    sel = jnp.concatenate([jnp.where(same_ge, 1.0, 0.0).astype(BF16),
                           jnp.where(same, 1.0, 0.0).astype(BF16)], axis=1)
    sums = jnp.dot(jnp.concatenate([g1, g2], axis=0), sel, preferred_element_type=F32)
    sums = sums[:2 * GDN_HEADS] + sums[2 * GDN_HEADS:]
    dec_rows = sums[:, :r]
    dec_c = dec_rows.T
    dend_c = sums[:, r:].T
    return ([dec_c[:, h:h + 1] for h in heads], [dec_rows[h:h + 1, :] for h in heads],
            [dend_c[:, h:h + 1] for h in heads], [beta_c[:, GDN_HEADS + h:GDN_HEADS + h + 1] for h in heads])


def _gdn_head_cols(h):
    return [slice(part * GDN_KEY + h * GDN_DK, part * GDN_KEY + (h + 1) * GDN_DK) for part in range(3)]


def _gdn_prepass(qkv_of_head, sm, alog_c, dtb_c, same_le, same_lt, same, block, heads):
    dcol, drow, dend, beta = ([x[h] for h in heads] for x in _gdn_decays(sm, alog_c, dtb_c, same_lt, same))
    idx = range(len(heads))
    gamma = [jnp.where(same_le, jnp.exp(jnp.where(same_le, dcol[i] - drow[i], 0.0)), 0.0) for i in idx]
    q, k, v = (list(x) for x in zip(*[qkv_of_head(h) for h in heads]))
    r = q[0].shape[0]
    q = [x * lax.rsqrt(jnp.sum(x * x, axis=-1, keepdims=True) + NORM_EPS) * (GDN_DK ** -0.5) for x in q]
    k = [x * lax.rsqrt(jnp.sum(x * x, axis=-1, keepdims=True) + NORM_EPS) for x in k]
    yield
    kb = [k[i] * beta[i] for i in idx]
    vb = [v[i] * beta[i] for i in idx]
    kq = [_dot_nt(jnp.concatenate([kb[i], q[i]], axis=0), k[i]) for i in idx]
    yield
    a_mat = [jnp.where(same_lt, kq[i][:r] * gamma[i], 0.0) for i in idx]
    qk = [(kq[i][r:] * gamma[i]).astype(BF16) for i in idx]
    t_inv = yield from _block_unit_lower_inverse(a_mat, block)
    edec = [jnp.exp(dcol[i]) for i in idx]
    uw = [_dot(t_inv[i], jnp.concatenate([vb[i], kb[i] * edec[i]], axis=1)) for i in idx]
    q_e = [(q[i] * edec[i]).astype(BF16) for i in idx]
    k_end_t = [(k[i] * jnp.exp(dend[i] - dcol[i])).T.astype(BF16) for i in idx]
    yield
    u = [x[:, :GDN_DV] for x in uw]
    w = [x[:, GDN_DV:].astype(BF16) for x in uw]
    return u, w, q_e, qk, k_end_t, dend


def _conv_silu(win, cw_ref):
    conv = win(0) * cw_ref[CONV_WIDTH - 1:CONV_WIDTH, :]
    for j in range(1, CONV_WIDTH):
        conv = conv + win(j) * cw_ref[CONV_WIDTH - 1 - j:CONV_WIDTH - j, :]
    return _silu(conv)

def _gdn_prompt_kernel(qkv_ref, small_ref, cs_ref, s0_ref, cw_ref, alog_ref, dtb_ref,
                       o_ref, sfin_ref, cnew_ref, xbuf, prev, s_scr, *, c):
    t = pl.program_id(0)
    nb, r, _ = qkv_ref.shape
    keep = CONV_WIDTH - 1
    base = SUBLANES

    @pl.when(t == 0)
    def _():
        s_scr[...] = s0_ref[...]
        prev[:, base - keep:base, :] = cs_ref[...]

    same_le, same_lt, same = _block_masks(r, c)
    zeros = jnp.zeros((c, GDN_DV), F32)
    n_sub = r // c

    def one_batch(b, slot, heads, first):
        xb = xbuf.at[slot]
        if first:
            xb[base - keep:base, :] = prev[b, base - keep:base, :]
            xb[base:base + r, :] = qkv_ref[b]
            prev[b, base - keep:base, :] = xb[base + r - keep:base + r, :]

        def qkv_of_head(h):
            return [_conv_silu(lambda j: xb[base - j:base - j + r, cols], cw_ref.at[:, cols])
                    for cols in _gdn_head_cols(h)]

        u, w, q_e, qk, k_end_t, dend = yield from _gdn_prepass(
            qkv_of_head, small_ref[b], alog_ref[...], dtb_ref[...], same_le, same_lt, same, c, heads)
        idx = range(len(heads))
        hs = [slice(h * GDN_DK, (h + 1) * GDN_DK) for h in heads]
        s = [s_scr[b, hs[i], :] for i in idx]
        for ci in range(n_sub):
            rows = slice(ci * c, (ci + 1) * c)
            ws = [_dot(jnp.concatenate([w[i][rows], q_e[i][rows]], axis=0), s[i]) for i in idx]
            yield
            v_new = [u[i][rows] - ws[i][:c] for i in idx]
            padded = [jnp.concatenate([zeros] * ci + [v_new[i]] + [zeros] * (n_sub - 1 - ci), axis=0)
                      for i in idx]
            upd = [_dot(jnp.concatenate([qk[i][rows], k_end_t[i]], axis=0), padded[i]) for i in idx]
            yield
            for i in idx:
                o_ref[b, rows, hs[i]] = (ws[i][c:] + upd[i][:c]).astype(o_ref.dtype)
            s = [s[i] * jnp.exp(dend[i][ci * c:ci * c + 1, :]) + upd[i][c:] for i in idx]
        for i in idx:
            s_scr[b, hs[i], :] = s[i]

    def per_step(step, carry):
        for g in range(0, GDN_HEADS, GDN_HEAD_GROUP):
            heads = tuple(range(g, g + GDN_HEAD_GROUP))
            _interleave([one_batch(step * GDN_BATCH_INTERLEAVE + j, j, heads, g == 0)
                         for j in range(GDN_BATCH_INTERLEAVE)])
        return carry

    lax.fori_loop(0, nb // GDN_BATCH_INTERLEAVE, per_step, None)

    @pl.when(t == pl.num_programs(0) - 1)
    def _():
        sfin_ref[...] = s_scr[...]
        cnew_ref[...] = prev[:, base - keep:base, :]


def _gdn_sample_kernel(qkv_ref, small_ref, cs_ref, s0_ref, cw_ref, alog_ref, dtb_ref,
                       o_ref, sfin_ref, cnew_ref, xbuf):
    ns, c, _ = qkv_ref.shape
    r = ns * c
    keep = CONV_WIDTH - 1
    base = SUBLANES
    xbuf[:, base - keep:base, :] = cs_ref[...]
    xbuf[:, base:base + c, :] = qkv_ref[...]
    conv = _conv_silu(lambda j: xbuf[:, base - j:base - j + c, :], cw_ref).reshape(r, CONV_CH)
    cnew_ref[...] = xbuf[:, base + c - keep:base + c, :]

    same_le, same_lt, same = _block_masks(r, c)
    seq_of_row = lax.broadcasted_iota(jnp.int32, (r, GDN_DV), 0) // c
    (u, w, q_e, qk, k_end_t, dend), = _interleave([_gdn_prepass(
        lambda h: [conv[:, cols] for cols in _gdn_head_cols(h)], small_ref[...].reshape(r, LANES),
        alog_ref[...], dtb_ref[...], same_le, same_lt, same, c, tuple(range(GDN_HEADS)))])
    for h in range(GDN_HEADS):
        hs = slice(h * GDN_DK, (h + 1) * GDN_DK)
        v_parts, o_parts = [], []
        for s in range(ns):
            rows = slice(s * c, (s + 1) * c)
            ws = _dot(jnp.concatenate([w[h][rows], q_e[h][rows]], axis=0), s0_ref[s, hs, :])
            v_parts.append(u[h][rows] - ws[:c])
            o_parts.append(ws[c:])
        v_new = jnp.concatenate(v_parts, axis=0)
        o = jnp.concatenate(o_parts, axis=0) + _dot(qk[h], v_new)
        o_ref[:, h * GDN_DV:(h + 1) * GDN_DV] = o.astype(o_ref.dtype)
        for s in range(ns):
            upd = _dot(k_end_t[h], jnp.where(seq_of_row == s, v_new, 0.0))
            sfin_ref[s, hs, :] = s0_ref[s, hs, :] * jnp.exp(dend[h][s * c:s * c + 1, :]) + upd


def _gdn_prompt(qkv3, small3, conv_state, s0, conv_w, alog_v, dtb_v, c):
    bsz, t_len, _ = qkv3.shape
    rows = GDN_HEADS * GDN_DK
    keep = CONV_WIDTH - 1
    r = GROUP_ROWS
    full3 = lambda t: (0, 0, 0)
    const2 = lambda t: (0, 0)
    return pl.pallas_call(
        functools.partial(_gdn_prompt_kernel, c=c),
        grid=(t_len // r,),
        in_specs=[
            pl.BlockSpec((bsz, r, CONV_CH), lambda t: (0, t, 0)),
            pl.BlockSpec((bsz, r, LANES), lambda t: (0, t, 0)),
            pl.BlockSpec((bsz, keep, CONV_CH), full3),
            pl.BlockSpec((bsz, rows, GDN_DV), full3),
            pl.BlockSpec((CONV_WIDTH, CONV_CH), const2),
            pl.BlockSpec((2 * GDN_HEADS, 1), const2),
            pl.BlockSpec((2 * GDN_HEADS, 1), const2),
        ],
        out_specs=[
            pl.BlockSpec((bsz, r, GDN_VAL), lambda t: (0, t, 0)),
            pl.BlockSpec((bsz, rows, GDN_DV), full3),
            pl.BlockSpec((bsz, keep, CONV_CH), full3),
        ],
        out_shape=[
            jax.ShapeDtypeStruct((bsz, t_len, GDN_VAL), BF16),
            jax.ShapeDtypeStruct((bsz, rows, GDN_DV), F32),
            jax.ShapeDtypeStruct((bsz, keep, CONV_CH), F32),
        ],
        scratch_shapes=[pltpu.VMEM((GDN_BATCH_INTERLEAVE, SUBLANES + r, CONV_CH), F32),
                        pltpu.VMEM((bsz, SUBLANES, CONV_CH), F32),
                        pltpu.VMEM((bsz, rows, GDN_DV), F32)],
        compiler_params=pltpu.CompilerParams(dimension_semantics=("arbitrary",),
                                             vmem_limit_bytes=VMEM_LIMIT),
        name="gdn_prompt",
    )(qkv3, small3, conv_state, s0, conv_w, alog_v, dtb_v)


def _gdn_sample(qkv3, small3, conv_state, s0, conv_w, alog_v, dtb_v):
    bsz, c, _ = qkv3.shape
    rows = GDN_HEADS * GDN_DK
    keep = CONV_WIDTH - 1
    ns = GROUP_ROWS // c
    grp = lambda g: (g, 0, 0)
    const2 = lambda g: (0, 0)
    return pl.pallas_call(
        _gdn_sample_kernel,
        grid=(bsz // ns,),
        in_specs=[
            pl.BlockSpec((ns, c, CONV_CH), grp),
            pl.BlockSpec((ns, c, LANES), grp),
            pl.BlockSpec((ns, keep, CONV_CH), grp),
            pl.BlockSpec((ns, rows, GDN_DV), grp),
            pl.BlockSpec((CONV_WIDTH, CONV_CH), const2),
            pl.BlockSpec((2 * GDN_HEADS, 1), const2),
            pl.BlockSpec((2 * GDN_HEADS, 1), const2),
        ],
        out_specs=[
            pl.BlockSpec((ns * c, GDN_VAL), lambda g: (g, 0)),
            pl.BlockSpec((ns, rows, GDN_DV), grp),
            pl.BlockSpec((ns, keep, CONV_CH), grp),
        ],
        out_shape=[
            jax.ShapeDtypeStruct((bsz * c, GDN_VAL), BF16),
            jax.ShapeDtypeStruct((bsz, rows, GDN_DV), F32),
            jax.ShapeDtypeStruct((bsz, keep, CONV_CH), F32),
        ],
        scratch_shapes=[pltpu.VMEM((ns, 2 * SUBLANES, CONV_CH), F32)],
        compiler_params=pltpu.CompilerParams(dimension_semantics=("arbitrary",),
                                             vmem_limit_bytes=VMEM_LIMIT),
        name="gdn_sample",
    )(qkv3, small3, conv_state, s0, conv_w, alog_v, dtb_v)


def _head_norm_gate(o, silu_z, w):
    parts = []
    for h in range(o.shape[-1] // LANES):
        oh = o[:, h * LANES:(h + 1) * LANES]
        parts.append(oh * lax.rsqrt(jnp.mean(oh * oh, axis=-1, keepdims=True) + NORM_EPS) * w)
    return jnp.concatenate(parts, axis=-1) * silu_z


def _out_kernel(oa_ref, ob_ref, ga_ref, gb_ref, za_ref, zb_ref, x_ref, p_ref, anw_ref, bnw_ref,
                wua_ref, wub_ref, wout_ref, wpg_ref, wp_ref, fnw_ref, y_ref):
    f32 = lambda ref: ref[...].astype(F32)
    ya = _dot(_head_norm_gate(f32(oa_ref), f32(za_ref), anw_ref[...]), wua_ref[...])
    yb = _dot(_head_norm_gate(f32(ob_ref), f32(zb_ref), bnw_ref[...]), wub_ref[...])
    merged = f32(ga_ref) * ya + f32(gb_ref) * yb
    h1 = x_ref[...] + _dot(merged, wout_ref[...])
    h2 = h1 + _sigmoid(_dot(h1, wpg_ref[...])) * _dot(p_ref[...], wp_ref[...])
    y_ref[...] = h2 * lax.rsqrt(jnp.mean(h2 * h2, axis=-1, keepdims=True) + NORM_EPS) * fnw_ref[...]


def _out_stage(o_a, o_b, gz, x2d, p2d, anw, bnw, wua, wub, wout, wpg, wp, fnw):
    n = x2d.shape[0]
    tm = min(ROW_TILE, n)
    const = lambda i: (0, 0)
    return pl.pallas_call(
        _out_kernel,
        grid=(n // tm,),
        in_specs=[
            pl.BlockSpec((tm, GLA_VAL), lambda i: (i, 0)),
            pl.BlockSpec((tm, GDN_VAL), lambda i: (i, 0)),
            pl.BlockSpec((tm, D_MODEL), lambda i: (i, GZ_GATE_A // D_MODEL)),
            pl.BlockSpec((tm, D_MODEL), lambda i: (i, GZ_GATE_B // D_MODEL)),
            pl.BlockSpec((tm, GLA_VAL), lambda i: (i, GZ_Z_A // GLA_VAL)),
            pl.BlockSpec((tm, GDN_VAL), lambda i: (i, GZ_Z_B // GDN_VAL)),
            pl.BlockSpec((tm, D_MODEL), lambda i: (i, 0)),
            pl.BlockSpec((tm, PLE_DIM), lambda i: (i, 0)),
            pl.BlockSpec((1, GLA_DV), const),
            pl.BlockSpec((1, GDN_DV), const),
            pl.BlockSpec((GLA_VAL, D_MODEL), const),
            pl.BlockSpec((GDN_VAL, D_MODEL), const),
            pl.BlockSpec((D_MODEL, D_MODEL), const),
            pl.BlockSpec((D_MODEL, D_MODEL), const),
            pl.BlockSpec((PLE_DIM, D_MODEL), const),
            pl.BlockSpec((1, D_MODEL), const),
        ],
        out_specs=pl.BlockSpec((tm, D_MODEL), lambda i: (i, 0)),
        out_shape=jax.ShapeDtypeStruct((n, D_MODEL), F32),
        compiler_params=pltpu.CompilerParams(dimension_semantics=("arbitrary",),
                                             vmem_limit_bytes=VMEM_LIMIT),
        name="out_stage",
    )(o_a, o_b, gz, gz, gz, gz, x2d, p2d, anw, bnw, wua, wub, wout, wpg, wp, fnw)


def _in_offsets():
    offs = [0]
    for s in IN_SPLITS:
        offs.append(offs[-1] + s)
    return offs


def _regroup_kernel(wt_ref, o_ref):
    offs = _in_offsets()
    (q_a, _, _, g_a, z_a, qkv_b, a_b, _, z_b, gate_a, gate_b, end) = offs
    piece = lambda lo, hi: wt_ref[lo:hi, :].astype(BF16)
    o_ref[P_GLA_QKV:P_GLA_QKV + GLA_QKV, :] = piece(q_a, g_a)
    o_ref[P_QKV_B:P_QKV_B + CONV_CH, :] = piece(qkv_b, a_b)
    small = jnp.concatenate([wt_ref[g_a:z_a, :], wt_ref[a_b:z_b, :],
                             jnp.zeros((LANES - (z_a - g_a) - (z_b - a_b), wt_ref.shape[1]), F32)], axis=0)
    o_ref[P_SMALL:P_SMALL + LANES, :] = small.astype(BF16)
    o_ref[P_GATES + GZ_GATE_A:P_GATES + GZ_GATE_A + D_MODEL, :] = piece(gate_a, gate_b)
    o_ref[P_GATES + GZ_GATE_B:P_GATES + GZ_GATE_B + D_MODEL, :] = piece(gate_b, end)
    o_ref[P_GATES + GZ_Z_A:P_GATES + GZ_Z_A + GLA_VAL, :] = piece(z_a, qkv_b)
    o_ref[P_GATES + GZ_Z_B:P_GATES + GZ_Z_B + GDN_VAL, :] = piece(z_b, gate_a)


def _regroup_w_in(w_in_t):
    cols = 256
    return pl.pallas_call(
        _regroup_kernel,
        grid=(D_MODEL // cols,),
        in_specs=[pl.BlockSpec((None, w_in_t.shape[1], cols), lambda i: (0, 0, i))],
        out_specs=pl.BlockSpec((P_COLS, cols), lambda i: (0, i)),
        out_shape=jax.ShapeDtypeStruct((P_COLS, D_MODEL), BF16),
        compiler_params=pltpu.CompilerParams(dimension_semantics=("arbitrary",),
                                             vmem_limit_bytes=VMEM_LIMIT),
        name="regroup_w_in",
    )(w_in_t)


def _head_param_col(v):
    return jnp.zeros((2 * GDN_HEADS, 1), F32).at[:GDN_HEADS, 0].set(v.astype(F32))


def _trunk(x, p, s_gla, s_gdn, conv_state, wts):
    bsz, t_len, _ = x.shape
    c = min(PROMPT_CHUNK, t_len)
    n = bsz * t_len
    x2d = x.reshape(n, D_MODEL)
    p2d = p.reshape(n, PLE_DIM)
    gla_in, gdn_in, small, gz = _inproj(x2d, wts["norm_w"], wts["w_in_r"])
    gla_in3 = gla_in.reshape(bsz, t_len, GLA_QKV)
    gdn_in3 = gdn_in.reshape(bsz, t_len, CONV_CH)
    small3 = small.reshape(bsz, t_len, LANES)
    s_gla2 = s_gla.reshape(bsz, GLA_HEADS * GLA_DK, GLA_DV)
    s_gdn2 = s_gdn.reshape(bsz, GDN_HEADS * GDN_DK, GDN_DV)
    if t_len % GROUP_ROWS == 0:
        o_a, gla_fin = _gla_prompt(gla_in3, small3, s_gla2, wts["wgg"], wts["bgg"], c)
        o_b, gdn_fin, conv_new = _gdn_prompt(gdn_in3, small3, conv_state, s_gdn2, wts["conv_w"],
                                             wts["alog_v"], wts["dtb_v"], c)
    else:
        assert GROUP_ROWS % t_len == 0 and bsz % (GROUP_ROWS // t_len) == 0 and t_len >= CONV_WIDTH - 1
        o_a, gla_fin = _gla_sample(gla_in3, small3, s_gla2, wts["wgg"], wts["bgg"])
        o_b, gdn_fin, conv_new = _gdn_sample(gdn_in3, small3, conv_state, s_gdn2, wts["conv_w"],
                                             wts["alog_v"], wts["dtb_v"])
    y = _out_stage(o_a.reshape(n, GLA_VAL), o_b.reshape(n, GDN_VAL), gz, x2d, p2d, wts["anw"], wts["bnw"],
                   wts["wua"], wts["wub"], wts["wout"], wts["wpg"], wts["wp"], wts["fnw"])
    return (y.reshape(bsz, t_len, D_MODEL),
            gla_fin.reshape(1, bsz, GLA_HEADS, GLA_DK, GLA_DV),
            gdn_fin.reshape(1, bsz, GDN_HEADS, GDN_DK, GDN_DV),
            conv_new.reshape(1, bsz, CONV_WIDTH - 1, CONV_CH))


def kernel(x_prompt, x_sample, state_gla, state_gdn, state_conv, p_prompt, p_sample, norm_w, w_in,
           w_gla_gate, b_gla_gate, gla_norm_w, conv_w, gdn_a_log, gdn_dt_bias, gdn_norm_w,
           w_up_gla, w_up_gdn, w_out, w_ple_gate, w_ple, final_norm_w):
    wgg = jnp.zeros((LANES, GLA_KEY), F32).at[SM_G:SM_G + GLA_GATE_RANK].set(w_gla_gate[0]).astype(BF16)
    wts = {
        "norm_w": norm_w[0].reshape(1, D_MODEL),
        "w_in_r": _regroup_w_in(jnp.swapaxes(w_in, 1, 2)),
        "wgg": wgg,
        "bgg": b_gla_gate[0].reshape(1, GLA_KEY),
        "conv_w": conv_w[0],
        "alog_v": _head_param_col(gdn_a_log[0]),
        "dtb_v": _head_param_col(gdn_dt_bias[0]),
        "anw": gla_norm_w[0].reshape(1, GLA_DV),
        "bnw": gdn_norm_w[0].reshape(1, GDN_DV),
        "wua": w_up_gla[0].astype(BF16),
        "wub": w_up_gdn[0].astype(BF16),
        "wout": w_out[0].astype(BF16),
        "wpg": w_ple_gate[0].astype(BF16),
        "wp": w_ple[0].astype(BF16),
        "fnw": final_norm_w.reshape(1, D_MODEL),
    }
    bsz = x_prompt.shape[0]
    dt = x_prompt.dtype
    y_p, gla_p, gdn_p, conv_p = _trunk(
        x_prompt, p_prompt[0],
        jnp.zeros((bsz, GLA_HEADS, GLA_DK, GLA_DV), dt), jnp.zeros((bsz, GDN_HEADS, GDN_DK, GDN_DV), dt),
        jnp.zeros((bsz, CONV_WIDTH - 1, CONV_CH), dt), wts)
    y_s, gla_s, gdn_s, conv_s = _trunk(x_sample, p_sample[0], state_gla[0], state_gdn[0], state_conv[0], wts)
    return (y_p, y_s, gla_p, gdn_p, conv_p, gla_s, gdn_s, conv_s)
```

```python
import functools

import jax
import jax.numpy as jnp
from jax import lax
from jax.experimental import pallas as pl
from jax.experimental.pallas import tpu as pltpu

F32 = jnp.float32
BF16 = jnp.bfloat16

D_MODEL = 1024
PLE_DIM = 256
NORM_EPS = 1e-6
GLA_HEADS = 4
GLA_DK = 64
GLA_DV = 128
GLA_KEY = GLA_HEADS * GLA_DK
GLA_VAL = GLA_HEADS * GLA_DV
GLA_GATE_RANK = 16
GLA_GATE_TEMP = 16.0
GDN_HEADS = 4
GDN_DK = 128
GDN_DV = 128
GDN_KEY = GDN_HEADS * GDN_DK
GDN_VAL = GDN_HEADS * GDN_DV
CONV_WIDTH = 4
CONV_CH = 2 * GDN_KEY + GDN_VAL
IN_SPLITS = (GLA_KEY, GLA_KEY, GLA_VAL, GLA_GATE_RANK, GLA_VAL, CONV_CH, GDN_HEADS, GDN_HEADS,
             GDN_VAL, D_MODEL, D_MODEL)

LANES = 128
SUBLANES = 8

GLA_QKV = 2 * GLA_KEY + GLA_VAL
P_GLA_QKV = 0
P_QKV_B = P_GLA_QKV + GLA_QKV
P_SMALL = P_QKV_B + CONV_CH
P_GATES = P_SMALL + LANES
GZ_COLS = 2 * D_MODEL + GLA_VAL + GDN_VAL
P_COLS = P_GATES + GZ_COLS
GZ_GATE_A = 0
GZ_GATE_B = D_MODEL
GZ_Z_A = 2 * D_MODEL
GZ_Z_B = 2 * D_MODEL + GLA_VAL
SM_G = 0
SM_A = GLA_GATE_RANK
SM_B = GLA_GATE_RANK + GDN_HEADS

PROMPT_CHUNK = 64
GROUP_ROWS = 128
BATCH_INTERLEAVE = 8
GDN_BATCH_INTERLEAVE = 8
GDN_HEAD_GROUP = 4
INVERSE_BASE_BLOCK = 16
ROW_TILE = 512
INPROJ_ROW_TILE = 1024
VMEM_LIMIT = 60 * 1024 * 1024


def _dot(a, b):
    return jnp.dot(a.astype(BF16), b.astype(BF16), preferred_element_type=F32)


def _dot_nt(a, b):
    return lax.dot_general(a.astype(BF16), b.astype(BF16), (((1,), (1,)), ((), ())),
                           preferred_element_type=F32)


def _split2(x):
    h1 = x.astype(BF16)
    return h1, (x - h1.astype(F32)).astype(BF16)


def _cumsum_rows(tri, x):
    x1, x2 = _split2(x)
    d = functools.partial(jnp.dot, preferred_element_type=F32)
    return d(tri, x1) + d(tri, x2)


def _softplus(x):
    return jnp.maximum(x, 0.0) + jnp.log(1.0 + jnp.exp(-jnp.abs(x)))


def _sigmoid(x):
    return 1.0 / (1.0 + jnp.exp(-x))


def _silu(x):
    return x * _sigmoid(x)


def _interleave(emitters):
    results = [None] * len(emitters)
    live = list(range(len(emitters)))
    while live:
        for i in list(live):
            try:
                next(emitters[i])
            except StopIteration as stop:
                results[i] = stop.value
                live.remove(i)
    return results


def _block_masks(r, block):
    row = lax.broadcasted_iota(jnp.int32, (r, r), 0)
    col = lax.broadcasted_iota(jnp.int32, (r, r), 1)
    same = (row // block) == (col // block)
    return same & (row >= col), same & (row > col), same


INPROJ_COL_STEP = 512


def _inproj_kernel(x_ref, nw_ref, w_ref, gla_ref, gdn_ref, small_ref, gz_ref):
    x = x_ref[...]
    xn = x * lax.rsqrt(jnp.mean(x * x, axis=-1, keepdims=True) + NORM_EPS) * nw_ref[...]
    xb = xn.astype(BF16)

    def emit(o_ref, w0, width, act=None, o0=0):
        for c0 in range(0, width, INPROJ_COL_STEP):
            c1 = min(c0 + INPROJ_COL_STEP, width)
            res = _dot_nt(xb, w_ref[w0 + c0:w0 + c1, :])
            o_ref[:, o0 + c0:o0 + c1] = (res if act is None else act(res)).astype(o_ref.dtype)

    emit(gla_ref, P_GLA_QKV, GLA_QKV)
    emit(gdn_ref, P_QKV_B, CONV_CH)
    emit(small_ref, P_SMALL, LANES)
    emit(gz_ref, P_GATES + GZ_GATE_A, 2 * D_MODEL, act=_sigmoid, o0=GZ_GATE_A)
    emit(gz_ref, P_GATES + GZ_Z_A, GLA_VAL + GDN_VAL, act=_silu, o0=GZ_Z_A)


def _inproj(x2d, norm_w, w_in_r):
    n = x2d.shape[0]
    tm = INPROJ_ROW_TILE if n >= 4 * INPROJ_ROW_TILE else min(ROW_TILE, n)
    rows = lambda i: (i, 0)
    return pl.pallas_call(
        _inproj_kernel,
        grid=(n // tm,),
        in_specs=[
            pl.BlockSpec((tm, D_MODEL), rows),
            pl.BlockSpec((1, D_MODEL), lambda i: (0, 0)),
            pl.BlockSpec((P_COLS, D_MODEL), lambda i: (0, 0), pipeline_mode=pl.Buffered(1)),
        ],
        out_specs=[
            pl.BlockSpec((tm, GLA_QKV), rows),
            pl.BlockSpec((tm, CONV_CH), rows),
            pl.BlockSpec((tm, LANES), rows),
            pl.BlockSpec((tm, GZ_COLS), rows),
        ],
        out_shape=[
            jax.ShapeDtypeStruct((n, GLA_QKV), F32),
            jax.ShapeDtypeStruct((n, CONV_CH), F32),
            jax.ShapeDtypeStruct((n, LANES), F32),
            jax.ShapeDtypeStruct((n, GZ_COLS), BF16),
        ],
        compiler_params=pltpu.CompilerParams(dimension_semantics=("arbitrary",),
                                             vmem_limit_bytes=VMEM_LIMIT),
        name="inproj",
    )(x2d, norm_w, w_in_r)


GLA_PAIRS = GLA_HEADS // 2
GLA_SUB_BLOCK = 16


def _gla_att_levels(r, block):
    row = lax.broadcasted_iota(jnp.int32, (r, r), 0)
    col = lax.broadcasted_iota(jnp.int32, (r, r), 1)
    sub = min(block, GLA_SUB_BLOCK)
    levels = [(None, ((row // sub) == (col // sub)) & (row >= col))]
    half = sub
    while half < block:
        levels.append((half, ((row // (2 * half)) == (col // (2 * half)))
                       & ((row // half) % 2 == 1) & ((col // half) % 2 == 0)))
        half *= 2
    return levels


def _rows_at(x, n, offset):
    return jnp.concatenate([jnp.broadcast_to(x[i + offset:i + offset + 1, :], (n, x.shape[1]))
                            for i in range(0, x.shape[0], n)], axis=0)


def _gla_prepass(q, k, v, sm, wgg, bgg, same_le, same, levels, block):
    r = q.shape[0]
    heads = range(GLA_HEADS)
    pairs = range(GLA_PAIRS)
    pre = _dot(sm, wgg) + bgg
    yield
    gk = (jnp.minimum(pre, 0.0) - jnp.log(1.0 + jnp.exp(-jnp.abs(pre)))) * (1.0 / GLA_GATE_TEMP)
    sums = _cumsum_rows(jnp.concatenate([jnp.where(same_le, 1.0, 0.0).astype(BF16),
                                         jnp.where(same, 1.0, 0.0).astype(BF16)], axis=0), gk)
    yield
    bcum = sums[:r]
    bend = sums[r:]
    bex = bcum - gk
    scale = GLA_DK ** -0.5
    q_e = q * jnp.exp(bcum) * scale
    lane = lax.broadcasted_iota(jnp.int32, (r, LANES), 1)
    in_head = [lane < GLA_DK, lane >= GLA_DK]
    pl_ = [slice(p * LANES, (p + 1) * LANES) for p in pairs]
    head_only = lambda x, h: jnp.where(in_head[h % 2], x[:, pl_[h // 2]], 0.0)
    qm = [head_only(q_e, h) for h in heads]
    sub = min(block, GLA_SUB_BLOCK)
    start = _rows_at(bex, sub, 0)
    groups = {None: (q * jnp.exp(bcum - start) * scale, [(levels[0][1], k * jnp.exp(start - bcum))])}
    for half, mask in levels[1:]:
        key = None if half == sub else half
        if key not in groups:
            groups[key] = (q * jnp.exp(bcum - _rows_at(bex, half, 0)) * scale, [])
        groups[key][1].append((mask, k * jnp.exp(_rows_at(bcum, half, half - 1) - bcum)))
    att = [jnp.zeros((r, r), F32)] * GLA_HEADS
    for q_l, parts in groups.values():
        for h in heads:
            keys = [k_l[:, pl_[h // 2]].astype(BF16) for _, k_l in parts]
            prod = _dot_nt(head_only(q_l, h), keys[0] if len(keys) == 1 else jnp.concatenate(keys, axis=0))
            for i, (mask, _) in enumerate(parts):
                att[h] = jnp.where(mask, prod[:, i * r:(i + 1) * r], att[h])
    yield
    o_intra = [_dot(att[h], v[h]) for h in heads]
    k_end = k * jnp.exp(bend - bcum)
    k_end_t = [k_end[:, pl_[p]].T for p in pairs]
    bend_t = [bend[:, pl_[p]].T for p in pairs]
    yield
    return qm, o_intra, k_end_t, bend_t


def _gla_prompt_kernel(qkv_ref, small_ref, s0_ref, wgg_ref, bgg_ref, o_ref, sfin_ref, s_scr, *, c):
    t = pl.program_id(0)
    nb, r, _ = qkv_ref.shape

    @pl.when(t == 0)
    def _():
        s_scr[...] = s0_ref[...]

    same_le, _, same = _block_masks(r, c)
    levels = _gla_att_levels(r, c)
    zeros = jnp.zeros((c, GLA_DV), F32)
    n_sub = r // c
    heads = range(GLA_HEADS)
    pairs = range(GLA_PAIRS)

    def one_batch(b):
        q = qkv_ref[b, :, 0:GLA_KEY]
        k = qkv_ref[b, :, GLA_KEY:2 * GLA_KEY]
        v = [qkv_ref[b, :, 2 * GLA_KEY + h * GLA_DV:2 * GLA_KEY + (h + 1) * GLA_DV] for h in heads]
        qm, o_intra, k_end_t, bend_t = yield from _gla_prepass(
            q, k, v, small_ref[b], wgg_ref[...], bgg_ref[...], same_le, same, levels, c)
        s = [s_scr[b, p * LANES:(p + 1) * LANES, :] for p in pairs]
        for i in range(n_sub):
            rows = slice(i * c, (i + 1) * c)
            ws = [_dot(jnp.concatenate([qm[2 * p][rows], qm[2 * p + 1][rows]], axis=0), s[p]) for p in pairs]
            padded = [jnp.concatenate([zeros] * i + [v[h][rows]] + [zeros] * (n_sub - 1 - i), axis=0)
                      for h in heads]
            upd = [_dot(k_end_t[h // 2][(h % 2) * GLA_DK:(h % 2 + 1) * GLA_DK, :], padded[h]) for h in heads]
            yield
            for h in heads:
                o_ref[b, rows, h * GLA_DV:(h + 1) * GLA_DV] = (
                    o_intra[h][rows] + ws[h // 2][(h % 2) * c:(h % 2 + 1) * c]).astype(o_ref.dtype)
            s = [s[p] * jnp.exp(bend_t[p][:, i * c:i * c + 1])
                 + jnp.concatenate([upd[2 * p], upd[2 * p + 1]], axis=0) for p in pairs]
        for p in pairs:
            s_scr[b, p * LANES:(p + 1) * LANES, :] = s[p]

    def per_step(i, carry):
        _interleave([one_batch(i * BATCH_INTERLEAVE + j) for j in range(BATCH_INTERLEAVE)])
        return carry

    lax.fori_loop(0, nb // BATCH_INTERLEAVE, per_step, None)

    @pl.when(t == pl.num_programs(0) - 1)
    def _():
        sfin_ref[...] = s_scr[...]


def _gla_sample_kernel(qkv_ref, small_ref, s0_ref, wgg_ref, bgg_ref, o_ref, sfin_ref):
    ns, c, _ = qkv_ref.shape
    r = ns * c
    heads = range(GLA_HEADS)
    same_le, _, same = _block_masks(r, c)
    levels = _gla_att_levels(r, c)
    q = qkv_ref[:, :, 0:GLA_KEY].reshape(r, GLA_KEY)
    k = qkv_ref[:, :, GLA_KEY:2 * GLA_KEY].reshape(r, GLA_KEY)
    v = [qkv_ref[:, :, 2 * GLA_KEY + h * GLA_DV:2 * GLA_KEY + (h + 1) * GLA_DV].reshape(r, GLA_DV)
         for h in heads]
    (qm, o_intra, k_end_t, bend_t), = _interleave([_gla_prepass(
        q, k, v, small_ref[...].reshape(r, LANES), wgg_ref[...], bgg_ref[...], same_le, same, levels, c)])
    seq_of_row = lax.broadcasted_iota(jnp.int32, (r, GLA_DV), 0) // c
    for p in range(GLA_PAIRS):
        ps = slice(p * LANES, (p + 1) * LANES)
        inter = [[], []]
        for s in range(ns):
            rows = slice(s * c, (s + 1) * c)
            ws = _dot(jnp.concatenate([qm[2 * p][rows], qm[2 * p + 1][rows]], axis=0), s0_ref[s, ps, :])
            inter[0].append(ws[:c])
            inter[1].append(ws[c:])
        for hh in range(2):
            h = 2 * p + hh
            o = o_intra[h] + jnp.concatenate(inter[hh], axis=0)
            o_ref[:, h * GLA_DV:(h + 1) * GLA_DV] = o.astype(o_ref.dtype)
        for s in range(ns):
            upd = [_dot(k_end_t[p][hh * GLA_DK:(hh + 1) * GLA_DK, :],
                        jnp.where(seq_of_row == s, v[2 * p + hh], 0.0)) for hh in range(2)]
            sfin_ref[s, ps, :] = (s0_ref[s, ps, :] * jnp.exp(bend_t[p][:, s * c:s * c + 1])
                                  + jnp.concatenate(upd, axis=0))


def _gla_prompt(qkv3, small3, s0, wgg, bgg, c):
    bsz, t_len, _ = qkv3.shape
    rows = GLA_HEADS * GLA_DK
    r = GROUP_ROWS
    full3 = lambda t: (0, 0, 0)
    const2 = lambda t: (0, 0)
    return pl.pallas_call(
        functools.partial(_gla_prompt_kernel, c=c),
        grid=(t_len // r,),
        in_specs=[
            pl.BlockSpec((bsz, r, GLA_QKV), lambda t: (0, t, 0)),
            pl.BlockSpec((bsz, r, LANES), lambda t: (0, t, 0)),
            pl.BlockSpec((bsz, rows, GLA_DV), full3),
            pl.BlockSpec((LANES, GLA_KEY), const2),
            pl.BlockSpec((1, GLA_KEY), const2),
        ],
        out_specs=[
            pl.BlockSpec((bsz, r, GLA_VAL), lambda t: (0, t, 0)),
            pl.BlockSpec((bsz, rows, GLA_DV), full3),
        ],
        out_shape=[
            jax.ShapeDtypeStruct((bsz, t_len, GLA_VAL), BF16),
            jax.ShapeDtypeStruct((bsz, rows, GLA_DV), F32),
        ],
        scratch_shapes=[pltpu.VMEM((bsz, rows, GLA_DV), F32)],
        compiler_params=pltpu.CompilerParams(dimension_semantics=("arbitrary",),
                                             vmem_limit_bytes=VMEM_LIMIT),
        name="gla_prompt",
    )(qkv3, small3, s0, wgg, bgg)


def _gla_sample(qkv3, small3, s0, wgg, bgg):
    bsz, c, _ = qkv3.shape
    rows = GLA_HEADS * GLA_DK
    ns = GROUP_ROWS // c
    grp = lambda g: (g, 0, 0)
    const2 = lambda g: (0, 0)
    return pl.pallas_call(
        _gla_sample_kernel,
        grid=(bsz // ns,),
        in_specs=[
            pl.BlockSpec((ns, c, GLA_QKV), grp),
            pl.BlockSpec((ns, c, LANES), grp),
            pl.BlockSpec((ns, rows, GLA_DV), grp),
            pl.BlockSpec((LANES, GLA_KEY), const2),
            pl.BlockSpec((1, GLA_KEY), const2),
        ],
        out_specs=[
            pl.BlockSpec((ns * c, GLA_VAL), lambda g: (g, 0)),
            pl.BlockSpec((ns, rows, GLA_DV), grp),
        ],
        out_shape=[
            jax.ShapeDtypeStruct((bsz * c, GLA_VAL), BF16),
            jax.ShapeDtypeStruct((bsz, rows, GLA_DV), F32),
        ],
        compiler_params=pltpu.CompilerParams(dimension_semantics=("arbitrary",),
                                             vmem_limit_bytes=VMEM_LIMIT),
        name="gla_sample",
    )(qkv3, small3, s0, wgg, bgg)


def _block_unit_lower_inverse(a_list, block):
    r = a_list[0].shape[0]
    row = lax.broadcasted_iota(jnp.int32, (r, r), 0)
    col = lax.broadcasted_iota(jnp.int32, (r, r), 1)
    in_block = lambda n: (row // n) == (col // n)
    base = min(block, INVERSE_BASE_BLOCK)
    a_base = a_list if base == block else [jnp.where(in_block(base), a, 0.0) for a in a_list]
    xs = [jnp.where(row == col, 1.0, 0.0) - a for a in a_base]
    ps = [_dot(a, a) for a in a_base]
    n = 2
    while n < base:
        yield
        last = 2 * n >= base
        ms = [_dot(x if last else jnp.concatenate([p.astype(BF16), x.astype(BF16)], axis=0), p)
              for p, x in zip(ps, xs)]
        ps = [m[:r] for m in ms]
        xs = [x + m[-r:] for x, m in zip(xs, ms)]
        n *= 2
    n = base
    while n < block:
        yield
        between = in_block(2 * n) & jnp.logical_not(in_block(n))
        ts = [_dot(jnp.where(between, a, 0.0), x) for a, x in zip(a_list, xs)]
        yield
        xs = [x - _dot(x, t) for x, t in zip(xs, ts)]
        n *= 2
    return xs


def _gdn_decays(sm, alog_c, dtb_c, same_lt, same):
    r = sm.shape[0]
    heads = range(GDN_HEADS)
    ab = sm.T[SM_A:SM_A + 2 * GDN_HEADS, :]
    g8 = -jnp.exp(alog_c) * _softplus(ab + dtb_c)
    beta_c = _sigmoid(ab).T
    g1, g2 = _split2(g8)
    same_ge = same & jnp.logical_not(same_lt)
    sel = jnp.concatenate([jnp.where(same_ge, 1.0, 0.0).astype(BF16),
                           jnp.where(same, 1.0, 0.0).astype(BF16)], axis=1)
    sums = jnp.dot(jnp.concatenate([g1, g2], axis=0), sel, preferred_element_type=F32)
    sums = sums[:2 * GDN_HEADS] + sums[2 * GDN_HEADS:]
    dec_rows = sums[:, :r]
    dec_c = dec_rows.T
    dend_c = sums[:, r:].T
    return ([dec_c[:, h:h + 1] for h in heads], [dec_rows[h:h + 1, :] for h in heads],
            [dend_c[:, h:h + 1] for h in heads], [beta_c[:, GDN_HEADS + h:GDN_HEADS + h + 1] for h in heads])


def _gdn_head_cols(h):
    return [slice(part * GDN_KEY + h * GDN_DK, part * GDN_KEY + (h + 1) * GDN_DK) for part in range(3)]


def _gdn_prepass(qkv_of_head, sm, alog_c, dtb_c, same_le, same_lt, same, block, heads):
    dcol, drow, dend, beta = ([x[h] for h in heads] for x in _gdn_decays(sm, alog_c, dtb_c, same_lt, same))
    idx = range(len(heads))
    gamma = [jnp.where(same_le, jnp.exp(jnp.where(same_le, dcol[i] - drow[i], 0.0)), 0.0) for i in idx]
    q, k, v = (list(x) for x in zip(*[qkv_of_head(h) for h in heads]))
    r = q[0].shape[0]
    q = [x * lax.rsqrt(jnp.sum(x * x, axis=-1, keepdims=True) + NORM_EPS) * (GDN_DK ** -0.5) for x in q]
    k = [x * lax.rsqrt(jnp.sum(x * x, axis=-1, keepdims=True) + NORM_EPS) for x in k]
    yield
    kb = [k[i] * beta[i] for i in idx]
    vb = [v[i] * beta[i] for i in idx]
    kq = [_dot_nt(jnp.concatenate([kb[i], q[i]], axis=0), k[i]) for i in idx]
    yield
    a_mat = [jnp.where(same_lt, kq[i][:r] * gamma[i], 0.0) for i in idx]
    qk = [(kq[i][r:] * gamma[i]).astype(BF16) for i in idx]
    t_inv = yield from _block_unit_lower_inverse(a_mat, block)
    edec = [jnp.exp(dcol[i]) for i in idx]
    uw = [_dot(t_inv[i], jnp.concatenate([vb[i], kb[i] * edec[i]], axis=1)) for i in idx]
    q_e = [(q[i] * edec[i]).astype(BF16) for i in idx]
    k_end_t = [(k[i] * jnp.exp(dend[i] - dcol[i])).T.astype(BF16) for i in idx]
    yield
    u = [x[:, :GDN_DV] for x in uw]
    w = [x[:, GDN_DV:].astype(BF16) for x in uw]
    return u, w, q_e, qk, k_end_t, dend


def _conv_silu(win, cw_ref):
    conv = win(0) * cw_ref[CONV_WIDTH - 1:CONV_WIDTH, :]
    for j in range(1, CONV_WIDTH):
        conv = conv + win(j) * cw_ref[CONV_WIDTH - 1 - j:CONV_WIDTH - j, :]
    return _silu(conv)

def _gdn_prompt_kernel(qkv_ref, small_ref, cs_ref, s0_ref, cw_ref, alog_ref, dtb_ref,
                       o_ref, sfin_ref, cnew_ref, xbuf, prev, s_scr, *, c):
    t = pl.program_id(0)
    nb, r, _ = qkv_ref.shape
    keep = CONV_WIDTH - 1
    base = SUBLANES

    @pl.when(t == 0)
    def _():
        s_scr[...] = s0_ref[...]
        prev[:, base - keep:base, :] = cs_ref[...]

    same_le, same_lt, same = _block_masks(r, c)
    zeros = jnp.zeros((c, GDN_DV), F32)
    n_sub = r // c

    def one_batch(b, slot, heads, first):
        xb = xbuf.at[slot]
        if first:
            xb[base - keep:base, :] = prev[b, base - keep:base, :]
            xb[base:base + r, :] = qkv_ref[b]
            prev[b, base - keep:base, :] = xb[base + r - keep:base + r, :]

        def qkv_of_head(h):
            return [_conv_silu(lambda j: xb[base - j:base - j + r, cols], cw_ref.at[:, cols])
                    for cols in _gdn_head_cols(h)]

        u, w, q_e, qk, k_end_t, dend = yield from _gdn_prepass(
            qkv_of_head, small_ref[b], alog_ref[...], dtb_ref[...], same_le, same_lt, same, c, heads)
        idx = range(len(heads))
        hs = [slice(h * GDN_DK, (h + 1) * GDN_DK) for h in heads]
        s = [s_scr[b, hs[i], :] for i in idx]
        for ci in range(n_sub):
            rows = slice(ci * c, (ci + 1) * c)
            ws = [_dot(jnp.concatenate([w[i][rows], q_e[i][rows]], axis=0), s[i]) for i in idx]
            yield
            v_new = [u[i][rows] - ws[i][:c] for i in idx]
            padded = [jnp.concatenate([zeros] * ci + [v_new[i]] + [zeros] * (n_sub - 1 - ci), axis=0)
                      for i in idx]
            upd = [_dot(jnp.concatenate([qk[i][rows], k_end_t[i]], axis=0), padded[i]) for i in idx]
            yield
            for i in idx:
                o_ref[b, rows, hs[i]] = (ws[i][c:] + upd[i][:c]).astype(o_ref.dtype)
            s = [s[i] * jnp.exp(dend[i][ci * c:ci * c + 1, :]) + upd[i][c:] for i in idx]
        for i in idx:
            s_scr[b, hs[i], :] = s[i]

    def per_step(step, carry):
        for g in range(0, GDN_HEADS, GDN_HEAD_GROUP):
            heads = tuple(range(g, g + GDN_HEAD_GROUP))
            _interleave([one_batch(step * GDN_BATCH_INTERLEAVE + j, j, heads, g == 0)
                         for j in range(GDN_BATCH_INTERLEAVE)])
        return carry

    lax.fori_loop(0, nb // GDN_BATCH_INTERLEAVE, per_step, None)

    @pl.when(t == pl.num_programs(0) - 1)
    def _():
        sfin_ref[...] = s_scr[...]
        cnew_ref[...] = prev[:, base - keep:base, :]


def _gdn_sample_kernel(qkv_ref, small_ref, cs_ref, s0_ref, cw_ref, alog_ref, dtb_ref,
                       o_ref, sfin_ref, cnew_ref, xbuf):
    ns, c, _ = qkv_ref.shape
    r = ns * c
    keep = CONV_WIDTH - 1
    base = SUBLANES
    xbuf[:, base - keep:base, :] = cs_ref[...]
    xbuf[:, base:base + c, :] = qkv_ref[...]
    conv = _conv_silu(lambda j: xbuf[:, base - j:base - j + c, :], cw_ref).reshape(r, CONV_CH)
    cnew_ref[...] = xbuf[:, base + c - keep:base + c, :]

    same_le, same_lt, same = _block_masks(r, c)
    seq_of_row = lax.broadcasted_iota(jnp.int32, (r, GDN_DV), 0) // c
    (u, w, q_e, qk, k_end_t, dend), = _interleave([_gdn_prepass(
        lambda h: [conv[:, cols] for cols in _gdn_head_cols(h)], small_ref[...].reshape(r, LANES),
        alog_ref[...], dtb_ref[...], same_le, same_lt, same, c, tuple(range(GDN_HEADS)))])
    for h in range(GDN_HEADS):
        hs = slice(h * GDN_DK, (h + 1) * GDN_DK)
        v_parts, o_parts = [], []
        for s in range(ns):
            rows = slice(s * c, (s + 1) * c)
            ws = _dot(jnp.concatenate([w[h][rows], q_e[h][rows]], axis=0), s0_ref[s, hs, :])
            v_parts.append(u[h][rows] - ws[:c])
            o_parts.append(ws[c:])
        v_new = jnp.concatenate(v_parts, axis=0)
        o = jnp.concatenate(o_parts, axis=0) + _dot(qk[h], v_new)
        o_ref[:, h * GDN_DV:(h + 1) * GDN_DV] = o.astype(o_ref.dtype)
        for s in range(ns):
            upd = _dot(k_end_t[h], jnp.where(seq_of_row == s, v_new, 0.0))
            sfin_ref[s, hs, :] = s0_ref[s, hs, :] * jnp.exp(dend[h][s * c:s * c + 1, :]) + upd


def _gdn_prompt(qkv3, small3, conv_state, s0, conv_w, alog_v, dtb_v, c):
    bsz, t_len, _ = qkv3.shape
    rows = GDN_HEADS * GDN_DK
    keep = CONV_WIDTH - 1
    r = GROUP_ROWS
    full3 = lambda t: (0, 0, 0)
    const2 = lambda t: (0, 0)
    return pl.pallas_call(
        functools.partial(_gdn_prompt_kernel, c=c),
        grid=(t_len // r,),
        in_specs=[
            pl.BlockSpec((bsz, r, CONV_CH), lambda t: (0, t, 0)),
            pl.BlockSpec((bsz, r, LANES), lambda t: (0, t, 0)),
            pl.BlockSpec((bsz, keep, CONV_CH), full3),
            pl.BlockSpec((bsz, rows, GDN_DV), full3),
            pl.BlockSpec((CONV_WIDTH, CONV_CH), const2),
            pl.BlockSpec((2 * GDN_HEADS, 1), const2),
            pl.BlockSpec((2 * GDN_HEADS, 1), const2),
        ],
        out_specs=[
            pl.BlockSpec((bsz, r, GDN_VAL), lambda t: (0, t, 0)),
            pl.BlockSpec((bsz, rows, GDN_DV), full3),
            pl.BlockSpec((bsz, keep, CONV_CH), full3),
        ],
        out_shape=[
            jax.ShapeDtypeStruct((bsz, t_len, GDN_VAL), BF16),
            jax.ShapeDtypeStruct((bsz, rows, GDN_DV), F32),
            jax.ShapeDtypeStruct((bsz, keep, CONV_CH), F32),
        ],
        scratch_shapes=[pltpu.VMEM((GDN_BATCH_INTERLEAVE, SUBLANES + r, CONV_CH), F32),
                        pltpu.VMEM((bsz, SUBLANES, CONV_CH), F32),
                        pltpu.VMEM((bsz, rows, GDN_DV), F32)],
        compiler_params=pltpu.CompilerParams(dimension_semantics=("arbitrary",),
                                             vmem_limit_bytes=VMEM_LIMIT),
        name="gdn_prompt",
    )(qkv3, small3, conv_state, s0, conv_w, alog_v, dtb_v)


def _gdn_sample(qkv3, small3, conv_state, s0, conv_w, alog_v, dtb_v):
    bsz, c, _ = qkv3.shape
    rows = GDN_HEADS * GDN_DK
    keep = CONV_WIDTH - 1
    ns = GROUP_ROWS // c
    grp = lambda g: (g, 0, 0)
    const2 = lambda g: (0, 0)
    return pl.pallas_call(
        _gdn_sample_kernel,
        grid=(bsz // ns,),
        in_specs=[
            pl.BlockSpec((ns, c, CONV_CH), grp),
            pl.BlockSpec((ns, c, LANES), grp),
            pl.BlockSpec((ns, keep, CONV_CH), grp),
            pl.BlockSpec((ns, rows, GDN_DV), grp),
            pl.BlockSpec((CONV_WIDTH, CONV_CH), const2),
            pl.BlockSpec((2 * GDN_HEADS, 1), const2),
            pl.BlockSpec((2 * GDN_HEADS, 1), const2),
        ],
        out_specs=[
            pl.BlockSpec((ns * c, GDN_VAL), lambda g: (g, 0)),
            pl.BlockSpec((ns, rows, GDN_DV), grp),
            pl.BlockSpec((ns, keep, CONV_CH), grp),
        ],
        out_shape=[
            jax.ShapeDtypeStruct((bsz * c, GDN_VAL), BF16),
            jax.ShapeDtypeStruct((bsz, rows, GDN_DV), F32),
            jax.ShapeDtypeStruct((bsz, keep, CONV_CH), F32),
        ],
        scratch_shapes=[pltpu.VMEM((ns, 2 * SUBLANES, CONV_CH), F32)],
        compiler_params=pltpu.CompilerParams(dimension_semantics=("arbitrary",),
                                             vmem_limit_bytes=VMEM_LIMIT),
        name="gdn_sample",
    )(qkv3, small3, conv_state, s0, conv_w, alog_v, dtb_v)


def _head_norm_gate(o, silu_z, w):
    parts = []
    for h in range(o.shape[-1] // LANES):
        oh = o[:, h * LANES:(h + 1) * LANES]
        parts.append(oh * lax.rsqrt(jnp.mean(oh * oh, axis=-1, keepdims=True) + NORM_EPS) * w)
    return jnp.concatenate(parts, axis=-1) * silu_z


def _out_kernel(oa_ref, ob_ref, ga_ref, gb_ref, za_ref, zb_ref, x_ref, p_ref, anw_ref, bnw_ref,
                wua_ref, wub_ref, wout_ref, wpg_ref, wp_ref, fnw_ref, y_ref):
    f32 = lambda ref: ref[...].astype(F32)
    ya = _dot(_head_norm_gate(f32(oa_ref), f32(za_ref), anw_ref[...]), wua_ref[...])
    yb = _dot(_head_norm_gate(f32(ob_ref), f32(zb_ref), bnw_ref[...]), wub_ref[...])
    merged = f32(ga_ref) * ya + f32(gb_ref) * yb
    h1 = x_ref[...] + _dot(merged, wout_ref[...])
    h2 = h1 + _sigmoid(_dot(h1, wpg_ref[...])) * _dot(p_ref[...], wp_ref[...])
    y_ref[...] = h2 * lax.rsqrt(jnp.mean(h2 * h2, axis=-1, keepdims=True) + NORM_EPS) * fnw_ref[...]


def _out_stage(o_a, o_b, gz, x2d, p2d, anw, bnw, wua, wub, wout, wpg, wp, fnw):
    n = x2d.shape[0]
    tm = min(ROW_TILE, n)
    const = lambda i: (0, 0)
    return pl.pallas_call(
        _out_kernel,
        grid=(n // tm,),
        in_specs=[
            pl.BlockSpec((tm, GLA_VAL), lambda i: (i, 0)),
            pl.BlockSpec((tm, GDN_VAL), lambda i: (i, 0)),
            pl.BlockSpec((tm, D_MODEL), lambda i: (i, GZ_GATE_A // D_MODEL)),
            pl.BlockSpec((tm, D_MODEL), lambda i: (i, GZ_GATE_B // D_MODEL)),
            pl.BlockSpec((tm, GLA_VAL), lambda i: (i, GZ_Z_A // GLA_VAL)),
            pl.BlockSpec((tm, GDN_VAL), lambda i: (i, GZ_Z_B // GDN_VAL)),
            pl.BlockSpec((tm, D_MODEL), lambda i: (i, 0)),
            pl.BlockSpec((tm, PLE_DIM), lambda i: (i, 0)),
            pl.BlockSpec((1, GLA_DV), const),
            pl.BlockSpec((1, GDN_DV), const),
            pl.BlockSpec((GLA_VAL, D_MODEL), const),
            pl.BlockSpec((GDN_VAL, D_MODEL), const),
            pl.BlockSpec((D_MODEL, D_MODEL), const),
            pl.BlockSpec((D_MODEL, D_MODEL), const),
            pl.BlockSpec((PLE_DIM, D_MODEL), const),
            pl.BlockSpec((1, D_MODEL), const),
        ],
        out_specs=pl.BlockSpec((tm, D_MODEL), lambda i: (i, 0)),
        out_shape=jax.ShapeDtypeStruct((n, D_MODEL), F32),
        compiler_params=pltpu.CompilerParams(dimension_semantics=("arbitrary",),
                                             vmem_limit_bytes=VMEM_LIMIT),
        name="out_stage",
    )(o_a, o_b, gz, gz, gz, gz, x2d, p2d, anw, bnw, wua, wub, wout, wpg, wp, fnw)


def _in_offsets():
    offs = [0]
    for s in IN_SPLITS:
        offs.append(offs[-1] + s)
    return offs


def _regroup_kernel(wt_ref, o_ref):
    offs = _in_offsets()
    (q_a, _, _, g_a, z_a, qkv_b, a_b, _, z_b, gate_a, gate_b, end) = offs
    piece = lambda lo, hi: wt_ref[lo:hi, :].astype(BF16)
    o_ref[P_GLA_QKV:P_GLA_QKV + GLA_QKV, :] = piece(q_a, g_a)
    o_ref[P_QKV_B:P_QKV_B + CONV_CH, :] = piece(qkv_b, a_b)
    small = jnp.concatenate([wt_ref[g_a:z_a, :], wt_ref[a_b:z_b, :],
                             jnp.zeros((LANES - (z_a - g_a) - (z_b - a_b), wt_ref.shape[1]), F32)], axis=0)
    o_ref[P_SMALL:P_SMALL + LANES, :] = small.astype(BF16)
    o_ref[P_GATES + GZ_GATE_A:P_GATES + GZ_GATE_A + D_MODEL, :] = piece(gate_a, gate_b)
    o_ref[P_GATES + GZ_GATE_B:P_GATES + GZ_GATE_B + D_MODEL, :] = piece(gate_b, end)
    o_ref[P_GATES + GZ_Z_A:P_GATES + GZ_Z_A + GLA_VAL, :] = piece(z_a, qkv_b)
    o_ref[P_GATES + GZ_Z_B:P_GATES + GZ_Z_B + GDN_VAL, :] = piece(z_b, gate_a)


def _regroup_w_in(w_in_t):
    cols = 256
    return pl.pallas_call(
        _regroup_kernel,
        grid=(D_MODEL // cols,),
        in_specs=[pl.BlockSpec((None, w_in_t.shape[1], cols), lambda i: (0, 0, i))],
        out_specs=pl.BlockSpec((P_COLS, cols), lambda i: (0, i)),
        out_shape=jax.ShapeDtypeStruct((P_COLS, D_MODEL), BF16),
        compiler_params=pltpu.CompilerParams(dimension_semantics=("arbitrary",),
                                             vmem_limit_bytes=VMEM_LIMIT),
        name="regroup_w_in",
    )(w_in_t)


def _head_param_col(v):
    return jnp.zeros((2 * GDN_HEADS, 1), F32).at[:GDN_HEADS, 0].set(v.astype(F32))


def _trunk(x, p, s_gla, s_gdn, conv_state, wts):
    bsz, t_len, _ = x.shape
    c = min(PROMPT_CHUNK, t_len)
    n = bsz * t_len
    x2d = x.reshape(n, D_MODEL)
    p2d = p.reshape(n, PLE_DIM)
    gla_in, gdn_in, small, gz = _inproj(x2d, wts["norm_w"], wts["w_in_r"])
    gla_in3 = gla_in.reshape(bsz, t_len, GLA_QKV)
    gdn_in3 = gdn_in.reshape(bsz, t_len, CONV_CH)
    small3 = small.reshape(bsz, t_len, LANES)
    s_gla2 = s_gla.reshape(bsz, GLA_HEADS * GLA_DK, GLA_DV)
    s_gdn2 = s_gdn.reshape(bsz, GDN_HEADS * GDN_DK, GDN_DV)
    if t_len % GROUP_ROWS == 0:
        o_a, gla_fin = _gla_prompt(gla_in3, small3, s_gla2, wts["wgg"], wts["bgg"], c)
        o_b, gdn_fin, conv_new = _gdn_prompt(gdn_in3, small3, conv_state, s_gdn2, wts["conv_w"],
                                             wts["alog_v"], wts["dtb_v"], c)
    else:
        assert GROUP_ROWS % t_len == 0 and bsz % (GROUP_ROWS // t_len) == 0 and t_len >= CONV_WIDTH - 1
        o_a, gla_fin = _gla_sample(gla_in3, small3, s_gla2, wts["wgg"], wts["bgg"])
        o_b, gdn_fin, conv_new = _gdn_sample(gdn_in3, small3, conv_state, s_gdn2, wts["conv_w"],
                                             wts["alog_v"], wts["dtb_v"])
    y = _out_stage(o_a.reshape(n, GLA_VAL), o_b.reshape(n, GDN_VAL), gz, x2d, p2d, wts["anw"], wts["bnw"],
                   wts["wua"], wts["wub"], wts["wout"], wts["wpg"], wts["wp"], wts["fnw"])
    return (y.reshape(bsz, t_len, D_MODEL),
            gla_fin.reshape(1, bsz, GLA_HEADS, GLA_DK, GLA_DV),
            gdn_fin.reshape(1, bsz, GDN_HEADS, GDN_DK, GDN_DV),
            conv_new.reshape(1, bsz, CONV_WIDTH - 1, CONV_CH))


def kernel(x_prompt, x_sample, state_gla, state_gdn, state_conv, p_prompt, p_sample, norm_w, w_in,
           w_gla_gate, b_gla_gate, gla_norm_w, conv_w, gdn_a_log, gdn_dt_bias, gdn_norm_w,
           w_up_gla, w_up_gdn, w_out, w_ple_gate, w_ple, final_norm_w):
    wgg = jnp.zeros((LANES, GLA_KEY), F32).at[SM_G:SM_G + GLA_GATE_RANK].set(w_gla_gate[0]).astype(BF16)
    wts = {
        "norm_w": norm_w[0].reshape(1, D_MODEL),
        "w_in_r": _regroup_w_in(jnp.swapaxes(w_in, 1, 2)),
        "wgg": wgg,
        "bgg": b_gla_gate[0].reshape(1, GLA_KEY),
        "conv_w": conv_w[0],
        "alog_v": _head_param_col(gdn_a_log[0]),
        "dtb_v": _head_param_col(gdn_dt_bias[0]),
        "anw": gla_norm_w[0].reshape(1, GLA_DV),
        "bnw": gdn_norm_w[0].reshape(1, GDN_DV),
        "wua": w_up_gla[0].astype(BF16),
        "wub": w_up_gdn[0].astype(BF16),
        "wout": w_out[0].astype(BF16),
        "wpg": w_ple_gate[0].astype(BF16),
        "wp": w_ple[0].astype(BF16),
        "fnw": final_norm_w.reshape(1, D_MODEL),
    }
    bsz = x_prompt.shape[0]
    dt = x_prompt.dtype
    y_p, gla_p, gdn_p, conv_p = _trunk(
        x_prompt, p_prompt[0],
        jnp.zeros((bsz, GLA_HEADS, GLA_DK, GLA_DV), dt), jnp.zeros((bsz, GDN_HEADS, GDN_DK, GDN_DV), dt),
        jnp.zeros((bsz, CONV_WIDTH - 1, CONV_CH), dt), wts)
    y_s, gla_s, gdn_s, conv_s = _trunk(x_sample, p_sample[0], state_gla[0], state_gdn[0], state_conv[0], wts)
    return (y_p, y_s, gla_p, gdn_p, conv_p, gla_s, gdn_s, conv_s)
```

```python
import functools

import jax
import jax.numpy as jnp
from jax import lax
from jax.experimental import pallas as pl
from jax.experimental.pallas import tpu as pltpu

F32 = jnp.float32
BF16 = jnp.bfloat16

D_MODEL = 1024
PLE_DIM = 256
NORM_EPS = 1e-6
GLA_HEADS = 4
GLA_DK = 64
GLA_DV = 128
GLA_KEY = GLA_HEADS * GLA_DK
GLA_VAL = GLA_HEADS * GLA_DV
GLA_GATE_RANK = 16
GLA_GATE_TEMP = 16.0
GDN_HEADS = 4
GDN_DK = 128
GDN_DV = 128
GDN_KEY = GDN_HEADS * GDN_DK
GDN_VAL = GDN_HEADS * GDN_DV
CONV_WIDTH = 4
CONV_CH = 2 * GDN_KEY + GDN_VAL
IN_SPLITS = (GLA_KEY, GLA_KEY, GLA_VAL, GLA_GATE_RANK, GLA_VAL, CONV_CH, GDN_HEADS, GDN_HEADS,
             GDN_VAL, D_MODEL, D_MODEL)

LANES = 128
SUBLANES = 8

GLA_QKV = 2 * GLA_KEY + GLA_VAL
P_GLA_QKV = 0
P_QKV_B = P_GLA_QKV + GLA_QKV
P_SMALL = P_QKV_B + CONV_CH
P_GATES = P_SMALL + LANES
GZ_COLS = 2 * D_MODEL + GLA_VAL + GDN_VAL
P_COLS = P_GATES + GZ_COLS
GZ_GATE_A = 0
GZ_GATE_B = D_MODEL
GZ_Z_A = 2 * D_MODEL
GZ_Z_B = 2 * D_MODEL + GLA_VAL
SM_G = 0
SM_A = GLA_GATE_RANK
SM_B = GLA_GATE_RANK + GDN_HEADS

PROMPT_CHUNK = 64
GROUP_ROWS = 128
BATCH_INTERLEAVE = 8
GDN_BATCH_INTERLEAVE = 8
GDN_HEAD_GROUP = 4
INVERSE_BASE_BLOCK = 16
ROW_TILE = 512
INPROJ_ROW_TILE = 1024
VMEM_LIMIT = 56 * 1024 * 1024
INPROJ_VMEM_LIMIT = 60 * 1024 * 1024


def _dot(a, b):
    return jnp.dot(a.astype(BF16), b.astype(BF16), preferred_element_type=F32)


def _dot_nt(a, b):
    return lax.dot_general(a.astype(BF16), b.astype(BF16), (((1,), (1,)), ((), ())),
                           preferred_element_type=F32)


def _split2(x):
    h1 = x.astype(BF16)
    return h1, (x - h1.astype(F32)).astype(BF16)


def _cumsum_rows(tri, x):
    x1, x2 = _split2(x)
    d = functools.partial(jnp.dot, preferred_element_type=F32)
    return d(tri, x1) + d(tri, x2)


def _softplus(x):
    return jnp.maximum(x, 0.0) + jnp.log(1.0 + jnp.exp(-jnp.abs(x)))


def _sigmoid(x):
    return 1.0 / (1.0 + jnp.exp(-x))


def _silu(x):
    return x * _sigmoid(x)


def _interleave(emitters):
    results = [None] * len(emitters)
    live = list(range(len(emitters)))
    while live:
        for i in list(live):
            try:
                next(emitters[i])
            except StopIteration as stop:
                results[i] = stop.value
                live.remove(i)
    return results


def _block_masks(r, block):
    row = lax.broadcasted_iota(jnp.int32, (r, r), 0)
    col = lax.broadcasted_iota(jnp.int32, (r, r), 1)
    same = (row // block) == (col // block)
    return same & (row >= col), same & (row > col), same


INPROJ_COL_STEP = 512


def _inproj_kernel(x_ref, nw_ref, w_ref, gla_ref, gdn_ref, small_ref, gz_ref):
    x = x_ref[...]
    xn = x * lax.rsqrt(jnp.mean(x * x, axis=-1, keepdims=True) + NORM_EPS) * nw_ref[...]
    xb = xn.astype(BF16)

    def emit(o_ref, w0, width, act=None, o0=0):
        for c0 in range(0, width, INPROJ_COL_STEP):
            c1 = min(c0 + INPROJ_COL_STEP, width)
            res = _dot_nt(xb, w_ref[w0 + c0:w0 + c1, :])
            o_ref[:, o0 + c0:o0 + c1] = (res if act is None else act(res)).astype(o_ref.dtype)

    emit(gla_ref, P_GLA_QKV, GLA_QKV)
    emit(gdn_ref, P_QKV_B, CONV_CH)
    emit(small_ref, P_SMALL, LANES)
    emit(gz_ref, P_GATES + GZ_GATE_A, 2 * D_MODEL, act=_sigmoid, o0=GZ_GATE_A)
    emit(gz_ref, P_GATES + GZ_Z_A, GLA_VAL + GDN_VAL, act=_silu, o0=GZ_Z_A)


def _inproj(x2d, norm_w, w_in_r):
    n = x2d.shape[0]
    tm = INPROJ_ROW_TILE if n >= 4 * INPROJ_ROW_TILE else min(ROW_TILE, n)
    rows = lambda i: (i, 0)
    return pl.pallas_call(
        _inproj_kernel,
        grid=(n // tm,),
        in_specs=[
            pl.BlockSpec((tm, D_MODEL), rows),
            pl.BlockSpec((1, D_MODEL), lambda i: (0, 0)),
            pl.BlockSpec((P_COLS, D_MODEL), lambda i: (0, 0), pipeline_mode=pl.Buffered(1)),
        ],
        out_specs=[
            pl.BlockSpec((tm, GLA_QKV), rows),
            pl.BlockSpec((tm, CONV_CH), rows),
            pl.BlockSpec((tm, LANES), rows),
            pl.BlockSpec((tm, GZ_COLS), rows),
        ],
        out_shape=[
            jax.ShapeDtypeStruct((n, GLA_QKV), F32),
            jax.ShapeDtypeStruct((n, CONV_CH), F32),
            jax.ShapeDtypeStruct((n, LANES), F32),
            jax.ShapeDtypeStruct((n, GZ_COLS), BF16),
        ],
        compiler_params=pltpu.CompilerParams(dimension_semantics=("arbitrary",),
                                             vmem_limit_bytes=INPROJ_VMEM_LIMIT),
        name="inproj",
    )(x2d, norm_w, w_in_r)


GLA_PAIRS = GLA_HEADS // 2
GLA_SUB_BLOCK = 16


def _gla_att_levels(r, block):
    row = lax.broadcasted_iota(jnp.int32, (r, r), 0)
    col = lax.broadcasted_iota(jnp.int32, (r, r), 1)
    sub = min(block, GLA_SUB_BLOCK)
    levels = [(None, ((row // sub) == (col // sub)) & (row >= col))]
    half = sub
    while half < block:
        levels.append((half, ((row // (2 * half)) == (col // (2 * half)))
                       & ((row // half) % 2 == 1) & ((col // half) % 2 == 0)))
        half *= 2
    return levels


def _rows_at(x, n, offset):
    return jnp.concatenate([jnp.broadcast_to(x[i + offset:i + offset + 1, :], (n, x.shape[1]))
                            for i in range(0, x.shape[0], n)], axis=0)


def _gla_prepass(q, k, v, sm, wgg, bgg, same_le, same, levels, block):
    r = q.shape[0]
    heads = range(GLA_HEADS)
    pairs = range(GLA_PAIRS)
    pre = _dot(sm, wgg) + bgg
    yield
    gk = (jnp.minimum(pre, 0.0) - jnp.log(1.0 + jnp.exp(-jnp.abs(pre)))) * (1.0 / GLA_GATE_TEMP)
    sums = _cumsum_rows(jnp.concatenate([jnp.where(same_le, 1.0, 0.0).astype(BF16),
                                         jnp.where(same, 1.0, 0.0).astype(BF16)], axis=0), gk)
    yield
    bcum = sums[:r]
    bend = sums[r:]
    bex = bcum - gk
    scale = GLA_DK ** -0.5
    q_e = q * jnp.exp(bcum) * scale
    lane = lax.broadcasted_iota(jnp.int32, (r, LANES), 1)
    in_head = [lane < GLA_DK, lane >= GLA_DK]
    pl_ = [slice(p * LANES, (p + 1) * LANES) for p in pairs]
    head_only = lambda x, h: jnp.where(in_head[h % 2], x[:, pl_[h // 2]], 0.0)
    qm = [head_only(q_e, h) for h in heads]
    sub = min(block, GLA_SUB_BLOCK)
    start = _rows_at(bex, sub, 0)
    groups = {None: (q * jnp.exp(bcum - start) * scale, [(levels[0][1], k * jnp.exp(start - bcum))])}
    for half, mask in levels[1:]:
        key = None if half == sub else half
        if key not in groups:
            groups[key] = (q * jnp.exp(bcum - _rows_at(bex, half, 0)) * scale, [])
        groups[key][1].append((mask, k * jnp.exp(_rows_at(bcum, half, half - 1) - bcum)))
    att = [jnp.zeros((r, r), F32)] * GLA_HEADS
    for q_l, parts in groups.values():
        for h in heads:
            keys = [k_l[:, pl_[h // 2]].astype(BF16) for _, k_l in parts]
            prod = _dot_nt(head_only(q_l, h), keys[0] if len(keys) == 1 else jnp.concatenate(keys, axis=0))
            for i, (mask, _) in enumerate(parts):
                att[h] = jnp.where(mask, prod[:, i * r:(i + 1) * r], att[h])
    yield
    o_intra = [_dot(att[h], v[h]) for h in heads]
    k_end = k * jnp.exp(bend - bcum)
    k_end_t = [k_end[:, pl_[p]].T for p in pairs]
    bend_t = [bend[:, pl_[p]].T for p in pairs]
    yield
    return qm, o_intra, k_end_t, bend_t


def _gla_prompt_kernel(qkv_ref, small_ref, s0_ref, wgg_ref, bgg_ref, o_ref, sfin_ref, s_scr, *, c):
    t = pl.program_id(0)
    nb, r, _ = qkv_ref.shape

    @pl.when(t == 0)
    def _():
        s_scr[...] = s0_ref[...]

    same_le, _, same = _block_masks(r, c)
    levels = _gla_att_levels(r, c)
    zeros = jnp.zeros((c, GLA_DV), F32)
    n_sub = r // c
    heads = range(GLA_HEADS)
    pairs = range(GLA_PAIRS)

    def one_batch(b):
        q = qkv_ref[b, :, 0:GLA_KEY]
        k = qkv_ref[b, :, GLA_KEY:2 * GLA_KEY]
        v = [qkv_ref[b, :, 2 * GLA_KEY + h * GLA_DV:2 * GLA_KEY + (h + 1) * GLA_DV] for h in heads]
        qm, o_intra, k_end_t, bend_t = yield from _gla_prepass(
            q, k, v, small_ref[b], wgg_ref[...], bgg_ref[...], same_le, same, levels, c)
        s = [s_scr[b, p * LANES:(p + 1) * LANES, :] for p in pairs]
        for i in range(n_sub):
            rows = slice(i * c, (i + 1) * c)
            ws = [_dot(jnp.concatenate([qm[2 * p][rows], qm[2 * p + 1][rows]], axis=0), s[p]) for p in pairs]
            padded = [jnp.concatenate([zeros] * i + [v[h][rows]] + [zeros] * (n_sub - 1 - i), axis=0)
                      for h in heads]
            upd = [_dot(k_end_t[h // 2][(h % 2) * GLA_DK:(h % 2 + 1) * GLA_DK, :], padded[h]) for h in heads]
            yield
            for h in heads:
                o_ref[b, rows, h * GLA_DV:(h + 1) * GLA_DV] = (
                    o_intra[h][rows] + ws[h // 2][(h % 2) * c:(h % 2 + 1) * c]).astype(o_ref.dtype)
            s = [s[p] * jnp.exp(bend_t[p][:, i * c:i * c + 1])
                 + jnp.concatenate([upd[2 * p], upd[2 * p + 1]], axis=0) for p in pairs]
        for p in pairs:
            s_scr[b, p * LANES:(p + 1) * LANES, :] = s[p]

    def per_step(i, carry):
        _interleave([one_batch(i * BATCH_INTERLEAVE + j) for j in range(BATCH_INTERLEAVE)])
        return carry

    lax.fori_loop(0, nb // BATCH_INTERLEAVE, per_step, None)

    @pl.when(t == pl.num_programs(0) - 1)
    def _():
        sfin_ref[...] = s_scr[...]


def _gla_sample_kernel(qkv_ref, small_ref, s0_ref, wgg_ref, bgg_ref, o_ref, sfin_ref):
    ns, c, _ = qkv_ref.shape
    r = ns * c
    heads = range(GLA_HEADS)
    same_le, _, same = _block_masks(r, c)
    levels = _gla_att_levels(r, c)
    q = qkv_ref[:, :, 0:GLA_KEY].reshape(r, GLA_KEY)
    k = qkv_ref[:, :, GLA_KEY:2 * GLA_KEY].reshape(r, GLA_KEY)
    v = [qkv_ref[:, :, 2 * GLA_KEY + h * GLA_DV:2 * GLA_KEY + (h + 1) * GLA_DV].reshape(r, GLA_DV)
         for h in heads]
    (qm, o_intra, k_end_t, bend_t), = _interleave([_gla_prepass(
        q, k, v, small_ref[...].reshape(r, LANES), wgg_ref[...], bgg_ref[...], same_le, same, levels, c)])
    seq_of_row = lax.broadcasted_iota(jnp.int32, (r, GLA_DV), 0) // c
    for p in range(GLA_PAIRS):
        ps = slice(p * LANES, (p + 1) * LANES)
        inter = [[], []]
        for s in range(ns):
            rows = slice(s * c, (s + 1) * c)
            ws = _dot(jnp.concatenate([qm[2 * p][rows], qm[2 * p + 1][rows]], axis=0), s0_ref[s, ps, :])
            inter[0].append(ws[:c])
            inter[1].append(ws[c:])
        for hh in range(2):
            h = 2 * p + hh
            o = o_intra[h] + jnp.concatenate(inter[hh], axis=0)
            o_ref[:, h * GLA_DV:(h + 1) * GLA_DV] = o.astype(o_ref.dtype)
        for s in range(ns):
            upd = [_dot(k_end_t[p][hh * GLA_DK:(hh + 1) * GLA_DK, :],
                        jnp.where(seq_of_row == s, v[2 * p + hh], 0.0)) for hh in range(2)]
            sfin_ref[s, ps, :] = (s0_ref[s, ps, :] * jnp.exp(bend_t[p][:, s * c:s * c + 1])
                                  + jnp.concatenate(upd, axis=0))


def _gla_prompt(qkv3, small3, s0, wgg, bgg, c):
    bsz, t_len, _ = qkv3.shape
    rows = GLA_HEADS * GLA_DK
    r = GROUP_ROWS
    full3 = lambda t: (0, 0, 0)
    const2 = lambda t: (0, 0)
    return pl.pallas_call(
        functools.partial(_gla_prompt_kernel, c=c),
        grid=(t_len // r,),
        in_specs=[
            pl.BlockSpec((bsz, r, GLA_QKV), lambda t: (0, t, 0)),
            pl.BlockSpec((bsz, r, LANES), lambda t: (0, t, 0)),
            pl.BlockSpec((bsz, rows, GLA_DV), full3),
            pl.BlockSpec((LANES, GLA_KEY), const2),
            pl.BlockSpec((1, GLA_KEY), const2),
        ],
        out_specs=[
            pl.BlockSpec((bsz, r, GLA_VAL), lambda t: (0, t, 0)),
            pl.BlockSpec((bsz, rows, GLA_DV), full3),
        ],
        out_shape=[
            jax.ShapeDtypeStruct((bsz, t_len, GLA_VAL), BF16),
            jax.ShapeDtypeStruct((bsz, rows, GLA_DV), F32),
        ],
        scratch_shapes=[pltpu.VMEM((bsz, rows, GLA_DV), F32)],
        compiler_params=pltpu.CompilerParams(dimension_semantics=("arbitrary",),
                                             vmem_limit_bytes=VMEM_LIMIT),
        name="gla_prompt",
    )(qkv3, small3, s0, wgg, bgg)


def _gla_sample(qkv3, small3, s0, wgg, bgg):
    bsz, c, _ = qkv3.shape
    rows = GLA_HEADS * GLA_DK
    ns = GROUP_ROWS // c
    grp = lambda g: (g, 0, 0)
    const2 = lambda g: (0, 0)
    return pl.pallas_call(
        _gla_sample_kernel,
        grid=(bsz // ns,),
        in_specs=[
            pl.BlockSpec((ns, c, GLA_QKV), grp),
            pl.BlockSpec((ns, c, LANES), grp),
            pl.BlockSpec((ns, rows, GLA_DV), grp),
            pl.BlockSpec((LANES, GLA_KEY), const2),
            pl.BlockSpec((1, GLA_KEY), const2),
        ],
        out_specs=[
            pl.BlockSpec((ns * c, GLA_VAL), lambda g: (g, 0)),
            pl.BlockSpec((ns, rows, GLA_DV), grp),
        ],
        out_shape=[
            jax.ShapeDtypeStruct((bsz * c, GLA_VAL), BF16),
            jax.ShapeDtypeStruct((bsz, rows, GLA_DV), F32),
        ],
        compiler_params=pltpu.CompilerParams(dimension_semantics=("arbitrary",),
                                             vmem_limit_bytes=VMEM_LIMIT),
        name="gla_sample",
    )(qkv3, small3, s0, wgg, bgg)


def _block_unit_lower_inverse(a_list, block):
    r = a_list[0].shape[0]
    row = lax.broadcasted_iota(jnp.int32, (r, r), 0)
    col = lax.broadcasted_iota(jnp.int32, (r, r), 1)
    in_block = lambda n: (row // n) == (col // n)
    base = min(block, INVERSE_BASE_BLOCK)
    a_base = a_list if base == block else [jnp.where(in_block(base), a, 0.0) for a in a_list]
    xs = [jnp.where(row == col, 1.0, 0.0) - a for a in a_base]
    ps = [_dot(a, a) for a in a_base]
    n = 2
    while n < base:
        yield
        last = 2 * n >= base
        ms = [_dot(x if last else jnp.concatenate([p.astype(BF16), x.astype(BF16)], axis=0), p)
              for p, x in zip(ps, xs)]
        ps = [m[:r] for m in ms]
        xs = [x + m[-r:] for x, m in zip(xs, ms)]
        n *= 2
    n = base
    while n < block:
        yield
        between = in_block(2 * n) & jnp.logical_not(in_block(n))
        ts = [_dot(jnp.where(between, a, 0.0), x) for a, x in zip(a_list, xs)]
        yield
        xs = [x - _dot(x, t) for x, t in zip(xs, ts)]
        n *= 2
    return xs


def _gdn_decays(sm, alog_c, dtb_c, same_lt, same):
    r = sm.shape[0]
    heads = range(GDN_HEADS)
    ab = sm.T[SM_A:SM_A + 2 * GDN_HEADS, :]
    g8 = -jnp.exp(alog_c) * _softplus(ab + dtb_c)
    beta_c = _sigmoid(ab).T
    g1, g2 = _split2(g8)
    same_ge = same & jnp.logical_not(same_lt)
    sel = jnp.concatenate([jnp.where(same_ge, 1.0, 0.0).astype(BF16),
                           jnp.where(same, 1.0, 0.0).astype(BF16)], axis=1)
    sums = jnp.dot(jnp.concatenate([g1, g2], axis=0), sel, preferred_element_type=F32)
    sums = sums[:2 * GDN_HEADS] + sums[2 * GDN_HEADS:]
    dec_rows = sums[:, :r]
    dec_c = dec_rows.T
    dend_c = sums[:, r:].T
    return ([dec_c[:, h:h + 1] for h in heads], [dec_rows[h:h + 1, :] for h in heads],
            [dend_c[:, h:h + 1] for h in heads], [beta_c[:, GDN_HEADS + h:GDN_HEADS + h + 1] for h in heads])


def _gdn_head_cols(h):
    return [slice(part * GDN_KEY + h * GDN_DK, part * GDN_KEY + (h + 1) * GDN_DK) for part in range(3)]


def _gdn_prepass(qkv_of_head, sm, alog_c, dtb_c, same_le, same_lt, same, block, heads):
    dcol, drow, dend, beta = ([x[h] for h in heads] for x in _gdn_decays(sm, alog_c, dtb_c, same_lt, same))
    idx = range(len(heads))
    gamma = [jnp.where(same_le, jnp.exp(jnp.where(same_le, dcol[i] - drow[i], 0.0)), 0.0) for i in idx]
    q, k, v = (list(x) for x in zip(*[qkv_of_head(h) for h in heads]))
    r = q[0].shape[0]
    q = [x * lax.rsqrt(jnp.sum(x * x, axis=-1, keepdims=True) + NORM_EPS) * (GDN_DK ** -0.5) for x in q]
    k = [x * lax.rsqrt(jnp.sum(x * x, axis=-1, keepdims=True) + NORM_EPS) for x in k]
    yield
    kb = [k[i] * beta[i] for i in idx]
    vb = [v[i] * beta[i] for i in idx]
    kq = [_dot_nt(jnp.concatenate([kb[i], q[i]], axis=0), k[i]) for i in idx]
    yield
    a_mat = [jnp.where(same_lt, kq[i][:r] * gamma[i], 0.0) for i in idx]
    qk = [(kq[i][r:] * gamma[i]).astype(BF16) for i in idx]
    t_inv = yield from _block_unit_lower_inverse(a_mat, block)
    edec = [jnp.exp(dcol[i]) for i in idx]
    uw = [_dot(t_inv[i], jnp.concatenate([vb[i], kb[i] * edec[i]], axis=1)) for i in idx]
    q_e = [(q[i] * edec[i]).astype(BF16) for i in idx]
    k_end_t = [(k[i] * jnp.exp(dend[i] - dcol[i])).T.astype(BF16) for i in idx]
    yield
    u = [x[:, :GDN_DV] for x in uw]
    w = [x[:, GDN_DV:].astype(BF16) for x in uw]
    return u, w, q_e, qk, k_end_t, dend


def _conv_silu(win, cw_ref):
    conv = win(0) * cw_ref[CONV_WIDTH - 1:CONV_WIDTH, :]
    for j in range(1, CONV_WIDTH):
        conv = conv + win(j) * cw_ref[CONV_WIDTH - 1 - j:CONV_WIDTH - j, :]
    return _silu(conv)

def _gdn_prompt_kernel(qkv_ref, small_ref, cs_ref, s0_ref, cw_ref, alog_ref, dtb_ref,
                       o_ref, sfin_ref, cnew_ref, xbuf, prev, s_scr, *, c):
    t = pl.program_id(0)
    nb, r, _ = qkv_ref.shape
    keep = CONV_WIDTH - 1
    base = SUBLANES

    @pl.when(t == 0)
    def _():
        s_scr[...] = s0_ref[...]
        prev[:, base - keep:base, :] = cs_ref[...]

    same_le, same_lt, same = _block_masks(r, c)
    zeros = jnp.zeros((c, GDN_DV), F32)
    n_sub = r // c

    def one_batch(b, slot, heads, first):
        xb = xbuf.at[slot]
        if first:
            xb[base - keep:base, :] = prev[b, base - keep:base, :]
            xb[base:base + r, :] = qkv_ref[b]
            prev[b, base - keep:base, :] = xb[base + r - keep:base + r, :]

        def qkv_of_head(h):
            return [_conv_silu(lambda j: xb[base - j:base - j + r, cols], cw_ref.at[:, cols])
                    for cols in _gdn_head_cols(h)]

        u, w, q_e, qk, k_end_t, dend = yield from _gdn_prepass(
            qkv_of_head, small_ref[b], alog_ref[...], dtb_ref[...], same_le, same_lt, same, c, heads)
        idx = range(len(heads))
        hs = [slice(h * GDN_DK, (h + 1) * GDN_DK) for h in heads]
        s = [s_scr[b, hs[i], :] for i in idx]
        for ci in range(n_sub):
            rows = slice(ci * c, (ci + 1) * c)
            ws = [_dot(jnp.concatenate([w[i][rows], q_e[i][rows]], axis=0), s[i]) for i in idx]
            yield
            v_new = [u[i][rows] - ws[i][:c] for i in idx]
            padded = [jnp.concatenate([zeros] * ci + [v_new[i]] + [zeros] * (n_sub - 1 - ci), axis=0)
                      for i in idx]
            upd = [_dot(jnp.concatenate([qk[i][rows], k_end_t[i]], axis=0), padded[i]) for i in idx]
            yield
            for i in idx:
                o_ref[b, rows, hs[i]] = (ws[i][c:] + upd[i][:c]).astype(o_ref.dtype)
            s = [s[i] * jnp.exp(dend[i][ci * c:ci * c + 1, :]) + upd[i][c:] for i in idx]
        for i in idx:
            s_scr[b, hs[i], :] = s[i]

    def per_step(step, carry):
        for g in range(0, GDN_HEADS, GDN_HEAD_GROUP):
            heads = tuple(range(g, g + GDN_HEAD_GROUP))
            _interleave([one_batch(step * GDN_BATCH_INTERLEAVE + j, j, heads, g == 0)
                         for j in range(GDN_BATCH_INTERLEAVE)])
        return carry

    lax.fori_loop(0, nb // GDN_BATCH_INTERLEAVE, per_step, None)

    @pl.when(t == pl.num_programs(0) - 1)
    def _():
        sfin_ref[...] = s_scr[...]
        cnew_ref[...] = prev[:, base - keep:base, :]


def _gdn_sample_kernel(qkv_ref, small_ref, cs_ref, s0_ref, cw_ref, alog_ref, dtb_ref,
                       o_ref, sfin_ref, cnew_ref, xbuf):
    ns, c, _ = qkv_ref.shape
    r = ns * c
    keep = CONV_WIDTH - 1
    base = SUBLANES
    xbuf[:, base - keep:base, :] = cs_ref[...]
    xbuf[:, base:base + c, :] = qkv_ref[...]
    conv = _conv_silu(lambda j: xbuf[:, base - j:base - j + c, :], cw_ref).reshape(r, CONV_CH)
    cnew_ref[...] = xbuf[:, base + c - keep:base + c, :]

    same_le, same_lt, same = _block_masks(r, c)
    seq_of_row = lax.broadcasted_iota(jnp.int32, (r, GDN_DV), 0) // c
    (u, w, q_e, qk, k_end_t, dend), = _interleave([_gdn_prepass(
        lambda h: [conv[:, cols] for cols in _gdn_head_cols(h)], small_ref[...].reshape(r, LANES),
        alog_ref[...], dtb_ref[...], same_le, same_lt, same, c, tuple(range(GDN_HEADS)))])
    for h in range(GDN_HEADS):
        hs = slice(h * GDN_DK, (h + 1) * GDN_DK)
        v_parts, o_parts = [], []
        for s in range(ns):
            rows = slice(s * c, (s + 1) * c)
            ws = _dot(jnp.concatenate([w[h][rows], q_e[h][rows]], axis=0), s0_ref[s, hs, :])
            v_parts.append(u[h][rows] - ws[:c])
            o_parts.append(ws[c:])
        v_new = jnp.concatenate(v_parts, axis=0)
        o = jnp.concatenate(o_parts, axis=0) + _dot(qk[h], v_new)
        o_ref[:, h * GDN_DV:(h + 1) * GDN_DV] = o.astype(o_ref.dtype)
        for s in range(ns):
            upd = _dot(k_end_t[h], jnp.where(seq_of_row == s, v_new, 0.0))
            sfin_ref[s, hs, :] = s0_ref[s, hs, :] * jnp.exp(dend[h][s * c:s * c + 1, :]) + upd


def _gdn_prompt(qkv3, small3, conv_state, s0, conv_w, alog_v, dtb_v, c):
    bsz, t_len, _ = qkv3.shape
    rows = GDN_HEADS * GDN_DK
    keep = CONV_WIDTH - 1
    r = GROUP_ROWS
    full3 = lambda t: (0, 0, 0)
    const2 = lambda t: (0, 0)
    return pl.pallas_call(
        functools.partial(_gdn_prompt_kernel, c=c),
        grid=(t_len // r,),
        in_specs=[
            pl.BlockSpec((bsz, r, CONV_CH), lambda t: (0, t, 0)),
            pl.BlockSpec((bsz, r, LANES), lambda t: (0, t, 0)),
            pl.BlockSpec((bsz, keep, CONV_CH), full3),
            pl.BlockSpec((bsz, rows, GDN_DV), full3),
            pl.BlockSpec((CONV_WIDTH, CONV_CH), const2),
            pl.BlockSpec((2 * GDN_HEADS, 1), const2),
            pl.BlockSpec((2 * GDN_HEADS, 1), const2),
        ],
        out_specs=[
            pl.BlockSpec((bsz, r, GDN_VAL), lambda t: (0, t, 0)),
            pl.BlockSpec((bsz, rows, GDN_DV), full3),
            pl.BlockSpec((bsz, keep, CONV_CH), full3),
        ],
        out_shape=[
            jax.ShapeDtypeStruct((bsz, t_len, GDN_VAL), BF16),
            jax.ShapeDtypeStruct((bsz, rows, GDN_DV), F32),
            jax.ShapeDtypeStruct((bsz, keep, CONV_CH), F32),
        ],
        scratch_shapes=[pltpu.VMEM((GDN_BATCH_INTERLEAVE, SUBLANES + r, CONV_CH), F32),
                        pltpu.VMEM((bsz, SUBLANES, CONV_CH), F32),
                        pltpu.VMEM((bsz, rows, GDN_DV), F32)],
        compiler_params=pltpu.CompilerParams(dimension_semantics=("arbitrary",),
                                             vmem_limit_bytes=VMEM_LIMIT),
        name="gdn_prompt",
    )(qkv3, small3, conv_state, s0, conv_w, alog_v, dtb_v)


def _gdn_sample(qkv3, small3, conv_state, s0, conv_w, alog_v, dtb_v):
    bsz, c, _ = qkv3.shape
    rows = GDN_HEADS * GDN_DK
    keep = CONV_WIDTH - 1
    ns = GROUP_ROWS // c
    grp = lambda g: (g, 0, 0)
    const2 = lambda g: (0, 0)
    return pl.pallas_call(
        _gdn_sample_kernel,
        grid=(bsz // ns,),
        in_specs=[
            pl.BlockSpec((ns, c, CONV_CH), grp),
            pl.BlockSpec((ns, c, LANES), grp),
            pl.BlockSpec((ns, keep, CONV_CH), grp),
            pl.BlockSpec((ns, rows, GDN_DV), grp),
            pl.BlockSpec((CONV_WIDTH, CONV_CH), const2),
            pl.BlockSpec((2 * GDN_HEADS, 1), const2),
            pl.BlockSpec((2 * GDN_HEADS, 1), const2),
        ],
        out_specs=[
            pl.BlockSpec((ns * c, GDN_VAL), lambda g: (g, 0)),
            pl.BlockSpec((ns, rows, GDN_DV), grp),
            pl.BlockSpec((ns, keep, CONV_CH), grp),
        ],
        out_shape=[
            jax.ShapeDtypeStruct((bsz * c, GDN_VAL), BF16),
            jax.ShapeDtypeStruct((bsz, rows, GDN_DV), F32),
            jax.ShapeDtypeStruct((bsz, keep, CONV_CH), F32),
        ],
        scratch_shapes=[pltpu.VMEM((ns, 2 * SUBLANES, CONV_CH), F32)],
        compiler_params=pltpu.CompilerParams(dimension_semantics=("arbitrary",),
                                             vmem_limit_bytes=VMEM_LIMIT),
        name="gdn_sample",
    )(qkv3, small3, conv_state, s0, conv_w, alog_v, dtb_v)


def _head_norm_gate(o, silu_z, w):
    parts = []
    for h in range(o.shape[-1] // LANES):
        oh = o[:, h * LANES:(h + 1) * LANES]
        parts.append(oh * lax.rsqrt(jnp.mean(oh * oh, axis=-1, keepdims=True) + NORM_EPS) * w)
    return jnp.concatenate(parts, axis=-1) * silu_z


def _out_kernel(oa_ref, ob_ref, ga_ref, gb_ref, za_ref, zb_ref, x_ref, p_ref, anw_ref, bnw_ref,
                wua_ref, wub_ref, wout_ref, wpg_ref, wp_ref, fnw_ref, y_ref):
    f32 = lambda ref: ref[...].astype(F32)
    ya = _dot(_head_norm_gate(f32(oa_ref), f32(za_ref), anw_ref[...]), wua_ref[...])
    yb = _dot(_head_norm_gate(f32(ob_ref), f32(zb_ref), bnw_ref[...]), wub_ref[...])
    merged = f32(ga_ref) * ya + f32(gb_ref) * yb
    h1 = x_ref[...] + _dot(merged, wout_ref[...])
    h2 = h1 + _sigmoid(_dot(h1, wpg_ref[...])) * _dot(p_ref[...], wp_ref[...])
    y_ref[...] = h2 * lax.rsqrt(jnp.mean(h2 * h2, axis=-1, keepdims=True) + NORM_EPS) * fnw_ref[...]


def _out_stage(o_a, o_b, gz, x2d, p2d, anw, bnw, wua, wub, wout, wpg, wp, fnw):
    n = x2d.shape[0]
    tm = min(ROW_TILE, n)
    const = lambda i: (0, 0)
    return pl.pallas_call(
        _out_kernel,
        grid=(n // tm,),
        in_specs=[
            pl.BlockSpec((tm, GLA_VAL), lambda i: (i, 0)),
            pl.BlockSpec((tm, GDN_VAL), lambda i: (i, 0)),
            pl.BlockSpec((tm, D_MODEL), lambda i: (i, GZ_GATE_A // D_MODEL)),
            pl.BlockSpec((tm, D_MODEL), lambda i: (i, GZ_GATE_B // D_MODEL)),
            pl.BlockSpec((tm, GLA_VAL), lambda i: (i, GZ_Z_A // GLA_VAL)),
            pl.BlockSpec((tm, GDN_VAL), lambda i: (i, GZ_Z_B // GDN_VAL)),
            pl.BlockSpec((tm, D_MODEL), lambda i: (i, 0)),
            pl.BlockSpec((tm, PLE_DIM), lambda i: (i, 0)),
            pl.BlockSpec((1, GLA_DV), const),
            pl.BlockSpec((1, GDN_DV), const),
            pl.BlockSpec((GLA_VAL, D_MODEL), const),
            pl.BlockSpec((GDN_VAL, D_MODEL), const),
            pl.BlockSpec((D_MODEL, D_MODEL), const),
            pl.BlockSpec((D_MODEL, D_MODEL), const),
            pl.BlockSpec((PLE_DIM, D_MODEL), const),
            pl.BlockSpec((1, D_MODEL), const),
        ],
        out_specs=pl.BlockSpec((tm, D_MODEL), lambda i: (i, 0)),
        out_shape=jax.ShapeDtypeStruct((n, D_MODEL), F32),
        compiler_params=pltpu.CompilerParams(dimension_semantics=("arbitrary",),
                                             vmem_limit_bytes=VMEM_LIMIT),
        name="out_stage",
    )(o_a, o_b, gz, gz, gz, gz, x2d, p2d, anw, bnw, wua, wub, wout, wpg, wp, fnw)


def _in_offsets():
    offs = [0]
    for s in IN_SPLITS:
        offs.append(offs[-1] + s)
    return offs


def _regroup_kernel(wt_ref, o_ref):
    offs = _in_offsets()
    (q_a, _, _, g_a, z_a, qkv_b, a_b, _, z_b, gate_a, gate_b, end) = offs
    piece = lambda lo, hi: wt_ref[lo:hi, :].astype(BF16)
    o_ref[P_GLA_QKV:P_GLA_QKV + GLA_QKV, :] = piece(q_a, g_a)
    o_ref[P_QKV_B:P_QKV_B + CONV_CH, :] = piece(qkv_b, a_b)
    small = jnp.concatenate([wt_ref[g_a:z_a, :], wt_ref[a_b:z_b, :],
                             jnp.zeros((LANES - (z_a - g_a) - (z_b - a_b), wt_ref.shape[1]), F32)], axis=0)
    o_ref[P_SMALL:P_SMALL + LANES, :] = small.astype(BF16)
    o_ref[P_GATES + GZ_GATE_A:P_GATES + GZ_GATE_A + D_MODEL, :] = piece(gate_a, gate_b)
    o_ref[P_GATES + GZ_GATE_B:P_GATES + GZ_GATE_B + D_MODEL, :] = piece(gate_b, end)
    o_ref[P_GATES + GZ_Z_A:P_GATES + GZ_Z_A + GLA_VAL, :] = piece(z_a, qkv_b)
    o_ref[P_GATES + GZ_Z_B:P_GATES + GZ_Z_B + GDN_VAL, :] = piece(z_b, gate_a)


def _regroup_w_in(w_in_t):
    cols = 256
    return pl.pallas_call(
        _regroup_kernel,
        grid=(D_MODEL // cols,),
        in_specs=[pl.BlockSpec((None, w_in_t.shape[1], cols), lambda i: (0, 0, i))],
        out_specs=pl.BlockSpec((P_COLS, cols), lambda i: (0, i)),
        out_shape=jax.ShapeDtypeStruct((P_COLS, D_MODEL), BF16),
        compiler_params=pltpu.CompilerParams(dimension_semantics=("arbitrary",),
                                             vmem_limit_bytes=VMEM_LIMIT),
        name="regroup_w_in",
    )(w_in_t)


def _head_param_col(v):
    return jnp.zeros((2 * GDN_HEADS, 1), F32).at[:GDN_HEADS, 0].set(v.astype(F32))


def _trunk(x, p, s_gla, s_gdn, conv_state, wts):
    bsz, t_len, _ = x.shape
    c = min(PROMPT_CHUNK, t_len)
    n = bsz * t_len
    x2d = x.reshape(n, D_MODEL)
    p2d = p.reshape(n, PLE_DIM)
    gla_in, gdn_in, small, gz = _inproj(x2d, wts["norm_w"], wts["w_in_r"])
    gla_in3 = gla_in.reshape(bsz, t_len, GLA_QKV)
    gdn_in3 = gdn_in.reshape(bsz, t_len, CONV_CH)
    small3 = small.reshape(bsz, t_len, LANES)
    s_gla2 = s_gla.reshape(bsz, GLA_HEADS * GLA_DK, GLA_DV)
    s_gdn2 = s_gdn.reshape(bsz, GDN_HEADS * GDN_DK, GDN_DV)
    if t_len % GROUP_ROWS == 0:
        o_a, gla_fin = _gla_prompt(gla_in3, small3, s_gla2, wts["wgg"], wts["bgg"], c)
        o_b, gdn_fin, conv_new = _gdn_prompt(gdn_in3, small3, conv_state, s_gdn2, wts["conv_w"],
                                             wts["alog_v"], wts["dtb_v"], c)
    else:
        assert GROUP_ROWS % t_len == 0 and bsz % (GROUP_ROWS // t_len) == 0 and t_len >= CONV_WIDTH - 1
        o_a, gla_fin = _gla_sample(gla_in3, small3, s_gla2, wts["wgg"], wts["bgg"])
        o_b, gdn_fin, conv_new = _gdn_sample(gdn_in3, small3, conv_state, s_gdn2, wts["conv_w"],
                                             wts["alog_v"], wts["dtb_v"])
    y = _out_stage(o_a.reshape(n, GLA_VAL), o_b.reshape(n, GDN_VAL), gz, x2d, p2d, wts["anw"], wts["bnw"],
                   wts["wua"], wts["wub"], wts["wout"], wts["wpg"], wts["wp"], wts["fnw"])
    return (y.reshape(bsz, t_len, D_MODEL),
            gla_fin.reshape(1, bsz, GLA_HEADS, GLA_DK, GLA_DV),
            gdn_fin.reshape(1, bsz, GDN_HEADS, GDN_DK, GDN_DV),
            conv_new.reshape(1, bsz, CONV_WIDTH - 1, CONV_CH))


def kernel(x_prompt, x_sample, state_gla, state_gdn, state_conv, p_prompt, p_sample, norm_w, w_in,
           w_gla_gate, b_gla_gate, gla_norm_w, conv_w, gdn_a_log, gdn_dt_bias, gdn_norm_w,
           w_up_gla, w_up_gdn, w_out, w_ple_gate, w_ple, final_norm_w):
    wgg = jnp.zeros((LANES, GLA_KEY), F32).at[SM_G:SM_G + GLA_GATE_RANK].set(w_gla_gate[0]).astype(BF16)
    wts = {
        "norm_w": norm_w[0].reshape(1, D_MODEL),
        "w_in_r": _regroup_w_in(jnp.swapaxes(w_in, 1, 2)),
        "wgg": wgg,
        "bgg": b_gla_gate[0].reshape(1, GLA_KEY),
        "conv_w": conv_w[0],
        "alog_v": _head_param_col(gdn_a_log[0]),
        "dtb_v": _head_param_col(gdn_dt_bias[0]),
        "anw": gla_norm_w[0].reshape(1, GLA_DV),
        "bnw": gdn_norm_w[0].reshape(1, GDN_DV),
        "wua": w_up_gla[0].astype(BF16),
        "wub": w_up_gdn[0].astype(BF16),
        "wout": w_out[0].astype(BF16),
        "wpg": w_ple_gate[0].astype(BF16),
        "wp": w_ple[0].astype(BF16),
        "fnw": final_norm_w.reshape(1, D_MODEL),
    }
    bsz = x_prompt.shape[0]
    dt = x_prompt.dtype
    y_p, gla_p, gdn_p, conv_p = _trunk(
        x_prompt, p_prompt[0],
        jnp.zeros((bsz, GLA_HEADS, GLA_DK, GLA_DV), dt), jnp.zeros((bsz, GDN_HEADS, GDN_DK, GDN_DV), dt),
        jnp.zeros((bsz, CONV_WIDTH - 1, CONV_CH), dt), wts)
    y_s, gla_s, gdn_s, conv_s = _trunk(x_sample, p_sample[0], state_gla[0], state_gdn[0], state_conv[0], wts)
    return (y_p, y_s, gla_p, gdn_p, conv_p, gla_s, gdn_s, conv_s)
```

```python
import functools

import jax
import jax.numpy as jnp
from jax import lax
from jax.experimental import pallas as pl
from jax.experimental.pallas import tpu as pltpu

F32 = jnp.float32
BF16 = jnp.bfloat16

D_MODEL = 1024
PLE_DIM = 256
NORM_EPS = 1e-6
GLA_HEADS = 4
GLA_DK = 64
GLA_DV = 128
GLA_KEY = GLA_HEADS * GLA_DK
GLA_VAL = GLA_HEADS * GLA_DV
GLA_GATE_RANK = 16
GLA_GATE_TEMP = 16.0
GDN_HEADS = 4
GDN_DK = 128
GDN_DV = 128
GDN_KEY = GDN_HEADS * GDN_DK
GDN_VAL = GDN_HEADS * GDN_DV
CONV_WIDTH = 4
CONV_CH = 2 * GDN_KEY + GDN_VAL
IN_SPLITS = (GLA_KEY, GLA_KEY, GLA_VAL, GLA_GATE_RANK, GLA_VAL, CONV_CH, GDN_HEADS, GDN_HEADS,
             GDN_VAL, D_MODEL, D_MODEL)

LANES = 128
SUBLANES = 8

GLA_QKV = 2 * GLA_KEY + GLA_VAL
P_GLA_QKV = 0
P_QKV_B = P_GLA_QKV + GLA_QKV
P_SMALL = P_QKV_B + CONV_CH
P_GATES = P_SMALL + LANES
GZ_COLS = 2 * D_MODEL + GLA_VAL + GDN_VAL
P_COLS = P_GATES + GZ_COLS
GZ_GATE_A = 0
GZ_GATE_B = D_MODEL
GZ_Z_A = 2 * D_MODEL
GZ_Z_B = 2 * D_MODEL + GLA_VAL
SM_G = 0
SM_A = GLA_GATE_RANK
SM_B = GLA_GATE_RANK + GDN_HEADS

PROMPT_CHUNK = 64
GROUP_ROWS = 128
BATCH_INTERLEAVE = 8
GDN_BATCH_INTERLEAVE = 8
GDN_HEAD_GROUP = 4
INVERSE_BASE_BLOCK = 16
ROW_TILE = 512
INPROJ_ROW_TILE = 512
VMEM_LIMIT = 40 * 1024 * 1024
INPROJ_VMEM_LIMIT = VMEM_LIMIT
GDN_VMEM_LIMIT = 48 * 1024 * 1024


def _dot(a, b):
    return jnp.dot(a.astype(BF16), b.astype(BF16), preferred_element_type=F32)


def _dot_nt(a, b):
    return lax.dot_general(a.astype(BF16), b.astype(BF16), (((1,), (1,)), ((), ())),
                           preferred_element_type=F32)


def _split2(x):
    h1 = x.astype(BF16)
    return h1, (x - h1.astype(F32)).astype(BF16)


def _cumsum_rows(tri, x):
    x1, x2 = _split2(x)
    d = functools.partial(jnp.dot, preferred_element_type=F32)
    return d(tri, x1) + d(tri, x2)


def _softplus(x):
    return jnp.maximum(x, 0.0) + jnp.log(1.0 + jnp.exp(-jnp.abs(x)))


def _sigmoid(x):
    return 1.0 / (1.0 + jnp.exp(-x))


def _silu(x):
    return x * _sigmoid(x)


def _interleave(emitters):
    results = [None] * len(emitters)
    live = list(range(len(emitters)))
    while live:
        for i in list(live):
            try:
                next(emitters[i])
            except StopIteration as stop:
                results[i] = stop.value
                live.remove(i)
    return results


def _block_masks(r, block):
    row = lax.broadcasted_iota(jnp.int32, (r, r), 0)
    col = lax.broadcasted_iota(jnp.int32, (r, r), 1)
    same = (row // block) == (col // block)
    return same & (row >= col), same & (row > col), same


INPROJ_COL_STEP = 512


def _inproj_kernel(x_ref, nw_ref, w_ref, gla_ref, gdn_ref, small_ref, gz_ref):
    x = x_ref[...]
    xn = x * lax.rsqrt(jnp.mean(x * x, axis=-1, keepdims=True) + NORM_EPS) * nw_ref[...]
    xb = xn.astype(BF16)

    def emit(o_ref, w0, width, act=None, o0=0):
        for c0 in range(0, width, INPROJ_COL_STEP):
            c1 = min(c0 + INPROJ_COL_STEP, width)
            res = _dot_nt(xb, w_ref[w0 + c0:w0 + c1, :])
            o_ref[:, o0 + c0:o0 + c1] = (res if act is None else act(res)).astype(o_ref.dtype)

    emit(gla_ref, P_GLA_QKV, GLA_QKV)
    emit(gdn_ref, P_QKV_B, CONV_CH)
    emit(small_ref, P_SMALL, LANES)
    emit(gz_ref, P_GATES + GZ_GATE_A, 2 * D_MODEL, act=_sigmoid, o0=GZ_GATE_A)
    emit(gz_ref, P_GATES + GZ_Z_A, GLA_VAL + GDN_VAL, act=_silu, o0=GZ_Z_A)


def _inproj(x2d, norm_w, w_in_r):
    n = x2d.shape[0]
    tm = INPROJ_ROW_TILE if n >= 4 * INPROJ_ROW_TILE else min(ROW_TILE, n)
    rows = lambda i: (i, 0)
    return pl.pallas_call(
        _inproj_kernel,
        grid=(n // tm,),
        in_specs=[
            pl.BlockSpec((tm, D_MODEL), rows),
            pl.BlockSpec((1, D_MODEL), lambda i: (0, 0)),
            pl.BlockSpec((P_COLS, D_MODEL), lambda i: (0, 0), pipeline_mode=pl.Buffered(1)),
        ],
        out_specs=[
            pl.BlockSpec((tm, GLA_QKV), rows),
            pl.BlockSpec((tm, CONV_CH), rows),
            pl.BlockSpec((tm, LANES), rows),
            pl.BlockSpec((tm, GZ_COLS), rows),
        ],
        out_shape=[
            jax.ShapeDtypeStruct((n, GLA_QKV), F32),
            jax.ShapeDtypeStruct((n, CONV_CH), F32),
            jax.ShapeDtypeStruct((n, LANES), F32),
            jax.ShapeDtypeStruct((n, GZ_COLS), BF16),
        ],
        compiler_params=pltpu.CompilerParams(dimension_semantics=("arbitrary",),
                                             vmem_limit_bytes=INPROJ_VMEM_LIMIT),
        name="inproj",
    )(x2d, norm_w, w_in_r)


GLA_PAIRS = GLA_HEADS // 2
GLA_SUB_BLOCK = 16


def _gla_att_levels(r, block):
    row = lax.broadcasted_iota(jnp.int32, (r, r), 0)
    col = lax.broadcasted_iota(jnp.int32, (r, r), 1)
    sub = min(block, GLA_SUB_BLOCK)
    levels = [(None, ((row // sub) == (col // sub)) & (row >= col))]
    half = sub
    while half < block:
        levels.append((half, ((row // (2 * half)) == (col // (2 * half)))
                       & ((row // half) % 2 == 1) & ((col // half) % 2 == 0)))
        half *= 2
    return levels


def _rows_at(x, n, offset):
    return jnp.concatenate([jnp.broadcast_to(x[i + offset:i + offset + 1, :], (n, x.shape[1]))
                            for i in range(0, x.shape[0], n)], axis=0)


def _gla_prepass(q, k, v, sm, wgg, bgg, same_le, same, levels, block):
    r = q.shape[0]
    heads = range(GLA_HEADS)
    pairs = range(GLA_PAIRS)
    pre = _dot(sm, wgg) + bgg
    yield
    gk = (jnp.minimum(pre, 0.0) - jnp.log(1.0 + jnp.exp(-jnp.abs(pre)))) * (1.0 / GLA_GATE_TEMP)
    sums = _cumsum_rows(jnp.concatenate([jnp.where(same_le, 1.0, 0.0).astype(BF16),
                                         jnp.where(same, 1.0, 0.0).astype(BF16)], axis=0), gk)
    yield
    bcum = sums[:r]
    bend = sums[r:]
    bex = bcum - gk
    scale = GLA_DK ** -0.5
    q_e = q * jnp.exp(bcum) * scale
    lane = lax.broadcasted_iota(jnp.int32, (r, LANES), 1)
    in_head = [lane < GLA_DK, lane >= GLA_DK]
    pl_ = [slice(p * LANES, (p + 1) * LANES) for p in pairs]
    head_only = lambda x, h: jnp.where(in_head[h % 2], x[:, pl_[h // 2]], 0.0)
    qm = [head_only(q_e, h) for h in heads]
    sub = min(block, GLA_SUB_BLOCK)
    start = _rows_at(bex, sub, 0)
    groups = {None: (q * jnp.exp(bcum - start) * scale, [(levels[0][1], k * jnp.exp(start - bcum))])}
    for half, mask in levels[1:]:
        key = None if half == sub else half
        if key not in groups:
            groups[key] = (q * jnp.exp(bcum - _rows_at(bex, half, 0)) * scale, [])
        groups[key][1].append((mask, k * jnp.exp(_rows_at(bcum, half, half - 1) - bcum)))
    att = [jnp.zeros((r, r), F32)] * GLA_HEADS
    for q_l, parts in groups.values():
        for h in heads:
            keys = [k_l[:, pl_[h // 2]].astype(BF16) for _, k_l in parts]
            prod = _dot_nt(head_only(q_l, h), keys[0] if len(keys) == 1 else jnp.concatenate(keys, axis=0))
            for i, (mask, _) in enumerate(parts):
                att[h] = jnp.where(mask, prod[:, i * r:(i + 1) * r], att[h])
    yield
    o_intra = [_dot(att[h], v[h]) for h in heads]
    k_end = k * jnp.exp(bend - bcum)
    k_end_t = [k_end[:, pl_[p]].T for p in pairs]
    bend_t = [bend[:, pl_[p]].T for p in pairs]
    yield
    return qm, o_intra, k_end_t, bend_t


def _gla_prompt_kernel(qkv_ref, small_ref, s0_ref, wgg_ref, bgg_ref, o_ref, sfin_ref, s_scr, *, c):
    t = pl.program_id(0)
    nb, r, _ = qkv_ref.shape

    @pl.when(t == 0)
    def _():
        s_scr[...] = s0_ref[...]

    same_le, _, same = _block_masks(r, c)
    levels = _gla_att_levels(r, c)
    zeros = jnp.zeros((c, GLA_DV), F32)
    n_sub = r // c
    heads = range(GLA_HEADS)
    pairs = range(GLA_PAIRS)

    def one_batch(b):
        q = qkv_ref[b, :, 0:GLA_KEY]
        k = qkv_ref[b, :, GLA_KEY:2 * GLA_KEY]
        v = [qkv_ref[b, :, 2 * GLA_KEY + h * GLA_DV:2 * GLA_KEY + (h + 1) * GLA_DV] for h in heads]
        qm, o_intra, k_end_t, bend_t = yield from _gla_prepass(
            q, k, v, small_ref[b], wgg_ref[...], bgg_ref[...], same_le, same, levels, c)
        s = [s_scr[b, p * LANES:(p + 1) * LANES, :] for p in pairs]
        for i in range(n_sub):
            rows = slice(i * c, (i + 1) * c)
            ws = [_dot(jnp.concatenate([qm[2 * p][rows], qm[2 * p + 1][rows]], axis=0), s[p]) for p in pairs]
            padded = [jnp.concatenate([zeros] * i + [v[h][rows]] + [zeros] * (n_sub - 1 - i), axis=0)
                      for h in heads]
            upd = [_dot(k_end_t[h // 2][(h % 2) * GLA_DK:(h % 2 + 1) * GLA_DK, :], padded[h]) for h in heads]
            yield
            for h in heads:
                o_ref[b, rows, h * GLA_DV:(h + 1) * GLA_DV] = (
                    o_intra[h][rows] + ws[h // 2][(h % 2) * c:(h % 2 + 1) * c]).astype(o_ref.dtype)
            s = [s[p] * jnp.exp(bend_t[p][:, i * c:i * c + 1])
                 + jnp.concatenate([upd[2 * p], upd[2 * p + 1]], axis=0) for p in pairs]
        for p in pairs:
            s_scr[b, p * LANES:(p + 1) * LANES, :] = s[p]

    def per_step(i, carry):
        _interleave([one_batch(i * BATCH_INTERLEAVE + j) for j in range(BATCH_INTERLEAVE)])
        return carry

    lax.fori_loop(0, nb // BATCH_INTERLEAVE, per_step, None)

    @pl.when(t == pl.num_programs(0) - 1)
    def _():
        sfin_ref[...] = s_scr[...]


def _gla_sample_kernel(qkv_ref, small_ref, s0_ref, wgg_ref, bgg_ref, o_ref, sfin_ref):
    ns, c, _ = qkv_ref.shape
    r = ns * c
    heads = range(GLA_HEADS)
    same_le, _, same = _block_masks(r, c)
    levels = _gla_att_levels(r, c)
    q = qkv_ref[:, :, 0:GLA_KEY].reshape(r, GLA_KEY)
    k = qkv_ref[:, :, GLA_KEY:2 * GLA_KEY].reshape(r, GLA_KEY)
    v = [qkv_ref[:, :, 2 * GLA_KEY + h * GLA_DV:2 * GLA_KEY + (h + 1) * GLA_DV].reshape(r, GLA_DV)
         for h in heads]
    (qm, o_intra, k_end_t, bend_t), = _interleave([_gla_prepass(
        q, k, v, small_ref[...].reshape(r, LANES), wgg_ref[...], bgg_ref[...], same_le, same, levels, c)])
    seq_of_row = lax.broadcasted_iota(jnp.int32, (r, GLA_DV), 0) // c
    for p in range(GLA_PAIRS):
        ps = slice(p * LANES, (p + 1) * LANES)
        inter = [[], []]
        for s in range(ns):
            rows = slice(s * c, (s + 1) * c)
            ws = _dot(jnp.concatenate([qm[2 * p][rows], qm[2 * p + 1][rows]], axis=0), s0_ref[s, ps, :])
            inter[0].append(ws[:c])
            inter[1].append(ws[c:])
        for hh in range(2):
            h = 2 * p + hh
            o = o_intra[h] + jnp.concatenate(inter[hh], axis=0)
            o_ref[:, h * GLA_DV:(h + 1) * GLA_DV] = o.astype(o_ref.dtype)
        for s in range(ns):
            upd = [_dot(k_end_t[p][hh * GLA_DK:(hh + 1) * GLA_DK, :],
                        jnp.where(seq_of_row == s, v[2 * p + hh], 0.0)) for hh in range(2)]
            sfin_ref[s, ps, :] = (s0_ref[s, ps, :] * jnp.exp(bend_t[p][:, s * c:s * c + 1])
                                  + jnp.concatenate(upd, axis=0))


def _gla_prompt(qkv3, small3, s0, wgg, bgg, c):
    bsz, t_len, _ = qkv3.shape
    rows = GLA_HEADS * GLA_DK
    r = GROUP_ROWS
    full3 = lambda t: (0, 0, 0)
    const2 = lambda t: (0, 0)
    return pl.pallas_call(
        functools.partial(_gla_prompt_kernel, c=c),
        grid=(t_len // r,),
        in_specs=[
            pl.BlockSpec((bsz, r, GLA_QKV), lambda t: (0, t, 0)),
            pl.BlockSpec((bsz, r, LANES), lambda t: (0, t, 0)),
            pl.BlockSpec((bsz, rows, GLA_DV), full3),
            pl.BlockSpec((LANES, GLA_KEY), const2),
            pl.BlockSpec((1, GLA_KEY), const2),
        ],
        out_specs=[
            pl.BlockSpec((bsz, r, GLA_VAL), lambda t: (0, t, 0)),
            pl.BlockSpec((bsz, rows, GLA_DV), full3),
        ],
        out_shape=[
            jax.ShapeDtypeStruct((bsz, t_len, GLA_VAL), BF16),
            jax.ShapeDtypeStruct((bsz, rows, GLA_DV), F32),
        ],
        scratch_shapes=[pltpu.VMEM((bsz, rows, GLA_DV), F32)],
        compiler_params=pltpu.CompilerParams(dimension_semantics=("arbitrary",),
                                             vmem_limit_bytes=VMEM_LIMIT),
        name="gla_prompt",
    )(qkv3, small3, s0, wgg, bgg)


def _gla_sample(qkv3, small3, s0, wgg, bgg):
    bsz, c, _ = qkv3.shape
    rows = GLA_HEADS * GLA_DK
    ns = GROUP_ROWS // c
    grp = lambda g: (g, 0, 0)
    const2 = lambda g: (0, 0)
    return pl.pallas_call(
        _gla_sample_kernel,
        grid=(bsz // ns,),
        in_specs=[
            pl.BlockSpec((ns, c, GLA_QKV), grp),
            pl.BlockSpec((ns, c, LANES), grp),
            pl.BlockSpec((ns, rows, GLA_DV), grp),
            pl.BlockSpec((LANES, GLA_KEY), const2),
            pl.BlockSpec((1, GLA_KEY), const2),
        ],
        out_specs=[
            pl.BlockSpec((ns * c, GLA_VAL), lambda g: (g, 0)),
            pl.BlockSpec((ns, rows, GLA_DV), grp),
        ],
        out_shape=[
            jax.ShapeDtypeStruct((bsz * c, GLA_VAL), BF16),
            jax.ShapeDtypeStruct((bsz, rows, GLA_DV), F32),
        ],
        compiler_params=pltpu.CompilerParams(dimension_semantics=("arbitrary",),
                                             vmem_limit_bytes=VMEM_LIMIT),
        name="gla_sample",
    )(qkv3, small3, s0, wgg, bgg)


def _block_unit_lower_inverse(a_list, block):
    r = a_list[0].shape[0]
    row = lax.broadcasted_iota(jnp.int32, (r, r), 0)
    col = lax.broadcasted_iota(jnp.int32, (r, r), 1)
    in_block = lambda n: (row // n) == (col // n)
    base = min(block, INVERSE_BASE_BLOCK)
    a_base = a_list if base == block else [jnp.where(in_block(base), a, 0.0) for a in a_list]
    xs = [jnp.where(row == col, 1.0, 0.0) - a for a in a_base]
    ps = [_dot(a, a) for a in a_base]
    n = 2
    while n < base:
        yield
        last = 2 * n >= base
        ms = [_dot(x if last else jnp.concatenate([p.astype(BF16), x.astype(BF16)], axis=0), p)
              for p, x in zip(ps, xs)]
        ps = [m[:r] for m in ms]
        xs = [x + m[-r:] for x, m in zip(xs, ms)]
        n *= 2
    n = base
    while n < block:
        yield
        between = in_block(2 * n) & jnp.logical_not(in_block(n))
        ts = [_dot(jnp.where(between, a, 0.0), x) for a, x in zip(a_list, xs)]
        yield
        xs = [x - _dot(x, t) for x, t in zip(xs, ts)]
        n *= 2
    return xs


def _gdn_decays(sm, alog_c, dtb_c, same_lt, same):
    r = sm.shape[0]
    heads = range(GDN_HEADS)
    ab = sm.T[SM_A:SM_A + 2 * GDN_HEADS, :]
    g8 = -jnp.exp(alog_c) * _softplus(ab + dtb_c)
    beta_c = _sigmoid(ab).T
    g1, g2 = _split2(g8)
    same_ge = same & jnp.logical_not(same_lt)
    sel = jnp.concatenate([jnp.where(same_ge, 1.0, 0.0).astype(BF16),
                           jnp.where(same, 1.0, 0.0).astype(BF16)], axis=1)
    sums = jnp.dot(jnp.concatenate([g1, g2], axis=0), sel, preferred_element_type=F32)
    sums = sums[:2 * GDN_HEADS] + sums[2 * GDN_HEADS:]
    dec_rows = sums[:, :r]
    dec_c = dec_rows.T
    dend_c = sums[:, r:].T
    return ([dec_c[:, h:h + 1] for h in heads], [dec_rows[h:h + 1, :] for h in heads],
            [dend_c[:, h:h + 1] for h in heads], [beta_c[:, GDN_HEADS + h:GDN_HEADS + h + 1] for h in heads])


def _gdn_head_cols(h):
    return [slice(part * GDN_KEY + h * GDN_DK, part * GDN_KEY + (h + 1) * GDN_DK) for part in range(3)]


def _gdn_prepass(qkv_of_head, sm, alog_c, dtb_c, same_le, same_lt, same, block, heads):
    dcol, drow, dend, beta = ([x[h] for h in heads] for x in _gdn_decays(sm, alog_c, dtb_c, same_lt, same))
    idx = range(len(heads))
    gamma = [jnp.where(same_le, jnp.exp(jnp.where(same_le, dcol[i] - drow[i], 0.0)), 0.0) for i in idx]
    q, k, v = (list(x) for x in zip(*[qkv_of_head(h) for h in heads]))
    r = q[0].shape[0]
    q = [x * lax.rsqrt(jnp.sum(x * x, axis=-1, keepdims=True) + NORM_EPS) * (GDN_DK ** -0.5) for x in q]
    k = [x * lax.rsqrt(jnp.sum(x * x, axis=-1, keepdims=True) + NORM_EPS) for x in k]
    yield
    kb = [k[i] * beta[i] for i in idx]
    vb = [v[i] * beta[i] for i in idx]
    kq = [_dot_nt(jnp.concatenate([kb[i], q[i]], axis=0), k[i]) for i in idx]
    yield
    a_mat = [jnp.where(same_lt, kq[i][:r] * gamma[i], 0.0) for i in idx]
    qk = [(kq[i][r:] * gamma[i]).astype(BF16) for i in idx]
    t_inv = yield from _block_unit_lower_inverse(a_mat, block)
    edec = [jnp.exp(dcol[i]) for i in idx]
    uw = [_dot(t_inv[i], jnp.concatenate([vb[i], kb[i] * edec[i]], axis=1)) for i in idx]
    q_e = [(q[i] * edec[i]).astype(BF16) for i in idx]
    k_end_t = [(k[i] * jnp.exp(dend[i] - dcol[i])).T.astype(BF16) for i in idx]
    yield
    u = [x[:, :GDN_DV] for x in uw]
    w = [x[:, GDN_DV:].astype(BF16) for x in uw]
    return u, w, q_e, qk, k_end_t, dend


def _conv_silu(win, cw_ref):
    conv = win(0) * cw_ref[CONV_WIDTH - 1:CONV_WIDTH, :]
    for j in range(1, CONV_WIDTH):
        conv = conv + win(j) * cw_ref[CONV_WIDTH - 1 - j:CONV_WIDTH - j, :]
    return _silu(conv)

def _gdn_prompt_kernel(qkv_ref, small_ref, cs_ref, s0_ref, cw_ref, alog_ref, dtb_ref,
                       o_ref, sfin_ref, cnew_ref, xbuf, prev, s_scr, *, c):
    t = pl.program_id(0)
    nb, r, _ = qkv_ref.shape
    keep = CONV_WIDTH - 1
    base = SUBLANES

    @pl.when(t == 0)
    def _():
        s_scr[...] = s0_ref[...]
        prev[:, base - keep:base, :] = cs_ref[...]

    same_le, same_lt, same = _block_masks(r, c)
    zeros = jnp.zeros((c, GDN_DV), F32)
    n_sub = r // c

    def one_batch(b, slot, heads, first):
        xb = xbuf.at[slot]
        if first:
            xb[base - keep:base, :] = prev[b, base - keep:base, :]
            xb[base:base + r, :] = qkv_ref[b]
            prev[b, base - keep:base, :] = xb[base + r - keep:base + r, :]

        def qkv_of_head(h):
            return [_conv_silu(lambda j: xb[base - j:base - j + r, cols], cw_ref.at[:, cols])
                    for cols in _gdn_head_cols(h)]

        u, w, q_e, qk, k_end_t, dend = yield from _gdn_prepass(
            qkv_of_head, small_ref[b], alog_ref[...], dtb_ref[...], same_le, same_lt, same, c, heads)
        idx = range(len(heads))
        hs = [slice(h * GDN_DK, (h + 1) * GDN_DK) for h in heads]
        s = [s_scr[b, hs[i], :] for i in idx]
        for ci in range(n_sub):
            rows = slice(ci * c, (ci + 1) * c)
            ws = [_dot(jnp.concatenate([w[i][rows], q_e[i][rows]], axis=0), s[i]) for i in idx]
            yield
            v_new = [u[i][rows] - ws[i][:c] for i in idx]
            padded = [jnp.concatenate([zeros] * ci + [v_new[i]] + [zeros] * (n_sub - 1 - ci), axis=0)
                      for i in idx]
            upd = [_dot(jnp.concatenate([qk[i][rows], k_end_t[i]], axis=0), padded[i]) for i in idx]
            yield
            for i in idx:
                o_ref[b, rows, hs[i]] = (ws[i][c:] + upd[i][:c]).astype(o_ref.dtype)
            s = [s[i] * jnp.exp(dend[i][ci * c:ci * c + 1, :]) + upd[i][c:] for i in idx]
        for i in idx:
            s_scr[b, hs[i], :] = s[i]

    def per_step(step, carry):
        for g in range(0, GDN_HEADS, GDN_HEAD_GROUP):
            heads = tuple(range(g, g + GDN_HEAD_GROUP))
            _interleave([one_batch(step * GDN_BATCH_INTERLEAVE + j, j, heads, g == 0)
                         for j in range(GDN_BATCH_INTERLEAVE)])
        return carry

    lax.fori_loop(0, nb // GDN_BATCH_INTERLEAVE, per_step, None)

    @pl.when(t == pl.num_programs(0) - 1)
    def _():
        sfin_ref[...] = s_scr[...]
        cnew_ref[...] = prev[:, base - keep:base, :]


def _gdn_sample_kernel(qkv_ref, small_ref, cs_ref, s0_ref, cw_ref, alog_ref, dtb_ref,
                       o_ref, sfin_ref, cnew_ref, xbuf):
    ns, c, _ = qkv_ref.shape
    r = ns * c
    keep = CONV_WIDTH - 1
    base = SUBLANES
    xbuf[:, base - keep:base, :] = cs_ref[...]
    xbuf[:, base:base + c, :] = qkv_ref[...]
    conv = _conv_silu(lambda j: xbuf[:, base - j:base - j + c, :], cw_ref).reshape(r, CONV_CH)
    cnew_ref[...] = xbuf[:, base + c - keep:base + c, :]

    same_le, same_lt, same = _block_masks(r, c)
    seq_of_row = lax.broadcasted_iota(jnp.int32, (r, GDN_DV), 0) // c
    (u, w, q_e, qk, k_end_t, dend), = _interleave([_gdn_prepass(
        lambda h: [conv[:, cols] for cols in _gdn_head_cols(h)], small_ref[...].reshape(r, LANES),
        alog_ref[...], dtb_ref[...], same_le, same_lt, same, c, tuple(range(GDN_HEADS)))])
    for h in range(GDN_HEADS):
        hs = slice(h * GDN_DK, (h + 1) * GDN_DK)
        v_parts, o_parts = [], []
        for s in range(ns):
            rows = slice(s * c, (s + 1) * c)
            ws = _dot(jnp.concatenate([w[h][rows], q_e[h][rows]], axis=0), s0_ref[s, hs, :])
            v_parts.append(u[h][rows] - ws[:c])
            o_parts.append(ws[c:])
        v_new = jnp.concatenate(v_parts, axis=0)
        o = jnp.concatenate(o_parts, axis=0) + _dot(qk[h], v_new)
        o_ref[:, h * GDN_DV:(h + 1) * GDN_DV] = o.astype(o_ref.dtype)
        for s in range(ns):
            upd = _dot(k_end_t[h], jnp.where(seq_of_row == s, v_new, 0.0))
            sfin_ref[s, hs, :] = s0_ref[s, hs, :] * jnp.exp(dend[h][s * c:s * c + 1, :]) + upd


def _gdn_prompt(qkv3, small3, conv_state, s0, conv_w, alog_v, dtb_v, c):
    bsz, t_len, _ = qkv3.shape
    rows = GDN_HEADS * GDN_DK
    keep = CONV_WIDTH - 1
    r = GROUP_ROWS
    full3 = lambda t: (0, 0, 0)
    const2 = lambda t: (0, 0)
    return pl.pallas_call(
        functools.partial(_gdn_prompt_kernel, c=c),
        grid=(t_len // r,),
        in_specs=[
            pl.BlockSpec((bsz, r, CONV_CH), lambda t: (0, t, 0)),
            pl.BlockSpec((bsz, r, LANES), lambda t: (0, t, 0)),
            pl.BlockSpec((bsz, keep, CONV_CH), full3),
            pl.BlockSpec((bsz, rows, GDN_DV), full3),
            pl.BlockSpec((CONV_WIDTH, CONV_CH), const2),
            pl.BlockSpec((2 * GDN_HEADS, 1), const2),
            pl.BlockSpec((2 * GDN_HEADS, 1), const2),
        ],
        out_specs=[
            pl.BlockSpec((bsz, r, GDN_VAL), lambda t: (0, t, 0)),
            pl.BlockSpec((bsz, rows, GDN_DV), full3),
            pl.BlockSpec((bsz, keep, CONV_CH), full3),
        ],
        out_shape=[
            jax.ShapeDtypeStruct((bsz, t_len, GDN_VAL), BF16),
            jax.ShapeDtypeStruct((bsz, rows, GDN_DV), F32),
            jax.ShapeDtypeStruct((bsz, keep, CONV_CH), F32),
        ],
        scratch_shapes=[pltpu.VMEM((GDN_BATCH_INTERLEAVE, SUBLANES + r, CONV_CH), F32),
                        pltpu.VMEM((bsz, SUBLANES, CONV_CH), F32),
                        pltpu.VMEM((bsz, rows, GDN_DV), F32)],
        compiler_params=pltpu.CompilerParams(dimension_semantics=("arbitrary",),
                                             vmem_limit_bytes=GDN_VMEM_LIMIT),
        name="gdn_prompt",
    )(qkv3, small3, conv_state, s0, conv_w, alog_v, dtb_v)


def _gdn_sample(qkv3, small3, conv_state, s0, conv_w, alog_v, dtb_v):
    bsz, c, _ = qkv3.shape
    rows = GDN_HEADS * GDN_DK
    keep = CONV_WIDTH - 1
    ns = GROUP_ROWS // c
    grp = lambda g: (g, 0, 0)
    const2 = lambda g: (0, 0)
    return pl.pallas_call(
        _gdn_sample_kernel,
        grid=(bsz // ns,),
        in_specs=[
            pl.BlockSpec((ns, c, CONV_CH), grp),
            pl.BlockSpec((ns, c, LANES), grp),
            pl.BlockSpec((ns, keep, CONV_CH), grp),
            pl.BlockSpec((ns, rows, GDN_DV), grp),
            pl.BlockSpec((CONV_WIDTH, CONV_CH), const2),
            pl.BlockSpec((2 * GDN_HEADS, 1), const2),
            pl.BlockSpec((2 * GDN_HEADS, 1), const2),
        ],
        out_specs=[
            pl.BlockSpec((ns * c, GDN_VAL), lambda g: (g, 0)),
            pl.BlockSpec((ns, rows, GDN_DV), grp),
            pl.BlockSpec((ns, keep, CONV_CH), grp),
        ],
        out_shape=[
            jax.ShapeDtypeStruct((bsz * c, GDN_VAL), BF16),
            jax.ShapeDtypeStruct((bsz, rows, GDN_DV), F32),
            jax.ShapeDtypeStruct((bsz, keep, CONV_CH), F32),
        ],
        scratch_shapes=[pltpu.VMEM((ns, 2 * SUBLANES, CONV_CH), F32)],
        compiler_params=pltpu.CompilerParams(dimension_semantics=("arbitrary",),
                                             vmem_limit_bytes=VMEM_LIMIT),
        name="gdn_sample",
    )(qkv3, small3, conv_state, s0, conv_w, alog_v, dtb_v)


def _head_norm_gate(o, silu_z, w):
    parts = []
    for h in range(o.shape[-1] // LANES):
        oh = o[:, h * LANES:(h + 1) * LANES]
        parts.append(oh * lax.rsqrt(jnp.mean(oh * oh, axis=-1, keepdims=True) + NORM_EPS) * w)
    return jnp.concatenate(parts, axis=-1) * silu_z


def _out_kernel(oa_ref, ob_ref, ga_ref, gb_ref, za_ref, zb_ref, x_ref, p_ref, anw_ref, bnw_ref,
                wua_ref, wub_ref, wout_ref, wpg_ref, wp_ref, fnw_ref, y_ref):
    f32 = lambda ref: ref[...].astype(F32)
    ya = _dot(_head_norm_gate(f32(oa_ref), f32(za_ref), anw_ref[...]), wua_ref[...])
    yb = _dot(_head_norm_gate(f32(ob_ref), f32(zb_ref), bnw_ref[...]), wub_ref[...])
    merged = f32(ga_ref) * ya + f32(gb_ref) * yb
    h1 = x_ref[...] + _dot(merged, wout_ref[...])
    h2 = h1 + _sigmoid(_dot(h1, wpg_ref[...])) * _dot(p_ref[...], wp_ref[...])
    y_ref[...] = h2 * lax.rsqrt(jnp.mean(h2 * h2, axis=-1, keepdims=True) + NORM_EPS) * fnw_ref[...]


def _out_stage(o_a, o_b, gz, x2d, p2d, anw, bnw, wua, wub, wout, wpg, wp, fnw):
    n = x2d.shape[0]
    tm = min(ROW_TILE, n)
    const = lambda i: (0, 0)
    return pl.pallas_call(
        _out_kernel,
        grid=(n // tm,),
        in_specs=[
            pl.BlockSpec((tm, GLA_VAL), lambda i: (i, 0)),
            pl.BlockSpec((tm, GDN_VAL), lambda i: (i, 0)),
            pl.BlockSpec((tm, D_MODEL), lambda i: (i, GZ_GATE_A // D_MODEL)),
            pl.BlockSpec((tm, D_MODEL), lambda i: (i, GZ_GATE_B // D_MODEL)),
            pl.BlockSpec((tm, GLA_VAL), lambda i: (i, GZ_Z_A // GLA_VAL)),
            pl.BlockSpec((tm, GDN_VAL), lambda i: (i, GZ_Z_B // GDN_VAL)),
            pl.BlockSpec((tm, D_MODEL), lambda i: (i, 0)),
            pl.BlockSpec((tm, PLE_DIM), lambda i: (i, 0)),
            pl.BlockSpec((1, GLA_DV), const),
            pl.BlockSpec((1, GDN_DV), const),
            pl.BlockSpec((GLA_VAL, D_MODEL), const),
            pl.BlockSpec((GDN_VAL, D_MODEL), const),
            pl.BlockSpec((D_MODEL, D_MODEL), const),
            pl.BlockSpec((D_MODEL, D_MODEL), const),
            pl.BlockSpec((PLE_DIM, D_MODEL), const),
            pl.BlockSpec((1, D_MODEL), const),
        ],
        out_specs=pl.BlockSpec((tm, D_MODEL), lambda i: (i, 0)),
        out_shape=jax.ShapeDtypeStruct((n, D_MODEL), F32),
        compiler_params=pltpu.CompilerParams(dimension_semantics=("arbitrary",),
                                             vmem_limit_bytes=VMEM_LIMIT),
        name="out_stage",
    )(o_a, o_b, gz, gz, gz, gz, x2d, p2d, anw, bnw, wua, wub, wout, wpg, wp, fnw)


def _in_offsets():
    offs = [0]
    for s in IN_SPLITS:
        offs.append(offs[-1] + s)
    return offs


def _regroup_kernel(wt_ref, o_ref):
    offs = _in_offsets()
    (q_a, _, _, g_a, z_a, qkv_b, a_b, _, z_b, gate_a, gate_b, end) = offs
    piece = lambda lo, hi: wt_ref[lo:hi, :].astype(BF16)
    o_ref[P_GLA_QKV:P_GLA_QKV + GLA_QKV, :] = piece(q_a, g_a)
    o_ref[P_QKV_B:P_QKV_B + CONV_CH, :] = piece(qkv_b, a_b)
    small = jnp.concatenate([wt_ref[g_a:z_a, :], wt_ref[a_b:z_b, :],
                             jnp.zeros((LANES - (z_a - g_a) - (z_b - a_b), wt_ref.shape[1]), F32)], axis=0)
    o_ref[P_SMALL:P_SMALL + LANES, :] = small.astype(BF16)
    o_ref[P_GATES + GZ_GATE_A:P_GATES + GZ_GATE_A + D_MODEL, :] = piece(gate_a, gate_b)
    o_ref[P_GATES + GZ_GATE_B:P_GATES + GZ_GATE_B + D_MODEL, :] = piece(gate_b, end)
    o_ref[P_GATES + GZ_Z_A:P_GATES + GZ_Z_A + GLA_VAL, :] = piece(z_a, qkv_b)
    o_ref[P_GATES + GZ_Z_B:P_GATES + GZ_Z_B + GDN_VAL, :] = piece(z_b, gate_a)


def _regroup_w_in(w_in_t):
    cols = 256
    return pl.pallas_call(
        _regroup_kernel,
        grid=(D_MODEL // cols,),
        in_specs=[pl.BlockSpec((None, w_in_t.shape[1], cols), lambda i: (0, 0, i))],
        out_specs=pl.BlockSpec((P_COLS, cols), lambda i: (0, i)),
        out_shape=jax.ShapeDtypeStruct((P_COLS, D_MODEL), BF16),
        compiler_params=pltpu.CompilerParams(dimension_semantics=("arbitrary",),
                                             vmem_limit_bytes=VMEM_LIMIT),
        name="regroup_w_in",
    )(w_in_t)


def _head_param_col(v):
    return jnp.zeros((2 * GDN_HEADS, 1), F32).at[:GDN_HEADS, 0].set(v.astype(F32))


def _trunk(x, p, s_gla, s_gdn, conv_state, wts):
    bsz, t_len, _ = x.shape
    c = min(PROMPT_CHUNK, t_len)
    n = bsz * t_len
    x2d = x.reshape(n, D_MODEL)
    p2d = p.reshape(n, PLE_DIM)
    gla_in, gdn_in, small, gz = _inproj(x2d, wts["norm_w"], wts["w_in_r"])
    gla_in3 = gla_in.reshape(bsz, t_len, GLA_QKV)
    gdn_in3 = gdn_in.reshape(bsz, t_len, CONV_CH)
    small3 = small.reshape(bsz, t_len, LANES)
    s_gla2 = s_gla.reshape(bsz, GLA_HEADS * GLA_DK, GLA_DV)
    s_gdn2 = s_gdn.reshape(bsz, GDN_HEADS * GDN_DK, GDN_DV)
    if t_len % GROUP_ROWS == 0:
        o_a, gla_fin = _gla_prompt(gla_in3, small3, s_gla2, wts["wgg"], wts["bgg"], c)
        o_b, gdn_fin, conv_new = _gdn_prompt(gdn_in3, small3, conv_state, s_gdn2, wts["conv_w"],
                                             wts["alog_v"], wts["dtb_v"], c)
    else:
        assert GROUP_ROWS % t_len == 0 and bsz % (GROUP_ROWS // t_len) == 0 and t_len >= CONV_WIDTH - 1
        o_a, gla_fin = _gla_sample(gla_in3, small3, s_gla2, wts["wgg"], wts["bgg"])
        o_b, gdn_fin, conv_new = _gdn_sample(gdn_in3, small3, conv_state, s_gdn2, wts["conv_w"],
                                             wts["alog_v"], wts["dtb_v"])
    y = _out_stage(o_a.reshape(n, GLA_VAL), o_b.reshape(n, GDN_VAL), gz, x2d, p2d, wts["anw"], wts["bnw"],
                   wts["wua"], wts["wub"], wts["wout"], wts["wpg"], wts["wp"], wts["fnw"])
    return (y.reshape(bsz, t_len, D_MODEL),
            gla_fin.reshape(1, bsz, GLA_HEADS, GLA_DK, GLA_DV),
            gdn_fin.reshape(1, bsz, GDN_HEADS, GDN_DK, GDN_DV),
            conv_new.reshape(1, bsz, CONV_WIDTH - 1, CONV_CH))


def kernel(x_prompt, x_sample, state_gla, state_gdn, state_conv, p_prompt, p_sample, norm_w, w_in,
           w_gla_gate, b_gla_gate, gla_norm_w, conv_w, gdn_a_log, gdn_dt_bias, gdn_norm_w,
           w_up_gla, w_up_gdn, w_out, w_ple_gate, w_ple, final_norm_w):
    wgg = jnp.zeros((LANES, GLA_KEY), F32).at[SM_G:SM_G + GLA_GATE_RANK].set(w_gla_gate[0]).astype(BF16)
    wts = {
        "norm_w": norm_w[0].reshape(1, D_MODEL),
        "w_in_r": _regroup_w_in(jnp.swapaxes(w_in, 1, 2)),
        "wgg": wgg,
        "bgg": b_gla_gate[0].reshape(1, GLA_KEY),
        "conv_w": conv_w[0],
        "alog_v": _head_param_col(gdn_a_log[0]),
        "dtb_v": _head_param_col(gdn_dt_bias[0]),
        "anw": gla_norm_w[0].reshape(1, GLA_DV),
        "bnw": gdn_norm_w[0].reshape(1, GDN_DV),
        "wua": w_up_gla[0].astype(BF16),
        "wub": w_up_gdn[0].astype(BF16),
        "wout": w_out[0].astype(BF16),
        "wpg": w_ple_gate[0].astype(BF16),
        "wp": w_ple[0].astype(BF16),
        "fnw": final_norm_w.reshape(1, D_MODEL),
    }
    bsz = x_prompt.shape[0]
    dt = x_prompt.dtype
    y_p, gla_p, gdn_p, conv_p = _trunk(
        x_prompt, p_prompt[0],
        jnp.zeros((bsz, GLA_HEADS, GLA_DK, GLA_DV), dt), jnp.zeros((bsz, GDN_HEADS, GDN_DK, GDN_DV), dt),
        jnp.zeros((bsz, CONV_WIDTH - 1, CONV_CH), dt), wts)
    y_s, gla_s, gdn_s, conv_s = _trunk(x_sample, p_sample[0], state_gla[0], state_gdn[0], state_conv[0], wts)
    return (y_p, y_s, gla_p, gdn_p, conv_p, gla_s, gdn_s, conv_s)
```

```python
import functools

import jax
import jax.numpy as jnp
from jax import lax
from jax.experimental import pallas as pl
from jax.experimental.pallas import tpu as pltpu

F32 = jnp.float32
BF16 = jnp.bfloat16

D_MODEL = 1024
PLE_DIM = 256
NORM_EPS = 1e-6
GLA_HEADS = 4
GLA_DK = 64
GLA_DV = 128
GLA_KEY = GLA_HEADS * GLA_DK
GLA_VAL = GLA_HEADS * GLA_DV
GLA_GATE_RANK = 16
GLA_GATE_TEMP = 16.0
GDN_HEADS = 4
GDN_DK = 128
GDN_DV = 128
GDN_KEY = GDN_HEADS * GDN_DK
GDN_VAL = GDN_HEADS * GDN_DV
CONV_WIDTH = 4
CONV_CH = 2 * GDN_KEY + GDN_VAL
IN_SPLITS = (GLA_KEY, GLA_KEY, GLA_VAL, GLA_GATE_RANK, GLA_VAL, CONV_CH, GDN_HEADS, GDN_HEADS,
             GDN_VAL, D_MODEL, D_MODEL)

LANES = 128
SUBLANES = 8

GLA_QKV = 2 * GLA_KEY + GLA_VAL
P_GLA_QKV = 0
P_QKV_B = P_GLA_QKV + GLA_QKV
P_SMALL = P_QKV_B + CONV_CH
P_GATES = P_SMALL + LANES
GZ_COLS = 2 * D_MODEL + GLA_VAL + GDN_VAL
P_COLS = P_GATES + GZ_COLS
GZ_GATE_A = 0
GZ_GATE_B = D_MODEL
GZ_Z_A = 2 * D_MODEL
GZ_Z_B = 2 * D_MODEL + GLA_VAL
SM_G = 0
SM_A = GLA_GATE_RANK
SM_B = GLA_GATE_RANK + GDN_HEADS

PROMPT_CHUNK = 64
GROUP_ROWS = 128
BATCH_INTERLEAVE = 8
GDN_BATCH_INTERLEAVE = 8
GDN_HEAD_GROUP = 4
INVERSE_BASE_BLOCK = 16
ROW_TILE = 512
INPROJ_ROW_TILE = 512
VMEM_MIB = {"regroup_w_in": 20, "inproj": 36, "gla_prompt": 28, "gla_sample": 24, "gdn_prompt": 46,
            "gdn_sample": 28, "out_stage": 36}


def _dot(a, b):
    return jnp.dot(a.astype(BF16), b.astype(BF16), preferred_element_type=F32)


def _dot_nt(a, b):
    return lax.dot_general(a.astype(BF16), b.astype(BF16), (((1,), (1,)), ((), ())),
                           preferred_element_type=F32)


def _split2(x):
    h1 = x.astype(BF16)
    return h1, (x - h1.astype(F32)).astype(BF16)


def _cumsum_rows(tri, x):
    x1, x2 = _split2(x)
    d = functools.partial(jnp.dot, preferred_element_type=F32)
    return d(tri, x1) + d(tri, x2)


def _softplus(x):
    return jnp.maximum(x, 0.0) + jnp.log(1.0 + jnp.exp(-jnp.abs(x)))


def _sigmoid(x):
    return 1.0 / (1.0 + jnp.exp(-x))


def _silu(x):
    return x * _sigmoid(x)


def _interleave(emitters):
    results = [None] * len(emitters)
    live = list(range(len(emitters)))
    while live:
        for i in list(live):
            try:
                next(emitters[i])
            except StopIteration as stop:
                results[i] = stop.value
                live.remove(i)
    return results


def _block_masks(r, block):
    row = lax.broadcasted_iota(jnp.int32, (r, r), 0)
    col = lax.broadcasted_iota(jnp.int32, (r, r), 1)
    same = (row // block) == (col // block)
    return same & (row >= col), same & (row > col), same


INPROJ_COL_STEP = 512


def _inproj_kernel(x_ref, nw_ref, w_ref, gla_ref, gdn_ref, small_ref, gz_ref):
    x = x_ref[...]
    xn = x * lax.rsqrt(jnp.mean(x * x, axis=-1, keepdims=True) + NORM_EPS) * nw_ref[...]
    xb = xn.astype(BF16)

    def emit(o_ref, w0, width, act=None, o0=0):
        for c0 in range(0, width, INPROJ_COL_STEP):
            c1 = min(c0 + INPROJ_COL_STEP, width)
            res = _dot_nt(xb, w_ref[w0 + c0:w0 + c1, :])
            o_ref[:, o0 + c0:o0 + c1] = (res if act is None else act(res)).astype(o_ref.dtype)

    emit(gla_ref, P_GLA_QKV, GLA_QKV)
    emit(gdn_ref, P_QKV_B, CONV_CH)
    emit(small_ref, P_SMALL, LANES)
    emit(gz_ref, P_GATES + GZ_GATE_A, 2 * D_MODEL, act=_sigmoid, o0=GZ_GATE_A)
    emit(gz_ref, P_GATES + GZ_Z_A, GLA_VAL + GDN_VAL, act=_silu, o0=GZ_Z_A)


def _inproj(x2d, norm_w, w_in_r):
    n = x2d.shape[0]
    tm = INPROJ_ROW_TILE if n >= 4 * INPROJ_ROW_TILE else min(ROW_TILE, n)
    rows = lambda i: (i, 0)
    return pl.pallas_call(
        _inproj_kernel,
        grid=(n // tm,),
        in_specs=[
            pl.BlockSpec((tm, D_MODEL), rows),
            pl.BlockSpec((1, D_MODEL), lambda i: (0, 0)),
            pl.BlockSpec((P_COLS, D_MODEL), lambda i: (0, 0), pipeline_mode=pl.Buffered(1)),
        ],
        out_specs=[
            pl.BlockSpec((tm, GLA_QKV), rows),
            pl.BlockSpec((tm, CONV_CH), rows),
            pl.BlockSpec((tm, LANES), rows),
            pl.BlockSpec((tm, GZ_COLS), rows),
        ],
        out_shape=[
            jax.ShapeDtypeStruct((n, GLA_QKV), F32),
            jax.ShapeDtypeStruct((n, CONV_CH), F32),
            jax.ShapeDtypeStruct((n, LANES), F32),
            jax.ShapeDtypeStruct((n, GZ_COLS), BF16),
        ],
        compiler_params=pltpu.CompilerParams(dimension_semantics=("arbitrary",),
                                             vmem_limit_bytes=VMEM_MIB["inproj"] * 2 ** 20),
        name="inproj",
    )(x2d, norm_w, w_in_r)


GLA_PAIRS = GLA_HEADS // 2
GLA_SUB_BLOCK = 16


def _gla_att_levels(r, block):
    row = lax.broadcasted_iota(jnp.int32, (r, r), 0)
    col = lax.broadcasted_iota(jnp.int32, (r, r), 1)
    sub = min(block, GLA_SUB_BLOCK)
    levels = [(None, ((row // sub) == (col // sub)) & (row >= col))]
    half = sub
    while half < block:
        levels.append((half, ((row // (2 * half)) == (col // (2 * half)))
                       & ((row // half) % 2 == 1) & ((col // half) % 2 == 0)))
        half *= 2
    return levels


def _rows_at(x, n, offset):
    return jnp.concatenate([jnp.broadcast_to(x[i + offset:i + offset + 1, :], (n, x.shape[1]))
                            for i in range(0, x.shape[0], n)], axis=0)


def _gla_prepass(q, k, v, sm, wgg, bgg, same_le, same, levels, block):
    r = q.shape[0]
    heads = range(GLA_HEADS)
    pairs = range(GLA_PAIRS)
    pre = _dot(sm, wgg) + bgg
    yield
    gk = (jnp.minimum(pre, 0.0) - jnp.log(1.0 + jnp.exp(-jnp.abs(pre)))) * (1.0 / GLA_GATE_TEMP)
    sums = _cumsum_rows(jnp.concatenate([jnp.where(same_le, 1.0, 0.0).astype(BF16),
                                         jnp.where(same, 1.0, 0.0).astype(BF16)], axis=0), gk)
    yield
    bcum = sums[:r]
    bend = sums[r:]
    bex = bcum - gk
    scale = GLA_DK ** -0.5
    q_e = q * jnp.exp(bcum) * scale
    lane = lax.broadcasted_iota(jnp.int32, (r, LANES), 1)
    in_head = [lane < GLA_DK, lane >= GLA_DK]
    pl_ = [slice(p * LANES, (p + 1) * LANES) for p in pairs]
    head_only = lambda x, h: jnp.where(in_head[h % 2], x[:, pl_[h // 2]], 0.0)
    qm = [head_only(q_e, h) for h in heads]
    sub = min(block, GLA_SUB_BLOCK)
    start = _rows_at(bex, sub, 0)
    groups = {None: (q * jnp.exp(bcum - start) * scale, [(levels[0][1], k * jnp.exp(start - bcum))])}
    for half, mask in levels[1:]:
        key = None if half == sub else half
        if key not in groups:
            groups[key] = (q * jnp.exp(bcum - _rows_at(bex, half, 0)) * scale, [])
        groups[key][1].append((mask, k * jnp.exp(_rows_at(bcum, half, half - 1) - bcum)))
    att = [jnp.zeros((r, r), F32)] * GLA_HEADS
    for q_l, parts in groups.values():
        for h in heads:
            keys = [k_l[:, pl_[h // 2]].astype(BF16) for _, k_l in parts]
            prod = _dot_nt(head_only(q_l, h), keys[0] if len(keys) == 1 else jnp.concatenate(keys, axis=0))
            for i, (mask, _) in enumerate(parts):
                att[h] = jnp.where(mask, prod[:, i * r:(i + 1) * r], att[h])
    yield
    o_intra = [_dot(att[h], v[h]) for h in heads]
    k_end = k * jnp.exp(bend - bcum)
    k_end_t = [k_end[:, pl_[p]].T for p in pairs]
    bend_t = [bend[:, pl_[p]].T for p in pairs]
    yield
    return qm, o_intra, k_end_t, bend_t


def _gla_prompt_kernel(qkv_ref, small_ref, s0_ref, wgg_ref, bgg_ref, o_ref, sfin_ref, s_scr, *, c):
    t = pl.program_id(0)
    nb, r, _ = qkv_ref.shape

    @pl.when(t == 0)
    def _():
        s_scr[...] = s0_ref[...]

    same_le, _, same = _block_masks(r, c)
    levels = _gla_att_levels(r, c)
    zeros = jnp.zeros((c, GLA_DV), F32)
    n_sub = r // c
    heads = range(GLA_HEADS)
    pairs = range(GLA_PAIRS)

    def one_batch(b):
        q = qkv_ref[b, :, 0:GLA_KEY]
        k = qkv_ref[b, :, GLA_KEY:2 * GLA_KEY]
        v = [qkv_ref[b, :, 2 * GLA_KEY + h * GLA_DV:2 * GLA_KEY + (h + 1) * GLA_DV] for h in heads]
        qm, o_intra, k_end_t, bend_t = yield from _gla_prepass(
            q, k, v, small_ref[b], wgg_ref[...], bgg_ref[...], same_le, same, levels, c)
        s = [s_scr[b, p * LANES:(p + 1) * LANES, :] for p in pairs]
        for i in range(n_sub):
            rows = slice(i * c, (i + 1) * c)
            ws = [_dot(jnp.concatenate([qm[2 * p][rows], qm[2 * p + 1][rows]], axis=0), s[p]) for p in pairs]
            padded = [jnp.concatenate([zeros] * i + [v[h][rows]] + [zeros] * (n_sub - 1 - i), axis=0)
                      for h in heads]
            upd = [_dot(k_end_t[h // 2][(h % 2) * GLA_DK:(h % 2 + 1) * GLA_DK, :], padded[h]) for h in heads]
            yield
            for h in heads:
                o_ref[b, rows, h * GLA_DV:(h + 1) * GLA_DV] = (
                    o_intra[h][rows] + ws[h // 2][(h % 2) * c:(h % 2 + 1) * c]).astype(o_ref.dtype)
            s = [s[p] * jnp.exp(bend_t[p][:, i * c:i * c + 1])
                 + jnp.concatenate([upd[2 * p], upd[2 * p + 1]], axis=0) for p in pairs]
        for p in pairs:
            s_scr[b, p * LANES:(p + 1) * LANES, :] = s[p]

    def per_step(i, carry):
        _interleave([one_batch(i * BATCH_INTERLEAVE + j) for j in range(BATCH_INTERLEAVE)])
        return carry

    lax.fori_loop(0, nb // BATCH_INTERLEAVE, per_step, None)

    @pl.when(t == pl.num_programs(0) - 1)
    def _():
        sfin_ref[...] = s_scr[...]


def _gla_sample_kernel(qkv_ref, small_ref, s0_ref, wgg_ref, bgg_ref, o_ref, sfin_ref):
    ns, c, _ = qkv_ref.shape
    r = ns * c
    heads = range(GLA_HEADS)
    same_le, _, same = _block_masks(r, c)
    levels = _gla_att_levels(r, c)
    q = qkv_ref[:, :, 0:GLA_KEY].reshape(r, GLA_KEY)
    k = qkv_ref[:, :, GLA_KEY:2 * GLA_KEY].reshape(r, GLA_KEY)
    v = [qkv_ref[:, :, 2 * GLA_KEY + h * GLA_DV:2 * GLA_KEY + (h + 1) * GLA_DV].reshape(r, GLA_DV)
         for h in heads]
    (qm, o_intra, k_end_t, bend_t), = _interleave([_gla_prepass(
        q, k, v, small_ref[...].reshape(r, LANES), wgg_ref[...], bgg_ref[...], same_le, same, levels, c)])
    seq_of_row = lax.broadcasted_iota(jnp.int32, (r, GLA_DV), 0) // c
    for p in range(GLA_PAIRS):
        ps = slice(p * LANES, (p + 1) * LANES)
        inter = [[], []]
        for s in range(ns):
            rows = slice(s * c, (s + 1) * c)
            ws = _dot(jnp.concatenate([qm[2 * p][rows], qm[2 * p + 1][rows]], axis=0), s0_ref[s, ps, :])
            inter[0].append(ws[:c])
            inter[1].append(ws[c:])
        for hh in range(2):
            h = 2 * p + hh
            o = o_intra[h] + jnp.concatenate(inter[hh], axis=0)
            o_ref[:, h * GLA_DV:(h + 1) * GLA_DV] = o.astype(o_ref.dtype)
        for s in range(ns):
            upd = [_dot(k_end_t[p][hh * GLA_DK:(hh + 1) * GLA_DK, :],
                        jnp.where(seq_of_row == s, v[2 * p + hh], 0.0)) for hh in range(2)]
            sfin_ref[s, ps, :] = (s0_ref[s, ps, :] * jnp.exp(bend_t[p][:, s * c:s * c + 1])
                                  + jnp.concatenate(upd, axis=0))


def _gla_prompt(qkv3, small3, s0, wgg, bgg, c):
    bsz, t_len, _ = qkv3.shape
    rows = GLA_HEADS * GLA_DK
    r = GROUP_ROWS
    full3 = lambda t: (0, 0, 0)
    const2 = lambda t: (0, 0)
    return pl.pallas_call(
        functools.partial(_gla_prompt_kernel, c=c),
        grid=(t_len // r,),
        in_specs=[
            pl.BlockSpec((bsz, r, GLA_QKV), lambda t: (0, t, 0)),
            pl.BlockSpec((bsz, r, LANES), lambda t: (0, t, 0)),
            pl.BlockSpec((bsz, rows, GLA_DV), full3),
            pl.BlockSpec((LANES, GLA_KEY), const2),
            pl.BlockSpec((1, GLA_KEY), const2),
        ],
        out_specs=[
            pl.BlockSpec((bsz, r, GLA_VAL), lambda t: (0, t, 0)),
            pl.BlockSpec((bsz, rows, GLA_DV), full3),
        ],
        out_shape=[
            jax.ShapeDtypeStruct((bsz, t_len, GLA_VAL), BF16),
            jax.ShapeDtypeStruct((bsz, rows, GLA_DV), F32),
        ],
        scratch_shapes=[pltpu.VMEM((bsz, rows, GLA_DV), F32)],
        compiler_params=pltpu.CompilerParams(dimension_semantics=("arbitrary",),
                                             vmem_limit_bytes=VMEM_MIB["gla_prompt"] * 2 ** 20),
        name="gla_prompt",
    )(qkv3, small3, s0, wgg, bgg)


def _gla_sample(qkv3, small3, s0, wgg, bgg):
    bsz, c, _ = qkv3.shape
    rows = GLA_HEADS * GLA_DK
    ns = GROUP_ROWS // c
    grp = lambda g: (g, 0, 0)
    const2 = lambda g: (0, 0)
    return pl.pallas_call(
        _gla_sample_kernel,
        grid=(bsz // ns,),
        in_specs=[
            pl.BlockSpec((ns, c, GLA_QKV), grp),
            pl.BlockSpec((ns, c, LANES), grp),
            pl.BlockSpec((ns, rows, GLA_DV), grp),
            pl.BlockSpec((LANES, GLA_KEY), const2),
            pl.BlockSpec((1, GLA_KEY), const2),
        ],
        out_specs=[
            pl.BlockSpec((ns * c, GLA_VAL), lambda g: (g, 0)),
            pl.BlockSpec((ns, rows, GLA_DV), grp),
        ],
        out_shape=[
            jax.ShapeDtypeStruct((bsz * c, GLA_VAL), BF16),
            jax.ShapeDtypeStruct((bsz, rows, GLA_DV), F32),
        ],
        compiler_params=pltpu.CompilerParams(dimension_semantics=("arbitrary",),
                                             vmem_limit_bytes=VMEM_MIB["gla_sample"] * 2 ** 20),
        name="gla_sample",
    )(qkv3, small3, s0, wgg, bgg)


def _block_unit_lower_inverse(a_list, block):
    r = a_list[0].shape[0]
    row = lax.broadcasted_iota(jnp.int32, (r, r), 0)
    col = lax.broadcasted_iota(jnp.int32, (r, r), 1)
    in_block = lambda n: (row // n) == (col // n)
    base = min(block, INVERSE_BASE_BLOCK)
    a_base = a_list if base == block else [jnp.where(in_block(base), a, 0.0) for a in a_list]
    xs = [jnp.where(row == col, 1.0, 0.0) - a for a in a_base]
    ps = [_dot(a, a) for a in a_base]
    n = 2
    while n < base:
        yield
        last = 2 * n >= base
        ms = [_dot(x if last else jnp.concatenate([p.astype(BF16), x.astype(BF16)], axis=0), p)
              for p, x in zip(ps, xs)]
        ps = [m[:r] for m in ms]
        xs = [x + m[-r:] for x, m in zip(xs, ms)]
        n *= 2
    n = base
    while n < block:
        yield
        between = in_block(2 * n) & jnp.logical_not(in_block(n))
        ts = [_dot(jnp.where(between, a, 0.0), x) for a, x in zip(a_list, xs)]
        yield
        xs = [x - _dot(x, t) for x, t in zip(xs, ts)]
        n *= 2
    return xs


def _gdn_decays(sm, alog_c, dtb_c, same_lt, same):
    r = sm.shape[0]
    heads = range(GDN_HEADS)
    ab = sm.T[SM_A:SM_A + 2 * GDN_HEADS, :]
    g8 = -jnp.exp(alog_c) * _softplus(ab + dtb_c)
    beta_c = _sigmoid(ab).T
    g1, g2 = _split2(g8)
    same_ge = same & jnp.logical_not(same_lt)
    sel = jnp.concatenate([jnp.where(same_ge, 1.0, 0.0).astype(BF16),
                           jnp.where(same, 1.0, 0.0).astype(BF16)], axis=1)
    sums = jnp.dot(jnp.concatenate([g1, g2], axis=0), sel, preferred_element_type=F32)
    sums = sums[:2 * GDN_HEADS] + sums[2 * GDN_HEADS:]
    dec_rows = sums[:, :r]
    dec_c = dec_rows.T
    dend_c = sums[:, r:].T
    return ([dec_c[:, h:h + 1] for h in heads], [dec_rows[h:h + 1, :] for h in heads],
            [dend_c[:, h:h + 1] for h in heads], [beta_c[:, GDN_HEADS + h:GDN_HEADS + h + 1] for h in heads])


def _gdn_head_cols(h):
    return [slice(part * GDN_KEY + h * GDN_DK, part * GDN_KEY + (h + 1) * GDN_DK) for part in range(3)]


def _gdn_prepass(qkv_of_head, sm, alog_c, dtb_c, same_le, same_lt, same, block, heads):
    dcol, drow, dend, beta = ([x[h] for h in heads] for x in _gdn_decays(sm, alog_c, dtb_c, same_lt, same))
    idx = range(len(heads))
    gamma = [jnp.where(same_le, jnp.exp(jnp.where(same_le, dcol[i] - drow[i], 0.0)), 0.0) for i in idx]
    q, k, v = (list(x) for x in zip(*[qkv_of_head(h) for h in heads]))
    r = q[0].shape[0]
    q = [x * lax.rsqrt(jnp.sum(x * x, axis=-1, keepdims=True) + NORM_EPS) * (GDN_DK ** -0.5) for x in q]
    k = [x * lax.rsqrt(jnp.sum(x * x, axis=-1, keepdims=True) + NORM_EPS) for x in k]
    yield
    kb = [k[i] * beta[i] for i in idx]
    vb = [v[i] * beta[i] for i in idx]
    kq = [_dot_nt(jnp.concatenate([kb[i], q[i]], axis=0), k[i]) for i in idx]
    yield
    a_mat = [jnp.where(same_lt, kq[i][:r] * gamma[i], 0.0) for i in idx]
    qk = [(kq[i][r:] * gamma[i]).astype(BF16) for i in idx]
    t_inv = yield from _block_unit_lower_inverse(a_mat, block)
    edec = [jnp.exp(dcol[i]) for i in idx]
    uw = [_dot(t_inv[i], jnp.concatenate([vb[i], kb[i] * edec[i]], axis=1)) for i in idx]
    q_e = [(q[i] * edec[i]).astype(BF16) for i in idx]
    k_end_t = [(k[i] * jnp.exp(dend[i] - dcol[i])).T.astype(BF16) for i in idx]
    yield
    u = [x[:, :GDN_DV] for x in uw]
    w = [x[:, GDN_DV:].astype(BF16) for x in uw]
    return u, w, q_e, qk, k_end_t, dend


def _conv_silu(win, cw_ref):
    conv = win(0) * cw_ref[CONV_WIDTH - 1:CONV_WIDTH, :]
    for j in range(1, CONV_WIDTH):
        conv = conv + win(j) * cw_ref[CONV_WIDTH - 1 - j:CONV_WIDTH - j, :]
    return _silu(conv)

def _gdn_prompt_kernel(qkv_ref, small_ref, cs_ref, s0_ref, cw_ref, alog_ref, dtb_ref,
                       o_ref, sfin_ref, cnew_ref, xbuf, prev, s_scr, *, c):
    t = pl.program_id(0)
    nb, r, _ = qkv_ref.shape
    keep = CONV_WIDTH - 1
    base = SUBLANES

    @pl.when(t == 0)
    def _():
        s_scr[...] = s0_ref[...]
        prev[:, base - keep:base, :] = cs_ref[...]

    same_le, same_lt, same = _block_masks(r, c)
    zeros = jnp.zeros((c, GDN_DV), F32)
    n_sub = r // c

    def one_batch(b, slot, heads, first):
        xb = xbuf.at[slot]
        if first:
            xb[base - keep:base, :] = prev[b, base - keep:base, :]
            xb[base:base + r, :] = qkv_ref[b]
            prev[b, base - keep:base, :] = xb[base + r - keep:base + r, :]

        def qkv_of_head(h):
            return [_conv_silu(lambda j: xb[base - j:base - j + r, cols], cw_ref.at[:, cols])
                    for cols in _gdn_head_cols(h)]

        u, w, q_e, qk, k_end_t, dend = yield from _gdn_prepass(
            qkv_of_head, small_ref[b], alog_ref[...], dtb_ref[...], same_le, same_lt, same, c, heads)
        idx = range(len(heads))
        hs = [slice(h * GDN_DK, (h + 1) * GDN_DK) for h in heads]
        s = [s_scr[b, hs[i], :] for i in idx]
        for ci in range(n_sub):
            rows = slice(ci * c, (ci + 1) * c)
            ws = [_dot(jnp.concatenate([w[i][rows], q_e[i][rows]], axis=0), s[i]) for i in idx]
            yield
            v_new = [u[i][rows] - ws[i][:c] for i in idx]
            padded = [jnp.concatenate([zeros] * ci + [v_new[i]] + [zeros] * (n_sub - 1 - ci), axis=0)
                      for i in idx]
            upd = [_dot(jnp.concatenate([qk[i][rows], k_end_t[i]], axis=0), padded[i]) for i in idx]
            yield
            for i in idx:
                o_ref[b, rows, hs[i]] = (ws[i][c:] + upd[i][:c]).astype(o_ref.dtype)
            s = [s[i] * jnp.exp(dend[i][ci * c:ci * c + 1, :]) + upd[i][c:] for i in idx]
        for i in idx:
            s_scr[b, hs[i], :] = s[i]

    def per_step(step, carry):
        for g in range(0, GDN_HEADS, GDN_HEAD_GROUP):
            heads = tuple(range(g, g + GDN_HEAD_GROUP))
            _interleave([one_batch(step * GDN_BATCH_INTERLEAVE + j, j, heads, g == 0)
                         for j in range(GDN_BATCH_INTERLEAVE)])
        return carry

    lax.fori_loop(0, nb // GDN_BATCH_INTERLEAVE, per_step, None)

    @pl.when(t == pl.num_programs(0) - 1)
    def _():
        sfin_ref[...] = s_scr[...]
        cnew_ref[...] = prev[:, base - keep:base, :]


def _gdn_sample_kernel(qkv_ref, small_ref, cs_ref, s0_ref, cw_ref, alog_ref, dtb_ref,
                       o_ref, sfin_ref, cnew_ref, xbuf):
    ns, c, _ = qkv_ref.shape
    r = ns * c
    keep = CONV_WIDTH - 1
    base = SUBLANES
    xbuf[:, base - keep:base, :] = cs_ref[...]
    xbuf[:, base:base + c, :] = qkv_ref[...]
    conv = _conv_silu(lambda j: xbuf[:, base - j:base - j + c, :], cw_ref).reshape(r, CONV_CH)
    cnew_ref[...] = xbuf[:, base + c - keep:base + c, :]

    same_le, same_lt, same = _block_masks(r, c)
    seq_of_row = lax.broadcasted_iota(jnp.int32, (r, GDN_DV), 0) // c
    (u, w, q_e, qk, k_end_t, dend), = _interleave([_gdn_prepass(
        lambda h: [conv[:, cols] for cols in _gdn_head_cols(h)], small_ref[...].reshape(r, LANES),
        alog_ref[...], dtb_ref[...], same_le, same_lt, same, c, tuple(range(GDN_HEADS)))])
    for h in range(GDN_HEADS):
        hs = slice(h * GDN_DK, (h + 1) * GDN_DK)
        v_parts, o_parts = [], []
        for s in range(ns):
            rows = slice(s * c, (s + 1) * c)
            ws = _dot(jnp.concatenate([w[h][rows], q_e[h][rows]], axis=0), s0_ref[s, hs, :])
            v_parts.append(u[h][rows] - ws[:c])
            o_parts.append(ws[c:])
        v_new = jnp.concatenate(v_parts, axis=0)
        o = jnp.concatenate(o_parts, axis=0) + _dot(qk[h], v_new)
        o_ref[:, h * GDN_DV:(h + 1) * GDN_DV] = o.astype(o_ref.dtype)
        for s in range(ns):
            upd = _dot(k_end_t[h], jnp.where(seq_of_row == s, v_new, 0.0))
            sfin_ref[s, hs, :] = s0_ref[s, hs, :] * jnp.exp(dend[h][s * c:s * c + 1, :]) + upd


def _gdn_prompt(qkv3, small3, conv_state, s0, conv_w, alog_v, dtb_v, c):
    bsz, t_len, _ = qkv3.shape
    rows = GDN_HEADS * GDN_DK
    keep = CONV_WIDTH - 1
    r = GROUP_ROWS
    full3 = lambda t: (0, 0, 0)
    const2 = lambda t: (0, 0)
    return pl.pallas_call(
        functools.partial(_gdn_prompt_kernel, c=c),
        grid=(t_len // r,),
        in_specs=[
            pl.BlockSpec((bsz, r, CONV_CH), lambda t: (0, t, 0)),
            pl.BlockSpec((bsz, r, LANES), lambda t: (0, t, 0)),
            pl.BlockSpec((bsz, keep, CONV_CH), full3),
            pl.BlockSpec((bsz, rows, GDN_DV), full3),
            pl.BlockSpec((CONV_WIDTH, CONV_CH), const2),
            pl.BlockSpec((2 * GDN_HEADS, 1), const2),
            pl.BlockSpec((2 * GDN_HEADS, 1), const2),
        ],
        out_specs=[
            pl.BlockSpec((bsz, r, GDN_VAL), lambda t: (0, t, 0)),
            pl.BlockSpec((bsz, rows, GDN_DV), full3),
            pl.BlockSpec((bsz, keep, CONV_CH), full3),
        ],
        out_shape=[
            jax.ShapeDtypeStruct((bsz, t_len, GDN_VAL), BF16),
            jax.ShapeDtypeStruct((bsz, rows, GDN_DV), F32),
            jax.ShapeDtypeStruct((bsz, keep, CONV_CH), F32),
        ],
        scratch_shapes=[pltpu.VMEM((GDN_BATCH_INTERLEAVE, SUBLANES + r, CONV_CH), F32),
                        pltpu.VMEM((bsz, SUBLANES, CONV_CH), F32),
                        pltpu.VMEM((bsz, rows, GDN_DV), F32)],
        compiler_params=pltpu.CompilerParams(dimension_semantics=("arbitrary",),
                                             vmem_limit_bytes=VMEM_MIB["gdn_prompt"] * 2 ** 20),
        name="gdn_prompt",
    )(qkv3, small3, conv_state, s0, conv_w, alog_v, dtb_v)


def _gdn_sample(qkv3, small3, conv_state, s0, conv_w, alog_v, dtb_v):
    bsz, c, _ = qkv3.shape
    rows = GDN_HEADS * GDN_DK
    keep = CONV_WIDTH - 1
    ns = GROUP_ROWS // c
    grp = lambda g: (g, 0, 0)
    const2 = lambda g: (0, 0)
    return pl.pallas_call(
        _gdn_sample_kernel,
        grid=(bsz // ns,),
        in_specs=[
            pl.BlockSpec((ns, c, CONV_CH), grp),
            pl.BlockSpec((ns, c, LANES), grp),
            pl.BlockSpec((ns, keep, CONV_CH), grp),
            pl.BlockSpec((ns, rows, GDN_DV), grp),
            pl.BlockSpec((CONV_WIDTH, CONV_CH), const2),
            pl.BlockSpec((2 * GDN_HEADS, 1), const2),
            pl.BlockSpec((2 * GDN_HEADS, 1), const2),
        ],
        out_specs=[
            pl.BlockSpec((ns * c, GDN_VAL), lambda g: (g, 0)),
            pl.BlockSpec((ns, rows, GDN_DV), grp),
            pl.BlockSpec((ns, keep, CONV_CH), grp),
        ],
        out_shape=[
            jax.ShapeDtypeStruct((bsz * c, GDN_VAL), BF16),
            jax.ShapeDtypeStruct((bsz, rows, GDN_DV), F32),
            jax.ShapeDtypeStruct((bsz, keep, CONV_CH), F32),
        ],
        scratch_shapes=[pltpu.VMEM((ns, 2 * SUBLANES, CONV_CH), F32)],
        compiler_params=pltpu.CompilerParams(dimension_semantics=("arbitrary",),
                                             vmem_limit_bytes=VMEM_MIB["gdn_sample"] * 2 ** 20),
        name="gdn_sample",
    )(qkv3, small3, conv_state, s0, conv_w, alog_v, dtb_v)


def _head_norm_gate(o, silu_z, w):
    parts = []
    for h in range(o.shape[-1] // LANES):
        oh = o[:, h * LANES:(h + 1) * LANES]
        parts.append(oh * lax.rsqrt(jnp.mean(oh * oh, axis=-1, keepdims=True) + NORM_EPS) * w)
    return jnp.concatenate(parts, axis=-1) * silu_z


def _out_kernel(oa_ref, ob_ref, ga_ref, gb_ref, za_ref, zb_ref, x_ref, p_ref, anw_ref, bnw_ref,
                wua_ref, wub_ref, wout_ref, wpg_ref, wp_ref, fnw_ref, y_ref):
    f32 = lambda ref: ref[...].astype(F32)
    ya = _dot(_head_norm_gate(f32(oa_ref), f32(za_ref), anw_ref[...]), wua_ref[...])
    yb = _dot(_head_norm_gate(f32(ob_ref), f32(zb_ref), bnw_ref[...]), wub_ref[...])
    merged = f32(ga_ref) * ya + f32(gb_ref) * yb
    h1 = x_ref[...] + _dot(merged, wout_ref[...])
    h2 = h1 + _sigmoid(_dot(h1, wpg_ref[...])) * _dot(p_ref[...], wp_ref[...])
    y_ref[...] = h2 * lax.rsqrt(jnp.mean(h2 * h2, axis=-1, keepdims=True) + NORM_EPS) * fnw_ref[...]


def _out_stage(o_a, o_b, gz, x2d, p2d, anw, bnw, wua, wub, wout, wpg, wp, fnw):
    n = x2d.shape[0]
    tm = min(ROW_TILE, n)
    const = lambda i: (0, 0)
    return pl.pallas_call(
        _out_kernel,
        grid=(n // tm,),
        in_specs=[
            pl.BlockSpec((tm, GLA_VAL), lambda i: (i, 0)),
            pl.BlockSpec((tm, GDN_VAL), lambda i: (i, 0)),
            pl.BlockSpec((tm, D_MODEL), lambda i: (i, GZ_GATE_A // D_MODEL)),
            pl.BlockSpec((tm, D_MODEL), lambda i: (i, GZ_GATE_B // D_MODEL)),
            pl.BlockSpec((tm, GLA_VAL), lambda i: (i, GZ_Z_A // GLA_VAL)),
            pl.BlockSpec((tm, GDN_VAL), lambda i: (i, GZ_Z_B // GDN_VAL)),
            pl.BlockSpec((tm, D_MODEL), lambda i: (i, 0)),
            pl.BlockSpec((tm, PLE_DIM), lambda i: (i, 0)),
            pl.BlockSpec((1, GLA_DV), const),
            pl.BlockSpec((1, GDN_DV), const),
            pl.BlockSpec((GLA_VAL, D_MODEL), const),
            pl.BlockSpec((GDN_VAL, D_MODEL), const),
            pl.BlockSpec((D_MODEL, D_MODEL), const),
            pl.BlockSpec((D_MODEL, D_MODEL), const),
            pl.BlockSpec((PLE_DIM, D_MODEL), const),
            pl.BlockSpec((1, D_MODEL), const),
        ],
        out_specs=pl.BlockSpec((tm, D_MODEL), lambda i: (i, 0)),
        out_shape=jax.ShapeDtypeStruct((n, D_MODEL), F32),
        compiler_params=pltpu.CompilerParams(dimension_semantics=("arbitrary",),
                                             vmem_limit_bytes=VMEM_MIB["out_stage"] * 2 ** 20),
        name="out_stage",
    )(o_a, o_b, gz, gz, gz, gz, x2d, p2d, anw, bnw, wua, wub, wout, wpg, wp, fnw)


def _in_offsets():
    offs = [0]
    for s in IN_SPLITS:
        offs.append(offs[-1] + s)
    return offs


def _regroup_kernel(wt_ref, o_ref):
    offs = _in_offsets()
    (q_a, _, _, g_a, z_a, qkv_b, a_b, _, z_b, gate_a, gate_b, end) = offs
    piece = lambda lo, hi: wt_ref[lo:hi, :].astype(BF16)
    o_ref[P_GLA_QKV:P_GLA_QKV + GLA_QKV, :] = piece(q_a, g_a)
    o_ref[P_QKV_B:P_QKV_B + CONV_CH, :] = piece(qkv_b, a_b)
    small = jnp.concatenate([wt_ref[g_a:z_a, :], wt_ref[a_b:z_b, :],
                             jnp.zeros((LANES - (z_a - g_a) - (z_b - a_b), wt_ref.shape[1]), F32)], axis=0)
    o_ref[P_SMALL:P_SMALL + LANES, :] = small.astype(BF16)
    o_ref[P_GATES + GZ_GATE_A:P_GATES + GZ_GATE_A + D_MODEL, :] = piece(gate_a, gate_b)
    o_ref[P_GATES + GZ_GATE_B:P_GATES + GZ_GATE_B + D_MODEL, :] = piece(gate_b, end)
    o_ref[P_GATES + GZ_Z_A:P_GATES + GZ_Z_A + GLA_VAL, :] = piece(z_a, qkv_b)
    o_ref[P_GATES + GZ_Z_B:P_GATES + GZ_Z_B + GDN_VAL, :] = piece(z_b, gate_a)


def _regroup_w_in(w_in_t):
    cols = 256
    return pl.pallas_call(
        _regroup_kernel,
        grid=(D_MODEL // cols,),
        in_specs=[pl.BlockSpec((None, w_in_t.shape[1], cols), lambda i: (0, 0, i))],
        out_specs=pl.BlockSpec((P_COLS, cols), lambda i: (0, i)),
        out_shape=jax.ShapeDtypeStruct((P_COLS, D_MODEL), BF16),
        compiler_params=pltpu.CompilerParams(dimension_semantics=("arbitrary",),
                                             vmem_limit_bytes=VMEM_MIB["regroup_w_in"] * 2 ** 20),
        name="regroup_w_in",
    )(w_in_t)


def _head_param_col(v):
    return jnp.zeros((2 * GDN_HEADS, 1), F32).at[:GDN_HEADS, 0].set(v.astype(F32))


def _trunk(x, p, s_gla, s_gdn, conv_state, wts):
    bsz, t_len, _ = x.shape
    c = min(PROMPT_CHUNK, t_len)
    n = bsz * t_len
    x2d = x.reshape(n, D_MODEL)
    p2d = p.reshape(n, PLE_DIM)
    gla_in, gdn_in, small, gz = _inproj(x2d, wts["norm_w"], wts["w_in_r"])
    gla_in3 = gla_in.reshape(bsz, t_len, GLA_QKV)
    gdn_in3 = gdn_in.reshape(bsz, t_len, CONV_CH)
    small3 = small.reshape(bsz, t_len, LANES)
    s_gla2 = s_gla.reshape(bsz, GLA_HEADS * GLA_DK, GLA_DV)
    s_gdn2 = s_gdn.reshape(bsz, GDN_HEADS * GDN_DK, GDN_DV)
    if t_len % GROUP_ROWS == 0:
        o_a, gla_fin = _gla_prompt(gla_in3, small3, s_gla2, wts["wgg"], wts["bgg"], c)
        o_b, gdn_fin, conv_new = _gdn_prompt(gdn_in3, small3, conv_state, s_gdn2, wts["conv_w"],
                                             wts["alog_v"], wts["dtb_v"], c)
    else:
        assert GROUP_ROWS % t_len == 0 and bsz % (GROUP_ROWS // t_len) == 0 and t_len >= CONV_WIDTH - 1
        o_a, gla_fin = _gla_sample(gla_in3, small3, s_gla2, wts["wgg"], wts["bgg"])
        o_b, gdn_fin, conv_new = _gdn_sample(gdn_in3, small3, conv_state, s_gdn2, wts["conv_w"],
                                             wts["alog_v"], wts["dtb_v"])
    y = _out_stage(o_a.reshape(n, GLA_VAL), o_b.reshape(n, GDN_VAL), gz, x2d, p2d, wts["anw"], wts["bnw"],
                   wts["wua"], wts["wub"], wts["wout"], wts["wpg"], wts["wp"], wts["fnw"])
    return (y.reshape(bsz, t_len, D_MODEL),
            gla_fin.reshape(1, bsz, GLA_HEADS, GLA_DK, GLA_DV),
            gdn_fin.reshape(1, bsz, GDN_HEADS, GDN_DK, GDN_DV),
            conv_new.reshape(1, bsz, CONV_WIDTH - 1, CONV_CH))


def kernel(x_prompt, x_sample, state_gla, state_gdn, state_conv, p_prompt, p_sample, norm_w, w_in,
           w_gla_gate, b_gla_gate, gla_norm_w, conv_w, gdn_a_log, gdn_dt_bias, gdn_norm_w,
           w_up_gla, w_up_gdn, w_out, w_ple_gate, w_ple, final_norm_w):
    wgg = jnp.zeros((LANES, GLA_KEY), F32).at[SM_G:SM_G + GLA_GATE_RANK].set(w_gla_gate[0]).astype(BF16)
    wts = {
        "norm_w": norm_w[0].reshape(1, D_MODEL),
        "w_in_r": _regroup_w_in(jnp.swapaxes(w_in, 1, 2)),
        "wgg": wgg,
        "bgg": b_gla_gate[0].reshape(1, GLA_KEY),
        "conv_w": conv_w[0],
        "alog_v": _head_param_col(gdn_a_log[0]),
        "dtb_v": _head_param_col(gdn_dt_bias[0]),
        "anw": gla_norm_w[0].reshape(1, GLA_DV),
        "bnw": gdn_norm_w[0].reshape(1, GDN_DV),
        "wua": w_up_gla[0].astype(BF16),
        "wub": w_up_gdn[0].astype(BF16),
        "wout": w_out[0].astype(BF16),
        "wpg": w_ple_gate[0].astype(BF16),
        "wp": w_ple[0].astype(BF16),
        "fnw": final_norm_w.reshape(1, D_MODEL),
    }
    bsz = x_prompt.shape[0]
    dt = x_prompt.dtype
    y_p, gla_p, gdn_p, conv_p = _trunk(
        x_prompt, p_prompt[0],
        jnp.zeros((bsz, GLA_HEADS, GLA_DK, GLA_DV), dt), jnp.zeros((bsz, GDN_HEADS, GDN_DK, GDN_DV), dt),
        jnp.zeros((bsz, CONV_WIDTH - 1, CONV_CH), dt), wts)
    y_s, gla_s, gdn_s, conv_s = _trunk(x_sample, p_sample[0], state_gla[0], state_gdn[0], state_conv[0], wts)
    return (y_p, y_s, gla_p, gdn_p, conv_p, gla_s, gdn_s, conv_s)
```

```python
import functools

import jax
import jax.numpy as jnp
from jax import lax
from jax.experimental import pallas as pl
from jax.experimental.pallas import tpu as pltpu

F32 = jnp.float32
BF16 = jnp.bfloat16

D_MODEL = 1024
PLE_DIM = 256
NORM_EPS = 1e-6
GLA_HEADS = 4
GLA_DK = 64
GLA_DV = 128
GLA_KEY = GLA_HEADS * GLA_DK
GLA_VAL = GLA_HEADS * GLA_DV
GLA_GATE_RANK = 16
GLA_GATE_TEMP = 16.0
GDN_HEADS = 4
GDN_DK = 128
GDN_DV = 128
GDN_KEY = GDN_HEADS * GDN_DK
GDN_VAL = GDN_HEADS * GDN_DV
CONV_WIDTH = 4
CONV_CH = 2 * GDN_KEY + GDN_VAL
IN_SPLITS = (GLA_KEY, GLA_KEY, GLA_VAL, GLA_GATE_RANK, GLA_VAL, CONV_CH, GDN_HEADS, GDN_HEADS,
             GDN_VAL, D_MODEL, D_MODEL)

LANES = 128
SUBLANES = 8

GLA_QKV = 2 * GLA_KEY + GLA_VAL
P_GLA_QKV = 0
P_QKV_B = P_GLA_QKV + GLA_QKV
P_SMALL = P_QKV_B + CONV_CH
P_GATES = P_SMALL + LANES
GZ_COLS = 2 * D_MODEL + GLA_VAL + GDN_VAL
P_COLS = P_GATES + GZ_COLS
GZ_GATE_A = 0
GZ_GATE_B = D_MODEL
GZ_Z_A = 2 * D_MODEL
GZ_Z_B = 2 * D_MODEL + GLA_VAL
SM_G = 0
SM_A = GLA_GATE_RANK
SM_B = GLA_GATE_RANK + GDN_HEADS

PROMPT_CHUNK = 64
GROUP_ROWS = 128
BATCH_INTERLEAVE = 8
GDN_BATCH_INTERLEAVE = 8
GDN_HEAD_GROUP = 4
INVERSE_BASE_BLOCK = 16
ROW_TILE = 512
INPROJ_ROW_TILE = 512
VMEM_MIB = {"regroup_w_in": 20, "inproj": 40, "gla_prompt": 40, "gla_sample": 24, "gdn_prompt": 48,
            "gdn_sample": 40, "out_stage": 40}


def _dot(a, b):
    return jnp.dot(a.astype(BF16), b.astype(BF16), preferred_element_type=F32)


def _dot_nt(a, b):
    return lax.dot_general(a.astype(BF16), b.astype(BF16), (((1,), (1,)), ((), ())),
                           preferred_element_type=F32)


def _split2(x):
    h1 = x.astype(BF16)
    return h1, (x - h1.astype(F32)).astype(BF16)


def _cumsum_rows(tri, x):
    x1, x2 = _split2(x)
    d = functools.partial(jnp.dot, preferred_element_type=F32)
    return d(tri, x1) + d(tri, x2)


def _softplus(x):
    return jnp.maximum(x, 0.0) + jnp.log(1.0 + jnp.exp(-jnp.abs(x)))


def _sigmoid(x):
    return 1.0 / (1.0 + jnp.exp(-x))


def _silu(x):
    return x * _sigmoid(x)


def _interleave(emitters):
    results = [None] * len(emitters)
    live = list(range(len(emitters)))
    while live:
        for i in list(live):
            try:
                next(emitters[i])
            except StopIteration as stop:
                results[i] = stop.value
                live.remove(i)
    return results


def _block_masks(r, block):
    row = lax.broadcasted_iota(jnp.int32, (r, r), 0)
    col = lax.broadcasted_iota(jnp.int32, (r, r), 1)
    same = (row // block) == (col // block)
    return same & (row >= col), same & (row > col), same


INPROJ_COL_STEP = 512


def _inproj_kernel(x_ref, nw_ref, w_ref, gla_ref, gdn_ref, small_ref, gz_ref):
    x = x_ref[...]
    xn = x * lax.rsqrt(jnp.mean(x * x, axis=-1, keepdims=True) + NORM_EPS) * nw_ref[...]
    xb = xn.astype(BF16)

    def emit(o_ref, w0, width, act=None, o0=0):
        for c0 in range(0, width, INPROJ_COL_STEP):
            c1 = min(c0 + INPROJ_COL_STEP, width)
            res = _dot_nt(xb, w_ref[w0 + c0:w0 + c1, :])
            o_ref[:, o0 + c0:o0 + c1] = (res if act is None else act(res)).astype(o_ref.dtype)

    emit(gla_ref, P_GLA_QKV, GLA_QKV)
    emit(gdn_ref, P_QKV_B, CONV_CH)
    emit(small_ref, P_SMALL, LANES)
    emit(gz_ref, P_GATES + GZ_GATE_A, 2 * D_MODEL, act=_sigmoid, o0=GZ_GATE_A)
    emit(gz_ref, P_GATES + GZ_Z_A, GLA_VAL + GDN_VAL, act=_silu, o0=GZ_Z_A)


def _inproj(x2d, norm_w, w_in_r):
    n = x2d.shape[0]
    tm = INPROJ_ROW_TILE if n >= 4 * INPROJ_ROW_TILE else min(ROW_TILE, n)
    rows = lambda i: (i, 0)
    return pl.pallas_call(
        _inproj_kernel,
        grid=(n // tm,),
        in_specs=[
            pl.BlockSpec((tm, D_MODEL), rows),
            pl.BlockSpec((1, D_MODEL), lambda i: (0, 0)),
            pl.BlockSpec((P_COLS, D_MODEL), lambda i: (0, 0), pipeline_mode=pl.Buffered(1)),
        ],
        out_specs=[
            pl.BlockSpec((tm, GLA_QKV), rows),
            pl.BlockSpec((tm, CONV_CH), rows),
            pl.BlockSpec((tm, LANES), rows),
            pl.BlockSpec((tm, GZ_COLS), rows),
        ],
        out_shape=[
            jax.ShapeDtypeStruct((n, GLA_QKV), F32),
            jax.ShapeDtypeStruct((n, CONV_CH), F32),
            jax.ShapeDtypeStruct((n, LANES), F32),
            jax.ShapeDtypeStruct((n, GZ_COLS), BF16),
        ],
        compiler_params=pltpu.CompilerParams(dimension_semantics=("arbitrary",),
                                             vmem_limit_bytes=VMEM_MIB["inproj"] * 2 ** 20),
        name="inproj",
    )(x2d, norm_w, w_in_r)


GLA_PAIRS = GLA_HEADS // 2
GLA_SUB_BLOCK = 16


def _gla_att_levels(r, block):
    row = lax.broadcasted_iota(jnp.int32, (r, r), 0)
    col = lax.broadcasted_iota(jnp.int32, (r, r), 1)
    sub = min(block, GLA_SUB_BLOCK)
    levels = [(None, ((row // sub) == (col // sub)) & (row >= col))]
    half = sub
    while half < block:
        levels.append((half, ((row // (2 * half)) == (col // (2 * half)))
                       & ((row // half) % 2 == 1) & ((col // half) % 2 == 0)))
        half *= 2
    return levels


def _rows_at(x, n, offset):
    return jnp.concatenate([jnp.broadcast_to(x[i + offset:i + offset + 1, :], (n, x.shape[1]))
                            for i in range(0, x.shape[0], n)], axis=0)


def _gla_prepass(q, k, v, sm, wgg, bgg, same_le, same, levels, block):
    r = q.shape[0]
    heads = range(GLA_HEADS)
    pairs = range(GLA_PAIRS)
    pre = _dot(sm, wgg) + bgg
    yield
    gk = (jnp.minimum(pre, 0.0) - jnp.log(1.0 + jnp.exp(-jnp.abs(pre)))) * (1.0 / GLA_GATE_TEMP)
    sums = _cumsum_rows(jnp.concatenate([jnp.where(same_le, 1.0, 0.0).astype(BF16),
                                         jnp.where(same, 1.0, 0.0).astype(BF16)], axis=0), gk)
    yield
    bcum = sums[:r]
    bend = sums[r:]
    bex = bcum - gk
    scale = GLA_DK ** -0.5
    q_e = q * jnp.exp(bcum) * scale
    lane = lax.broadcasted_iota(jnp.int32, (r, LANES), 1)
    in_head = [lane < GLA_DK, lane >= GLA_DK]
    pl_ = [slice(p * LANES, (p + 1) * LANES) for p in pairs]
    head_only = lambda x, h: jnp.where(in_head[h % 2], x[:, pl_[h // 2]], 0.0)
    qm = [head_only(q_e, h) for h in heads]
    sub = min(block, GLA_SUB_BLOCK)
    start = _rows_at(bex, sub, 0)
    groups = {None: (q * jnp.exp(bcum - start) * scale, [(levels[0][1], k * jnp.exp(start - bcum))])}
    for half, mask in levels[1:]:
        key = None if half == sub else half
        if key not in groups:
            groups[key] = (q * jnp.exp(bcum - _rows_at(bex, half, 0)) * scale, [])
        groups[key][1].append((mask, k * jnp.exp(_rows_at(bcum, half, half - 1) - bcum)))
    att = [jnp.zeros((r, r), F32)] * GLA_HEADS
    for q_l, parts in groups.values():
        for h in heads:
            keys = [k_l[:, pl_[h // 2]].astype(BF16) for _, k_l in parts]
            prod = _dot_nt(head_only(q_l, h), keys[0] if len(keys) == 1 else jnp.concatenate(keys, axis=0))
            for i, (mask, _) in enumerate(parts):
                att[h] = jnp.where(mask, prod[:, i * r:(i + 1) * r], att[h])
    yield
    o_intra = [_dot(att[h], v[h]) for h in heads]
    k_end = k * jnp.exp(bend - bcum)
    k_end_t = [k_end[:, pl_[p]].T for p in pairs]
    bend_t = [bend[:, pl_[p]].T for p in pairs]
    yield
    return qm, o_intra, k_end_t, bend_t


def _gla_prompt_kernel(qkv_ref, small_ref, s0_ref, wgg_ref, bgg_ref, o_ref, sfin_ref, s_scr, *, c):
    t = pl.program_id(0)
    nb, r, _ = qkv_ref.shape

    @pl.when(t == 0)
    def _():
        s_scr[...] = s0_ref[...]

    same_le, _, same = _block_masks(r, c)
    levels = _gla_att_levels(r, c)
    zeros = jnp.zeros((c, GLA_DV), F32)
    n_sub = r // c
    heads = range(GLA_HEADS)
    pairs = range(GLA_PAIRS)

    def one_batch(b):
        q = qkv_ref[b, :, 0:GLA_KEY]
        k = qkv_ref[b, :, GLA_KEY:2 * GLA_KEY]
        v = [qkv_ref[b, :, 2 * GLA_KEY + h * GLA_DV:2 * GLA_KEY + (h + 1) * GLA_DV] for h in heads]
        qm, o_intra, k_end_t, bend_t = yield from _gla_prepass(
            q, k, v, small_ref[b], wgg_ref[...], bgg_ref[...], same_le, same, levels, c)
        s = [s_scr[b, p * LANES:(p + 1) * LANES, :] for p in pairs]
        for i in range(n_sub):
            rows = slice(i * c, (i + 1) * c)
            ws = [_dot(jnp.concatenate([qm[2 * p][rows], qm[2 * p + 1][rows]], axis=0), s[p]) for p in pairs]
            padded = [jnp.concatenate([zeros] * i + [v[h][rows]] + [zeros] * (n_sub - 1 - i), axis=0)
                      for h in heads]
            upd = [_dot(k_end_t[h // 2][(h % 2) * GLA_DK:(h % 2 + 1) * GLA_DK, :], padded[h]) for h in heads]
            yield
            for h in heads:
                o_ref[b, rows, h * GLA_DV:(h + 1) * GLA_DV] = (
                    o_intra[h][rows] + ws[h // 2][(h % 2) * c:(h % 2 + 1) * c]).astype(o_ref.dtype)
            s = [s[p] * jnp.exp(bend_t[p][:, i * c:i * c + 1])
                 + jnp.concatenate([upd[2 * p], upd[2 * p + 1]], axis=0) for p in pairs]
        for p in pairs:
            s_scr[b, p * LANES:(p + 1) * LANES, :] = s[p]

    def per_step(i, carry):
        _interleave([one_batch(i * BATCH_INTERLEAVE + j) for j in range(BATCH_INTERLEAVE)])
        return carry

    lax.fori_loop(0, nb // BATCH_INTERLEAVE, per_step, None)

    @pl.when(t == pl.num_programs(0) - 1)
    def _():
        sfin_ref[...] = s_scr[...]


def _gla_sample_kernel(qkv_ref, small_ref, s0_ref, wgg_ref, bgg_ref, o_ref, sfin_ref):
    ns, c, _ = qkv_ref.shape
    r = ns * c
    heads = range(GLA_HEADS)
    same_le, _, same = _block_masks(r, c)
    levels = _gla_att_levels(r, c)
    q = qkv_ref[:, :, 0:GLA_KEY].reshape(r, GLA_KEY)
    k = qkv_ref[:, :, GLA_KEY:2 * GLA_KEY].reshape(r, GLA_KEY)
    v = [qkv_ref[:, :, 2 * GLA_KEY + h * GLA_DV:2 * GLA_KEY + (h + 1) * GLA_DV].reshape(r, GLA_DV)
         for h in heads]
    (qm, o_intra, k_end_t, bend_t), = _interleave([_gla_prepass(
        q, k, v, small_ref[...].reshape(r, LANES), wgg_ref[...], bgg_ref[...], same_le, same, levels, c)])
    seq_of_row = lax.broadcasted_iota(jnp.int32, (r, GLA_DV), 0) // c
    for p in range(GLA_PAIRS):
        ps = slice(p * LANES, (p + 1) * LANES)
        inter = [[], []]
        for s in range(ns):
            rows = slice(s * c, (s + 1) * c)
            ws = _dot(jnp.concatenate([qm[2 * p][rows], qm[2 * p + 1][rows]], axis=0), s0_ref[s, ps, :])
            inter[0].append(ws[:c])
            inter[1].append(ws[c:])
        for hh in range(2):
            h = 2 * p + hh
            o = o_intra[h] + jnp.concatenate(inter[hh], axis=0)
            o_ref[:, h * GLA_DV:(h + 1) * GLA_DV] = o.astype(o_ref.dtype)
        for s in range(ns):
            upd = [_dot(k_end_t[p][hh * GLA_DK:(hh + 1) * GLA_DK, :],
                        jnp.where(seq_of_row == s, v[2 * p + hh], 0.0)) for hh in range(2)]
            sfin_ref[s, ps, :] = (s0_ref[s, ps, :] * jnp.exp(bend_t[p][:, s * c:s * c + 1])
                                  + jnp.concatenate(upd, axis=0))


def _gla_prompt(qkv3, small3, s0, wgg, bgg, c):
    bsz, t_len, _ = qkv3.shape
    rows = GLA_HEADS * GLA_DK
    r = GROUP_ROWS
    full3 = lambda t: (0, 0, 0)
    const2 = lambda t: (0, 0)
    return pl.pallas_call(
        functools.partial(_gla_prompt_kernel, c=c),
        grid=(t_len // r,),
        in_specs=[
            pl.BlockSpec((bsz, r, GLA_QKV), lambda t: (0, t, 0)),
            pl.BlockSpec((bsz, r, LANES), lambda t: (0, t, 0)),
            pl.BlockSpec((bsz, rows, GLA_DV), full3),
            pl.BlockSpec((LANES, GLA_KEY), const2),
            pl.BlockSpec((1, GLA_KEY), const2),
        ],
        out_specs=[
            pl.BlockSpec((bsz, r, GLA_VAL), lambda t: (0, t, 0)),
            pl.BlockSpec((bsz, rows, GLA_DV), full3),
        ],
        out_shape=[
            jax.ShapeDtypeStruct((bsz, t_len, GLA_VAL), BF16),
            jax.ShapeDtypeStruct((bsz, rows, GLA_DV), F32),
        ],
        scratch_shapes=[pltpu.VMEM((bsz, rows, GLA_DV), F32)],
        compiler_params=pltpu.CompilerParams(dimension_semantics=("arbitrary",),
                                             vmem_limit_bytes=VMEM_MIB["gla_prompt"] * 2 ** 20),
        name="gla_prompt",
    )(qkv3, small3, s0, wgg, bgg)


def _gla_sample(qkv3, small3, s0, wgg, bgg):
    bsz, c, _ = qkv3.shape
    rows = GLA_HEADS * GLA_DK
    ns = GROUP_ROWS // c
    grp = lambda g: (g, 0, 0)
    const2 = lambda g: (0, 0)
    return pl.pallas_call(
        _gla_sample_kernel,
        grid=(bsz // ns,),
        in_specs=[
            pl.BlockSpec((ns, c, GLA_QKV), grp),
            pl.BlockSpec((ns, c, LANES), grp),
            pl.BlockSpec((ns, rows, GLA_DV), grp),
            pl.BlockSpec((LANES, GLA_KEY), const2),
            pl.BlockSpec((1, GLA_KEY), const2),
        ],
        out_specs=[
            pl.BlockSpec((ns * c, GLA_VAL), lambda g: (g, 0)),
            pl.BlockSpec((ns, rows, GLA_DV), grp),
        ],
        out_shape=[
            jax.ShapeDtypeStruct((bsz * c, GLA_VAL), BF16),
            jax.ShapeDtypeStruct((bsz, rows, GLA_DV), F32),
        ],
        compiler_params=pltpu.CompilerParams(dimension_semantics=("arbitrary",),
                                             vmem_limit_bytes=VMEM_MIB["gla_sample"] * 2 ** 20),
        name="gla_sample",
    )(qkv3, small3, s0, wgg, bgg)


def _block_unit_lower_inverse(a_list, block):
    r = a_list[0].shape[0]
    row = lax.broadcasted_iota(jnp.int32, (r, r), 0)
    col = lax.broadcasted_iota(jnp.int32, (r, r), 1)
    in_block = lambda n: (row // n) == (col // n)
    base = min(block, INVERSE_BASE_BLOCK)
    a_base = a_list if base == block else [jnp.where(in_block(base), a, 0.0) for a in a_list]
    xs = [jnp.where(row == col, 1.0, 0.0) - a for a in a_base]
    ps = [_dot(a, a) for a in a_base]
    n = 2
    while n < base:
        yield
        last = 2 * n >= base
        ms = [_dot(x if last else jnp.concatenate([p.astype(BF16), x.astype(BF16)], axis=0), p)
              for p, x in zip(ps, xs)]
        ps = [m[:r] for m in ms]
        xs = [x + m[-r:] for x, m in zip(xs, ms)]
        n *= 2
    n = base
    while n < block:
        yield
        between = in_block(2 * n) & jnp.logical_not(in_block(n))
        ts = [_dot(jnp.where(between, a, 0.0), x) for a, x in zip(a_list, xs)]
        yield
        xs = [x - _dot(x, t) for x, t in zip(xs, ts)]
        n *= 2
    return xs


def _gdn_decays(sm, alog_c, dtb_c, same_lt, same):
    r = sm.shape[0]
    heads = range(GDN_HEADS)
    ab = sm.T[SM_A:SM_A + 2 * GDN_HEADS, :]
    g8 = -jnp.exp(alog_c) * _softplus(ab + dtb_c)
    beta_c = _sigmoid(ab).T
    g1, g2 = _split2(g8)
    same_ge = same & jnp.logical_not(same_lt)
    sel = jnp.concatenate([jnp.where(same_ge, 1.0, 0.0).astype(BF16),
                           jnp.where(same, 1.0, 0.0).astype(BF16)], axis=1)
    sums = jnp.dot(jnp.concatenate([g1, g2], axis=0), sel, preferred_element_type=F32)
    sums = sums[:2 * GDN_HEADS] + sums[2 * GDN_HEADS:]
    dec_rows = sums[:, :r]
    dec_c = dec_rows.T
    dend_c = sums[:, r:].T
    return ([dec_c[:, h:h + 1] for h in heads], [dec_rows[h:h + 1, :] for h in heads],
            [dend_c[:, h:h + 1] for h in heads], [beta_c[:, GDN_HEADS + h:GDN_HEADS + h + 1] for h in heads])


def _gdn_head_cols(h):
    return [slice(part * GDN_KEY + h * GDN_DK, part * GDN_KEY + (h + 1) * GDN_DK) for part in range(3)]


def _gdn_prepass(qkv_of_head, sm, alog_c, dtb_c, same_le, same_lt, same, block, heads):
    dcol, drow, dend, beta = ([x[h] for h in heads] for x in _gdn_decays(sm, alog_c, dtb_c, same_lt, same))
    idx = range(len(heads))
    gamma = [jnp.where(same_le, jnp.exp(jnp.where(same_le, dcol[i] - drow[i], 0.0)), 0.0) for i in idx]
    q, k, v = (list(x) for x in zip(*[qkv_of_head(h) for h in heads]))
    r = q[0].shape[0]
    q = [x * lax.rsqrt(jnp.sum(x * x, axis=-1, keepdims=True) + NORM_EPS) * (GDN_DK ** -0.5) for x in q]
    k = [x * lax.rsqrt(jnp.sum(x * x, axis=-1, keepdims=True) + NORM_EPS) for x in k]
    yield
    kb = [k[i] * beta[i] for i in idx]
    vb = [v[i] * beta[i] for i in idx]
    kq = [_dot_nt(jnp.concatenate([kb[i], q[i]], axis=0), k[i]) for i in idx]
    yield
    a_mat = [jnp.where(same_lt, kq[i][:r] * gamma[i], 0.0) for i in idx]
    qk = [(kq[i][r:] * gamma[i]).astype(BF16) for i in idx]
    t_inv = yield from _block_unit_lower_inverse(a_mat, block)
    edec = [jnp.exp(dcol[i]) for i in idx]
    uw = [_dot(t_inv[i], jnp.concatenate([vb[i], kb[i] * edec[i]], axis=1)) for i in idx]
    q_e = [(q[i] * edec[i]).astype(BF16) for i in idx]
    k_end_t = [(k[i] * jnp.exp(dend[i] - dcol[i])).T.astype(BF16) for i in idx]
    yield
    u = [x[:, :GDN_DV] for x in uw]
    w = [x[:, GDN_DV:].astype(BF16) for x in uw]
    return u, w, q_e, qk, k_end_t, dend


def _conv_silu(win, cw_ref):
    conv = win(0) * cw_ref[CONV_WIDTH - 1:CONV_WIDTH, :]
    for j in range(1, CONV_WIDTH):
        conv = conv + win(j) * cw_ref[CONV_WIDTH - 1 - j:CONV_WIDTH - j, :]
    return _silu(conv)

def _gdn_prompt_kernel(qkv_ref, small_ref, cs_ref, s0_ref, cw_ref, alog_ref, dtb_ref,
                       o_ref, sfin_ref, cnew_ref, xbuf, prev, s_scr, *, c):
    t = pl.program_id(0)
    nb, r, _ = qkv_ref.shape
    keep = CONV_WIDTH - 1
    base = SUBLANES

    @pl.when(t == 0)
    def _():
        s_scr[...] = s0_ref[...]
        prev[:, base - keep:base, :] = cs_ref[...]

    same_le, same_lt, same = _block_masks(r, c)
    zeros = jnp.zeros((c, GDN_DV), F32)
    n_sub = r // c

    def one_batch(b, slot, heads, first):
        xb = xbuf.at[slot]
        if first:
            xb[base - keep:base, :] = prev[b, base - keep:base, :]
            xb[base:base + r, :] = qkv_ref[b]
            prev[b, base - keep:base, :] = xb[base + r - keep:base + r, :]

        def qkv_of_head(h):
            return [_conv_silu(lambda j: xb[base - j:base - j + r, cols], cw_ref.at[:, cols])
                    for cols in _gdn_head_cols(h)]

        u, w, q_e, qk, k_end_t, dend = yield from _gdn_prepass(
            qkv_of_head, small_ref[b], alog_ref[...], dtb_ref[...], same_le, same_lt, same, c, heads)
        idx = range(len(heads))
        hs = [slice(h * GDN_DK, (h + 1) * GDN_DK) for h in heads]
        s = [s_scr[b, hs[i], :] for i in idx]
        for ci in range(n_sub):
            rows = slice(ci * c, (ci + 1) * c)
            ws = [_dot(jnp.concatenate([w[i][rows], q_e[i][rows]], axis=0), s[i]) for i in idx]
            yield
            v_new = [u[i][rows] - ws[i][:c] for i in idx]
            padded = [jnp.concatenate([zeros] * ci + [v_new[i]] + [zeros] * (n_sub - 1 - ci), axis=0)
                      for i in idx]
            upd = [_dot(jnp.concatenate([qk[i][rows], k_end_t[i]], axis=0), padded[i]) for i in idx]
            yield
            for i in idx:
                o_ref[b, rows, hs[i]] = (ws[i][c:] + upd[i][:c]).astype(o_ref.dtype)
            s = [s[i] * jnp.exp(dend[i][ci * c:ci * c + 1, :]) + upd[i][c:] for i in idx]
        for i in idx:
            s_scr[b, hs[i], :] = s[i]

    def per_step(step, carry):
        for g in range(0, GDN_HEADS, GDN_HEAD_GROUP):
            heads = tuple(range(g, g + GDN_HEAD_GROUP))
            _interleave([one_batch(step * GDN_BATCH_INTERLEAVE + j, j, heads, g == 0)
                         for j in range(GDN_BATCH_INTERLEAVE)])
        return carry

    lax.fori_loop(0, nb // GDN_BATCH_INTERLEAVE, per_step, None)

    @pl.when(t == pl.num_programs(0) - 1)
    def _():
        sfin_ref[...] = s_scr[...]
        cnew_ref[...] = prev[:, base - keep:base, :]


def _gdn_sample_kernel(qkv_ref, small_ref, cs_ref, s0_ref, cw_ref, alog_ref, dtb_ref,
                       o_ref, sfin_ref, cnew_ref, xbuf):
    ns, c, _ = qkv_ref.shape
    r = ns * c
    keep = CONV_WIDTH - 1
    base = SUBLANES
    xbuf[:, base - keep:base, :] = cs_ref[...]
    xbuf[:, base:base + c, :] = qkv_ref[...]
    conv = _conv_silu(lambda j: xbuf[:, base - j:base - j + c, :], cw_ref).reshape(r, CONV_CH)
    cnew_ref[...] = xbuf[:, base + c - keep:base + c, :]

    same_le, same_lt, same = _block_masks(r, c)
    seq_of_row = lax.broadcasted_iota(jnp.int32, (r, GDN_DV), 0) // c
    (u, w, q_e, qk, k_end_t, dend), = _interleave([_gdn_prepass(
        lambda h: [conv[:, cols] for cols in _gdn_head_cols(h)], small_ref[...].reshape(r, LANES),
        alog_ref[...], dtb_ref[...], same_le, same_lt, same, c, tuple(range(GDN_HEADS)))])
    for h in range(GDN_HEADS):
        hs = slice(h * GDN_DK, (h + 1) * GDN_DK)
        v_parts, o_parts = [], []
        for s in range(ns):
            rows = slice(s * c, (s + 1) * c)
            ws = _dot(jnp.concatenate([w[h][rows], q_e[h][rows]], axis=0), s0_ref[s, hs, :])
            v_parts.append(u[h][rows] - ws[:c])
            o_parts.append(ws[c:])
        v_new = jnp.concatenate(v_parts, axis=0)
        o = jnp.concatenate(o_parts, axis=0) + _dot(qk[h], v_new)
        o_ref[:, h * GDN_DV:(h + 1) * GDN_DV] = o.astype(o_ref.dtype)
        for s in range(ns):
            upd = _dot(k_end_t[h], jnp.where(seq_of_row == s, v_new, 0.0))
            sfin_ref[s, hs, :] = s0_ref[s, hs, :] * jnp.exp(dend[h][s * c:s * c + 1, :]) + upd


def _gdn_prompt(qkv3, small3, conv_state, s0, conv_w, alog_v, dtb_v, c):
    bsz, t_len, _ = qkv3.shape
    rows = GDN_HEADS * GDN_DK
    keep = CONV_WIDTH - 1
    r = GROUP_ROWS
    full3 = lambda t: (0, 0, 0)
    const2 = lambda t: (0, 0)
    return pl.pallas_call(
        functools.partial(_gdn_prompt_kernel, c=c),
        grid=(t_len // r,),
        in_specs=[
            pl.BlockSpec((bsz, r, CONV_CH), lambda t: (0, t, 0)),
            pl.BlockSpec((bsz, r, LANES), lambda t: (0, t, 0)),
            pl.BlockSpec((bsz, keep, CONV_CH), full3),
            pl.BlockSpec((bsz, rows, GDN_DV), full3),
            pl.BlockSpec((CONV_WIDTH, CONV_CH), const2),
            pl.BlockSpec((2 * GDN_HEADS, 1), const2),
            pl.BlockSpec((2 * GDN_HEADS, 1), const2),
        ],
        out_specs=[
            pl.BlockSpec((bsz, r, GDN_VAL), lambda t: (0, t, 0)),
            pl.BlockSpec((bsz, rows, GDN_DV), full3),
            pl.BlockSpec((bsz, keep, CONV_CH), full3),
        ],
        out_shape=[
            jax.ShapeDtypeStruct((bsz, t_len, GDN_VAL), BF16),
            jax.ShapeDtypeStruct((bsz, rows, GDN_DV), F32),
            jax.ShapeDtypeStruct((bsz, keep, CONV_CH), F32),
        ],
        scratch_shapes=[pltpu.VMEM((GDN_BATCH_INTERLEAVE, SUBLANES + r, CONV_CH), F32),
                        pltpu.VMEM((bsz, SUBLANES, CONV_CH), F32),
                        pltpu.VMEM((bsz, rows, GDN_DV), F32)],
        compiler_params=pltpu.CompilerParams(dimension_semantics=("arbitrary",),
                                             vmem_limit_bytes=VMEM_MIB["gdn_prompt"] * 2 ** 20),
        name="gdn_prompt",
    )(qkv3, small3, conv_state, s0, conv_w, alog_v, dtb_v)


def _gdn_sample(qkv3, small3, conv_state, s0, conv_w, alog_v, dtb_v):
    bsz, c, _ = qkv3.shape
    rows = GDN_HEADS * GDN_DK
    keep = CONV_WIDTH - 1
    ns = GROUP_ROWS // c
    grp = lambda g: (g, 0, 0)
    const2 = lambda g: (0, 0)
    return pl.pallas_call(
        _gdn_sample_kernel,
        grid=(bsz // ns,),
        in_specs=[
            pl.BlockSpec((ns, c, CONV_CH), grp),
            pl.BlockSpec((ns, c, LANES), grp),
            pl.BlockSpec((ns, keep, CONV_CH), grp),
            pl.BlockSpec((ns, rows, GDN_DV), grp),
            pl.BlockSpec((CONV_WIDTH, CONV_CH), const2),
            pl.BlockSpec((2 * GDN_HEADS, 1), const2),
            pl.BlockSpec((2 * GDN_HEADS, 1), const2),
        ],
        out_specs=[
            pl.BlockSpec((ns * c, GDN_VAL), lambda g: (g, 0)),
            pl.BlockSpec((ns, rows, GDN_DV), grp),
            pl.BlockSpec((ns, keep, CONV_CH), grp),
        ],
        out_shape=[
            jax.ShapeDtypeStruct((bsz * c, GDN_VAL), BF16),
            jax.ShapeDtypeStruct((bsz, rows, GDN_DV), F32),
            jax.ShapeDtypeStruct((bsz, keep, CONV_CH), F32),
        ],
        scratch_shapes=[pltpu.VMEM((ns, 2 * SUBLANES, CONV_CH), F32)],
        compiler_params=pltpu.CompilerParams(dimension_semantics=("arbitrary",),
                                             vmem_limit_bytes=VMEM_MIB["gdn_sample"] * 2 ** 20),
        name="gdn_sample",
    )(qkv3, small3, conv_state, s0, conv_w, alog_v, dtb_v)


def _head_norm_gate(o, silu_z, w):
    parts = []
    for h in range(o.shape[-1] // LANES):
        oh = o[:, h * LANES:(h + 1) * LANES]
        parts.append(oh * lax.rsqrt(jnp.mean(oh * oh, axis=-1, keepdims=True) + NORM_EPS) * w)
    return jnp.concatenate(parts, axis=-1) * silu_z


def _out_kernel(oa_ref, ob_ref, ga_ref, gb_ref, za_ref, zb_ref, x_ref, p_ref, anw_ref, bnw_ref,
                wua_ref, wub_ref, wout_ref, wpg_ref, wp_ref, fnw_ref, y_ref):
    f32 = lambda ref: ref[...].astype(F32)
    ya = _dot(_head_norm_gate(f32(oa_ref), f32(za_ref), anw_ref[...]), wua_ref[...])
    yb = _dot(_head_norm_gate(f32(ob_ref), f32(zb_ref), bnw_ref[...]), wub_ref[...])
    merged = f32(ga_ref) * ya + f32(gb_ref) * yb
    h1 = x_ref[...] + _dot(merged, wout_ref[...])
    h2 = h1 + _sigmoid(_dot(h1, wpg_ref[...])) * _dot(p_ref[...], wp_ref[...])
    y_ref[...] = h2 * lax.rsqrt(jnp.mean(h2 * h2, axis=-1, keepdims=True) + NORM_EPS) * fnw_ref[...]


def _out_stage(o_a, o_b, gz, x2d, p2d, anw, bnw, wua, wub, wout, wpg, wp, fnw):
    n = x2d.shape[0]
    tm = min(ROW_TILE, n)
    const = lambda i: (0, 0)
    return pl.pallas_call(
        _out_kernel,
        grid=(n // tm,),
        in_specs=[
            pl.BlockSpec((tm, GLA_VAL), lambda i: (i, 0)),
            pl.BlockSpec((tm, GDN_VAL), lambda i: (i, 0)),
            pl.BlockSpec((tm, D_MODEL), lambda i: (i, GZ_GATE_A // D_MODEL)),
            pl.BlockSpec((tm, D_MODEL), lambda i: (i, GZ_GATE_B // D_MODEL)),
            pl.BlockSpec((tm, GLA_VAL), lambda i: (i, GZ_Z_A // GLA_VAL)),
            pl.BlockSpec((tm, GDN_VAL), lambda i: (i, GZ_Z_B // GDN_VAL)),
            pl.BlockSpec((tm, D_MODEL), lambda i: (i, 0)),
            pl.BlockSpec((tm, PLE_DIM), lambda i: (i, 0)),
            pl.BlockSpec((1, GLA_DV), const),
            pl.BlockSpec((1, GDN_DV), const),
            pl.BlockSpec((GLA_VAL, D_MODEL), const),
            pl.BlockSpec((GDN_VAL, D_MODEL), const),
            pl.BlockSpec((D_MODEL, D_MODEL), const),
            pl.BlockSpec((D_MODEL, D_MODEL), const),
            pl.BlockSpec((PLE_DIM, D_MODEL), const),
            pl.BlockSpec((1, D_MODEL), const),
        ],
        out_specs=pl.BlockSpec((tm, D_MODEL), lambda i: (i, 0)),
        out_shape=jax.ShapeDtypeStruct((n, D_MODEL), F32),
        compiler_params=pltpu.CompilerParams(dimension_semantics=("arbitrary",),
                                             vmem_limit_bytes=VMEM_MIB["out_stage"] * 2 ** 20),
        name="out_stage",
    )(o_a, o_b, gz, gz, gz, gz, x2d, p2d, anw, bnw, wua, wub, wout, wpg, wp, fnw)


def _in_offsets():
    offs = [0]
    for s in IN_SPLITS:
        offs.append(offs[-1] + s)
    return offs


def _regroup_kernel(wt_ref, o_ref):
    offs = _in_offsets()
    (q_a, _, _, g_a, z_a, qkv_b, a_b, _, z_b, gate_a, gate_b, end) = offs
    piece = lambda lo, hi: wt_ref[lo:hi, :].astype(BF16)
    o_ref[P_GLA_QKV:P_GLA_QKV + GLA_QKV, :] = piece(q_a, g_a)
    o_ref[P_QKV_B:P_QKV_B + CONV_CH, :] = piece(qkv_b, a_b)
    small = jnp.concatenate([wt_ref[g_a:z_a, :], wt_ref[a_b:z_b, :],
                             jnp.zeros((LANES - (z_a - g_a) - (z_b - a_b), wt_ref.shape[1]), F32)], axis=0)
    o_ref[P_SMALL:P_SMALL + LANES, :] = small.astype(BF16)
    o_ref[P_GATES + GZ_GATE_A:P_GATES + GZ_GATE_A + D_MODEL, :] = piece(gate_a, gate_b)
    o_ref[P_GATES + GZ_GATE_B:P_GATES + GZ_GATE_B + D_MODEL, :] = piece(gate_b, end)
    o_ref[P_GATES + GZ_Z_A:P_GATES + GZ_Z_A + GLA_VAL, :] = piece(z_a, qkv_b)
    o_ref[P_GATES + GZ_Z_B:P_GATES + GZ_Z_B + GDN_VAL, :] = piece(z_b, gate_a)


def _regroup_w_in(w_in_t):
    cols = 256
    return pl.pallas_call(
        _regroup_kernel,
        grid=(D_MODEL // cols,),
        in_specs=[pl.BlockSpec((None, w_in_t.shape[1], cols), lambda i: (0, 0, i))],
        out_specs=pl.BlockSpec((P_COLS, cols), lambda i: (0, i)),
        out_shape=jax.ShapeDtypeStruct((P_COLS, D_MODEL), BF16),
        compiler_params=pltpu.CompilerParams(dimension_semantics=("arbitrary",),
                                             vmem_limit_bytes=VMEM_MIB["regroup_w_in"] * 2 ** 20),
        name="regroup_w_in",
    )(w_in_t)


def _head_param_col(v):
    return jnp.zeros((2 * GDN_HEADS, 1), F32).at[:GDN_HEADS, 0].set(v.astype(F32))


def _trunk(x, p, s_gla, s_gdn, conv_state, wts):
    bsz, t_len, _ = x.shape
    c = min(PROMPT_CHUNK, t_len)
    n = bsz * t_len
    x2d = x.reshape(n, D_MODEL)
    p2d = p.reshape(n, PLE_DIM)
    gla_in, gdn_in, small, gz = _inproj(x2d, wts["norm_w"], wts["w_in_r"])
    gla_in3 = gla_in.reshape(bsz, t_len, GLA_QKV)
    gdn_in3 = gdn_in.reshape(bsz, t_len, CONV_CH)
    small3 = small.reshape(bsz, t_len, LANES)
    s_gla2 = s_gla.reshape(bsz, GLA_HEADS * GLA_DK, GLA_DV)
    s_gdn2 = s_gdn.reshape(bsz, GDN_HEADS * GDN_DK, GDN_DV)
    if t_len % GROUP_ROWS == 0:
        o_a, gla_fin = _gla_prompt(gla_in3, small3, s_gla2, wts["wgg"], wts["bgg"], c)
        o_b, gdn_fin, conv_new = _gdn_prompt(gdn_in3, small3, conv_state, s_gdn2, wts["conv_w"],
                                             wts["alog_v"], wts["dtb_v"], c)
    else:
        assert GROUP_ROWS % t_len == 0 and bsz % (GROUP_ROWS // t_len) == 0 and t_len >= CONV_WIDTH - 1
        o_a, gla_fin = _gla_sample(gla_in3, small3, s_gla2, wts["wgg"], wts["bgg"])
        o_b, gdn_fin, conv_new = _gdn_sample(gdn_in3, small3, conv_state, s_gdn2, wts["conv_w"],
                                             wts["alog_v"], wts["dtb_v"])
    y = _out_stage(o_a.reshape(n, GLA_VAL), o_b.reshape(n, GDN_VAL), gz, x2d, p2d, wts["anw"], wts["bnw"],
                   wts["wua"], wts["wub"], wts["wout"], wts["wpg"], wts["wp"], wts["fnw"])
    return (y.reshape(bsz, t_len, D_MODEL),
            gla_fin.reshape(1, bsz, GLA_HEADS, GLA_DK, GLA_DV),
            gdn_fin.reshape(1, bsz, GDN_HEADS, GDN_DK, GDN_DV),
            conv_new.reshape(1, bsz, CONV_WIDTH - 1, CONV_CH))


def kernel(x_prompt, x_sample, state_gla, state_gdn, state_conv, p_prompt, p_sample, norm_w, w_in,
           w_gla_gate, b_gla_gate, gla_norm_w, conv_w, gdn_a_log, gdn_dt_bias, gdn_norm_w,
           w_up_gla, w_up_gdn, w_out, w_ple_gate, w_ple, final_norm_w):
    wgg = jnp.zeros((LANES, GLA_KEY), F32).at[SM_G:SM_G + GLA_GATE_RANK].set(w_gla_gate[0]).astype(BF16)
    wts = {
        "norm_w": norm_w[0].reshape(1, D_MODEL),
        "w_in_r": _regroup_w_in(jnp.swapaxes(w_in, 1, 2)),
        "wgg": wgg,
        "bgg": b_gla_gate[0].reshape(1, GLA_KEY),
        "conv_w": conv_w[0],
        "alog_v": _head_param_col(gdn_a_log[0]),
        "dtb_v": _head_param_col(gdn_dt_bias[0]),
        "anw": gla_norm_w[0].reshape(1, GLA_DV),
        "bnw": gdn_norm_w[0].reshape(1, GDN_DV),
        "wua": w_up_gla[0].astype(BF16),
        "wub": w_up_gdn[0].astype(BF16),
        "wout": w_out[0].astype(BF16),
        "wpg": w_ple_gate[0].astype(BF16),
        "wp": w_ple[0].astype(BF16),
        "fnw": final_norm_w.reshape(1, D_MODEL),
    }
    bsz = x_prompt.shape[0]
    dt = x_prompt.dtype
    y_p, gla_p, gdn_p, conv_p = _trunk(
        x_prompt, p_prompt[0],
        jnp.zeros((bsz, GLA_HEADS, GLA_DK, GLA_DV), dt), jnp.zeros((bsz, GDN_HEADS, GDN_DK, GDN_DV), dt),
        jnp.zeros((bsz, CONV_WIDTH - 1, CONV_CH), dt), wts)
    y_s, gla_s, gdn_s, conv_s = _trunk(x_sample, p_sample[0], state_gla[0], state_gdn[0], state_conv[0], wts)
    return (y_p, y_s, gla_p, gdn_p, conv_p, gla_s, gdn_s, conv_s)
```

```python
import functools

import jax
import jax.numpy as jnp
from jax import lax
from jax.experimental import pallas as pl
from jax.experimental.pallas import tpu as pltpu

F32 = jnp.float32
BF16 = jnp.bfloat16

D_MODEL = 1024
PLE_DIM = 256
NORM_EPS = 1e-6
GLA_HEADS = 4
GLA_DK = 64
GLA_DV = 128
GLA_KEY = GLA_HEADS * GLA_DK
GLA_VAL = GLA_HEADS * GLA_DV
GLA_GATE_RANK = 16
GLA_GATE_TEMP = 16.0
GDN_HEADS = 4
GDN_DK = 128
GDN_DV = 128
GDN_KEY = GDN_HEADS * GDN_DK
GDN_VAL = GDN_HEADS * GDN_DV
CONV_WIDTH = 4
CONV_CH = 2 * GDN_KEY + GDN_VAL
IN_SPLITS = (GLA_KEY, GLA_KEY, GLA_VAL, GLA_GATE_RANK, GLA_VAL, CONV_CH, GDN_HEADS, GDN_HEADS,
             GDN_VAL, D_MODEL, D_MODEL)

LANES = 128
SUBLANES = 8

GLA_QKV = 2 * GLA_KEY + GLA_VAL
P_GLA_QKV = 0
P_QKV_B = P_GLA_QKV + GLA_QKV
P_SMALL = P_QKV_B + CONV_CH
P_GATES = P_SMALL + LANES
GZ_COLS = 2 * D_MODEL + GLA_VAL + GDN_VAL
P_COLS = P_GATES + GZ_COLS
GZ_GATE_A = 0
GZ_GATE_B = D_MODEL
GZ_Z_A = 2 * D_MODEL
GZ_Z_B = 2 * D_MODEL + GLA_VAL
SM_G = 0
SM_A = GLA_GATE_RANK
SM_B = GLA_GATE_RANK + GDN_HEADS

PROMPT_CHUNK = 64
GROUP_ROWS = 128
BATCH_INTERLEAVE = 8
GDN_BATCH_INTERLEAVE = 8
GDN_HEAD_GROUP = 4
INVERSE_BASE_BLOCK = 16
ROW_TILE = 512
VMEM_MIB = {"regroup_w_in": 40, "inproj": 40, "gla_prompt": 40, "gla_sample": 40, "gdn_prompt": 48,
            "gdn_sample": 40, "out_stage": 40}


def _dot(a, b):
    return jnp.dot(a.astype(BF16), b.astype(BF16), preferred_element_type=F32)


def _dot_nt(a, b):
    return lax.dot_general(a.astype(BF16), b.astype(BF16), (((1,), (1,)), ((), ())),
                           preferred_element_type=F32)


def _split2(x):
    h1 = x.astype(BF16)
    return h1, (x - h1.astype(F32)).astype(BF16)


def _cumsum_rows(tri, x):
    x1, x2 = _split2(x)
    d = functools.partial(jnp.dot, preferred_element_type=F32)
    return d(tri, x1) + d(tri, x2)


def _softplus(x):
    return jnp.maximum(x, 0.0) + jnp.log(1.0 + jnp.exp(-jnp.abs(x)))


def _sigmoid(x):
    return 1.0 / (1.0 + jnp.exp(-x))


def _silu(x):
    return x * _sigmoid(x)


def _interleave(emitters):
    results = [None] * len(emitters)
    live = list(range(len(emitters)))
    while live:
        for i in list(live):
            try:
                next(emitters[i])
            except StopIteration as stop:
                results[i] = stop.value
                live.remove(i)
    return results


def _block_masks(r, block):
    row = lax.broadcasted_iota(jnp.int32, (r, r), 0)
    col = lax.broadcasted_iota(jnp.int32, (r, r), 1)
    same = (row // block) == (col // block)
    return same & (row >= col), same & (row > col), same


INPROJ_COL_STEP = 512


def _inproj_kernel(x_ref, nw_ref, w_ref, gla_ref, gdn_ref, small_ref, gz_ref):
    x = x_ref[...]
    xn = x * lax.rsqrt(jnp.mean(x * x, axis=-1, keepdims=True) + NORM_EPS) * nw_ref[...]
    xb = xn.astype(BF16)

    def emit(o_ref, w0, width, act=None, o0=0):
        for c0 in range(0, width, INPROJ_COL_STEP):
            c1 = min(c0 + INPROJ_COL_STEP, width)
            res = _dot_nt(xb, w_ref[w0 + c0:w0 + c1, :])
            o_ref[:, o0 + c0:o0 + c1] = (res if act is None else act(res)).astype(o_ref.dtype)

    emit(gla_ref, P_GLA_QKV, GLA_QKV)
    emit(gdn_ref, P_QKV_B, CONV_CH)
    emit(small_ref, P_SMALL, LANES)
    emit(gz_ref, P_GATES + GZ_GATE_A, 2 * D_MODEL, act=_sigmoid, o0=GZ_GATE_A)
    emit(gz_ref, P_GATES + GZ_Z_A, GLA_VAL + GDN_VAL, act=_silu, o0=GZ_Z_A)


def _inproj(x2d, norm_w, w_in_r):
    n = x2d.shape[0]
    tm = min(ROW_TILE, n)
    rows = lambda i: (i, 0)
    return pl.pallas_call(
        _inproj_kernel,
        grid=(n // tm,),
        in_specs=[
            pl.BlockSpec((tm, D_MODEL), rows),
            pl.BlockSpec((1, D_MODEL), lambda i: (0, 0)),
            pl.BlockSpec((P_COLS, D_MODEL), lambda i: (0, 0), pipeline_mode=pl.Buffered(1)),
        ],
        out_specs=[
            pl.BlockSpec((tm, GLA_QKV), rows),
            pl.BlockSpec((tm, CONV_CH), rows),
            pl.BlockSpec((tm, LANES), rows),
            pl.BlockSpec((tm, GZ_COLS), rows),
        ],
        out_shape=[
            jax.ShapeDtypeStruct((n, GLA_QKV), F32),
            jax.ShapeDtypeStruct((n, CONV_CH), F32),
            jax.ShapeDtypeStruct((n, LANES), F32),
            jax.ShapeDtypeStruct((n, GZ_COLS), BF16),
        ],
        compiler_params=pltpu.CompilerParams(dimension_semantics=("arbitrary",),
                                             vmem_limit_bytes=VMEM_MIB["inproj"] * 2 ** 20),
        name="inproj",
    )(x2d, norm_w, w_in_r)


GLA_PAIRS = GLA_HEADS // 2
GLA_SUB_BLOCK = 16


def _gla_att_levels(r, block):
    row = lax.broadcasted_iota(jnp.int32, (r, r), 0)
    col = lax.broadcasted_iota(jnp.int32, (r, r), 1)
    sub = min(block, GLA_SUB_BLOCK)
    levels = [(None, ((row // sub) == (col // sub)) & (row >= col))]
    half = sub
    while half < block:
        levels.append((half, ((row // (2 * half)) == (col // (2 * half)))
                       & ((row // half) % 2 == 1) & ((col // half) % 2 == 0)))
        half *= 2
    return levels


def _rows_at(x, n, offset):
    return jnp.concatenate([jnp.broadcast_to(x[i + offset:i + offset + 1, :], (n, x.shape[1]))
                            for i in range(0, x.shape[0], n)], axis=0)


def _gla_prepass(q, k, v, sm, wgg, bgg, same_le, same, levels, block):
    r = q.shape[0]
    heads = range(GLA_HEADS)
    pairs = range(GLA_PAIRS)
    pre = _dot(sm, wgg) + bgg
    yield
    gk = (jnp.minimum(pre, 0.0) - jnp.log(1.0 + jnp.exp(-jnp.abs(pre)))) * (1.0 / GLA_GATE_TEMP)
    sums = _cumsum_rows(jnp.concatenate([jnp.where(same_le, 1.0, 0.0).astype(BF16),
                                         jnp.where(same, 1.0, 0.0).astype(BF16)], axis=0), gk)
    yield
    bcum = sums[:r]
    bend = sums[r:]
    bex = bcum - gk
    scale = GLA_DK ** -0.5
    q_e = q * jnp.exp(bcum) * scale
    lane = lax.broadcasted_iota(jnp.int32, (r, LANES), 1)
    in_head = [lane < GLA_DK, lane >= GLA_DK]
    pl_ = [slice(p * LANES, (p + 1) * LANES) for p in pairs]
    head_only = lambda x, h: jnp.where(in_head[h % 2], x[:, pl_[h // 2]], 0.0)
    qm = [head_only(q_e, h) for h in heads]
    sub = min(block, GLA_SUB_BLOCK)
    start = _rows_at(bex, sub, 0)
    groups = {None: (q * jnp.exp(bcum - start) * scale, [(levels[0][1], k * jnp.exp(start - bcum))])}
    for half, mask in levels[1:]:
        key = None if half == sub else half
        if key not in groups:
            groups[key] = (q * jnp.exp(bcum - _rows_at(bex, half, 0)) * scale, [])
        groups[key][1].append((mask, k * jnp.exp(_rows_at(bcum, half, half - 1) - bcum)))
    att = [jnp.zeros((r, r), F32)] * GLA_HEADS
    for q_l, parts in groups.values():
        for h in heads:
            keys = [k_l[:, pl_[h // 2]].astype(BF16) for _, k_l in parts]
            prod = _dot_nt(head_only(q_l, h), keys[0] if len(keys) == 1 else jnp.concatenate(keys, axis=0))
            for i, (mask, _) in enumerate(parts):
                att[h] = jnp.where(mask, prod[:, i * r:(i + 1) * r], att[h])
    yield
    o_intra = [_dot(att[h], v[h]) for h in heads]
    k_end = k * jnp.exp(bend - bcum)
    k_end_t = [k_end[:, pl_[p]].T for p in pairs]
    bend_t = [bend[:, pl_[p]].T for p in pairs]
    yield
    return qm, o_intra, k_end_t, bend_t


def _gla_prompt_kernel(qkv_ref, small_ref, s0_ref, wgg_ref, bgg_ref, o_ref, sfin_ref, s_scr, *, c):
    t = pl.program_id(0)
    nb, r, _ = qkv_ref.shape

    @pl.when(t == 0)
    def _():
        s_scr[...] = s0_ref[...]

    same_le, _, same = _block_masks(r, c)
    levels = _gla_att_levels(r, c)
    zeros = jnp.zeros((c, GLA_DV), F32)
    n_sub = r // c
    heads = range(GLA_HEADS)
    pairs = range(GLA_PAIRS)

    def one_batch(b):
        q = qkv_ref[b, :, 0:GLA_KEY]
        k = qkv_ref[b, :, GLA_KEY:2 * GLA_KEY]
        v = [qkv_ref[b, :, 2 * GLA_KEY + h * GLA_DV:2 * GLA_KEY + (h + 1) * GLA_DV] for h in heads]
        qm, o_intra, k_end_t, bend_t = yield from _gla_prepass(
            q, k, v, small_ref[b], wgg_ref[...], bgg_ref[...], same_le, same, levels, c)
        s = [s_scr[b, p * LANES:(p + 1) * LANES, :] for p in pairs]
        for i in range(n_sub):
            rows = slice(i * c, (i + 1) * c)
            ws = [_dot(jnp.concatenate([qm[2 * p][rows], qm[2 * p + 1][rows]], axis=0), s[p]) for p in pairs]
            padded = [jnp.concatenate([zeros] * i + [v[h][rows]] + [zeros] * (n_sub - 1 - i), axis=0)
                      for h in heads]
            upd = [_dot(k_end_t[h // 2][(h % 2) * GLA_DK:(h % 2 + 1) * GLA_DK, :], padded[h]) for h in heads]
            yield
            for h in heads:
                o_ref[b, rows, h * GLA_DV:(h + 1) * GLA_DV] = (
                    o_intra[h][rows] + ws[h // 2][(h % 2) * c:(h % 2 + 1) * c]).astype(o_ref.dtype)
            s = [s[p] * jnp.exp(bend_t[p][:, i * c:i * c + 1])
                 + jnp.concatenate([upd[2 * p], upd[2 * p + 1]], axis=0) for p in pairs]
        for p in pairs:
            s_scr[b, p * LANES:(p + 1) * LANES, :] = s[p]

    def per_step(i, carry):
        _interleave([one_batch(i * BATCH_INTERLEAVE + j) for j in range(BATCH_INTERLEAVE)])
        return carry

    lax.fori_loop(0, nb // BATCH_INTERLEAVE, per_step, None)

    @pl.when(t == pl.num_programs(0) - 1)
    def _():
        sfin_ref[...] = s_scr[...]


def _gla_sample_kernel(qkv_ref, small_ref, s0_ref, wgg_ref, bgg_ref, o_ref, sfin_ref):
    ns, c, _ = qkv_ref.shape
    r = ns * c
    heads = range(GLA_HEADS)
    same_le, _, same = _block_masks(r, c)
    levels = _gla_att_levels(r, c)
    q = qkv_ref[:, :, 0:GLA_KEY].reshape(r, GLA_KEY)
    k = qkv_ref[:, :, GLA_KEY:2 * GLA_KEY].reshape(r, GLA_KEY)
    v = [qkv_ref[:, :, 2 * GLA_KEY + h * GLA_DV:2 * GLA_KEY + (h + 1) * GLA_DV].reshape(r, GLA_DV)
         for h in heads]
    (qm, o_intra, k_end_t, bend_t), = _interleave([_gla_prepass(
        q, k, v, small_ref[...].reshape(r, LANES), wgg_ref[...], bgg_ref[...], same_le, same, levels, c)])
    seq_of_row = lax.broadcasted_iota(jnp.int32, (r, GLA_DV), 0) // c
    for p in range(GLA_PAIRS):
        ps = slice(p * LANES, (p + 1) * LANES)
        inter = [[], []]
        for s in range(ns):
            rows = slice(s * c, (s + 1) * c)
            ws = _dot(jnp.concatenate([qm[2 * p][rows], qm[2 * p + 1][rows]], axis=0), s0_ref[s, ps, :])
            inter[0].append(ws[:c])
            inter[1].append(ws[c:])
        for hh in range(2):
            h = 2 * p + hh
            o = o_intra[h] + jnp.concatenate(inter[hh], axis=0)
            o_ref[:, h * GLA_DV:(h + 1) * GLA_DV] = o.astype(o_ref.dtype)
        for s in range(ns):
            upd = [_dot(k_end_t[p][hh * GLA_DK:(hh + 1) * GLA_DK, :],
                        jnp.where(seq_of_row == s, v[2 * p + hh], 0.0)) for hh in range(2)]
            sfin_ref[s, ps, :] = (s0_ref[s, ps, :] * jnp.exp(bend_t[p][:, s * c:s * c + 1])
                                  + jnp.concatenate(upd, axis=0))


def _gla_prompt(qkv3, small3, s0, wgg, bgg, c):
    bsz, t_len, _ = qkv3.shape
    assert bsz % BATCH_INTERLEAVE == 0 and t_len % GROUP_ROWS == 0
    rows = GLA_HEADS * GLA_DK
    r = GROUP_ROWS
    full3 = lambda t: (0, 0, 0)
    const2 = lambda t: (0, 0)
    return pl.pallas_call(
        functools.partial(_gla_prompt_kernel, c=c),
        grid=(t_len // r,),
        in_specs=[
            pl.BlockSpec((bsz, r, GLA_QKV), lambda t: (0, t, 0)),
            pl.BlockSpec((bsz, r, LANES), lambda t: (0, t, 0)),
            pl.BlockSpec((bsz, rows, GLA_DV), full3),
            pl.BlockSpec((LANES, GLA_KEY), const2),
            pl.BlockSpec((1, GLA_KEY), const2),
        ],
        out_specs=[
            pl.BlockSpec((bsz, r, GLA_VAL), lambda t: (0, t, 0)),
            pl.BlockSpec((bsz, rows, GLA_DV), full3),
        ],
        out_shape=[
            jax.ShapeDtypeStruct((bsz, t_len, GLA_VAL), BF16),
            jax.ShapeDtypeStruct((bsz, rows, GLA_DV), F32),
        ],
        scratch_shapes=[pltpu.VMEM((bsz, rows, GLA_DV), F32)],
        compiler_params=pltpu.CompilerParams(dimension_semantics=("arbitrary",),
                                             vmem_limit_bytes=VMEM_MIB["gla_prompt"] * 2 ** 20),
        name="gla_prompt",
    )(qkv3, small3, s0, wgg, bgg)


def _gla_sample(qkv3, small3, s0, wgg, bgg):
    bsz, c, _ = qkv3.shape
    rows = GLA_HEADS * GLA_DK
    ns = GROUP_ROWS // c
    grp = lambda g: (g, 0, 0)
    const2 = lambda g: (0, 0)
    return pl.pallas_call(
        _gla_sample_kernel,
        grid=(bsz // ns,),
        in_specs=[
            pl.BlockSpec((ns, c, GLA_QKV), grp),
            pl.BlockSpec((ns, c, LANES), grp),
            pl.BlockSpec((ns, rows, GLA_DV), grp),
            pl.BlockSpec((LANES, GLA_KEY), const2),
            pl.BlockSpec((1, GLA_KEY), const2),
        ],
        out_specs=[
            pl.BlockSpec((ns * c, GLA_VAL), lambda g: (g, 0)),
            pl.BlockSpec((ns, rows, GLA_DV), grp),
        ],
        out_shape=[
            jax.ShapeDtypeStruct((bsz * c, GLA_VAL), BF16),
            jax.ShapeDtypeStruct((bsz, rows, GLA_DV), F32),
        ],
        compiler_params=pltpu.CompilerParams(dimension_semantics=("arbitrary",),
                                             vmem_limit_bytes=VMEM_MIB["gla_sample"] * 2 ** 20),
        name="gla_sample",
    )(qkv3, small3, s0, wgg, bgg)


def _block_unit_lower_inverse(a_list, block):
    r = a_list[0].shape[0]
    row = lax.broadcasted_iota(jnp.int32, (r, r), 0)
    col = lax.broadcasted_iota(jnp.int32, (r, r), 1)
    in_block = lambda n: (row // n) == (col // n)
    base = min(block, INVERSE_BASE_BLOCK)
    a_base = a_list if base == block else [jnp.where(in_block(base), a, 0.0) for a in a_list]
    xs = [jnp.where(row == col, 1.0, 0.0) - a for a in a_base]
    ps = [_dot(a, a) for a in a_base]
    n = 2
    while n < base:
        yield
        last = 2 * n >= base
        ms = [_dot(x if last else jnp.concatenate([p.astype(BF16), x.astype(BF16)], axis=0), p)
              for p, x in zip(ps, xs)]
        ps = [m[:r] for m in ms]
        xs = [x + m[-r:] for x, m in zip(xs, ms)]
        n *= 2
    n = base
    while n < block:
        yield
        between = in_block(2 * n) & jnp.logical_not(in_block(n))
        ts = [_dot(jnp.where(between, a, 0.0), x) for a, x in zip(a_list, xs)]
        yield
        xs = [x - _dot(x, t) for x, t in zip(xs, ts)]
        n *= 2
    return xs


def _gdn_decays(sm, alog_c, dtb_c, same_lt, same):
    r = sm.shape[0]
    heads = range(GDN_HEADS)
    ab = sm.T[SM_A:SM_A + 2 * GDN_HEADS, :]
    g8 = -jnp.exp(alog_c) * _softplus(ab + dtb_c)
    beta_c = _sigmoid(ab).T
    g1, g2 = _split2(g8)
    same_ge = same & jnp.logical_not(same_lt)
    sel = jnp.concatenate([jnp.where(same_ge, 1.0, 0.0).astype(BF16),
                           jnp.where(same, 1.0, 0.0).astype(BF16)], axis=1)
    sums = jnp.dot(jnp.concatenate([g1, g2], axis=0), sel, preferred_element_type=F32)
    sums = sums[:2 * GDN_HEADS] + sums[2 * GDN_HEADS:]
    dec_rows = sums[:, :r]
    dec_c = dec_rows.T
    dend_c = sums[:, r:].T
    return ([dec_c[:, h:h + 1] for h in heads], [dec_rows[h:h + 1, :] for h in heads],
            [dend_c[:, h:h + 1] for h in heads], [beta_c[:, GDN_HEADS + h:GDN_HEADS + h + 1] for h in heads])


def _gdn_head_cols(h):
    return [slice(part * GDN_KEY + h * GDN_DK, part * GDN_KEY + (h + 1) * GDN_DK) for part in range(3)]


def _gdn_prepass(qkv_of_head, sm, alog_c, dtb_c, same_le, same_lt, same, block, heads):
    dcol, drow, dend, beta = ([x[h] for h in heads] for x in _gdn_decays(sm, alog_c, dtb_c, same_lt, same))
    idx = range(len(heads))
    gamma = [jnp.where(same_le, jnp.exp(jnp.where(same_le, dcol[i] - drow[i], 0.0)), 0.0) for i in idx]
    q, k, v = (list(x) for x in zip(*[qkv_of_head(h) for h in heads]))
    r = q[0].shape[0]
    q = [x * lax.rsqrt(jnp.sum(x * x, axis=-1, keepdims=True) + NORM_EPS) * (GDN_DK ** -0.5) for x in q]
    k = [x * lax.rsqrt(jnp.sum(x * x, axis=-1, keepdims=True) + NORM_EPS) for x in k]
    yield
    kb = [k[i] * beta[i] for i in idx]
    vb = [v[i] * beta[i] for i in idx]
    kq = [_dot_nt(jnp.concatenate([kb[i], q[i]], axis=0), k[i]) for i in idx]
    yield
    a_mat = [jnp.where(same_lt, kq[i][:r] * gamma[i], 0.0) for i in idx]
    qk = [(kq[i][r:] * gamma[i]).astype(BF16) for i in idx]
    t_inv = yield from _block_unit_lower_inverse(a_mat, block)
    edec = [jnp.exp(dcol[i]) for i in idx]
    uw = [_dot(t_inv[i], jnp.concatenate([vb[i], kb[i] * edec[i]], axis=1)) for i in idx]
    q_e = [(q[i] * edec[i]).astype(BF16) for i in idx]
    k_end_t = [(k[i] * jnp.exp(dend[i] - dcol[i])).T.astype(BF16) for i in idx]
    yield
    u = [x[:, :GDN_DV] for x in uw]
    w = [x[:, GDN_DV:].astype(BF16) for x in uw]
    return u, w, q_e, qk, k_end_t, dend


def _conv_silu(win, cw_ref):
    conv = win(0) * cw_ref[CONV_WIDTH - 1:CONV_WIDTH, :]
    for j in range(1, CONV_WIDTH):
        conv = conv + win(j) * cw_ref[CONV_WIDTH - 1 - j:CONV_WIDTH - j, :]
    return _silu(conv)

def _gdn_prompt_kernel(qkv_ref, small_ref, cs_ref, s0_ref, cw_ref, alog_ref, dtb_ref,
                       o_ref, sfin_ref, cnew_ref, xbuf, prev, s_scr, *, c):
    t = pl.program_id(0)
    nb, r, _ = qkv_ref.shape
    keep = CONV_WIDTH - 1
    base = SUBLANES

    @pl.when(t == 0)
    def _():
        s_scr[...] = s0_ref[...]
        prev[:, base - keep:base, :] = cs_ref[...]

    same_le, same_lt, same = _block_masks(r, c)
    zeros = jnp.zeros((c, GDN_DV), F32)
    n_sub = r // c

    def one_batch(b, slot, heads, first):
        xb = xbuf.at[slot]
        if first:
            xb[base - keep:base, :] = prev[b, base - keep:base, :]
            xb[base:base + r, :] = qkv_ref[b]
            prev[b, base - keep:base, :] = xb[base + r - keep:base + r, :]

        def qkv_of_head(h):
            return [_conv_silu(lambda j: xb[base - j:base - j + r, cols], cw_ref.at[:, cols])
                    for cols in _gdn_head_cols(h)]

        u, w, q_e, qk, k_end_t, dend = yield from _gdn_prepass(
            qkv_of_head, small_ref[b], alog_ref[...], dtb_ref[...], same_le, same_lt, same, c, heads)
        idx = range(len(heads))
        hs = [slice(h * GDN_DK, (h + 1) * GDN_DK) for h in heads]
        s = [s_scr[b, hs[i], :] for i in idx]
        for ci in range(n_sub):
            rows = slice(ci * c, (ci + 1) * c)
            ws = [_dot(jnp.concatenate([w[i][rows], q_e[i][rows]], axis=0), s[i]) for i in idx]
            yield
            v_new = [u[i][rows] - ws[i][:c] for i in idx]
            padded = [jnp.concatenate([zeros] * ci + [v_new[i]] + [zeros] * (n_sub - 1 - ci), axis=0)
                      for i in idx]
            upd = [_dot(jnp.concatenate([qk[i][rows], k_end_t[i]], axis=0), padded[i]) for i in idx]
            yield
            for i in idx:
                o_ref[b, rows, hs[i]] = (ws[i][c:] + upd[i][:c]).astype(o_ref.dtype)
            s = [s[i] * jnp.exp(dend[i][ci * c:ci * c + 1, :]) + upd[i][c:] for i in idx]
        for i in idx:
            s_scr[b, hs[i], :] = s[i]

    def per_step(step, carry):
        for g in range(0, GDN_HEADS, GDN_HEAD_GROUP):
            heads = tuple(range(g, g + GDN_HEAD_GROUP))
            _interleave([one_batch(step * GDN_BATCH_INTERLEAVE + j, j, heads, g == 0)
                         for j in range(GDN_BATCH_INTERLEAVE)])
        return carry

    lax.fori_loop(0, nb // GDN_BATCH_INTERLEAVE, per_step, None)

    @pl.when(t == pl.num_programs(0) - 1)
    def _():
        sfin_ref[...] = s_scr[...]
        cnew_ref[...] = prev[:, base - keep:base, :]


def _gdn_sample_kernel(qkv_ref, small_ref, cs_ref, s0_ref, cw_ref, alog_ref, dtb_ref,
                       o_ref, sfin_ref, cnew_ref, xbuf):
    ns, c, _ = qkv_ref.shape
    r = ns * c
    keep = CONV_WIDTH - 1
    base = SUBLANES
    xbuf[:, base - keep:base, :] = cs_ref[...]
    xbuf[:, base:base + c, :] = qkv_ref[...]
    conv = _conv_silu(lambda j: xbuf[:, base - j:base - j + c, :], cw_ref).reshape(r, CONV_CH)
    cnew_ref[...] = xbuf[:, base + c - keep:base + c, :]

    same_le, same_lt, same = _block_masks(r, c)
    seq_of_row = lax.broadcasted_iota(jnp.int32, (r, GDN_DV), 0) // c
    (u, w, q_e, qk, k_end_t, dend), = _interleave([_gdn_prepass(
        lambda h: [conv[:, cols] for cols in _gdn_head_cols(h)], small_ref[...].reshape(r, LANES),
        alog_ref[...], dtb_ref[...], same_le, same_lt, same, c, tuple(range(GDN_HEADS)))])
    for h in range(GDN_HEADS):
        hs = slice(h * GDN_DK, (h + 1) * GDN_DK)
        v_parts, o_parts = [], []
        for s in range(ns):
            rows = slice(s * c, (s + 1) * c)
            ws = _dot(jnp.concatenate([w[h][rows], q_e[h][rows]], axis=0), s0_ref[s, hs, :])
            v_parts.append(u[h][rows] - ws[:c])
            o_parts.append(ws[c:])
        v_new = jnp.concatenate(v_parts, axis=0)
        o = jnp.concatenate(o_parts, axis=0) + _dot(qk[h], v_new)
        o_ref[:, h * GDN_DV:(h + 1) * GDN_DV] = o.astype(o_ref.dtype)
        for s in range(ns):
            upd = _dot(k_end_t[h], jnp.where(seq_of_row == s, v_new, 0.0))
            sfin_ref[s, hs, :] = s0_ref[s, hs, :] * jnp.exp(dend[h][s * c:s * c + 1, :]) + upd


def _gdn_prompt(qkv3, small3, conv_state, s0, conv_w, alog_v, dtb_v, c):
    bsz, t_len, _ = qkv3.shape
    assert bsz % GDN_BATCH_INTERLEAVE == 0 and t_len % GROUP_ROWS == 0 and GDN_HEADS % GDN_HEAD_GROUP == 0
    rows = GDN_HEADS * GDN_DK
    keep = CONV_WIDTH - 1
    r = GROUP_ROWS
    full3 = lambda t: (0, 0, 0)
    const2 = lambda t: (0, 0)
    return pl.pallas_call(
        functools.partial(_gdn_prompt_kernel, c=c),
        grid=(t_len // r,),
        in_specs=[
            pl.BlockSpec((bsz, r, CONV_CH), lambda t: (0, t, 0)),
            pl.BlockSpec((bsz, r, LANES), lambda t: (0, t, 0)),
            pl.BlockSpec((bsz, keep, CONV_CH), full3),
            pl.BlockSpec((bsz, rows, GDN_DV), full3),
            pl.BlockSpec((CONV_WIDTH, CONV_CH), const2),
            pl.BlockSpec((2 * GDN_HEADS, 1), const2),
            pl.BlockSpec((2 * GDN_HEADS, 1), const2),
        ],
        out_specs=[
            pl.BlockSpec((bsz, r, GDN_VAL), lambda t: (0, t, 0)),
            pl.BlockSpec((bsz, rows, GDN_DV), full3),
            pl.BlockSpec((bsz, keep, CONV_CH), full3),
        ],
        out_shape=[
            jax.ShapeDtypeStruct((bsz, t_len, GDN_VAL), BF16),
            jax.ShapeDtypeStruct((bsz, rows, GDN_DV), F32),
            jax.ShapeDtypeStruct((bsz, keep, CONV_CH), F32),
        ],
        scratch_shapes=[pltpu.VMEM((GDN_BATCH_INTERLEAVE, SUBLANES + r, CONV_CH), F32),
                        pltpu.VMEM((bsz, SUBLANES, CONV_CH), F32),
                        pltpu.VMEM((bsz, rows, GDN_DV), F32)],
        compiler_params=pltpu.CompilerParams(dimension_semantics=("arbitrary",),
                                             vmem_limit_bytes=VMEM_MIB["gdn_prompt"] * 2 ** 20),
        name="gdn_prompt",
    )(qkv3, small3, conv_state, s0, conv_w, alog_v, dtb_v)


def _gdn_sample(qkv3, small3, conv_state, s0, conv_w, alog_v, dtb_v):
    bsz, c, _ = qkv3.shape
    rows = GDN_HEADS * GDN_DK
    keep = CONV_WIDTH - 1
    ns = GROUP_ROWS // c
    grp = lambda g: (g, 0, 0)
    const2 = lambda g: (0, 0)
    return pl.pallas_call(
        _gdn_sample_kernel,
        grid=(bsz // ns,),
        in_specs=[
            pl.BlockSpec((ns, c, CONV_CH), grp),
            pl.BlockSpec((ns, c, LANES), grp),
            pl.BlockSpec((ns, keep, CONV_CH), grp),
            pl.BlockSpec((ns, rows, GDN_DV), grp),
            pl.BlockSpec((CONV_WIDTH, CONV_CH), const2),
            pl.BlockSpec((2 * GDN_HEADS, 1), const2),
            pl.BlockSpec((2 * GDN_HEADS, 1), const2),
        ],
        out_specs=[
            pl.BlockSpec((ns * c, GDN_VAL), lambda g: (g, 0)),
            pl.BlockSpec((ns, rows, GDN_DV), grp),
            pl.BlockSpec((ns, keep, CONV_CH), grp),
        ],
        out_shape=[
            jax.ShapeDtypeStruct((bsz * c, GDN_VAL), BF16),
            jax.ShapeDtypeStruct((bsz, rows, GDN_DV), F32),
            jax.ShapeDtypeStruct((bsz, keep, CONV_CH), F32),
        ],
        scratch_shapes=[pltpu.VMEM((ns, 2 * SUBLANES, CONV_CH), F32)],
        compiler_params=pltpu.CompilerParams(dimension_semantics=("arbitrary",),
                                             vmem_limit_bytes=VMEM_MIB["gdn_sample"] * 2 ** 20),
        name="gdn_sample",
    )(qkv3, small3, conv_state, s0, conv_w, alog_v, dtb_v)


def _head_norm_gate(o, silu_z, w):
    parts = []
    for h in range(o.shape[-1] // LANES):
        oh = o[:, h * LANES:(h + 1) * LANES]
        parts.append(oh * lax.rsqrt(jnp.mean(oh * oh, axis=-1, keepdims=True) + NORM_EPS) * w)
    return jnp.concatenate(parts, axis=-1) * silu_z


def _out_kernel(oa_ref, ob_ref, ga_ref, gb_ref, za_ref, zb_ref, x_ref, p_ref, anw_ref, bnw_ref,
                wua_ref, wub_ref, wout_ref, wpg_ref, wp_ref, fnw_ref, y_ref):
    f32 = lambda ref: ref[...].astype(F32)
    ya = _dot(_head_norm_gate(f32(oa_ref), f32(za_ref), anw_ref[...]), wua_ref[...])
    yb = _dot(_head_norm_gate(f32(ob_ref), f32(zb_ref), bnw_ref[...]), wub_ref[...])
    merged = f32(ga_ref) * ya + f32(gb_ref) * yb
    h1 = x_ref[...] + _dot(merged, wout_ref[...])
    h2 = h1 + _sigmoid(_dot(h1, wpg_ref[...])) * _dot(p_ref[...], wp_ref[...])
    y_ref[...] = h2 * lax.rsqrt(jnp.mean(h2 * h2, axis=-1, keepdims=True) + NORM_EPS) * fnw_ref[...]


def _out_stage(o_a, o_b, gz, x2d, p2d, anw, bnw, wua, wub, wout, wpg, wp, fnw):
    n = x2d.shape[0]
    tm = min(ROW_TILE, n)
    const = lambda i: (0, 0)
    return pl.pallas_call(
        _out_kernel,
        grid=(n // tm,),
        in_specs=[
            pl.BlockSpec((tm, GLA_VAL), lambda i: (i, 0)),
            pl.BlockSpec((tm, GDN_VAL), lambda i: (i, 0)),
            pl.BlockSpec((tm, D_MODEL), lambda i: (i, GZ_GATE_A // D_MODEL)),
            pl.BlockSpec((tm, D_MODEL), lambda i: (i, GZ_GATE_B // D_MODEL)),
            pl.BlockSpec((tm, GLA_VAL), lambda i: (i, GZ_Z_A // GLA_VAL)),
            pl.BlockSpec((tm, GDN_VAL), lambda i: (i, GZ_Z_B // GDN_VAL)),
            pl.BlockSpec((tm, D_MODEL), lambda i: (i, 0)),
            pl.BlockSpec((tm, PLE_DIM), lambda i: (i, 0)),
            pl.BlockSpec((1, GLA_DV), const),
            pl.BlockSpec((1, GDN_DV), const),
            pl.BlockSpec((GLA_VAL, D_MODEL), const),
            pl.BlockSpec((GDN_VAL, D_MODEL), const),
            pl.BlockSpec((D_MODEL, D_MODEL), const),
            pl.BlockSpec((D_MODEL, D_MODEL), const),
            pl.BlockSpec((PLE_DIM, D_MODEL), const),
            pl.BlockSpec((1, D_MODEL), const),
        ],
        out_specs=pl.BlockSpec((tm, D_MODEL), lambda i: (i, 0)),
        out_shape=jax.ShapeDtypeStruct((n, D_MODEL), F32),
        compiler_params=pltpu.CompilerParams(dimension_semantics=("arbitrary",),
                                             vmem_limit_bytes=VMEM_MIB["out_stage"] * 2 ** 20),
        name="out_stage",
    )(o_a, o_b, gz, gz, gz, gz, x2d, p2d, anw, bnw, wua, wub, wout, wpg, wp, fnw)


def _in_offsets():
    offs = [0]
    for s in IN_SPLITS:
        offs.append(offs[-1] + s)
    return offs


def _regroup_kernel(wt_ref, o_ref):
    offs = _in_offsets()
    (q_a, _, _, g_a, z_a, qkv_b, a_b, _, z_b, gate_a, gate_b, end) = offs
    piece = lambda lo, hi: wt_ref[lo:hi, :].astype(BF16)
    o_ref[P_GLA_QKV:P_GLA_QKV + GLA_QKV, :] = piece(q_a, g_a)
    o_ref[P_QKV_B:P_QKV_B + CONV_CH, :] = piece(qkv_b, a_b)
    small = jnp.concatenate([wt_ref[g_a:z_a, :], wt_ref[a_b:z_b, :],
                             jnp.zeros((LANES - (z_a - g_a) - (z_b - a_b), wt_ref.shape[1]), F32)], axis=0)
    o_ref[P_SMALL:P_SMALL + LANES, :] = small.astype(BF16)
    o_ref[P_GATES + GZ_GATE_A:P_GATES + GZ_GATE_A + D_MODEL, :] = piece(gate_a, gate_b)
    o_ref[P_GATES + GZ_GATE_B:P_GATES + GZ_GATE_B + D_MODEL, :] = piece(gate_b, end)
    o_ref[P_GATES + GZ_Z_A:P_GATES + GZ_Z_A + GLA_VAL, :] = piece(z_a, qkv_b)
    o_ref[P_GATES + GZ_Z_B:P_GATES + GZ_Z_B + GDN_VAL, :] = piece(z_b, gate_a)


def _regroup_w_in(w_in_t):
    cols = 256
    return pl.pallas_call(
        _regroup_kernel,
        grid=(D_MODEL // cols,),
        in_specs=[pl.BlockSpec((None, w_in_t.shape[1], cols), lambda i: (0, 0, i))],
        out_specs=pl.BlockSpec((P_COLS, cols), lambda i: (0, i)),
        out_shape=jax.ShapeDtypeStruct((P_COLS, D_MODEL), BF16),
        compiler_params=pltpu.CompilerParams(dimension_semantics=("arbitrary",),
                                             vmem_limit_bytes=VMEM_MIB["regroup_w_in"] * 2 ** 20),
        name="regroup_w_in",
    )(w_in_t)


def _head_param_col(v):
    return jnp.zeros((2 * GDN_HEADS, 1), F32).at[:GDN_HEADS, 0].set(v.astype(F32))


def _trunk(x, p, s_gla, s_gdn, conv_state, wts):
    bsz, t_len, _ = x.shape
    c = min(PROMPT_CHUNK, t_len)
    n = bsz * t_len
    x2d = x.reshape(n, D_MODEL)
    p2d = p.reshape(n, PLE_DIM)
    gla_in, gdn_in, small, gz = _inproj(x2d, wts["norm_w"], wts["w_in_r"])
    gla_in3 = gla_in.reshape(bsz, t_len, GLA_QKV)
    gdn_in3 = gdn_in.reshape(bsz, t_len, CONV_CH)
    small3 = small.reshape(bsz, t_len, LANES)
    s_gla2 = s_gla.reshape(bsz, GLA_HEADS * GLA_DK, GLA_DV)
    s_gdn2 = s_gdn.reshape(bsz, GDN_HEADS * GDN_DK, GDN_DV)
    if t_len % GROUP_ROWS == 0:
        o_a, gla_fin = _gla_prompt(gla_in3, small3, s_gla2, wts["wgg"], wts["bgg"], c)
        o_b, gdn_fin, conv_new = _gdn_prompt(gdn_in3, small3, conv_state, s_gdn2, wts["conv_w"],
                                             wts["alog_v"], wts["dtb_v"], c)
    else:
        assert GROUP_ROWS % t_len == 0 and bsz % (GROUP_ROWS // t_len) == 0 and t_len >= CONV_WIDTH - 1
        o_a, gla_fin = _gla_sample(gla_in3, small3, s_gla2, wts["wgg"], wts["bgg"])
        o_b, gdn_fin, conv_new = _gdn_sample(gdn_in3, small3, conv_state, s_gdn2, wts["conv_w"],
                                             wts["alog_v"], wts["dtb_v"])
    y = _out_stage(o_a.reshape(n, GLA_VAL), o_b.reshape(n, GDN_VAL), gz, x2d, p2d, wts["anw"], wts["bnw"],
                   wts["wua"], wts["wub"], wts["wout"], wts["wpg"], wts["wp"], wts["fnw"])
    return (y.reshape(bsz, t_len, D_MODEL),
            gla_fin.reshape(1, bsz, GLA_HEADS, GLA_DK, GLA_DV),
            gdn_fin.reshape(1, bsz, GDN_HEADS, GDN_DK, GDN_DV),
            conv_new.reshape(1, bsz, CONV_WIDTH - 1, CONV_CH))


def kernel(x_prompt, x_sample, state_gla, state_gdn, state_conv, p_prompt, p_sample, norm_w, w_in,
           w_gla_gate, b_gla_gate, gla_norm_w, conv_w, gdn_a_log, gdn_dt_bias, gdn_norm_w,
           w_up_gla, w_up_gdn, w_out, w_ple_gate, w_ple, final_norm_w):
    wgg = jnp.zeros((LANES, GLA_KEY), F32).at[SM_G:SM_G + GLA_GATE_RANK].set(w_gla_gate[0]).astype(BF16)
    wts = {
        "norm_w": norm_w[0].reshape(1, D_MODEL),
        "w_in_r": _regroup_w_in(jnp.swapaxes(w_in, 1, 2)),
        "wgg": wgg,
        "bgg": b_gla_gate[0].reshape(1, GLA_KEY),
        "conv_w": conv_w[0],
        "alog_v": _head_param_col(gdn_a_log[0]),
        "dtb_v": _head_param_col(gdn_dt_bias[0]),
        "anw": gla_norm_w[0].reshape(1, GLA_DV),
        "bnw": gdn_norm_w[0].reshape(1, GDN_DV),
        "wua": w_up_gla[0].astype(BF16),
        "wub": w_up_gdn[0].astype(BF16),
        "wout": w_out[0].astype(BF16),
        "wpg": w_ple_gate[0].astype(BF16),
        "wp": w_ple[0].astype(BF16),
        "fnw": final_norm_w.reshape(1, D_MODEL),
    }
    bsz = x_prompt.shape[0]
    dt = x_prompt.dtype
    y_p, gla_p, gdn_p, conv_p = _trunk(
        x_prompt, p_prompt[0],
        jnp.zeros((bsz, GLA_HEADS, GLA_DK, GLA_DV), dt), jnp.zeros((bsz, GDN_HEADS, GDN_DK, GDN_DV), dt),
        jnp.zeros((bsz, CONV_WIDTH - 1, CONV_CH), dt), wts)
    y_s, gla_s, gdn_s, conv_s = _trunk(x_sample, p_sample[0], state_gla[0], state_gdn[0], state_conv[0], wts)
    return (y_p, y_s, gla_p, gdn_p, conv_p, gla_s, gdn_s, conv_s)
```

```python
import functools

import jax
import jax.numpy as jnp
from jax import lax
from jax.experimental import pallas as pl
from jax.experimental.pallas import tpu as pltpu

F32 = jnp.float32
BF16 = jnp.bfloat16

D_MODEL = 1024
PLE_DIM = 256
NORM_EPS = 1e-6
GLA_HEADS = 4
GLA_DK = 64
GLA_DV = 128
GLA_KEY = GLA_HEADS * GLA_DK
GLA_VAL = GLA_HEADS * GLA_DV
GLA_GATE_RANK = 16
GLA_GATE_TEMP = 16.0
GDN_HEADS = 4
GDN_DK = 128
GDN_DV = 128
GDN_KEY = GDN_HEADS * GDN_DK
GDN_VAL = GDN_HEADS * GDN_DV
CONV_WIDTH = 4
CONV_CH = 2 * GDN_KEY + GDN_VAL
IN_SPLITS = (GLA_KEY, GLA_KEY, GLA_VAL, GLA_GATE_RANK, GLA_VAL, CONV_CH, GDN_HEADS, GDN_HEADS,
             GDN_VAL, D_MODEL, D_MODEL)

LANES = 128
SUBLANES = 8

GLA_QKV = 2 * GLA_KEY + GLA_VAL
P_GLA_QKV = 0
P_QKV_B = P_GLA_QKV + GLA_QKV
P_SMALL = P_QKV_B + CONV_CH
P_GATES = P_SMALL + LANES
GZ_COLS = 2 * D_MODEL + GLA_VAL + GDN_VAL
P_COLS = P_GATES + GZ_COLS
GZ_GATE_A = 0
GZ_GATE_B = D_MODEL
GZ_Z_A = 2 * D_MODEL
GZ_Z_B = 2 * D_MODEL + GLA_VAL
SM_G = 0
SM_A = GLA_GATE_RANK
SM_B = GLA_GATE_RANK + GDN_HEADS

PROMPT_CHUNK = 64
GROUP_ROWS = 128
BATCH_INTERLEAVE = 8
GDN_BATCH_INTERLEAVE = 8
GDN_HEAD_GROUP = 4
INVERSE_BASE_BLOCK = 16
ROW_TILE = 512
VMEM_MIB = {"regroup_w_in": 40, "inproj": 40, "gla_prompt": 40, "gla_sample": 40, "gdn_prompt": 48,
            "gdn_sample": 40, "out_stage": 40}


def _dot(a, b):
    return jnp.dot(a.astype(BF16), b.astype(BF16), preferred_element_type=F32)


def _dot_nt(a, b):
    return lax.dot_general(a.astype(BF16), b.astype(BF16), (((1,), (1,)), ((), ())),
                           preferred_element_type=F32)


def _split2(x):
    h1 = x.astype(BF16)
    return h1, (x - h1.astype(F32)).astype(BF16)


def _cumsum_rows(tri, x):
    x1, x2 = _split2(x)
    d = functools.partial(jnp.dot, preferred_element_type=F32)
    return d(tri, x1) + d(tri, x2)


def _softplus(x):
    return jnp.maximum(x, 0.0) + jnp.log(1.0 + jnp.exp(-jnp.abs(x)))


def _sigmoid(x):
    return 1.0 / (1.0 + jnp.exp(-x))


def _silu(x):
    return x * _sigmoid(x)


def _interleave(emitters):
    results = [None] * len(emitters)
    live = list(range(len(emitters)))
    while live:
        for i in list(live):
            try:
                next(emitters[i])
            except StopIteration as stop:
                results[i] = stop.value
                live.remove(i)
    return results


def _block_masks(r, block):
    row = lax.broadcasted_iota(jnp.int32, (r, r), 0)
    col = lax.broadcasted_iota(jnp.int32, (r, r), 1)
    same = (row // block) == (col // block)
    return same & (row >= col), same & (row > col), same


INPROJ_COL_STEP = 512


def _inproj_kernel(x_ref, nw_ref, w_ref, gla_ref, gdn_ref, small_ref, gz_ref):
    x = x_ref[...]
    xn = x * lax.rsqrt(jnp.mean(x * x, axis=-1, keepdims=True) + NORM_EPS) * nw_ref[...]
    xb = xn.astype(BF16)

    def emit(o_ref, w0, width, act=None, o0=0):
        for c0 in range(0, width, INPROJ_COL_STEP):
            c1 = min(c0 + INPROJ_COL_STEP, width)
            res = _dot_nt(xb, w_ref[w0 + c0:w0 + c1, :])
            o_ref[:, o0 + c0:o0 + c1] = (res if act is None else act(res)).astype(o_ref.dtype)

    emit(gla_ref, P_GLA_QKV, GLA_QKV)
    emit(gdn_ref, P_QKV_B, CONV_CH)
    emit(small_ref, P_SMALL, LANES)
    emit(gz_ref, P_GATES + GZ_GATE_A, 2 * D_MODEL, act=_sigmoid, o0=GZ_GATE_A)
    emit(gz_ref, P_GATES + GZ_Z_A, GLA_VAL + GDN_VAL, act=_silu, o0=GZ_Z_A)


def _inproj(x2d, norm_w, w_in_r):
    n = x2d.shape[0]
    tm = min(ROW_TILE, n)
    rows = lambda i: (i, 0)
    return pl.pallas_call(
        _inproj_kernel,
        grid=(n // tm,),
        in_specs=[
            pl.BlockSpec((tm, D_MODEL), rows),
            pl.BlockSpec((1, D_MODEL), lambda i: (0, 0)),
            pl.BlockSpec((P_COLS, D_MODEL), lambda i: (0, 0), pipeline_mode=pl.Buffered(1)),
        ],
        out_specs=[
            pl.BlockSpec((tm, GLA_QKV), rows),
            pl.BlockSpec((tm, CONV_CH), rows),
            pl.BlockSpec((tm, LANES), rows),
            pl.BlockSpec((tm, GZ_COLS), rows),
        ],
        out_shape=[
            jax.ShapeDtypeStruct((n, GLA_QKV), F32),
            jax.ShapeDtypeStruct((n, CONV_CH), F32),
            jax.ShapeDtypeStruct((n, LANES), F32),
            jax.ShapeDtypeStruct((n, GZ_COLS), BF16),
        ],
        compiler_params=pltpu.CompilerParams(dimension_semantics=("arbitrary",),
                                             vmem_limit_bytes=VMEM_MIB["inproj"] * 2 ** 20),
        name="inproj",
    )(x2d, norm_w, w_in_r)


GLA_PAIRS = GLA_HEADS // 2
GLA_SUB_BLOCK = 16


def _gla_att_levels(r, block):
    row = lax.broadcasted_iota(jnp.int32, (r, r), 0)
    col = lax.broadcasted_iota(jnp.int32, (r, r), 1)
    sub = min(block, GLA_SUB_BLOCK)
    levels = [(None, ((row // sub) == (col // sub)) & (row >= col))]
    half = sub
    while half < block:
        levels.append((half, ((row // (2 * half)) == (col // (2 * half)))
                       & ((row // half) % 2 == 1) & ((col // half) % 2 == 0)))
        half *= 2
    return levels


def _rows_at(x, n, offset):
    return jnp.concatenate([jnp.broadcast_to(x[i + offset:i + offset + 1, :], (n, x.shape[1]))
                            for i in range(0, x.shape[0], n)], axis=0)


def _gla_prepass(q, k, v, sm, wgg, bgg, same_le, same, levels, block):
    r = q.shape[0]
    heads = range(GLA_HEADS)
    pairs = range(GLA_PAIRS)
    pre = _dot(sm, wgg) + bgg
    yield
    gk = (jnp.minimum(pre, 0.0) - jnp.log(1.0 + jnp.exp(-jnp.abs(pre)))) * (1.0 / GLA_GATE_TEMP)
    sums = _cumsum_rows(jnp.concatenate([jnp.where(same_le, 1.0, 0.0).astype(BF16),
                                         jnp.where(same, 1.0, 0.0).astype(BF16)], axis=0), gk)
    yield
    bcum = sums[:r]
    bend = sums[r:]
    bex = bcum - gk
    scale = GLA_DK ** -0.5
    q_e = q * jnp.exp(bcum) * scale
    lane = lax.broadcasted_iota(jnp.int32, (r, LANES), 1)
    in_head = [lane < GLA_DK, lane >= GLA_DK]
    pl_ = [slice(p * LANES, (p + 1) * LANES) for p in pairs]
    head_only = lambda x, h: jnp.where(in_head[h % 2], x[:, pl_[h // 2]], 0.0)
    qm = [head_only(q_e, h) for h in heads]
    sub = min(block, GLA_SUB_BLOCK)
    start = _rows_at(bex, sub, 0)
    groups = {None: (q * jnp.exp(bcum - start) * scale, [(levels[0][1], k * jnp.exp(start - bcum))])}
    for half, mask in levels[1:]:
        key = None if half == sub else half
        if key not in groups:
            groups[key] = (q * jnp.exp(bcum - _rows_at(bex, half, 0)) * scale, [])
        groups[key][1].append((mask, k * jnp.exp(_rows_at(bcum, half, half - 1) - bcum)))
    att = [jnp.zeros((r, r), F32)] * GLA_HEADS
    for q_l, parts in groups.values():
        for h in heads:
            keys = [k_l[:, pl_[h // 2]].astype(BF16) for _, k_l in parts]
            prod = _dot_nt(head_only(q_l, h), keys[0] if len(keys) == 1 else jnp.concatenate(keys, axis=0))
            for i, (mask, _) in enumerate(parts):
                att[h] = jnp.where(mask, prod[:, i * r:(i + 1) * r], att[h])
    yield
    o_intra = [_dot(att[h], v[h]) for h in heads]
    k_end = k * jnp.exp(bend - bcum)
    k_end_t = [k_end[:, pl_[p]].T for p in pairs]
    bend_t = [bend[:, pl_[p]].T for p in pairs]
    yield
    return qm, o_intra, k_end_t, bend_t


def _gla_prompt_kernel(qkv_ref, small_ref, s0_ref, wgg_ref, bgg_ref, o_ref, sfin_ref, s_scr, *, c):
    t = pl.program_id(0)
    nb, r, _ = qkv_ref.shape

    @pl.when(t == 0)
    def _():
        s_scr[...] = s0_ref[...]

    same_le, _, same = _block_masks(r, c)
    levels = _gla_att_levels(r, c)
    zeros = jnp.zeros((c, GLA_DV), F32)
    n_sub = r // c
    heads = range(GLA_HEADS)
    pairs = range(GLA_PAIRS)

    def one_batch(b):
        q = qkv_ref[b, :, 0:GLA_KEY]
        k = qkv_ref[b, :, GLA_KEY:2 * GLA_KEY]
        v = [qkv_ref[b, :, 2 * GLA_KEY + h * GLA_DV:2 * GLA_KEY + (h + 1) * GLA_DV] for h in heads]
        qm, o_intra, k_end_t, bend_t = yield from _gla_prepass(
            q, k, v, small_ref[b], wgg_ref[...], bgg_ref[...], same_le, same, levels, c)
        s = [s_scr[b, p * LANES:(p + 1) * LANES, :] for p in pairs]
        for i in range(n_sub):
            rows = slice(i * c, (i + 1) * c)
            ws = [_dot(jnp.concatenate([qm[2 * p][rows], qm[2 * p + 1][rows]], axis=0), s[p]) for p in pairs]
            padded = [jnp.concatenate([zeros] * i + [v[h][rows]] + [zeros] * (n_sub - 1 - i), axis=0)
                      for h in heads]
            upd = [_dot(k_end_t[h // 2][(h % 2) * GLA_DK:(h % 2 + 1) * GLA_DK, :], padded[h]) for h in heads]
            yield
            for h in heads:
                o_ref[b, rows, h * GLA_DV:(h + 1) * GLA_DV] = (
                    o_intra[h][rows] + ws[h // 2][(h % 2) * c:(h % 2 + 1) * c]).astype(o_ref.dtype)
            s = [s[p] * jnp.exp(bend_t[p][:, i * c:i * c + 1])
                 + jnp.concatenate([upd[2 * p], upd[2 * p + 1]], axis=0) for p in pairs]
        for p in pairs:
            s_scr[b, p * LANES:(p + 1) * LANES, :] = s[p]

    def per_step(i, carry):
        _interleave([one_batch(i * BATCH_INTERLEAVE + j) for j in range(BATCH_INTERLEAVE)])
        return carry

    lax.fori_loop(0, nb // BATCH_INTERLEAVE, per_step, None)

    @pl.when(t == pl.num_programs(0) - 1)
    def _():
        sfin_ref[...] = s_scr[...]


def _gla_sample_kernel(qkv_ref, small_ref, s0_ref, wgg_ref, bgg_ref, o_ref, sfin_ref):
    ns, c, _ = qkv_ref.shape
    r = ns * c
    heads = range(GLA_HEADS)
    same_le, _, same = _block_masks(r, c)
    levels = _gla_att_levels(r, c)
    q = qkv_ref[:, :, 0:GLA_KEY].reshape(r, GLA_KEY)
    k = qkv_ref[:, :, GLA_KEY:2 * GLA_KEY].reshape(r, GLA_KEY)
    v = [qkv_ref[:, :, 2 * GLA_KEY + h * GLA_DV:2 * GLA_KEY + (h + 1) * GLA_DV].reshape(r, GLA_DV)
         for h in heads]
    (qm, o_intra, k_end_t, bend_t), = _interleave([_gla_prepass(
        q, k, v, small_ref[...].reshape(r, LANES), wgg_ref[...], bgg_ref[...], same_le, same, levels, c)])
    seq_of_row = lax.broadcasted_iota(jnp.int32, (r, GLA_DV), 0) // c
    for p in range(GLA_PAIRS):
        ps = slice(p * LANES, (p + 1) * LANES)
        inter = [[], []]
        for s in range(ns):
            rows = slice(s * c, (s + 1) * c)
            ws = _dot(jnp.concatenate([qm[2 * p][rows], qm[2 * p + 1][rows]], axis=0), s0_ref[s, ps, :])
            inter[0].append(ws[:c])
            inter[1].append(ws[c:])
        for hh in range(2):
            h = 2 * p + hh
            o = o_intra[h] + jnp.concatenate(inter[hh], axis=0)
            o_ref[:, h * GLA_DV:(h + 1) * GLA_DV] = o.astype(o_ref.dtype)
        for s in range(ns):
            upd = [_dot(k_end_t[p][hh * GLA_DK:(hh + 1) * GLA_DK, :],
                        jnp.where(seq_of_row == s, v[2 * p + hh], 0.0)) for hh in range(2)]
            sfin_ref[s, ps, :] = (s0_ref[s, ps, :] * jnp.exp(bend_t[p][:, s * c:s * c + 1])
                                  + jnp.concatenate(upd, axis=0))


def _gla_prompt(qkv3, small3, s0, wgg, bgg, c):
    bsz, t_len, _ = qkv3.shape
    assert bsz % BATCH_INTERLEAVE == 0 and t_len % GROUP_ROWS == 0
    rows = GLA_HEADS * GLA_DK
    r = GROUP_ROWS
    full3 = lambda t: (0, 0, 0)
    const2 = lambda t: (0, 0)
    return pl.pallas_call(
        functools.partial(_gla_prompt_kernel, c=c),
        grid=(t_len // r,),
        in_specs=[
            pl.BlockSpec((bsz, r, GLA_QKV), lambda t: (0, t, 0)),
            pl.BlockSpec((bsz, r, LANES), lambda t: (0, t, 0)),
            pl.BlockSpec((bsz, rows, GLA_DV), full3, pipeline_mode=pl.Buffered(1)),
            pl.BlockSpec((LANES, GLA_KEY), const2),
            pl.BlockSpec((1, GLA_KEY), const2),
        ],
        out_specs=[
            pl.BlockSpec((bsz, r, GLA_VAL), lambda t: (0, t, 0)),
            pl.BlockSpec((bsz, rows, GLA_DV), full3),
        ],
        out_shape=[
            jax.ShapeDtypeStruct((bsz, t_len, GLA_VAL), BF16),
            jax.ShapeDtypeStruct((bsz, rows, GLA_DV), F32),
        ],
        scratch_shapes=[pltpu.VMEM((bsz, rows, GLA_DV), F32)],
        compiler_params=pltpu.CompilerParams(dimension_semantics=("arbitrary",),
                                             vmem_limit_bytes=VMEM_MIB["gla_prompt"] * 2 ** 20),
        name="gla_prompt",
    )(qkv3, small3, s0, wgg, bgg)


def _gla_sample(qkv3, small3, s0, wgg, bgg):
    bsz, c, _ = qkv3.shape
    rows = GLA_HEADS * GLA_DK
    ns = GROUP_ROWS // c
    grp = lambda g: (g, 0, 0)
    const2 = lambda g: (0, 0)
    return pl.pallas_call(
        _gla_sample_kernel,
        grid=(bsz // ns,),
        in_specs=[
            pl.BlockSpec((ns, c, GLA_QKV), grp),
            pl.BlockSpec((ns, c, LANES), grp),
            pl.BlockSpec((ns, rows, GLA_DV), grp),
            pl.BlockSpec((LANES, GLA_KEY), const2),
            pl.BlockSpec((1, GLA_KEY), const2),
        ],
        out_specs=[
            pl.BlockSpec((ns * c, GLA_VAL), lambda g: (g, 0)),
            pl.BlockSpec((ns, rows, GLA_DV), grp),
        ],
        out_shape=[
            jax.ShapeDtypeStruct((bsz * c, GLA_VAL), BF16),
            jax.ShapeDtypeStruct((bsz, rows, GLA_DV), F32),
        ],
        compiler_params=pltpu.CompilerParams(dimension_semantics=("arbitrary",),
                                             vmem_limit_bytes=VMEM_MIB["gla_sample"] * 2 ** 20),
        name="gla_sample",
    )(qkv3, small3, s0, wgg, bgg)


def _block_unit_lower_inverse(a_list, block):
    r = a_list[0].shape[0]
    row = lax.broadcasted_iota(jnp.int32, (r, r), 0)
    col = lax.broadcasted_iota(jnp.int32, (r, r), 1)
    in_block = lambda n: (row // n) == (col // n)
    base = min(block, INVERSE_BASE_BLOCK)
    a_base = a_list if base == block else [jnp.where(in_block(base), a, 0.0) for a in a_list]
    xs = [jnp.where(row == col, 1.0, 0.0) - a for a in a_base]
    ps = [_dot(a, a) for a in a_base]
    n = 2
    while n < base:
        yield
        last = 2 * n >= base
        ms = [_dot(x if last else jnp.concatenate([p.astype(BF16), x.astype(BF16)], axis=0), p)
              for p, x in zip(ps, xs)]
        ps = [m[:r] for m in ms]
        xs = [x + m[-r:] for x, m in zip(xs, ms)]
        n *= 2
    n = base
    while n < block:
        yield
        between = in_block(2 * n) & jnp.logical_not(in_block(n))
        ts = [_dot(jnp.where(between, a, 0.0), x) for a, x in zip(a_list, xs)]
        yield
        xs = [x - _dot(x, t) for x, t in zip(xs, ts)]
        n *= 2
    return xs


def _gdn_decays(sm, alog_c, dtb_c, same_lt, same):
    r = sm.shape[0]
    heads = range(GDN_HEADS)
    ab = sm.T[SM_A:SM_A + 2 * GDN_HEADS, :]
    g8 = -jnp.exp(alog_c) * _softplus(ab + dtb_c)
    beta_c = _sigmoid(ab).T
    g1, g2 = _split2(g8)
    same_ge = same & jnp.logical_not(same_lt)
    sel = jnp.concatenate([jnp.where(same_ge, 1.0, 0.0).astype(BF16),
                           jnp.where(same, 1.0, 0.0).astype(BF16)], axis=1)
    sums = jnp.dot(jnp.concatenate([g1, g2], axis=0), sel, preferred_element_type=F32)
    sums = sums[:2 * GDN_HEADS] + sums[2 * GDN_HEADS:]
    dec_rows = sums[:, :r]
    dec_c = dec_rows.T
    dend_c = sums[:, r:].T
    return ([dec_c[:, h:h + 1] for h in heads], [dec_rows[h:h + 1, :] for h in heads],
            [dend_c[:, h:h + 1] for h in heads], [beta_c[:, GDN_HEADS + h:GDN_HEADS + h + 1] for h in heads])


def _gdn_head_cols(h):
    return [slice(part * GDN_KEY + h * GDN_DK, part * GDN_KEY + (h + 1) * GDN_DK) for part in range(3)]


def _gdn_prepass(qkv_of_head, sm, alog_c, dtb_c, same_le, same_lt, same, block, heads):
    dcol, drow, dend, beta = ([x[h] for h in heads] for x in _gdn_decays(sm, alog_c, dtb_c, same_lt, same))
    idx = range(len(heads))
    gamma = [jnp.where(same_le, jnp.exp(jnp.where(same_le, dcol[i] - drow[i], 0.0)), 0.0) for i in idx]
    q, k, v = (list(x) for x in zip(*[qkv_of_head(h) for h in heads]))
    r = q[0].shape[0]
    q = [x * lax.rsqrt(jnp.sum(x * x, axis=-1, keepdims=True) + NORM_EPS) * (GDN_DK ** -0.5) for x in q]
    k = [x * lax.rsqrt(jnp.sum(x * x, axis=-1, keepdims=True) + NORM_EPS) for x in k]
    yield
    kb = [k[i] * beta[i] for i in idx]
    vb = [v[i] * beta[i] for i in idx]
    kq = [_dot_nt(jnp.concatenate([kb[i], q[i]], axis=0), k[i]) for i in idx]
    yield
    a_mat = [jnp.where(same_lt, kq[i][:r] * gamma[i], 0.0) for i in idx]
    qk = [(kq[i][r:] * gamma[i]).astype(BF16) for i in idx]
    t_inv = yield from _block_unit_lower_inverse(a_mat, block)
    edec = [jnp.exp(dcol[i]) for i in idx]
    uw = [_dot(t_inv[i], jnp.concatenate([vb[i], kb[i] * edec[i]], axis=1)) for i in idx]
    q_e = [(q[i] * edec[i]).astype(BF16) for i in idx]
    k_end_t = [(k[i] * jnp.exp(dend[i] - dcol[i])).T.astype(BF16) for i in idx]
    yield
    u = [x[:, :GDN_DV] for x in uw]
    w = [x[:, GDN_DV:].astype(BF16) for x in uw]
    return u, w, q_e, qk, k_end_t, dend


def _conv_silu(win, cw_ref):
    conv = win(0) * cw_ref[CONV_WIDTH - 1:CONV_WIDTH, :]
    for j in range(1, CONV_WIDTH):
        conv = conv + win(j) * cw_ref[CONV_WIDTH - 1 - j:CONV_WIDTH - j, :]
    return _silu(conv)

def _gdn_prompt_kernel(qkv_ref, small_ref, cs_ref, s0_ref, cw_ref, alog_ref, dtb_ref,
                       o_ref, sfin_ref, cnew_ref, xbuf, prev, s_scr, *, c):
    t = pl.program_id(0)
    nb, r, _ = qkv_ref.shape
    keep = CONV_WIDTH - 1
    base = SUBLANES

    @pl.when(t == 0)
    def _():
        s_scr[...] = s0_ref[...]
        prev[:, base - keep:base, :] = cs_ref[...]

    same_le, same_lt, same = _block_masks(r, c)
    zeros = jnp.zeros((c, GDN_DV), F32)
    n_sub = r // c

    def one_batch(b, slot, heads, first):
        xb = xbuf.at[slot]
        if first:
            xb[base - keep:base, :] = prev[b, base - keep:base, :]
            xb[base:base + r, :] = qkv_ref[b]
            prev[b, base - keep:base, :] = xb[base + r - keep:base + r, :]

        def qkv_of_head(h):
            return [_conv_silu(lambda j: xb[base - j:base - j + r, cols], cw_ref.at[:, cols])
                    for cols in _gdn_head_cols(h)]

        u, w, q_e, qk, k_end_t, dend = yield from _gdn_prepass(
            qkv_of_head, small_ref[b], alog_ref[...], dtb_ref[...], same_le, same_lt, same, c, heads)
        idx = range(len(heads))
        hs = [slice(h * GDN_DK, (h + 1) * GDN_DK) for h in heads]
        s = [s_scr[b, hs[i], :] for i in idx]
        for ci in range(n_sub):
            rows = slice(ci * c, (ci + 1) * c)
            ws = [_dot(jnp.concatenate([w[i][rows], q_e[i][rows]], axis=0), s[i]) for i in idx]
            yield
            v_new = [u[i][rows] - ws[i][:c] for i in idx]
            padded = [jnp.concatenate([zeros] * ci + [v_new[i]] + [zeros] * (n_sub - 1 - ci), axis=0)
                      for i in idx]
            upd = [_dot(jnp.concatenate([qk[i][rows], k_end_t[i]], axis=0), padded[i]) for i in idx]
            yield
            for i in idx:
                o_ref[b, rows, hs[i]] = (ws[i][c:] + upd[i][:c]).astype(o_ref.dtype)
            s = [s[i] * jnp.exp(dend[i][ci * c:ci * c + 1, :]) + upd[i][c:] for i in idx]
        for i in idx:
            s_scr[b, hs[i], :] = s[i]

    def per_step(step, carry):
        for g in range(0, GDN_HEADS, GDN_HEAD_GROUP):
            heads = tuple(range(g, g + GDN_HEAD_GROUP))
            _interleave([one_batch(step * GDN_BATCH_INTERLEAVE + j, j, heads, g == 0)
                         for j in range(GDN_BATCH_INTERLEAVE)])
        return carry

    lax.fori_loop(0, nb // GDN_BATCH_INTERLEAVE, per_step, None)

    @pl.when(t == pl.num_programs(0) - 1)
    def _():
        sfin_ref[...] = s_scr[...]
        cnew_ref[...] = prev[:, base - keep:base, :]


def _gdn_sample_kernel(qkv_ref, small_ref, cs_ref, s0_ref, cw_ref, alog_ref, dtb_ref,
                       o_ref, sfin_ref, cnew_ref, xbuf):
    ns, c, _ = qkv_ref.shape
    r = ns * c
    keep = CONV_WIDTH - 1
    base = SUBLANES
    xbuf[:, base - keep:base, :] = cs_ref[...]
    xbuf[:, base:base + c, :] = qkv_ref[...]
    conv = _conv_silu(lambda j: xbuf[:, base - j:base - j + c, :], cw_ref).reshape(r, CONV_CH)
    cnew_ref[...] = xbuf[:, base + c - keep:base + c, :]

    same_le, same_lt, same = _block_masks(r, c)
    seq_of_row = lax.broadcasted_iota(jnp.int32, (r, GDN_DV), 0) // c
    (u, w, q_e, qk, k_end_t, dend), = _interleave([_gdn_prepass(
        lambda h: [conv[:, cols] for cols in _gdn_head_cols(h)], small_ref[...].reshape(r, LANES),
        alog_ref[...], dtb_ref[...], same_le, same_lt, same, c, tuple(range(GDN_HEADS)))])
    for h in range(GDN_HEADS):
        hs = slice(h * GDN_DK, (h + 1) * GDN_DK)
        v_parts, o_parts = [], []
        for s in range(ns):
            rows = slice(s * c, (s + 1) * c)
            ws = _dot(jnp.concatenate([w[h][rows], q_e[h][rows]], axis=0), s0_ref[s, hs, :])
            v_parts.append(u[h][rows] - ws[:c])
            o_parts.append(ws[c:])
        v_new = jnp.concatenate(v_parts, axis=0)
        o = jnp.concatenate(o_parts, axis=0) + _dot(qk[h], v_new)
        o_ref[:, h * GDN_DV:(h + 1) * GDN_DV] = o.astype(o_ref.dtype)
        for s in range(ns):
            upd = _dot(k_end_t[h], jnp.where(seq_of_row == s, v_new, 0.0))
            sfin_ref[s, hs, :] = s0_ref[s, hs, :] * jnp.exp(dend[h][s * c:s * c + 1, :]) + upd


def _gdn_prompt(qkv3, small3, conv_state, s0, conv_w, alog_v, dtb_v, c):
    bsz, t_len, _ = qkv3.shape
    assert bsz % GDN_BATCH_INTERLEAVE == 0 and t_len % GROUP_ROWS == 0 and GDN_HEADS % GDN_HEAD_GROUP == 0
    rows = GDN_HEADS * GDN_DK
    keep = CONV_WIDTH - 1
    r = GROUP_ROWS
    full3 = lambda t: (0, 0, 0)
    const2 = lambda t: (0, 0)
    return pl.pallas_call(
        functools.partial(_gdn_prompt_kernel, c=c),
        grid=(t_len // r,),
        in_specs=[
            pl.BlockSpec((bsz, r, CONV_CH), lambda t: (0, t, 0)),
            pl.BlockSpec((bsz, r, LANES), lambda t: (0, t, 0)),
            pl.BlockSpec((bsz, keep, CONV_CH), full3),
            pl.BlockSpec((bsz, rows, GDN_DV), full3, pipeline_mode=pl.Buffered(1)),
            pl.BlockSpec((CONV_WIDTH, CONV_CH), const2),
            pl.BlockSpec((2 * GDN_HEADS, 1), const2),
            pl.BlockSpec((2 * GDN_HEADS, 1), const2),
        ],
        out_specs=[
            pl.BlockSpec((bsz, r, GDN_VAL), lambda t: (0, t, 0)),
            pl.BlockSpec((bsz, rows, GDN_DV), full3),
            pl.BlockSpec((bsz, keep, CONV_CH), full3),
        ],
        out_shape=[
            jax.ShapeDtypeStruct((bsz, t_len, GDN_VAL), BF16),
            jax.ShapeDtypeStruct((bsz, rows, GDN_DV), F32),
            jax.ShapeDtypeStruct((bsz, keep, CONV_CH), F32),
        ],
        scratch_shapes=[pltpu.VMEM((GDN_BATCH_INTERLEAVE, SUBLANES + r, CONV_CH), F32),
                        pltpu.VMEM((bsz, SUBLANES, CONV_CH), F32),
                        pltpu.VMEM((bsz, rows, GDN_DV), F32)],
        compiler_params=pltpu.CompilerParams(dimension_semantics=("arbitrary",),
                                             vmem_limit_bytes=VMEM_MIB["gdn_prompt"] * 2 ** 20),
        name="gdn_prompt",
    )(qkv3, small3, conv_state, s0, conv_w, alog_v, dtb_v)


def _gdn_sample(qkv3, small3, conv_state, s0, conv_w, alog_v, dtb_v):
    bsz, c, _ = qkv3.shape
    rows = GDN_HEADS * GDN_DK
    keep = CONV_WIDTH - 1
    ns = GROUP_ROWS // c
    grp = lambda g: (g, 0, 0)
    const2 = lambda g: (0, 0)
    return pl.pallas_call(
        _gdn_sample_kernel,
        grid=(bsz // ns,),
        in_specs=[
            pl.BlockSpec((ns, c, CONV_CH), grp),
            pl.BlockSpec((ns, c, LANES), grp),
            pl.BlockSpec((ns, keep, CONV_CH), grp),
            pl.BlockSpec((ns, rows, GDN_DV), grp),
            pl.BlockSpec((CONV_WIDTH, CONV_CH), const2),
            pl.BlockSpec((2 * GDN_HEADS, 1), const2),
            pl.BlockSpec((2 * GDN_HEADS, 1), const2),
        ],
        out_specs=[
            pl.BlockSpec((ns * c, GDN_VAL), lambda g: (g, 0)),
            pl.BlockSpec((ns, rows, GDN_DV), grp),
            pl.BlockSpec((ns, keep, CONV_CH), grp),
        ],
        out_shape=[
            jax.ShapeDtypeStruct((bsz * c, GDN_VAL), BF16),
            jax.ShapeDtypeStruct((bsz, rows, GDN_DV), F32),
            jax.ShapeDtypeStruct((bsz, keep, CONV_CH), F32),
        ],
        scratch_shapes=[pltpu.VMEM((ns, 2 * SUBLANES, CONV_CH), F32)],
        compiler_params=pltpu.CompilerParams(dimension_semantics=("arbitrary",),
                                             vmem_limit_bytes=VMEM_MIB["gdn_sample"] * 2 ** 20),
        name="gdn_sample",
    )(qkv3, small3, conv_state, s0, conv_w, alog_v, dtb_v)


def _head_norm_gate(o, silu_z, w):
    parts = []
    for h in range(o.shape[-1] // LANES):
        oh = o[:, h * LANES:(h + 1) * LANES]
        parts.append(oh * lax.rsqrt(jnp.mean(oh * oh, axis=-1, keepdims=True) + NORM_EPS) * w)
    return jnp.concatenate(parts, axis=-1) * silu_z


def _out_kernel(oa_ref, ob_ref, ga_ref, gb_ref, za_ref, zb_ref, x_ref, p_ref, anw_ref, bnw_ref,
                wua_ref, wub_ref, wout_ref, wpg_ref, wp_ref, fnw_ref, y_ref):
    f32 = lambda ref: ref[...].astype(F32)
    ya = _dot(_head_norm_gate(f32(oa_ref), f32(za_ref), anw_ref[...]), wua_ref[...])
    yb = _dot(_head_norm_gate(f32(ob_ref), f32(zb_ref), bnw_ref[...]), wub_ref[...])
    merged = f32(ga_ref) * ya + f32(gb_ref) * yb
    h1 = x_ref[...] + _dot(merged, wout_ref[...])
    h2 = h1 + _sigmoid(_dot(h1, wpg_ref[...])) * _dot(p_ref[...], wp_ref[...])
    y_ref[...] = h2 * lax.rsqrt(jnp.mean(h2 * h2, axis=-1, keepdims=True) + NORM_EPS) * fnw_ref[...]


def _out_stage(o_a, o_b, gz, x2d, p2d, anw, bnw, wua, wub, wout, wpg, wp, fnw):
    n = x2d.shape[0]
    tm = min(ROW_TILE, n)
    const = lambda i: (0, 0)
    return pl.pallas_call(
        _out_kernel,
        grid=(n // tm,),
        in_specs=[
            pl.BlockSpec((tm, GLA_VAL), lambda i: (i, 0)),
            pl.BlockSpec((tm, GDN_VAL), lambda i: (i, 0)),
            pl.BlockSpec((tm, D_MODEL), lambda i: (i, GZ_GATE_A // D_MODEL)),
            pl.BlockSpec((tm, D_MODEL), lambda i: (i, GZ_GATE_B // D_MODEL)),
            pl.BlockSpec((tm, GLA_VAL), lambda i: (i, GZ_Z_A // GLA_VAL)),
            pl.BlockSpec((tm, GDN_VAL), lambda i: (i, GZ_Z_B // GDN_VAL)),
            pl.BlockSpec((tm, D_MODEL), lambda i: (i, 0)),
            pl.BlockSpec((tm, PLE_DIM), lambda i: (i, 0)),
            pl.BlockSpec((1, GLA_DV), const),
            pl.BlockSpec((1, GDN_DV), const),
            pl.BlockSpec((GLA_VAL, D_MODEL), const, pipeline_mode=pl.Buffered(1)),
            pl.BlockSpec((GDN_VAL, D_MODEL), const, pipeline_mode=pl.Buffered(1)),
            pl.BlockSpec((D_MODEL, D_MODEL), const, pipeline_mode=pl.Buffered(1)),
            pl.BlockSpec((D_MODEL, D_MODEL), const, pipeline_mode=pl.Buffered(1)),
            pl.BlockSpec((PLE_DIM, D_MODEL), const, pipeline_mode=pl.Buffered(1)),
            pl.BlockSpec((1, D_MODEL), const),
        ],
        out_specs=pl.BlockSpec((tm, D_MODEL), lambda i: (i, 0)),
        out_shape=jax.ShapeDtypeStruct((n, D_MODEL), F32),
        compiler_params=pltpu.CompilerParams(dimension_semantics=("arbitrary",),
                                             vmem_limit_bytes=VMEM_MIB["out_stage"] * 2 ** 20),
        name="out_stage",
    )(o_a, o_b, gz, gz, gz, gz, x2d, p2d, anw, bnw, wua, wub, wout, wpg, wp, fnw)


def _in_offsets():
    offs = [0]
    for s in IN_SPLITS:
        offs.append(offs[-1] + s)
    return offs


def _regroup_kernel(wt_ref, o_ref):
    offs = _in_offsets()
    (q_a, _, _, g_a, z_a, qkv_b, a_b, _, z_b, gate_a, gate_b, end) = offs
    piece = lambda lo, hi: wt_ref[lo:hi, :].astype(BF16)
    o_ref[P_GLA_QKV:P_GLA_QKV + GLA_QKV, :] = piece(q_a, g_a)
    o_ref[P_QKV_B:P_QKV_B + CONV_CH, :] = piece(qkv_b, a_b)
    small = jnp.concatenate([wt_ref[g_a:z_a, :], wt_ref[a_b:z_b, :],
                             jnp.zeros((LANES - (z_a - g_a) - (z_b - a_b), wt_ref.shape[1]), F32)], axis=0)
    o_ref[P_SMALL:P_SMALL + LANES, :] = small.astype(BF16)
    o_ref[P_GATES + GZ_GATE_A:P_GATES + GZ_GATE_A + D_MODEL, :] = piece(gate_a, gate_b)
    o_ref[P_GATES + GZ_GATE_B:P_GATES + GZ_GATE_B + D_MODEL, :] = piece(gate_b, end)
    o_ref[P_GATES + GZ_Z_A:P_GATES + GZ_Z_A + GLA_VAL, :] = piece(z_a, qkv_b)
    o_ref[P_GATES + GZ_Z_B:P_GATES + GZ_Z_B + GDN_VAL, :] = piece(z_b, gate_a)


def _regroup_w_in(w_in_t):
    cols = 256
    return pl.pallas_call(
        _regroup_kernel,
        grid=(D_MODEL // cols,),
        in_specs=[pl.BlockSpec((None, w_in_t.shape[1], cols), lambda i: (0, 0, i))],
        out_specs=pl.BlockSpec((P_COLS, cols), lambda i: (0, i)),
        out_shape=jax.ShapeDtypeStruct((P_COLS, D_MODEL), BF16),
        compiler_params=pltpu.CompilerParams(dimension_semantics=("arbitrary",),
                                             vmem_limit_bytes=VMEM_MIB["regroup_w_in"] * 2 ** 20),
        name="regroup_w_in",
    )(w_in_t)


def _head_param_col(v):
    return jnp.zeros((2 * GDN_HEADS, 1), F32).at[:GDN_HEADS, 0].set(v.astype(F32))


def _trunk(x, p, s_gla, s_gdn, conv_state, wts):
    bsz, t_len, _ = x.shape
    c = min(PROMPT_CHUNK, t_len)
    n = bsz * t_len
    x2d = x.reshape(n, D_MODEL)
    p2d = p.reshape(n, PLE_DIM)
    gla_in, gdn_in, small, gz = _inproj(x2d, wts["norm_w"], wts["w_in_r"])
    gla_in3 = gla_in.reshape(bsz, t_len, GLA_QKV)
    gdn_in3 = gdn_in.reshape(bsz, t_len, CONV_CH)
    small3 = small.reshape(bsz, t_len, LANES)
    s_gla2 = s_gla.reshape(bsz, GLA_HEADS * GLA_DK, GLA_DV)
    s_gdn2 = s_gdn.reshape(bsz, GDN_HEADS * GDN_DK, GDN_DV)
    if t_len % GROUP_ROWS == 0:
        o_a, gla_fin = _gla_prompt(gla_in3, small3, s_gla2, wts["wgg"], wts["bgg"], c)
        o_b, gdn_fin, conv_new = _gdn_prompt(gdn_in3, small3, conv_state, s_gdn2, wts["conv_w"],
                                             wts["alog_v"], wts["dtb_v"], c)
    else:
        assert GROUP_ROWS % t_len == 0 and bsz % (GROUP_ROWS // t_len) == 0 and t_len >= CONV_WIDTH - 1
        o_a, gla_fin = _gla_sample(gla_in3, small3, s_gla2, wts["wgg"], wts["bgg"])
        o_b, gdn_fin, conv_new = _gdn_sample(gdn_in3, small3, conv_state, s_gdn2, wts["conv_w"],
                                             wts["alog_v"], wts["dtb_v"])
    y = _out_stage(o_a.reshape(n, GLA_VAL), o_b.reshape(n, GDN_VAL), gz, x2d, p2d, wts["anw"], wts["bnw"],
                   wts["wua"], wts["wub"], wts["wout"], wts["wpg"], wts["wp"], wts["fnw"])
    return (y.reshape(bsz, t_len, D_MODEL),
            gla_fin.reshape(1, bsz, GLA_HEADS, GLA_DK, GLA_DV),
            gdn_fin.reshape(1, bsz, GDN_HEADS, GDN_DK, GDN_DV),
            conv_new.reshape(1, bsz, CONV_WIDTH - 1, CONV_CH))


def kernel(x_prompt, x_sample, state_gla, state_gdn, state_conv, p_prompt, p_sample, norm_w, w_in,
           w_gla_gate, b_gla_gate, gla_norm_w, conv_w, gdn_a_log, gdn_dt_bias, gdn_norm_w,
           w_up_gla, w_up_gdn, w_out, w_ple_gate, w_ple, final_norm_w):
    wgg = jnp.zeros((LANES, GLA_KEY), F32).at[SM_G:SM_G + GLA_GATE_RANK].set(w_gla_gate[0]).astype(BF16)
    wts = {
        "norm_w": norm_w[0].reshape(1, D_MODEL),
        "w_in_r": _regroup_w_in(jnp.swapaxes(w_in, 1, 2)),
        "wgg": wgg,
        "bgg": b_gla_gate[0].reshape(1, GLA_KEY),
        "conv_w": conv_w[0],
        "alog_v": _head_param_col(gdn_a_log[0]),
        "dtb_v": _head_param_col(gdn_dt_bias[0]),
        "anw": gla_norm_w[0].reshape(1, GLA_DV),
        "bnw": gdn_norm_w[0].reshape(1, GDN_DV),
        "wua": w_up_gla[0].astype(BF16),
        "wub": w_up_gdn[0].astype(BF16),
        "wout": w_out[0].astype(BF16),
        "wpg": w_ple_gate[0].astype(BF16),
        "wp": w_ple[0].astype(BF16),
        "fnw": final_norm_w.reshape(1, D_MODEL),
    }
    bsz = x_prompt.shape[0]
    dt = x_prompt.dtype
    y_p, gla_p, gdn_p, conv_p = _trunk(
        x_prompt, p_prompt[0],
        jnp.zeros((bsz, GLA_HEADS, GLA_DK, GLA_DV), dt), jnp.zeros((bsz, GDN_HEADS, GDN_DK, GDN_DV), dt),
        jnp.zeros((bsz, CONV_WIDTH - 1, CONV_CH), dt), wts)
    y_s, gla_s, gdn_s, conv_s = _trunk(x_sample, p_sample[0], state_gla[0], state_gdn[0], state_conv[0], wts)
    return (y_p, y_s, gla_p, gdn_p, conv_p, gla_s, gdn_s, conv_s)
```

```python
import functools

import jax
import jax.numpy as jnp
from jax import lax
from jax.experimental import pallas as pl
from jax.experimental.pallas import tpu as pltpu

F32 = jnp.float32
BF16 = jnp.bfloat16

D_MODEL = 1024
PLE_DIM = 256
NORM_EPS = 1e-6
GLA_HEADS = 4
GLA_DK = 64
GLA_DV = 128
GLA_KEY = GLA_HEADS * GLA_DK
GLA_VAL = GLA_HEADS * GLA_DV
GLA_GATE_RANK = 16
GLA_GATE_TEMP = 16.0
GDN_HEADS = 4
GDN_DK = 128
GDN_DV = 128
GDN_KEY = GDN_HEADS * GDN_DK
GDN_VAL = GDN_HEADS * GDN_DV
CONV_WIDTH = 4
CONV_CH = 2 * GDN_KEY + GDN_VAL
IN_SPLITS = (GLA_KEY, GLA_KEY, GLA_VAL, GLA_GATE_RANK, GLA_VAL, CONV_CH, GDN_HEADS, GDN_HEADS,
             GDN_VAL, D_MODEL, D_MODEL)

LANES = 128
SUBLANES = 8

GLA_QKV = 2 * GLA_KEY + GLA_VAL
P_GLA_QKV = 0
P_QKV_B = P_GLA_QKV + GLA_QKV
P_SMALL = P_QKV_B + CONV_CH
P_GATES = P_SMALL + LANES
GZ_COLS = 2 * D_MODEL + GLA_VAL + GDN_VAL
P_COLS = P_GATES + GZ_COLS
GZ_GATE_A = 0
GZ_GATE_B = D_MODEL
GZ_Z_A = 2 * D_MODEL
GZ_Z_B = 2 * D_MODEL + GLA_VAL
SM_G = 0
SM_A = GLA_GATE_RANK
SM_B = GLA_GATE_RANK + GDN_HEADS

PROMPT_CHUNK = 64
GROUP_ROWS = 128
BATCH_INTERLEAVE = 8
GDN_BATCH_INTERLEAVE = 8
GDN_HEAD_GROUP = 4
INVERSE_BASE_BLOCK = 16
ROW_TILE = 512
VMEM_MIB = {"regroup_w_in": 40, "inproj": 40, "gla_prompt": 40, "gla_sample": 40, "gdn_prompt": 48,
            "gdn_sample": 40, "out_stage": 40}


def _dot(a, b):
    return jnp.dot(a.astype(BF16), b.astype(BF16), preferred_element_type=F32)


def _dot_nt(a, b):
    return lax.dot_general(a.astype(BF16), b.astype(BF16), (((1,), (1,)), ((), ())),
                           preferred_element_type=F32)


def _split2(x):
    h1 = x.astype(BF16)
    return h1, (x - h1.astype(F32)).astype(BF16)


def _cumsum_rows(tri, x):
    x1, x2 = _split2(x)
    d = functools.partial(jnp.dot, preferred_element_type=F32)
    return d(tri, x1) + d(tri, x2)


def _softplus(x):
    return jnp.maximum(x, 0.0) + jnp.log(1.0 + jnp.exp(-jnp.abs(x)))


def _sigmoid(x):
    return 1.0 / (1.0 + jnp.exp(-x))


def _silu(x):
    return x * _sigmoid(x)


def _sigmoid_tanh(x):
    return 0.5 * jnp.tanh(0.5 * x) + 0.5


def _silu_tanh(x):
    half = 0.5 * x
    return half * jnp.tanh(half) + half


def _interleave(emitters):
    results = [None] * len(emitters)
    live = list(range(len(emitters)))
    while live:
        for i in list(live):
            try:
                next(emitters[i])
            except StopIteration as stop:
                results[i] = stop.value
                live.remove(i)
    return results


def _block_masks(r, block):
    row = lax.broadcasted_iota(jnp.int32, (r, r), 0)
    col = lax.broadcasted_iota(jnp.int32, (r, r), 1)
    same = (row // block) == (col // block)
    return same & (row >= col), same & (row > col), same


INPROJ_COL_STEP = 512


def _inproj_kernel(x_ref, nw_ref, w_ref, gla_ref, gdn_ref, small_ref, gz_ref):
    x = x_ref[...]
    xn = x * lax.rsqrt(jnp.mean(x * x, axis=-1, keepdims=True) + NORM_EPS) * nw_ref[...]
    xb = xn.astype(BF16)

    def emit(o_ref, w0, width, act=None, o0=0):
        for c0 in range(0, width, INPROJ_COL_STEP):
            c1 = min(c0 + INPROJ_COL_STEP, width)
            res = _dot_nt(xb, w_ref[w0 + c0:w0 + c1, :])
            o_ref[:, o0 + c0:o0 + c1] = (res if act is None else act(res)).astype(o_ref.dtype)

    emit(gla_ref, P_GLA_QKV, GLA_QKV)
    emit(gdn_ref, P_QKV_B, CONV_CH)
    emit(small_ref, P_SMALL, LANES)
    emit(gz_ref, P_GATES + GZ_GATE_A, 2 * D_MODEL, act=_sigmoid_tanh, o0=GZ_GATE_A)
    emit(gz_ref, P_GATES + GZ_Z_A, GLA_VAL + GDN_VAL, act=_silu_tanh, o0=GZ_Z_A)


def _inproj(x2d, norm_w, w_in_r):
    n = x2d.shape[0]
    tm = min(ROW_TILE, n)
    rows = lambda i: (i, 0)
    return pl.pallas_call(
        _inproj_kernel,
        grid=(n // tm,),
        in_specs=[
            pl.BlockSpec((tm, D_MODEL), rows),
            pl.BlockSpec((1, D_MODEL), lambda i: (0, 0)),
            pl.BlockSpec((P_COLS, D_MODEL), lambda i: (0, 0), pipeline_mode=pl.Buffered(1)),
        ],
        out_specs=[
            pl.BlockSpec((tm, GLA_QKV), rows),
            pl.BlockSpec((tm, CONV_CH), rows),
            pl.BlockSpec((tm, LANES), rows),
            pl.BlockSpec((tm, GZ_COLS), rows),
        ],
        out_shape=[
            jax.ShapeDtypeStruct((n, GLA_QKV), F32),
            jax.ShapeDtypeStruct((n, CONV_CH), F32),
            jax.ShapeDtypeStruct((n, LANES), F32),
            jax.ShapeDtypeStruct((n, GZ_COLS), BF16),
        ],
        compiler_params=pltpu.CompilerParams(dimension_semantics=("arbitrary",),
                                             vmem_limit_bytes=VMEM_MIB["inproj"] * 2 ** 20),
        name="inproj",
    )(x2d, norm_w, w_in_r)


GLA_PAIRS = GLA_HEADS // 2
GLA_SUB_BLOCK = 16


def _gla_att_levels(r, block):
    row = lax.broadcasted_iota(jnp.int32, (r, r), 0)
    col = lax.broadcasted_iota(jnp.int32, (r, r), 1)
    sub = min(block, GLA_SUB_BLOCK)
    levels = [(None, ((row // sub) == (col // sub)) & (row >= col))]
    half = sub
    while half < block:
        levels.append((half, ((row // (2 * half)) == (col // (2 * half)))
                       & ((row // half) % 2 == 1) & ((col // half) % 2 == 0)))
        half *= 2
    return levels


def _rows_at(x, n, offset):
    return jnp.concatenate([jnp.broadcast_to(x[i + offset:i + offset + 1, :], (n, x.shape[1]))
                            for i in range(0, x.shape[0], n)], axis=0)


def _gla_prepass(q, k, v, sm, wgg, bgg, same_le, same, levels, block):
    r = q.shape[0]
    heads = range(GLA_HEADS)
    pairs = range(GLA_PAIRS)
    pre = _dot(sm, wgg) + bgg
    yield
    gk = (jnp.minimum(pre, 0.0) - jnp.log(1.0 + jnp.exp(-jnp.abs(pre)))) * (1.0 / GLA_GATE_TEMP)
    sums = _cumsum_rows(jnp.concatenate([jnp.where(same_le, 1.0, 0.0).astype(BF16),
                                         jnp.where(same, 1.0, 0.0).astype(BF16)], axis=0), gk)
    yield
    bcum = sums[:r]
    bend = sums[r:]
    bex = bcum - gk
    scale = GLA_DK ** -0.5
    q_e = q * jnp.exp(bcum) * scale
    lane = lax.broadcasted_iota(jnp.int32, (r, LANES), 1)
    in_head = [lane < GLA_DK, lane >= GLA_DK]
    pl_ = [slice(p * LANES, (p + 1) * LANES) for p in pairs]
    head_only = lambda x, h: jnp.where(in_head[h % 2], x[:, pl_[h // 2]], 0.0)
    qm = [head_only(q_e, h) for h in heads]
    sub = min(block, GLA_SUB_BLOCK)
    start = _rows_at(bex, sub, 0)
    groups = {None: (q * jnp.exp(bcum - start) * scale, [(levels[0][1], k * jnp.exp(start - bcum))])}
    for half, mask in levels[1:]:
        key = None if half == sub else half
        if key not in groups:
            groups[key] = (q * jnp.exp(bcum - _rows_at(bex, half, 0)) * scale, [])
        groups[key][1].append((mask, k * jnp.exp(_rows_at(bcum, half, half - 1) - bcum)))
    att = [jnp.zeros((r, r), F32)] * GLA_HEADS
    for q_l, parts in groups.values():
        for h in heads:
            keys = [k_l[:, pl_[h // 2]].astype(BF16) for _, k_l in parts]
            prod = _dot_nt(head_only(q_l, h), keys[0] if len(keys) == 1 else jnp.concatenate(keys, axis=0))
            for i, (mask, _) in enumerate(parts):
                att[h] = jnp.where(mask, prod[:, i * r:(i + 1) * r], att[h])
    yield
    o_intra = [_dot(att[h], v[h]) for h in heads]
    k_end = k * jnp.exp(bend - bcum)
    k_end_t = [k_end[:, pl_[p]].T for p in pairs]
    bend_t = [bend[:, pl_[p]].T for p in pairs]
    yield
    return qm, o_intra, k_end_t, bend_t


def _gla_prompt_kernel(qkv_ref, small_ref, s0_ref, wgg_ref, bgg_ref, o_ref, sfin_ref, s_scr, *, c):
    t = pl.program_id(0)
    nb, r, _ = qkv_ref.shape

    @pl.when(t == 0)
    def _():
        s_scr[...] = s0_ref[...]

    same_le, _, same = _block_masks(r, c)
    levels = _gla_att_levels(r, c)
    zeros = jnp.zeros((c, GLA_DV), F32)
    n_sub = r // c
    heads = range(GLA_HEADS)
    pairs = range(GLA_PAIRS)

    def one_batch(b):
        q = qkv_ref[b, :, 0:GLA_KEY]
        k = qkv_ref[b, :, GLA_KEY:2 * GLA_KEY]
        v = [qkv_ref[b, :, 2 * GLA_KEY + h * GLA_DV:2 * GLA_KEY + (h + 1) * GLA_DV] for h in heads]
        qm, o_intra, k_end_t, bend_t = yield from _gla_prepass(
            q, k, v, small_ref[b], wgg_ref[...], bgg_ref[...], same_le, same, levels, c)
        s = [s_scr[b, p * LANES:(p + 1) * LANES, :] for p in pairs]
        for i in range(n_sub):
            rows = slice(i * c, (i + 1) * c)
            ws = [_dot(jnp.concatenate([qm[2 * p][rows], qm[2 * p + 1][rows]], axis=0), s[p]) for p in pairs]
            padded = [jnp.concatenate([zeros] * i + [v[h][rows]] + [zeros] * (n_sub - 1 - i), axis=0)
                      for h in heads]
            upd = [_dot(k_end_t[h // 2][(h % 2) * GLA_DK:(h % 2 + 1) * GLA_DK, :], padded[h]) for h in heads]
            yield
            for h in heads:
                o_ref[b, rows, h * GLA_DV:(h + 1) * GLA_DV] = (
                    o_intra[h][rows] + ws[h // 2][(h % 2) * c:(h % 2 + 1) * c]).astype(o_ref.dtype)
            s = [s[p] * jnp.exp(bend_t[p][:, i * c:i * c + 1])
                 + jnp.concatenate([upd[2 * p], upd[2 * p + 1]], axis=0) for p in pairs]
        for p in pairs:
            s_scr[b, p * LANES:(p + 1) * LANES, :] = s[p]

    def per_step(i, carry):
        _interleave([one_batch(i * BATCH_INTERLEAVE + j) for j in range(BATCH_INTERLEAVE)])
        return carry

    lax.fori_loop(0, nb // BATCH_INTERLEAVE, per_step, None)

    @pl.when(t == pl.num_programs(0) - 1)
    def _():
        sfin_ref[...] = s_scr[...]


def _gla_sample_kernel(qkv_ref, small_ref, s0_ref, wgg_ref, bgg_ref, o_ref, sfin_ref):
    ns, c, _ = qkv_ref.shape
    r = ns * c
    heads = range(GLA_HEADS)
    same_le, _, same = _block_masks(r, c)
    levels = _gla_att_levels(r, c)
    q = qkv_ref[:, :, 0:GLA_KEY].reshape(r, GLA_KEY)
    k = qkv_ref[:, :, GLA_KEY:2 * GLA_KEY].reshape(r, GLA_KEY)
    v = [qkv_ref[:, :, 2 * GLA_KEY + h * GLA_DV:2 * GLA_KEY + (h + 1) * GLA_DV].reshape(r, GLA_DV)
         for h in heads]
    (qm, o_intra, k_end_t, bend_t), = _interleave([_gla_prepass(
        q, k, v, small_ref[...].reshape(r, LANES), wgg_ref[...], bgg_ref[...], same_le, same, levels, c)])
    seq_of_row = lax.broadcasted_iota(jnp.int32, (r, GLA_DV), 0) // c
    for p in range(GLA_PAIRS):
        ps = slice(p * LANES, (p + 1) * LANES)
        inter = [[], []]
        for s in range(ns):
            rows = slice(s * c, (s + 1) * c)
            ws = _dot(jnp.concatenate([qm[2 * p][rows], qm[2 * p + 1][rows]], axis=0), s0_ref[s, ps, :])
            inter[0].append(ws[:c])
            inter[1].append(ws[c:])
        for hh in range(2):
            h = 2 * p + hh
            o = o_intra[h] + jnp.concatenate(inter[hh], axis=0)
            o_ref[:, h * GLA_DV:(h + 1) * GLA_DV] = o.astype(o_ref.dtype)
        for s in range(ns):
            upd = [_dot(k_end_t[p][hh * GLA_DK:(hh + 1) * GLA_DK, :],
                        jnp.where(seq_of_row == s, v[2 * p + hh], 0.0)) for hh in range(2)]
            sfin_ref[s, ps, :] = (s0_ref[s, ps, :] * jnp.exp(bend_t[p][:, s * c:s * c + 1])
                                  + jnp.concatenate(upd, axis=0))


def _gla_prompt(qkv3, small3, s0, wgg, bgg, c):
    bsz, t_len, _ = qkv3.shape
    assert bsz % BATCH_INTERLEAVE == 0 and t_len % GROUP_ROWS == 0
    rows = GLA_HEADS * GLA_DK
    r = GROUP_ROWS
    full3 = lambda t: (0, 0, 0)
    const2 = lambda t: (0, 0)
    return pl.pallas_call(
        functools.partial(_gla_prompt_kernel, c=c),
        grid=(t_len // r,),
        in_specs=[
            pl.BlockSpec((bsz, r, GLA_QKV), lambda t: (0, t, 0)),
            pl.BlockSpec((bsz, r, LANES), lambda t: (0, t, 0)),
            pl.BlockSpec((bsz, rows, GLA_DV), full3),
            pl.BlockSpec((LANES, GLA_KEY), const2),
            pl.BlockSpec((1, GLA_KEY), const2),
        ],
        out_specs=[
            pl.BlockSpec((bsz, r, GLA_VAL), lambda t: (0, t, 0)),
            pl.BlockSpec((bsz, rows, GLA_DV), full3),
        ],
        out_shape=[
            jax.ShapeDtypeStruct((bsz, t_len, GLA_VAL), BF16),
            jax.ShapeDtypeStruct((bsz, rows, GLA_DV), F32),
        ],
        scratch_shapes=[pltpu.VMEM((bsz, rows, GLA_DV), F32)],
        compiler_params=pltpu.CompilerParams(dimension_semantics=("arbitrary",),
                                             vmem_limit_bytes=VMEM_MIB["gla_prompt"] * 2 ** 20),
        name="gla_prompt",
    )(qkv3, small3, s0, wgg, bgg)


def _gla_sample(qkv3, small3, s0, wgg, bgg):
    bsz, c, _ = qkv3.shape
    rows = GLA_HEADS * GLA_DK
    ns = GROUP_ROWS // c
    grp = lambda g: (g, 0, 0)
    const2 = lambda g: (0, 0)
    return pl.pallas_call(
        _gla_sample_kernel,
        grid=(bsz // ns,),
        in_specs=[
            pl.BlockSpec((ns, c, GLA_QKV), grp),
            pl.BlockSpec((ns, c, LANES), grp),
            pl.BlockSpec((ns, rows, GLA_DV), grp),
            pl.BlockSpec((LANES, GLA_KEY), const2),
            pl.BlockSpec((1, GLA_KEY), const2),
        ],
        out_specs=[
            pl.BlockSpec((ns * c, GLA_VAL), lambda g: (g, 0)),
            pl.BlockSpec((ns, rows, GLA_DV), grp),
        ],
        out_shape=[
            jax.ShapeDtypeStruct((bsz * c, GLA_VAL), BF16),
            jax.ShapeDtypeStruct((bsz, rows, GLA_DV), F32),
        ],
        compiler_params=pltpu.CompilerParams(dimension_semantics=("arbitrary",),
                                             vmem_limit_bytes=VMEM_MIB["gla_sample"] * 2 ** 20),
        name="gla_sample",
    )(qkv3, small3, s0, wgg, bgg)


def _block_unit_lower_inverse(a_list, block):
    r = a_list[0].shape[0]
    row = lax.broadcasted_iota(jnp.int32, (r, r), 0)
    col = lax.broadcasted_iota(jnp.int32, (r, r), 1)
    in_block = lambda n: (row // n) == (col // n)
    base = min(block, INVERSE_BASE_BLOCK)
    a_base = a_list if base == block else [jnp.where(in_block(base), a, 0.0) for a in a_list]
    xs = [jnp.where(row == col, 1.0, 0.0) - a for a in a_base]
    ps = [_dot(a, a) for a in a_base]
    n = 2
    while n < base:
        yield
        last = 2 * n >= base
        ms = [_dot(x if last else jnp.concatenate([p.astype(BF16), x.astype(BF16)], axis=0), p)
              for p, x in zip(ps, xs)]
        ps = [m[:r] for m in ms]
        xs = [x + m[-r:] for x, m in zip(xs, ms)]
        n *= 2
    n = base
    while n < block:
        yield
        between = in_block(2 * n) & jnp.logical_not(in_block(n))
        ts = [_dot(jnp.where(between, a, 0.0), x) for a, x in zip(a_list, xs)]
        yield
        xs = [x - _dot(x, t) for x, t in zip(xs, ts)]
        n *= 2
    return xs


def _gdn_decays(sm, alog_c, dtb_c, same_lt, same):
    r = sm.shape[0]
    heads = range(GDN_HEADS)
    ab = sm.T[SM_A:SM_A + 2 * GDN_HEADS, :]
    g8 = -jnp.exp(alog_c) * _softplus(ab + dtb_c)
    beta_c = _sigmoid(ab).T
    g1, g2 = _split2(g8)
    same_ge = same & jnp.logical_not(same_lt)
    sel = jnp.concatenate([jnp.where(same_ge, 1.0, 0.0).astype(BF16),
                           jnp.where(same, 1.0, 0.0).astype(BF16)], axis=1)
    sums = jnp.dot(jnp.concatenate([g1, g2], axis=0), sel, preferred_element_type=F32)
    sums = sums[:2 * GDN_HEADS] + sums[2 * GDN_HEADS:]
    dec_rows = sums[:, :r]
    dec_c = dec_rows.T
    dend_c = sums[:, r:].T
    return ([dec_c[:, h:h + 1] for h in heads], [dec_rows[h:h + 1, :] for h in heads],
            [dend_c[:, h:h + 1] for h in heads], [beta_c[:, GDN_HEADS + h:GDN_HEADS + h + 1] for h in heads])


def _gdn_head_cols(h):
    return [slice(part * GDN_KEY + h * GDN_DK, part * GDN_KEY + (h + 1) * GDN_DK) for part in range(3)]


def _gdn_prepass(qkv_of_head, sm, alog_c, dtb_c, same_le, same_lt, same, block, heads):
    dcol, drow, dend, beta = ([x[h] for h in heads] for x in _gdn_decays(sm, alog_c, dtb_c, same_lt, same))
    idx = range(len(heads))
    gamma = [jnp.where(same_le, jnp.exp(jnp.where(same_le, dcol[i] - drow[i], 0.0)), 0.0) for i in idx]
    q, k, v = (list(x) for x in zip(*[qkv_of_head(h) for h in heads]))
    r = q[0].shape[0]
    q = [x * lax.rsqrt(jnp.sum(x * x, axis=-1, keepdims=True) + NORM_EPS) * (GDN_DK ** -0.5) for x in q]
    k = [x * lax.rsqrt(jnp.sum(x * x, axis=-1, keepdims=True) + NORM_EPS) for x in k]
    yield
    kb = [k[i] * beta[i] for i in idx]
    vb = [v[i] * beta[i] for i in idx]
    kq = [_dot_nt(jnp.concatenate([kb[i], q[i]], axis=0), k[i]) for i in idx]
    yield
    a_mat = [jnp.where(same_lt, kq[i][:r] * gamma[i], 0.0) for i in idx]
    qk = [(kq[i][r:] * gamma[i]).astype(BF16) for i in idx]
    t_inv = yield from _block_unit_lower_inverse(a_mat, block)
    edec = [jnp.exp(dcol[i]) for i in idx]
    uw = [_dot(t_inv[i], jnp.concatenate([vb[i], kb[i] * edec[i]], axis=1)) for i in idx]
    q_e = [(q[i] * edec[i]).astype(BF16) for i in idx]
    k_end_t = [(k[i] * jnp.exp(dend[i] - dcol[i])).T.astype(BF16) for i in idx]
    yield
    u = [x[:, :GDN_DV] for x in uw]
    w = [x[:, GDN_DV:].astype(BF16) for x in uw]
    return u, w, q_e, qk, k_end_t, dend


def _conv_silu(win, cw_ref):
    conv = win(0) * cw_ref[CONV_WIDTH - 1:CONV_WIDTH, :]
    for j in range(1, CONV_WIDTH):
        conv = conv + win(j) * cw_ref[CONV_WIDTH - 1 - j:CONV_WIDTH - j, :]
    return _silu(conv)

def _gdn_prompt_kernel(qkv_ref, small_ref, cs_ref, s0_ref, cw_ref, alog_ref, dtb_ref,
                       o_ref, sfin_ref, cnew_ref, xbuf, prev, s_scr, *, c):
    t = pl.program_id(0)
    nb, r, _ = qkv_ref.shape
    keep = CONV_WIDTH - 1
    base = SUBLANES

    @pl.when(t == 0)
    def _():
        s_scr[...] = s0_ref[...]
        prev[:, base - keep:base, :] = cs_ref[...]

    same_le, same_lt, same = _block_masks(r, c)
    zeros = jnp.zeros((c, GDN_DV), F32)
    n_sub = r // c

    def one_batch(b, slot, heads, first):
        xb = xbuf.at[slot]
        if first:
            xb[base - keep:base, :] = prev[b, base - keep:base, :]
            xb[base:base + r, :] = qkv_ref[b]
            prev[b, base - keep:base, :] = xb[base + r - keep:base + r, :]

        def qkv_of_head(h):
            return [_conv_silu(lambda j: xb[base - j:base - j + r, cols], cw_ref.at[:, cols])
                    for cols in _gdn_head_cols(h)]

        u, w, q_e, qk, k_end_t, dend = yield from _gdn_prepass(
            qkv_of_head, small_ref[b], alog_ref[...], dtb_ref[...], same_le, same_lt, same, c, heads)
        idx = range(len(heads))
        hs = [slice(h * GDN_DK, (h + 1) * GDN_DK) for h in heads]
        s = [s_scr[b, hs[i], :] for i in idx]
        for ci in range(n_sub):
            rows = slice(ci * c, (ci + 1) * c)
            ws = [_dot(jnp.concatenate([w[i][rows], q_e[i][rows]], axis=0), s[i]) for i in idx]
            yield
            v_new = [u[i][rows] - ws[i][:c] for i in idx]
            padded = [jnp.concatenate([zeros] * ci + [v_new[i]] + [zeros] * (n_sub - 1 - ci), axis=0)
                      for i in idx]
            upd = [_dot(jnp.concatenate([qk[i][rows], k_end_t[i]], axis=0), padded[i]) for i in idx]
            yield
            for i in idx:
                o_ref[b, rows, hs[i]] = (ws[i][c:] + upd[i][:c]).astype(o_ref.dtype)
            s = [s[i] * jnp.exp(dend[i][ci * c:ci * c + 1, :]) + upd[i][c:] for i in idx]
        for i in idx:
            s_scr[b, hs[i], :] = s[i]

    def per_step(step, carry):
        for g in range(0, GDN_HEADS, GDN_HEAD_GROUP):
            heads = tuple(range(g, g + GDN_HEAD_GROUP))
            _interleave([one_batch(step * GDN_BATCH_INTERLEAVE + j, j, heads, g == 0)
                         for j in range(GDN_BATCH_INTERLEAVE)])
        return carry

    lax.fori_loop(0, nb // GDN_BATCH_INTERLEAVE, per_step, None)

    @pl.when(t == pl.num_programs(0) - 1)
    def _():
        sfin_ref[...] = s_scr[...]
        cnew_ref[...] = prev[:, base - keep:base, :]


def _gdn_sample_kernel(qkv_ref, small_ref, cs_ref, s0_ref, cw_ref, alog_ref, dtb_ref,
                       o_ref, sfin_ref, cnew_ref, xbuf):
    ns, c, _ = qkv_ref.shape
    r = ns * c
    keep = CONV_WIDTH - 1
    base = SUBLANES
    xbuf[:, base - keep:base, :] = cs_ref[...]
    xbuf[:, base:base + c, :] = qkv_ref[...]
    conv = _conv_silu(lambda j: xbuf[:, base - j:base - j + c, :], cw_ref).reshape(r, CONV_CH)
    cnew_ref[...] = xbuf[:, base + c - keep:base + c, :]

    same_le, same_lt, same = _block_masks(r, c)
    seq_of_row = lax.broadcasted_iota(jnp.int32, (r, GDN_DV), 0) // c
    (u, w, q_e, qk, k_end_t, dend), = _interleave([_gdn_prepass(
        lambda h: [conv[:, cols] for cols in _gdn_head_cols(h)], small_ref[...].reshape(r, LANES),
        alog_ref[...], dtb_ref[...], same_le, same_lt, same, c, tuple(range(GDN_HEADS)))])
    for h in range(GDN_HEADS):
        hs = slice(h * GDN_DK, (h + 1) * GDN_DK)
        v_parts, o_parts = [], []
        for s in range(ns):
            rows = slice(s * c, (s + 1) * c)
            ws = _dot(jnp.concatenate([w[h][rows], q_e[h][rows]], axis=0), s0_ref[s, hs, :])
            v_parts.append(u[h][rows] - ws[:c])
            o_parts.append(ws[c:])
        v_new = jnp.concatenate(v_parts, axis=0)
        o = jnp.concatenate(o_parts, axis=0) + _dot(qk[h], v_new)
        o_ref[:, h * GDN_DV:(h + 1) * GDN_DV] = o.astype(o_ref.dtype)
        for s in range(ns):
            upd = _dot(k_end_t[h], jnp.where(seq_of_row == s, v_new, 0.0))
            sfin_ref[s, hs, :] = s0_ref[s, hs, :] * jnp.exp(dend[h][s * c:s * c + 1, :]) + upd


def _gdn_prompt(qkv3, small3, conv_state, s0, conv_w, alog_v, dtb_v, c):
    bsz, t_len, _ = qkv3.shape
    assert bsz % GDN_BATCH_INTERLEAVE == 0 and t_len % GROUP_ROWS == 0 and GDN_HEADS % GDN_HEAD_GROUP == 0
    rows = GDN_HEADS * GDN_DK
    keep = CONV_WIDTH - 1
    r = GROUP_ROWS
    full3 = lambda t: (0, 0, 0)
    const2 = lambda t: (0, 0)
    return pl.pallas_call(
        functools.partial(_gdn_prompt_kernel, c=c),
        grid=(t_len // r,),
        in_specs=[
            pl.BlockSpec((bsz, r, CONV_CH), lambda t: (0, t, 0)),
            pl.BlockSpec((bsz, r, LANES), lambda t: (0, t, 0)),
            pl.BlockSpec((bsz, keep, CONV_CH), full3),
            pl.BlockSpec((bsz, rows, GDN_DV), full3),
            pl.BlockSpec((CONV_WIDTH, CONV_CH), const2),
            pl.BlockSpec((2 * GDN_HEADS, 1), const2),
            pl.BlockSpec((2 * GDN_HEADS, 1), const2),
        ],
        out_specs=[
            pl.BlockSpec((bsz, r, GDN_VAL), lambda t: (0, t, 0)),
            pl.BlockSpec((bsz, rows, GDN_DV), full3),
            pl.BlockSpec((bsz, keep, CONV_CH), full3),
        ],
        out_shape=[
            jax.ShapeDtypeStruct((bsz, t_len, GDN_VAL), BF16),
            jax.ShapeDtypeStruct((bsz, rows, GDN_DV), F32),
            jax.ShapeDtypeStruct((bsz, keep, CONV_CH), F32),
        ],
        scratch_shapes=[pltpu.VMEM((GDN_BATCH_INTERLEAVE, SUBLANES + r, CONV_CH), F32),
                        pltpu.VMEM((bsz, SUBLANES, CONV_CH), F32),
                        pltpu.VMEM((bsz, rows, GDN_DV), F32)],
        compiler_params=pltpu.CompilerParams(dimension_semantics=("arbitrary",),
                                             vmem_limit_bytes=VMEM_MIB["gdn_prompt"] * 2 ** 20),
        name="gdn_prompt",
    )(qkv3, small3, conv_state, s0, conv_w, alog_v, dtb_v)


def _gdn_sample(qkv3, small3, conv_state, s0, conv_w, alog_v, dtb_v):
    bsz, c, _ = qkv3.shape
    rows = GDN_HEADS * GDN_DK
    keep = CONV_WIDTH - 1
    ns = GROUP_ROWS // c
    grp = lambda g: (g, 0, 0)
    const2 = lambda g: (0, 0)
    return pl.pallas_call(
        _gdn_sample_kernel,
        grid=(bsz // ns,),
        in_specs=[
            pl.BlockSpec((ns, c, CONV_CH), grp),
            pl.BlockSpec((ns, c, LANES), grp),
            pl.BlockSpec((ns, keep, CONV_CH), grp),
            pl.BlockSpec((ns, rows, GDN_DV), grp),
            pl.BlockSpec((CONV_WIDTH, CONV_CH), const2),
            pl.BlockSpec((2 * GDN_HEADS, 1), const2),
            pl.BlockSpec((2 * GDN_HEADS, 1), const2),
        ],
        out_specs=[
            pl.BlockSpec((ns * c, GDN_VAL), lambda g: (g, 0)),
            pl.BlockSpec((ns, rows, GDN_DV), grp),
            pl.BlockSpec((ns, keep, CONV_CH), grp),
        ],
        out_shape=[
            jax.ShapeDtypeStruct((bsz * c, GDN_VAL), BF16),
            jax.ShapeDtypeStruct((bsz, rows, GDN_DV), F32),
            jax.ShapeDtypeStruct((bsz, keep, CONV_CH), F32),
        ],
        scratch_shapes=[pltpu.VMEM((ns, 2 * SUBLANES, CONV_CH), F32)],
        compiler_params=pltpu.CompilerParams(dimension_semantics=("arbitrary",),
                                             vmem_limit_bytes=VMEM_MIB["gdn_sample"] * 2 ** 20),
        name="gdn_sample",
    )(qkv3, small3, conv_state, s0, conv_w, alog_v, dtb_v)


def _head_norm_gate(o, silu_z, w):
    parts = []
    for h in range(o.shape[-1] // LANES):
        oh = o[:, h * LANES:(h + 1) * LANES]
        parts.append(oh * lax.rsqrt(jnp.mean(oh * oh, axis=-1, keepdims=True) + NORM_EPS) * w)
    return jnp.concatenate(parts, axis=-1) * silu_z


def _out_kernel(oa_ref, ob_ref, ga_ref, gb_ref, za_ref, zb_ref, x_ref, p_ref, anw_ref, bnw_ref,
                wua_ref, wub_ref, wout_ref, wpg_ref, wp_ref, fnw_ref, y_ref):
    f32 = lambda ref: ref[...].astype(F32)
    ya = _dot(_head_norm_gate(f32(oa_ref), f32(za_ref), anw_ref[...]), wua_ref[...])
    yb = _dot(_head_norm_gate(f32(ob_ref), f32(zb_ref), bnw_ref[...]), wub_ref[...])
    merged = f32(ga_ref) * ya + f32(gb_ref) * yb
    h1 = x_ref[...] + _dot(merged, wout_ref[...])
    h2 = h1 + _sigmoid_tanh(_dot(h1, wpg_ref[...])) * _dot(p_ref[...], wp_ref[...])
    y_ref[...] = h2 * lax.rsqrt(jnp.mean(h2 * h2, axis=-1, keepdims=True) + NORM_EPS) * fnw_ref[...]


def _out_stage(o_a, o_b, gz, x2d, p2d, anw, bnw, wua, wub, wout, wpg, wp, fnw):
    n = x2d.shape[0]
    tm = min(ROW_TILE, n)
    const = lambda i: (0, 0)
    return pl.pallas_call(
        _out_kernel,
        grid=(n // tm,),
        in_specs=[
            pl.BlockSpec((tm, GLA_VAL), lambda i: (i, 0)),
            pl.BlockSpec((tm, GDN_VAL), lambda i: (i, 0)),
            pl.BlockSpec((tm, D_MODEL), lambda i: (i, GZ_GATE_A // D_MODEL)),
            pl.BlockSpec((tm, D_MODEL), lambda i: (i, GZ_GATE_B // D_MODEL)),
            pl.BlockSpec((tm, GLA_VAL), lambda i: (i, GZ_Z_A // GLA_VAL)),
            pl.BlockSpec((tm, GDN_VAL), lambda i: (i, GZ_Z_B // GDN_VAL)),
            pl.BlockSpec((tm, D_MODEL), lambda i: (i, 0)),
            pl.BlockSpec((tm, PLE_DIM), lambda i: (i, 0)),
            pl.BlockSpec((1, GLA_DV), const),
            pl.BlockSpec((1, GDN_DV), const),
            pl.BlockSpec((GLA_VAL, D_MODEL), const),
            pl.BlockSpec((GDN_VAL, D_MODEL), const),
            pl.BlockSpec((D_MODEL, D_MODEL), const),
            pl.BlockSpec((D_MODEL, D_MODEL), const),
            pl.BlockSpec((PLE_DIM, D_MODEL), const),
            pl.BlockSpec((1, D_MODEL), const),
        ],
        out_specs=pl.BlockSpec((tm, D_MODEL), lambda i: (i, 0)),
        out_shape=jax.ShapeDtypeStruct((n, D_MODEL), F32),
        compiler_params=pltpu.CompilerParams(dimension_semantics=("arbitrary",),
                                             vmem_limit_bytes=VMEM_MIB["out_stage"] * 2 ** 20),
        name="out_stage",
    )(o_a, o_b, gz, gz, gz, gz, x2d, p2d, anw, bnw, wua, wub, wout, wpg, wp, fnw)


def _in_offsets():
    offs = [0]
    for s in IN_SPLITS:
        offs.append(offs[-1] + s)
    return offs


def _regroup_kernel(wt_ref, o_ref):
    offs = _in_offsets()
    (q_a, _, _, g_a, z_a, qkv_b, a_b, _, z_b, gate_a, gate_b, end) = offs
    piece = lambda lo, hi: wt_ref[lo:hi, :].astype(BF16)
    o_ref[P_GLA_QKV:P_GLA_QKV + GLA_QKV, :] = piece(q_a, g_a)
    o_ref[P_QKV_B:P_QKV_B + CONV_CH, :] = piece(qkv_b, a_b)
    small = jnp.concatenate([wt_ref[g_a:z_a, :], wt_ref[a_b:z_b, :],
                             jnp.zeros((LANES - (z_a - g_a) - (z_b - a_b), wt_ref.shape[1]), F32)], axis=0)
    o_ref[P_SMALL:P_SMALL + LANES, :] = small.astype(BF16)
    o_ref[P_GATES + GZ_GATE_A:P_GATES + GZ_GATE_A + D_MODEL, :] = piece(gate_a, gate_b)
    o_ref[P_GATES + GZ_GATE_B:P_GATES + GZ_GATE_B + D_MODEL, :] = piece(gate_b, end)
    o_ref[P_GATES + GZ_Z_A:P_GATES + GZ_Z_A + GLA_VAL, :] = piece(z_a, qkv_b)
    o_ref[P_GATES + GZ_Z_B:P_GATES + GZ_Z_B + GDN_VAL, :] = piece(z_b, gate_a)


def _regroup_w_in(w_in_t):
    cols = 256
    return pl.pallas_call(
        _regroup_kernel,
        grid=(D_MODEL // cols,),
        in_specs=[pl.BlockSpec((None, w_in_t.shape[1], cols), lambda i: (0, 0, i))],
        out_specs=pl.BlockSpec((P_COLS, cols), lambda i: (0, i)),
        out_shape=jax.ShapeDtypeStruct((P_COLS, D_MODEL), BF16),
        compiler_params=pltpu.CompilerParams(dimension_semantics=("arbitrary",),
                                             vmem_limit_bytes=VMEM_MIB["regroup_w_in"] * 2 ** 20),
        name="regroup_w_in",
    )(w_in_t)


def _head_param_col(v):
    return jnp.zeros((2 * GDN_HEADS, 1), F32).at[:GDN_HEADS, 0].set(v.astype(F32))


def _trunk(x, p, s_gla, s_gdn, conv_state, wts):
    bsz, t_len, _ = x.shape
    c = min(PROMPT_CHUNK, t_len)
    n = bsz * t_len
    x2d = x.reshape(n, D_MODEL)
    p2d = p.reshape(n, PLE_DIM)
    gla_in, gdn_in, small, gz = _inproj(x2d, wts["norm_w"], wts["w_in_r"])
    gla_in3 = gla_in.reshape(bsz, t_len, GLA_QKV)
    gdn_in3 = gdn_in.reshape(bsz, t_len, CONV_CH)
    small3 = small.reshape(bsz, t_len, LANES)
    s_gla2 = s_gla.reshape(bsz, GLA_HEADS * GLA_DK, GLA_DV)
    s_gdn2 = s_gdn.reshape(bsz, GDN_HEADS * GDN_DK, GDN_DV)
    if t_len % GROUP_ROWS == 0:
        o_a, gla_fin = _gla_prompt(gla_in3, small3, s_gla2, wts["wgg"], wts["bgg"], c)
        o_b, gdn_fin, conv_new = _gdn_prompt(gdn_in3, small3, conv_state, s_gdn2, wts["conv_w"],
                                             wts["alog_v"], wts["dtb_v"], c)
    else:
        assert GROUP_ROWS % t_len == 0 and bsz % (GROUP_ROWS // t_len) == 0 and t_len >= CONV_WIDTH - 1
        o_a, gla_fin = _gla_sample(gla_in3, small3, s_gla2, wts["wgg"], wts["bgg"])
        o_b, gdn_fin, conv_new = _gdn_sample(gdn_in3, small3, conv_state, s_gdn2, wts["conv_w"],
                                             wts["alog_v"], wts["dtb_v"])
    y = _out_stage(o_a.reshape(n, GLA_VAL), o_b.reshape(n, GDN_VAL), gz, x2d, p2d, wts["anw"], wts["bnw"],
                   wts["wua"], wts["wub"], wts["wout"], wts["wpg"], wts["wp"], wts["fnw"])
    return (y.reshape(bsz, t_len, D_MODEL),
            gla_fin.reshape(1, bsz, GLA_HEADS, GLA_DK, GLA_DV),
            gdn_fin.reshape(1, bsz, GDN_HEADS, GDN_DK, GDN_DV),
            conv_new.reshape(1, bsz, CONV_WIDTH - 1, CONV_CH))


def kernel(x_prompt, x_sample, state_gla, state_gdn, state_conv, p_prompt, p_sample, norm_w, w_in,
           w_gla_gate, b_gla_gate, gla_norm_w, conv_w, gdn_a_log, gdn_dt_bias, gdn_norm_w,
           w_up_gla, w_up_gdn, w_out, w_ple_gate, w_ple, final_norm_w):
    wgg = jnp.zeros((LANES, GLA_KEY), F32).at[SM_G:SM_G + GLA_GATE_RANK].set(w_gla_gate[0]).astype(BF16)
    wts = {
        "norm_w": norm_w[0].reshape(1, D_MODEL),
        "w_in_r": _regroup_w_in(jnp.swapaxes(w_in, 1, 2)),
        "wgg": wgg,
        "bgg": b_gla_gate[0].reshape(1, GLA_KEY),
        "conv_w": conv_w[0],
        "alog_v": _head_param_col(gdn_a_log[0]),
        "dtb_v": _head_param_col(gdn_dt_bias[0]),
        "anw": gla_norm_w[0].reshape(1, GLA_DV),
        "bnw": gdn_norm_w[0].reshape(1, GDN_DV),
        "wua": w_up_gla[0].astype(BF16),
        "wub": w_up_gdn[0].astype(BF16),
        "wout": w_out[0].astype(BF16),
        "wpg": w_ple_gate[0].astype(BF16),
        "wp": w_ple[0].astype(BF16),
        "fnw": final_norm_w.reshape(1, D_MODEL),
    }
    bsz = x_prompt.shape[0]
    dt = x_prompt.dtype
    y_p, gla_p, gdn_p, conv_p = _trunk(
        x_prompt, p_prompt[0],
        jnp.zeros((bsz, GLA_HEADS, GLA_DK, GLA_DV), dt), jnp.zeros((bsz, GDN_HEADS, GDN_DK, GDN_DV), dt),
        jnp.zeros((bsz, CONV_WIDTH - 1, CONV_CH), dt), wts)
    y_s, gla_s, gdn_s, conv_s = _trunk(x_sample, p_sample[0], state_gla[0], state_gdn[0], state_conv[0], wts)
    return (y_p, y_s, gla_p, gdn_p, conv_p, gla_s, gdn_s, conv_s)
```

```python
import functools

import jax
import jax.numpy as jnp
from jax import lax
from jax.experimental import pallas as pl
from jax.experimental.pallas import tpu as pltpu

F32 = jnp.float32
BF16 = jnp.bfloat16

D_MODEL = 1024
PLE_DIM = 256
NORM_EPS = 1e-6
GLA_HEADS = 4
GLA_DK = 64
GLA_DV = 128
GLA_KEY = GLA_HEADS * GLA_DK
GLA_VAL = GLA_HEADS * GLA_DV
GLA_GATE_RANK = 16
GLA_GATE_TEMP = 16.0
GDN_HEADS = 4
GDN_DK = 128
GDN_DV = 128
GDN_KEY = GDN_HEADS * GDN_DK
GDN_VAL = GDN_HEADS * GDN_DV
CONV_WIDTH = 4
CONV_CH = 2 * GDN_KEY + GDN_VAL
IN_SPLITS = (GLA_KEY, GLA_KEY, GLA_VAL, GLA_GATE_RANK, GLA_VAL, CONV_CH, GDN_HEADS, GDN_HEADS,
             GDN_VAL, D_MODEL, D_MODEL)

LANES = 128
SUBLANES = 8

GLA_QKV = 2 * GLA_KEY + GLA_VAL
P_GLA_QKV = 0
P_QKV_B = P_GLA_QKV + GLA_QKV
P_SMALL = P_QKV_B + CONV_CH
P_GATES = P_SMALL + LANES
GZ_COLS = 2 * D_MODEL + GLA_VAL + GDN_VAL
P_COLS = P_GATES + GZ_COLS
GZ_GATE_A = 0
GZ_GATE_B = D_MODEL
GZ_Z_A = 2 * D_MODEL
GZ_Z_B = 2 * D_MODEL + GLA_VAL
SM_G = 0
SM_A = GLA_GATE_RANK
SM_B = GLA_GATE_RANK + GDN_HEADS

PROMPT_CHUNK = 64
GROUP_ROWS = 128
BATCH_INTERLEAVE = 8
GDN_BATCH_INTERLEAVE = 8
GDN_HEAD_GROUP = 4
INVERSE_BASE_BLOCK = 16
ROW_TILE = 512
VMEM_MIB = {"regroup_w_in": 40, "inproj": 40, "gla_prompt": 40, "gla_sample": 40, "gdn_prompt": 48,
            "gdn_sample": 40, "out_stage": 40}


def _dot(a, b):
    return jnp.dot(a.astype(BF16), b.astype(BF16), preferred_element_type=F32)


def _dot_nt(a, b):
    return lax.dot_general(a.astype(BF16), b.astype(BF16), (((1,), (1,)), ((), ())),
                           preferred_element_type=F32)


def _split2(x):
    h1 = x.astype(BF16)
    return h1, (x - h1.astype(F32)).astype(BF16)


def _cumsum_rows(tri, x):
    x1, x2 = _split2(x)
    d = functools.partial(jnp.dot, preferred_element_type=F32)
    return d(tri, x1) + d(tri, x2)


def _softplus(x):
    return jnp.maximum(x, 0.0) + jnp.log(1.0 + jnp.exp(-jnp.abs(x)))


def _sigmoid(x):
    return 0.5 * jnp.tanh(0.5 * x) + 0.5


def _silu(x):
    half = 0.5 * x
    return half * jnp.tanh(half) + half


def _interleave(emitters):
    results = [None] * len(emitters)
    live = list(range(len(emitters)))
    while live:
        for i in list(live):
            try:
                next(emitters[i])
            except StopIteration as stop:
                results[i] = stop.value
                live.remove(i)
    return results


def _block_masks(r, block):
    row = lax.broadcasted_iota(jnp.int32, (r, r), 0)
    col = lax.broadcasted_iota(jnp.int32, (r, r), 1)
    same = (row // block) == (col // block)
    return same & (row >= col), same & (row > col), same


INPROJ_COL_STEP = 512


def _inproj_kernel(x_ref, nw_ref, w_ref, gla_ref, gdn_ref, small_ref, gz_ref):
    x = x_ref[...]
    xn = x * lax.rsqrt(jnp.mean(x * x, axis=-1, keepdims=True) + NORM_EPS) * nw_ref[...]
    xb = xn.astype(BF16)

    def emit(o_ref, w0, width, act=None, o0=0):
        for c0 in range(0, width, INPROJ_COL_STEP):
            c1 = min(c0 + INPROJ_COL_STEP, width)
            res = _dot_nt(xb, w_ref[w0 + c0:w0 + c1, :])
            o_ref[:, o0 + c0:o0 + c1] = (res if act is None else act(res)).astype(o_ref.dtype)

    emit(gla_ref, P_GLA_QKV, GLA_QKV)
    emit(gdn_ref, P_QKV_B, CONV_CH)
    emit(small_ref, P_SMALL, LANES)
    emit(gz_ref, P_GATES + GZ_GATE_A, 2 * D_MODEL, act=_sigmoid, o0=GZ_GATE_A)
    emit(gz_ref, P_GATES + GZ_Z_A, GLA_VAL + GDN_VAL, act=_silu, o0=GZ_Z_A)


def _inproj(x2d, norm_w, w_in_r):
    n = x2d.shape[0]
    tm = min(ROW_TILE, n)
    rows = lambda i: (i, 0)
    return pl.pallas_call(
        _inproj_kernel,
        grid=(n // tm,),
        in_specs=[
            pl.BlockSpec((tm, D_MODEL), rows),
            pl.BlockSpec((1, D_MODEL), lambda i: (0, 0)),
            pl.BlockSpec((P_COLS, D_MODEL), lambda i: (0, 0), pipeline_mode=pl.Buffered(1)),
        ],
        out_specs=[
            pl.BlockSpec((tm, GLA_QKV), rows),
            pl.BlockSpec((tm, CONV_CH), rows),
            pl.BlockSpec((tm, LANES), rows),
            pl.BlockSpec((tm, GZ_COLS), rows),
        ],
        out_shape=[
            jax.ShapeDtypeStruct((n, GLA_QKV), F32),
            jax.ShapeDtypeStruct((n, CONV_CH), F32),
            jax.ShapeDtypeStruct((n, LANES), F32),
            jax.ShapeDtypeStruct((n, GZ_COLS), BF16),
        ],
        compiler_params=pltpu.CompilerParams(dimension_semantics=("arbitrary",),
                                             vmem_limit_bytes=VMEM_MIB["inproj"] * 2 ** 20),
        name="inproj",
    )(x2d, norm_w, w_in_r)


GLA_PAIRS = GLA_HEADS // 2
GLA_SUB_BLOCK = 16


def _gla_att_levels(r, block):
    row = lax.broadcasted_iota(jnp.int32, (r, r), 0)
    col = lax.broadcasted_iota(jnp.int32, (r, r), 1)
    sub = min(block, GLA_SUB_BLOCK)
    levels = [(None, ((row // sub) == (col // sub)) & (row >= col))]
    half = sub
    while half < block:
        levels.append((half, ((row // (2 * half)) == (col // (2 * half)))
                       & ((row // half) % 2 == 1) & ((col // half) % 2 == 0)))
        half *= 2
    return levels


def _rows_at(x, n, offset):
    return jnp.concatenate([jnp.broadcast_to(x[i + offset:i + offset + 1, :], (n, x.shape[1]))
                            for i in range(0, x.shape[0], n)], axis=0)


def _gla_prepass(q, k, v, sm, wgg, bgg, same_le, same, levels, block):
    r = q.shape[0]
    heads = range(GLA_HEADS)
    pairs = range(GLA_PAIRS)
    pre = _dot(sm, wgg) + bgg
    yield
    gk = (jnp.minimum(pre, 0.0) - jnp.log(1.0 + jnp.exp(-jnp.abs(pre)))) * (1.0 / GLA_GATE_TEMP)
    sums = _cumsum_rows(jnp.concatenate([jnp.where(same_le, 1.0, 0.0).astype(BF16),
                                         jnp.where(same, 1.0, 0.0).astype(BF16)], axis=0), gk)
    yield
    bcum = sums[:r]
    bend = sums[r:]
    bex = bcum - gk
    scale = GLA_DK ** -0.5
    q_e = q * jnp.exp(bcum) * scale
    lane = lax.broadcasted_iota(jnp.int32, (r, LANES), 1)
    in_head = [lane < GLA_DK, lane >= GLA_DK]
    pl_ = [slice(p * LANES, (p + 1) * LANES) for p in pairs]
    head_only = lambda x, h: jnp.where(in_head[h % 2], x[:, pl_[h // 2]], 0.0)
    qm = [head_only(q_e, h) for h in heads]
    sub = min(block, GLA_SUB_BLOCK)
    start = _rows_at(bex, sub, 0)
    groups = {None: (q * jnp.exp(bcum - start) * scale, [(levels[0][1], k * jnp.exp(start - bcum))])}
    for half, mask in levels[1:]:
        key = None if half == sub else half
        if key not in groups:
            groups[key] = (q * jnp.exp(bcum - _rows_at(bex, half, 0)) * scale, [])
        groups[key][1].append((mask, k * jnp.exp(_rows_at(bcum, half, half - 1) - bcum)))
    att = [jnp.zeros((r, r), F32)] * GLA_HEADS
    for q_l, parts in groups.values():
        for h in heads:
            keys = [k_l[:, pl_[h // 2]].astype(BF16) for _, k_l in parts]
            prod = _dot_nt(head_only(q_l, h), keys[0] if len(keys) == 1 else jnp.concatenate(keys, axis=0))
            for i, (mask, _) in enumerate(parts):
                att[h] = jnp.where(mask, prod[:, i * r:(i + 1) * r], att[h])
    yield
    o_intra = [_dot(att[h], v[h]) for h in heads]
    k_end = k * jnp.exp(bend - bcum)
    k_end_t = [k_end[:, pl_[p]].T for p in pairs]
    bend_t = [bend[:, pl_[p]].T for p in pairs]
    yield
    return qm, o_intra, k_end_t, bend_t


def _gla_prompt_kernel(qkv_ref, small_ref, s0_ref, wgg_ref, bgg_ref, o_ref, sfin_ref, s_scr, *, c):
    t = pl.program_id(0)
    nb, r, _ = qkv_ref.shape

    @pl.when(t == 0)
    def _():
        s_scr[...] = s0_ref[...]

    same_le, _, same = _block_masks(r, c)
    levels = _gla_att_levels(r, c)
    zeros = jnp.zeros((c, GLA_DV), F32)
    n_sub = r // c
    heads = range(GLA_HEADS)
    pairs = range(GLA_PAIRS)

    def one_batch(b):
        q = qkv_ref[b, :, 0:GLA_KEY]
        k = qkv_ref[b, :, GLA_KEY:2 * GLA_KEY]
        v = [qkv_ref[b, :, 2 * GLA_KEY + h * GLA_DV:2 * GLA_KEY + (h + 1) * GLA_DV] for h in heads]
        qm, o_intra, k_end_t, bend_t = yield from _gla_prepass(
            q, k, v, small_ref[b], wgg_ref[...], bgg_ref[...], same_le, same, levels, c)
        s = [s_scr[b, p * LANES:(p + 1) * LANES, :] for p in pairs]
        for i in range(n_sub):
            rows = slice(i * c, (i + 1) * c)
            ws = [_dot(jnp.concatenate([qm[2 * p][rows], qm[2 * p + 1][rows]], axis=0), s[p]) for p in pairs]
            padded = [jnp.concatenate([zeros] * i + [v[h][rows]] + [zeros] * (n_sub - 1 - i), axis=0)
                      for h in heads]
            upd = [_dot(k_end_t[h // 2][(h % 2) * GLA_DK:(h % 2 + 1) * GLA_DK, :], padded[h]) for h in heads]
            yield
            for h in heads:
                o_ref[b, rows, h * GLA_DV:(h + 1) * GLA_DV] = (
                    o_intra[h][rows] + ws[h // 2][(h % 2) * c:(h % 2 + 1) * c]).astype(o_ref.dtype)
            s = [s[p] * jnp.exp(bend_t[p][:, i * c:i * c + 1])
                 + jnp.concatenate([upd[2 * p], upd[2 * p + 1]], axis=0) for p in pairs]
        for p in pairs:
            s_scr[b, p * LANES:(p + 1) * LANES, :] = s[p]

    def per_step(i, carry):
        _interleave([one_batch(i * BATCH_INTERLEAVE + j) for j in range(BATCH_INTERLEAVE)])
        return carry

    lax.fori_loop(0, nb // BATCH_INTERLEAVE, per_step, None)

    @pl.when(t == pl.num_programs(0) - 1)
    def _():
        sfin_ref[...] = s_scr[...]


def _gla_sample_kernel(qkv_ref, small_ref, s0_ref, wgg_ref, bgg_ref, o_ref, sfin_ref):
    ns, c, _ = qkv_ref.shape
    r = ns * c
    heads = range(GLA_HEADS)
    same_le, _, same = _block_masks(r, c)
    levels = _gla_att_levels(r, c)
    q = qkv_ref[:, :, 0:GLA_KEY].reshape(r, GLA_KEY)
    k = qkv_ref[:, :, GLA_KEY:2 * GLA_KEY].reshape(r, GLA_KEY)
    v = [qkv_ref[:, :, 2 * GLA_KEY + h * GLA_DV:2 * GLA_KEY + (h + 1) * GLA_DV].reshape(r, GLA_DV)
         for h in heads]
    (qm, o_intra, k_end_t, bend_t), = _interleave([_gla_prepass(
        q, k, v, small_ref[...].reshape(r, LANES), wgg_ref[...], bgg_ref[...], same_le, same, levels, c)])
    seq_of_row = lax.broadcasted_iota(jnp.int32, (r, GLA_DV), 0) // c
    for p in range(GLA_PAIRS):
        ps = slice(p * LANES, (p + 1) * LANES)
        inter = [[], []]
        for s in range(ns):
            rows = slice(s * c, (s + 1) * c)
            ws = _dot(jnp.concatenate([qm[2 * p][rows], qm[2 * p + 1][rows]], axis=0), s0_ref[s, ps, :])
            inter[0].append(ws[:c])
            inter[1].append(ws[c:])
        for hh in range(2):
            h = 2 * p + hh
            o = o_intra[h] + jnp.concatenate(inter[hh], axis=0)
            o_ref[:, h * GLA_DV:(h + 1) * GLA_DV] = o.astype(o_ref.dtype)
        for s in range(ns):
            upd = [_dot(k_end_t[p][hh * GLA_DK:(hh + 1) * GLA_DK, :],
                        jnp.where(seq_of_row == s, v[2 * p + hh], 0.0)) for hh in range(2)]
            sfin_ref[s, ps, :] = (s0_ref[s, ps, :] * jnp.exp(bend_t[p][:, s * c:s * c + 1])
                                  + jnp.concatenate(upd, axis=0))


def _gla_prompt(qkv3, small3, s0, wgg, bgg, c):
    bsz, t_len, _ = qkv3.shape
    assert bsz % BATCH_INTERLEAVE == 0 and t_len % GROUP_ROWS == 0
    rows = GLA_HEADS * GLA_DK
    r = GROUP_ROWS
    full3 = lambda t: (0, 0, 0)
    const2 = lambda t: (0, 0)
    return pl.pallas_call(
        functools.partial(_gla_prompt_kernel, c=c),
        grid=(t_len // r,),
        in_specs=[
            pl.BlockSpec((bsz, r, GLA_QKV), lambda t: (0, t, 0)),
            pl.BlockSpec((bsz, r, LANES), lambda t: (0, t, 0)),
            pl.BlockSpec((bsz, rows, GLA_DV), full3),
            pl.BlockSpec((LANES, GLA_KEY), const2),
            pl.BlockSpec((1, GLA_KEY), const2),
        ],
        out_specs=[
            pl.BlockSpec((bsz, r, GLA_VAL), lambda t: (0, t, 0)),
            pl.BlockSpec((bsz, rows, GLA_DV), full3),
        ],
        out_shape=[
            jax.ShapeDtypeStruct((bsz, t_len, GLA_VAL), BF16),
            jax.ShapeDtypeStruct((bsz, rows, GLA_DV), F32),
        ],
        scratch_shapes=[pltpu.VMEM((bsz, rows, GLA_DV), F32)],
        compiler_params=pltpu.CompilerParams(dimension_semantics=("arbitrary",),
                                             vmem_limit_bytes=VMEM_MIB["gla_prompt"] * 2 ** 20),
        name="gla_prompt",
    )(qkv3, small3, s0, wgg, bgg)


def _gla_sample(qkv3, small3, s0, wgg, bgg):
    bsz, c, _ = qkv3.shape
    rows = GLA_HEADS * GLA_DK
    ns = GROUP_ROWS // c
    grp = lambda g: (g, 0, 0)
    const2 = lambda g: (0, 0)
    return pl.pallas_call(
        _gla_sample_kernel,
        grid=(bsz // ns,),
        in_specs=[
            pl.BlockSpec((ns, c, GLA_QKV), grp),
            pl.BlockSpec((ns, c, LANES), grp),
            pl.BlockSpec((ns, rows, GLA_DV), grp),
            pl.BlockSpec((LANES, GLA_KEY), const2),
            pl.BlockSpec((1, GLA_KEY), const2),
        ],
        out_specs=[
            pl.BlockSpec((ns * c, GLA_VAL), lambda g: (g, 0)),
            pl.BlockSpec((ns, rows, GLA_DV), grp),
        ],
        out_shape=[
            jax.ShapeDtypeStruct((bsz * c, GLA_VAL), BF16),
            jax.ShapeDtypeStruct((bsz, rows, GLA_DV), F32),
        ],
        compiler_params=pltpu.CompilerParams(dimension_semantics=("arbitrary",),
                                             vmem_limit_bytes=VMEM_MIB["gla_sample"] * 2 ** 20),
        name="gla_sample",
    )(qkv3, small3, s0, wgg, bgg)


def _block_unit_lower_inverse(a_list, block):
    r = a_list[0].shape[0]
    row = lax.broadcasted_iota(jnp.int32, (r, r), 0)
    col = lax.broadcasted_iota(jnp.int32, (r, r), 1)
    in_block = lambda n: (row // n) == (col // n)
    base = min(block, INVERSE_BASE_BLOCK)
    a_base = a_list if base == block else [jnp.where(in_block(base), a, 0.0) for a in a_list]
    xs = [jnp.where(row == col, 1.0, 0.0) - a for a in a_base]
    ps = [_dot(a, a) for a in a_base]
    n = 2
    while n < base:
        yield
        last = 2 * n >= base
        ms = [_dot(x if last else jnp.concatenate([p.astype(BF16), x.astype(BF16)], axis=0), p)
              for p, x in zip(ps, xs)]
        ps = [m[:r] for m in ms]
        xs = [x + m[-r:] for x, m in zip(xs, ms)]
        n *= 2
    n = base
    while n < block:
        yield
        between = in_block(2 * n) & jnp.logical_not(in_block(n))
        ts = [_dot(jnp.where(between, a, 0.0), x) for a, x in zip(a_list, xs)]
        yield
        xs = [x - _dot(x, t) for x, t in zip(xs, ts)]
        n *= 2
    return xs


def _gdn_decays(sm, alog_c, dtb_c, same_lt, same):
    r = sm.shape[0]
    heads = range(GDN_HEADS)
    ab = sm.T[SM_A:SM_A + 2 * GDN_HEADS, :]
    g8 = -jnp.exp(alog_c) * _softplus(ab + dtb_c)
    beta_c = _sigmoid(ab).T
    g1, g2 = _split2(g8)
    same_ge = same & jnp.logical_not(same_lt)
    sel = jnp.concatenate([jnp.where(same_ge, 1.0, 0.0).astype(BF16),
                           jnp.where(same, 1.0, 0.0).astype(BF16)], axis=1)
    sums = jnp.dot(jnp.concatenate([g1, g2], axis=0), sel, preferred_element_type=F32)
    sums = sums[:2 * GDN_HEADS] + sums[2 * GDN_HEADS:]
    dec_rows = sums[:, :r]
    dec_c = dec_rows.T
    dend_c = sums[:, r:].T
    return ([dec_c[:, h:h + 1] for h in heads], [dec_rows[h:h + 1, :] for h in heads],
            [dend_c[:, h:h + 1] for h in heads], [beta_c[:, GDN_HEADS + h:GDN_HEADS + h + 1] for h in heads])


def _gdn_head_cols(h):
    return [slice(part * GDN_KEY + h * GDN_DK, part * GDN_KEY + (h + 1) * GDN_DK) for part in range(3)]


def _gdn_prepass(qkv_of_head, sm, alog_c, dtb_c, same_le, same_lt, same, block, heads):
    dcol, drow, dend, beta = ([x[h] for h in heads] for x in _gdn_decays(sm, alog_c, dtb_c, same_lt, same))
    idx = range(len(heads))
    gamma = [jnp.where(same_le, jnp.exp(jnp.where(same_le, dcol[i] - drow[i], 0.0)), 0.0) for i in idx]
    q, k, v = (list(x) for x in zip(*[qkv_of_head(h) for h in heads]))
    r = q[0].shape[0]
    q = [x * lax.rsqrt(jnp.sum(x * x, axis=-1, keepdims=True) + NORM_EPS) * (GDN_DK ** -0.5) for x in q]
    k = [x * lax.rsqrt(jnp.sum(x * x, axis=-1, keepdims=True) + NORM_EPS) for x in k]
    yield
    kb = [k[i] * beta[i] for i in idx]
    vb = [v[i] * beta[i] for i in idx]
    kq = [_dot_nt(jnp.concatenate([kb[i], q[i]], axis=0), k[i]) for i in idx]
    yield
    a_mat = [jnp.where(same_lt, kq[i][:r] * gamma[i], 0.0) for i in idx]
    qk = [(kq[i][r:] * gamma[i]).astype(BF16) for i in idx]
    t_inv = yield from _block_unit_lower_inverse(a_mat, block)
    edec = [jnp.exp(dcol[i]) for i in idx]
    uw = [_dot(t_inv[i], jnp.concatenate([vb[i], kb[i] * edec[i]], axis=1)) for i in idx]
    q_e = [(q[i] * edec[i]).astype(BF16) for i in idx]
    k_end_t = [(k[i] * jnp.exp(dend[i] - dcol[i])).T.astype(BF16) for i in idx]
    yield
    u = [x[:, :GDN_DV] for x in uw]
    w = [x[:, GDN_DV:].astype(BF16) for x in uw]
    return u, w, q_e, qk, k_end_t, dend


def _conv_silu(win, cw_ref):
    conv = win(0) * cw_ref[CONV_WIDTH - 1:CONV_WIDTH, :]
    for j in range(1, CONV_WIDTH):
        conv = conv + win(j) * cw_ref[CONV_WIDTH - 1 - j:CONV_WIDTH - j, :]
    return _silu(conv)

def _gdn_prompt_kernel(qkv_ref, small_ref, cs_ref, s0_ref, cw_ref, alog_ref, dtb_ref,
                       o_ref, sfin_ref, cnew_ref, xbuf, prev, s_scr, *, c):
    t = pl.program_id(0)
    nb, r, _ = qkv_ref.shape
    keep = CONV_WIDTH - 1
    base = SUBLANES

    @pl.when(t == 0)
    def _():
        s_scr[...] = s0_ref[...]
        prev[:, base - keep:base, :] = cs_ref[...]

    same_le, same_lt, same = _block_masks(r, c)
    zeros = jnp.zeros((c, GDN_DV), F32)
    n_sub = r // c

    def one_batch(b, slot, heads, first):
        xb = xbuf.at[slot]
        if first:
            xb[base - keep:base, :] = prev[b, base - keep:base, :]
            xb[base:base + r, :] = qkv_ref[b]
            prev[b, base - keep:base, :] = xb[base + r - keep:base + r, :]

        def qkv_of_head(h):
            return [_conv_silu(lambda j: xb[base - j:base - j + r, cols], cw_ref.at[:, cols])
                    for cols in _gdn_head_cols(h)]

        u, w, q_e, qk, k_end_t, dend = yield from _gdn_prepass(
            qkv_of_head, small_ref[b], alog_ref[...], dtb_ref[...], same_le, same_lt, same, c, heads)
        idx = range(len(heads))
        hs = [slice(h * GDN_DK, (h + 1) * GDN_DK) for h in heads]
        s = [s_scr[b, hs[i], :] for i in idx]
        for ci in range(n_sub):
            rows = slice(ci * c, (ci + 1) * c)
            ws = [_dot(jnp.concatenate([w[i][rows], q_e[i][rows]], axis=0), s[i]) for i in idx]
            yield
            v_new = [u[i][rows] - ws[i][:c] for i in idx]
            padded = [jnp.concatenate([zeros] * ci + [v_new[i]] + [zeros] * (n_sub - 1 - ci), axis=0)
                      for i in idx]
            upd = [_dot(jnp.concatenate([qk[i][rows], k_end_t[i]], axis=0), padded[i]) for i in idx]
            yield
            for i in idx:
                o_ref[b, rows, hs[i]] = (ws[i][c:] + upd[i][:c]).astype(o_ref.dtype)
            s = [s[i] * jnp.exp(dend[i][ci * c:ci * c + 1, :]) + upd[i][c:] for i in idx]
        for i in idx:
            s_scr[b, hs[i], :] = s[i]

    def per_step(step, carry):
        for g in range(0, GDN_HEADS, GDN_HEAD_GROUP):
            heads = tuple(range(g, g + GDN_HEAD_GROUP))
            _interleave([one_batch(step * GDN_BATCH_INTERLEAVE + j, j, heads, g == 0)
                         for j in range(GDN_BATCH_INTERLEAVE)])
        return carry

    lax.fori_loop(0, nb // GDN_BATCH_INTERLEAVE, per_step, None)

    @pl.when(t == pl.num_programs(0) - 1)
    def _():
        sfin_ref[...] = s_scr[...]
        cnew_ref[...] = prev[:, base - keep:base, :]


def _gdn_sample_kernel(qkv_ref, small_ref, cs_ref, s0_ref, cw_ref, alog_ref, dtb_ref,
                       o_ref, sfin_ref, cnew_ref, xbuf):
    ns, c, _ = qkv_ref.shape
    r = ns * c
    keep = CONV_WIDTH - 1
    base = SUBLANES
    xbuf[:, base - keep:base, :] = cs_ref[...]
    xbuf[:, base:base + c, :] = qkv_ref[...]
    conv = _conv_silu(lambda j: xbuf[:, base - j:base - j + c, :], cw_ref).reshape(r, CONV_CH)
    cnew_ref[...] = xbuf[:, base + c - keep:base + c, :]

    same_le, same_lt, same = _block_masks(r, c)
    seq_of_row = lax.broadcasted_iota(jnp.int32, (r, GDN_DV), 0) // c
    (u, w, q_e, qk, k_end_t, dend), = _interleave([_gdn_prepass(
        lambda h: [conv[:, cols] for cols in _gdn_head_cols(h)], small_ref[...].reshape(r, LANES),
        alog_ref[...], dtb_ref[...], same_le, same_lt, same, c, tuple(range(GDN_HEADS)))])
    for h in range(GDN_HEADS):
        hs = slice(h * GDN_DK, (h + 1) * GDN_DK)
        v_parts, o_parts = [], []
        for s in range(ns):
            rows = slice(s * c, (s + 1) * c)
            ws = _dot(jnp.concatenate([w[h][rows], q_e[h][rows]], axis=0), s0_ref[s, hs, :])
            v_parts.append(u[h][rows] - ws[:c])
            o_parts.append(ws[c:])
        v_new = jnp.concatenate(v_parts, axis=0)
        o = jnp.concatenate(o_parts, axis=0) + _dot(qk[h], v_new)
        o_ref[:, h * GDN_DV:(h + 1) * GDN_DV] = o.astype(o_ref.dtype)
        for s in range(ns):
            upd = _dot(k_end_t[h], jnp.where(seq_of_row == s, v_new, 0.0))
            sfin_ref[s, hs, :] = s0_ref[s, hs, :] * jnp.exp(dend[h][s * c:s * c + 1, :]) + upd


def _gdn_prompt(qkv3, small3, conv_state, s0, conv_w, alog_v, dtb_v, c):
    bsz, t_len, _ = qkv3.shape
    assert bsz % GDN_BATCH_INTERLEAVE == 0 and t_len % GROUP_ROWS == 0 and GDN_HEADS % GDN_HEAD_GROUP == 0
    rows = GDN_HEADS * GDN_DK
    keep = CONV_WIDTH - 1
    r = GROUP_ROWS
    full3 = lambda t: (0, 0, 0)
    const2 = lambda t: (0, 0)
    return pl.pallas_call(
        functools.partial(_gdn_prompt_kernel, c=c),
        grid=(t_len // r,),
        in_specs=[
            pl.BlockSpec((bsz, r, CONV_CH), lambda t: (0, t, 0)),
            pl.BlockSpec((bsz, r, LANES), lambda t: (0, t, 0)),
            pl.BlockSpec((bsz, keep, CONV_CH), full3),
            pl.BlockSpec((bsz, rows, GDN_DV), full3),
            pl.BlockSpec((CONV_WIDTH, CONV_CH), const2),
            pl.BlockSpec((2 * GDN_HEADS, 1), const2),
            pl.BlockSpec((2 * GDN_HEADS, 1), const2),
        ],
        out_specs=[
            pl.BlockSpec((bsz, r, GDN_VAL), lambda t: (0, t, 0)),
            pl.BlockSpec((bsz, rows, GDN_DV), full3),
            pl.BlockSpec((bsz, keep, CONV_CH), full3),
        ],
        out_shape=[
            jax.ShapeDtypeStruct((bsz, t_len, GDN_VAL), BF16),
            jax.ShapeDtypeStruct((bsz, rows, GDN_DV), F32),
            jax.ShapeDtypeStruct((bsz, keep, CONV_CH), F32),
        ],
        scratch_shapes=[pltpu.VMEM((GDN_BATCH_INTERLEAVE, SUBLANES + r, CONV_CH), F32),
                        pltpu.VMEM((bsz, SUBLANES, CONV_CH), F32),
                        pltpu.VMEM((bsz, rows, GDN_DV), F32)],
        compiler_params=pltpu.CompilerParams(dimension_semantics=("arbitrary",),
                                             vmem_limit_bytes=VMEM_MIB["gdn_prompt"] * 2 ** 20),
        name="gdn_prompt",
    )(qkv3, small3, conv_state, s0, conv_w, alog_v, dtb_v)


def _gdn_sample(qkv3, small3, conv_state, s0, conv_w, alog_v, dtb_v):
    bsz, c, _ = qkv3.shape
    rows = GDN_HEADS * GDN_DK
    keep = CONV_WIDTH - 1
    ns = GROUP_ROWS // c
    grp = lambda g: (g, 0, 0)
    const2 = lambda g: (0, 0)
    return pl.pallas_call(
        _gdn_sample_kernel,
        grid=(bsz // ns,),
        in_specs=[
            pl.BlockSpec((ns, c, CONV_CH), grp),
            pl.BlockSpec((ns, c, LANES), grp),
            pl.BlockSpec((ns, keep, CONV_CH), grp),
            pl.BlockSpec((ns, rows, GDN_DV), grp),
            pl.BlockSpec((CONV_WIDTH, CONV_CH), const2),
            pl.BlockSpec((2 * GDN_HEADS, 1), const2),
            pl.BlockSpec((2 * GDN_HEADS, 1), const2),
        ],
        out_specs=[
            pl.BlockSpec((ns * c, GDN_VAL), lambda g: (g, 0)),
            pl.BlockSpec((ns, rows, GDN_DV), grp),
            pl.BlockSpec((ns, keep, CONV_CH), grp),
        ],
        out_shape=[
            jax.ShapeDtypeStruct((bsz * c, GDN_VAL), BF16),
            jax.ShapeDtypeStruct((bsz, rows, GDN_DV), F32),
            jax.ShapeDtypeStruct((bsz, keep, CONV_CH), F32),
        ],
        scratch_shapes=[pltpu.VMEM((ns, 2 * SUBLANES, CONV_CH), F32)],
        compiler_params=pltpu.CompilerParams(dimension_semantics=("arbitrary",),
                                             vmem_limit_bytes=VMEM_MIB["gdn_sample"] * 2 ** 20),
        name="gdn_sample",
    )(qkv3, small3, conv_state, s0, conv_w, alog_v, dtb_v)


def _head_norm_gate(o, silu_z, w):
    parts = []
    for h in range(o.shape[-1] // LANES):
        oh = o[:, h * LANES:(h + 1) * LANES]
        parts.append(oh * lax.rsqrt(jnp.mean(oh * oh, axis=-1, keepdims=True) + NORM_EPS) * w)
    return jnp.concatenate(parts, axis=-1) * silu_z


def _out_kernel(oa_ref, ob_ref, ga_ref, gb_ref, za_ref, zb_ref, x_ref, p_ref, anw_ref, bnw_ref,
                wua_ref, wub_ref, wout_ref, wpg_ref, wp_ref, fnw_ref, y_ref):
    f32 = lambda ref: ref[...].astype(F32)
    ya = _dot(_head_norm_gate(f32(oa_ref), f32(za_ref), anw_ref[...]), wua_ref[...])
    yb = _dot(_head_norm_gate(f32(ob_ref), f32(zb_ref), bnw_ref[...]), wub_ref[...])
    merged = f32(ga_ref) * ya + f32(gb_ref) * yb
    h1 = x_ref[...] + _dot(merged, wout_ref[...])
    h2 = h1 + _sigmoid(_dot(h1, wpg_ref[...])) * _dot(p_ref[...], wp_ref[...])
    y_ref[...] = h2 * lax.rsqrt(jnp.mean(h2 * h2, axis=-1, keepdims=True) + NORM_EPS) * fnw_ref[...]


def _out_stage(o_a, o_b, gz, x2d, p2d, anw, bnw, wua, wub, wout, wpg, wp, fnw):
    n = x2d.shape[0]
    tm = min(ROW_TILE, n)
    const = lambda i: (0, 0)
    return pl.pallas_call(
        _out_kernel,
        grid=(n // tm,),
        in_specs=[
            pl.BlockSpec((tm, GLA_VAL), lambda i: (i, 0)),
            pl.BlockSpec((tm, GDN_VAL), lambda i: (i, 0)),
            pl.BlockSpec((tm, D_MODEL), lambda i: (i, GZ_GATE_A // D_MODEL)),
            pl.BlockSpec((tm, D_MODEL), lambda i: (i, GZ_GATE_B // D_MODEL)),
            pl.BlockSpec((tm, GLA_VAL), lambda i: (i, GZ_Z_A // GLA_VAL)),
            pl.BlockSpec((tm, GDN_VAL), lambda i: (i, GZ_Z_B // GDN_VAL)),
            pl.BlockSpec((tm, D_MODEL), lambda i: (i, 0)),
            pl.BlockSpec((tm, PLE_DIM), lambda i: (i, 0)),
            pl.BlockSpec((1, GLA_DV), const),
            pl.BlockSpec((1, GDN_DV), const),
            pl.BlockSpec((GLA_VAL, D_MODEL), const),
            pl.BlockSpec((GDN_VAL, D_MODEL), const),
            pl.BlockSpec((D_MODEL, D_MODEL), const),
            pl.BlockSpec((D_MODEL, D_MODEL), const),
            pl.BlockSpec((PLE_DIM, D_MODEL), const),
            pl.BlockSpec((1, D_MODEL), const),
        ],
        out_specs=pl.BlockSpec((tm, D_MODEL), lambda i: (i, 0)),
        out_shape=jax.ShapeDtypeStruct((n, D_MODEL), F32),
        compiler_params=pltpu.CompilerParams(dimension_semantics=("arbitrary",),
                                             vmem_limit_bytes=VMEM_MIB["out_stage"] * 2 ** 20),
        name="out_stage",
    )(o_a, o_b, gz, gz, gz, gz, x2d, p2d, anw, bnw, wua, wub, wout, wpg, wp, fnw)


def _in_offsets():
    offs = [0]
    for s in IN_SPLITS:
        offs.append(offs[-1] + s)
    return offs


def _regroup_kernel(wt_ref, o_ref):
    offs = _in_offsets()
    (q_a, _, _, g_a, z_a, qkv_b, a_b, _, z_b, gate_a, gate_b, end) = offs
    piece = lambda lo, hi: wt_ref[lo:hi, :].astype(BF16)
    o_ref[P_GLA_QKV:P_GLA_QKV + GLA_QKV, :] = piece(q_a, g_a)
    o_ref[P_QKV_B:P_QKV_B + CONV_CH, :] = piece(qkv_b, a_b)
    small = jnp.concatenate([wt_ref[g_a:z_a, :], wt_ref[a_b:z_b, :],
                             jnp.zeros((LANES - (z_a - g_a) - (z_b - a_b), wt_ref.shape[1]), F32)], axis=0)
    o_ref[P_SMALL:P_SMALL + LANES, :] = small.astype(BF16)
    o_ref[P_GATES + GZ_GATE_A:P_GATES + GZ_GATE_A + D_MODEL, :] = piece(gate_a, gate_b)
    o_ref[P_GATES + GZ_GATE_B:P_GATES + GZ_GATE_B + D_MODEL, :] = piece(gate_b, end)
    o_ref[P_GATES + GZ_Z_A:P_GATES + GZ_Z_A + GLA_VAL, :] = piece(z_a, qkv_b)
    o_ref[P_GATES + GZ_Z_B:P_GATES + GZ_Z_B + GDN_VAL, :] = piece(z_b, gate_a)


def _regroup_w_in(w_in_t):
    cols = 256
    return pl.pallas_call(
        _regroup_kernel,
        grid=(D_MODEL // cols,),
        in_specs=[pl.BlockSpec((None, w_in_t.shape[1], cols), lambda i: (0, 0, i))],
        out_specs=pl.BlockSpec((P_COLS, cols), lambda i: (0, i)),
        out_shape=jax.ShapeDtypeStruct((P_COLS, D_MODEL), BF16),
        compiler_params=pltpu.CompilerParams(dimension_semantics=("arbitrary",),
                                             vmem_limit_bytes=VMEM_MIB["regroup_w_in"] * 2 ** 20),
        name="regroup_w_in",
    )(w_in_t)


def _head_param_col(v):
    return jnp.zeros((2 * GDN_HEADS, 1), F32).at[:GDN_HEADS, 0].set(v.astype(F32))


def _trunk(x, p, s_gla, s_gdn, conv_state, wts):
    bsz, t_len, _ = x.shape
    c = min(PROMPT_CHUNK, t_len)
    n = bsz * t_len
    x2d = x.reshape(n, D_MODEL)
    p2d = p.reshape(n, PLE_DIM)
    gla_in, gdn_in, small, gz = _inproj(x2d, wts["norm_w"], wts["w_in_r"])
    gla_in3 = gla_in.reshape(bsz, t_len, GLA_QKV)
    gdn_in3 = gdn_in.reshape(bsz, t_len, CONV_CH)
    small3 = small.reshape(bsz, t_len, LANES)
    s_gla2 = s_gla.reshape(bsz, GLA_HEADS * GLA_DK, GLA_DV)
    s_gdn2 = s_gdn.reshape(bsz, GDN_HEADS * GDN_DK, GDN_DV)
    if t_len % GROUP_ROWS == 0:
        o_a, gla_fin = _gla_prompt(gla_in3, small3, s_gla2, wts["wgg"], wts["bgg"], c)
        o_b, gdn_fin, conv_new = _gdn_prompt(gdn_in3, small3, conv_state, s_gdn2, wts["conv_w"],
                                             wts["alog_v"], wts["dtb_v"], c)
    else:
        assert GROUP_ROWS % t_len == 0 and bsz % (GROUP_ROWS // t_len) == 0 and t_len >= CONV_WIDTH - 1
        o_a, gla_fin = _gla_sample(gla_in3, small3, s_gla2, wts["wgg"], wts["bgg"])
        o_b, gdn_fin, conv_new = _gdn_sample(gdn_in3, small3, conv_state, s_gdn2, wts["conv_w"],
                                             wts["alog_v"], wts["dtb_v"])
    y = _out_stage(o_a.reshape(n, GLA_VAL), o_b.reshape(n, GDN_VAL), gz, x2d, p2d, wts["anw"], wts["bnw"],
                   wts["wua"], wts["wub"], wts["wout"], wts["wpg"], wts["wp"], wts["fnw"])
    return (y.reshape(bsz, t_len, D_MODEL),
            gla_fin.reshape(1, bsz, GLA_HEADS, GLA_DK, GLA_DV),
            gdn_fin.reshape(1, bsz, GDN_HEADS, GDN_DK, GDN_DV),
            conv_new.reshape(1, bsz, CONV_WIDTH - 1, CONV_CH))


def kernel(x_prompt, x_sample, state_gla, state_gdn, state_conv, p_prompt, p_sample, norm_w, w_in,
           w_gla_gate, b_gla_gate, gla_norm_w, conv_w, gdn_a_log, gdn_dt_bias, gdn_norm_w,
           w_up_gla, w_up_gdn, w_out, w_ple_gate, w_ple, final_norm_w):
    wgg = jnp.zeros((LANES, GLA_KEY), F32).at[SM_G:SM_G + GLA_GATE_RANK].set(w_gla_gate[0]).astype(BF16)
    wts = {
        "norm_w": norm_w[0].reshape(1, D_MODEL),
        "w_in_r": _regroup_w_in(jnp.swapaxes(w_in, 1, 2)),
        "wgg": wgg,
        "bgg": b_gla_gate[0].reshape(1, GLA_KEY),
        "conv_w": conv_w[0],
        "alog_v": _head_param_col(gdn_a_log[0]),
        "dtb_v": _head_param_col(gdn_dt_bias[0]),
        "anw": gla_norm_w[0].reshape(1, GLA_DV),
        "bnw": gdn_norm_w[0].reshape(1, GDN_DV),
        "wua": w_up_gla[0].astype(BF16),
        "wub": w_up_gdn[0].astype(BF16),
        "wout": w_out[0].astype(BF16),
        "wpg": w_ple_gate[0].astype(BF16),
        "wp": w_ple[0].astype(BF16),
        "fnw": final_norm_w.reshape(1, D_MODEL),
    }
    bsz = x_prompt.shape[0]
    dt = x_prompt.dtype
    y_p, gla_p, gdn_p, conv_p = _trunk(
        x_prompt, p_prompt[0],
        jnp.zeros((bsz, GLA_HEADS, GLA_DK, GLA_DV), dt), jnp.zeros((bsz, GDN_HEADS, GDN_DK, GDN_DV), dt),
        jnp.zeros((bsz, CONV_WIDTH - 1, CONV_CH), dt), wts)
    y_s, gla_s, gdn_s, conv_s = _trunk(x_sample, p_sample[0], state_gla[0], state_gdn[0], state_conv[0], wts)
    return (y_p, y_s, gla_p, gdn_p, conv_p, gla_s, gdn_s, conv_s)
```

```python
import functools

import jax
import jax.numpy as jnp
from jax import lax
from jax.experimental import pallas as pl
from jax.experimental.pallas import tpu as pltpu

F32 = jnp.float32
BF16 = jnp.bfloat16

D_MODEL = 1024
PLE_DIM = 256
NORM_EPS = 1e-6
GLA_HEADS = 4
GLA_DK = 64
GLA_DV = 128
GLA_KEY = GLA_HEADS * GLA_DK
GLA_VAL = GLA_HEADS * GLA_DV
GLA_GATE_RANK = 16
GLA_GATE_TEMP = 16.0
GDN_HEADS = 4
GDN_DK = 128
GDN_DV = 128
GDN_KEY = GDN_HEADS * GDN_DK
GDN_VAL = GDN_HEADS * GDN_DV
CONV_WIDTH = 4
CONV_CH = 2 * GDN_KEY + GDN_VAL
IN_SPLITS = (GLA_KEY, GLA_KEY, GLA_VAL, GLA_GATE_RANK, GLA_VAL, CONV_CH, GDN_HEADS, GDN_HEADS,
             GDN_VAL, D_MODEL, D_MODEL)

LANES = 128
SUBLANES = 8

GLA_QKV = 2 * GLA_KEY + GLA_VAL
P_GLA_QKV = 0
P_QKV_B = P_GLA_QKV + GLA_QKV
P_SMALL = P_QKV_B + CONV_CH
P_GATES = P_SMALL + LANES
GZ_COLS = 2 * D_MODEL + GLA_VAL + GDN_VAL
P_COLS = P_GATES + GZ_COLS
GZ_GATE_A = 0
GZ_GATE_B = D_MODEL
GZ_Z_A = 2 * D_MODEL
GZ_Z_B = 2 * D_MODEL + GLA_VAL
SM_G = 0
SM_A = GLA_GATE_RANK
SM_B = GLA_GATE_RANK + GDN_HEADS

PROMPT_CHUNK = 64
GROUP_ROWS = 128
BATCH_INTERLEAVE = 8
GDN_BATCH_INTERLEAVE = 8
GDN_HEAD_GROUP = 4
INVERSE_BASE_BLOCK = 16
ROW_TILE = 512
VMEM_MIB = {"regroup_w_in": 40, "cast_weights": 40, "inproj": 40, "gla_prompt": 40, "gla_sample": 40, "gdn_prompt": 48,
            "gdn_sample": 40, "out_stage": 40}


def _dot(a, b):
    return jnp.dot(a.astype(BF16), b.astype(BF16), preferred_element_type=F32)


def _dot_nt(a, b):
    return lax.dot_general(a.astype(BF16), b.astype(BF16), (((1,), (1,)), ((), ())),
                           preferred_element_type=F32)


def _split2(x):
    h1 = x.astype(BF16)
    return h1, (x - h1.astype(F32)).astype(BF16)


def _cumsum_rows(tri, x):
    x1, x2 = _split2(x)
    d = functools.partial(jnp.dot, preferred_element_type=F32)
    return d(tri, x1) + d(tri, x2)


def _softplus(x):
    return jnp.maximum(x, 0.0) + jnp.log(1.0 + jnp.exp(-jnp.abs(x)))


def _sigmoid(x):
    return 0.5 * jnp.tanh(0.5 * x) + 0.5


def _silu(x):
    half = 0.5 * x
    return half * jnp.tanh(half) + half


def _interleave(emitters):
    results = [None] * len(emitters)
    live = list(range(len(emitters)))
    while live:
        for i in list(live):
            try:
                next(emitters[i])
            except StopIteration as stop:
                results[i] = stop.value
                live.remove(i)
    return results


def _block_masks(r, block):
    row = lax.broadcasted_iota(jnp.int32, (r, r), 0)
    col = lax.broadcasted_iota(jnp.int32, (r, r), 1)
    same = (row // block) == (col // block)
    return same & (row >= col), same & (row > col), same


INPROJ_COL_STEP = 512


def _inproj_kernel(x_ref, nw_ref, w_ref, gla_ref, gdn_ref, small_ref, gz_ref):
    x = x_ref[...]
    xn = x * lax.rsqrt(jnp.mean(x * x, axis=-1, keepdims=True) + NORM_EPS) * nw_ref[...]
    xb = xn.astype(BF16)

    def emit(o_ref, w0, width, act=None, o0=0):
        for c0 in range(0, width, INPROJ_COL_STEP):
            c1 = min(c0 + INPROJ_COL_STEP, width)
            res = _dot_nt(xb, w_ref[w0 + c0:w0 + c1, :])
            o_ref[:, o0 + c0:o0 + c1] = (res if act is None else act(res)).astype(o_ref.dtype)

    emit(gla_ref, P_GLA_QKV, GLA_QKV)
    emit(gdn_ref, P_QKV_B, CONV_CH)
    emit(small_ref, P_SMALL, LANES)
    emit(gz_ref, P_GATES + GZ_GATE_A, 2 * D_MODEL, act=_sigmoid, o0=GZ_GATE_A)
    emit(gz_ref, P_GATES + GZ_Z_A, GLA_VAL + GDN_VAL, act=_silu, o0=GZ_Z_A)


def _inproj(x2d, norm_w, w_in_r):
    n = x2d.shape[0]
    tm = min(ROW_TILE, n)
    rows = lambda i: (i, 0)
    return pl.pallas_call(
        _inproj_kernel,
        grid=(n // tm,),
        in_specs=[
            pl.BlockSpec((tm, D_MODEL), rows),
            pl.BlockSpec((1, D_MODEL), lambda i: (0, 0)),
            pl.BlockSpec((P_COLS, D_MODEL), lambda i: (0, 0), pipeline_mode=pl.Buffered(1)),
        ],
        out_specs=[
            pl.BlockSpec((tm, GLA_QKV), rows),
            pl.BlockSpec((tm, CONV_CH), rows),
            pl.BlockSpec((tm, LANES), rows),
            pl.BlockSpec((tm, GZ_COLS), rows),
        ],
        out_shape=[
            jax.ShapeDtypeStruct((n, GLA_QKV), F32),
            jax.ShapeDtypeStruct((n, CONV_CH), F32),
            jax.ShapeDtypeStruct((n, LANES), F32),
            jax.ShapeDtypeStruct((n, GZ_COLS), BF16),
        ],
        compiler_params=pltpu.CompilerParams(dimension_semantics=("arbitrary",),
                                             vmem_limit_bytes=VMEM_MIB["inproj"] * 2 ** 20),
        name="inproj",
    )(x2d, norm_w, w_in_r)


GLA_PAIRS = GLA_HEADS // 2
GLA_SUB_BLOCK = 16


def _gla_att_levels(r, block):
    row = lax.broadcasted_iota(jnp.int32, (r, r), 0)
    col = lax.broadcasted_iota(jnp.int32, (r, r), 1)
    sub = min(block, GLA_SUB_BLOCK)
    levels = [(None, ((row // sub) == (col // sub)) & (row >= col))]
    half = sub
    while half < block:
        levels.append((half, ((row // (2 * half)) == (col // (2 * half)))
                       & ((row // half) % 2 == 1) & ((col // half) % 2 == 0)))
        half *= 2
    return levels


def _rows_at(x, n, offset):
    return jnp.concatenate([jnp.broadcast_to(x[i + offset:i + offset + 1, :], (n, x.shape[1]))
                            for i in range(0, x.shape[0], n)], axis=0)


def _gla_prepass(q, k, v, sm, wgg, bgg, same_le, same, levels, block):
    r = q.shape[0]
    heads = range(GLA_HEADS)
    pairs = range(GLA_PAIRS)
    pre = _dot(sm, wgg) + bgg
    yield
    gk = (jnp.minimum(pre, 0.0) - jnp.log(1.0 + jnp.exp(-jnp.abs(pre)))) * (1.0 / GLA_GATE_TEMP)
    sums = _cumsum_rows(jnp.concatenate([jnp.where(same_le, 1.0, 0.0).astype(BF16),
                                         jnp.where(same, 1.0, 0.0).astype(BF16)], axis=0), gk)
    yield
    bcum = sums[:r]
    bend = sums[r:]
    bex = bcum - gk
    scale = GLA_DK ** -0.5
    q_e = q * jnp.exp(bcum) * scale
    lane = lax.broadcasted_iota(jnp.int32, (r, LANES), 1)
    in_head = [lane < GLA_DK, lane >= GLA_DK]
    pl_ = [slice(p * LANES, (p + 1) * LANES) for p in pairs]
    head_only = lambda x, h: jnp.where(in_head[h % 2], x[:, pl_[h // 2]], 0.0)
    qm = [head_only(q_e, h) for h in heads]
    sub = min(block, GLA_SUB_BLOCK)
    start = _rows_at(bex, sub, 0)
    groups = {None: (q * jnp.exp(bcum - start) * scale, [(levels[0][1], k * jnp.exp(start - bcum))])}
    for half, mask in levels[1:]:
        key = None if half == sub else half
        if key not in groups:
            groups[key] = (q * jnp.exp(bcum - _rows_at(bex, half, 0)) * scale, [])
        groups[key][1].append((mask, k * jnp.exp(_rows_at(bcum, half, half - 1) - bcum)))
    att = [jnp.zeros((r, r), F32)] * GLA_HEADS
    for q_l, parts in groups.values():
        for h in heads:
            keys = [k_l[:, pl_[h // 2]].astype(BF16) for _, k_l in parts]
            prod = _dot_nt(head_only(q_l, h), keys[0] if len(keys) == 1 else jnp.concatenate(keys, axis=0))
            for i, (mask, _) in enumerate(parts):
                att[h] = jnp.where(mask, prod[:, i * r:(i + 1) * r], att[h])
    yield
    o_intra = [_dot(att[h], v[h]) for h in heads]
    k_end = k * jnp.exp(bend - bcum)
    k_end_t = [k_end[:, pl_[p]].T for p in pairs]
    bend_t = [bend[:, pl_[p]].T for p in pairs]
    yield
    return qm, o_intra, k_end_t, bend_t


def _gla_prompt_kernel(qkv_ref, small_ref, s0_ref, wgg_ref, bgg_ref, o_ref, sfin_ref, s_scr, *, c):
    t = pl.program_id(0)
    nb, r, _ = qkv_ref.shape

    @pl.when(t == 0)
    def _():
        s_scr[...] = s0_ref[...]

    same_le, _, same = _block_masks(r, c)
    levels = _gla_att_levels(r, c)
    zeros = jnp.zeros((c, GLA_DV), F32)
    n_sub = r // c
    heads = range(GLA_HEADS)
    pairs = range(GLA_PAIRS)

    def one_batch(b):
        q = qkv_ref[b, :, 0:GLA_KEY]
        k = qkv_ref[b, :, GLA_KEY:2 * GLA_KEY]
        v = [qkv_ref[b, :, 2 * GLA_KEY + h * GLA_DV:2 * GLA_KEY + (h + 1) * GLA_DV] for h in heads]
        qm, o_intra, k_end_t, bend_t = yield from _gla_prepass(
            q, k, v, small_ref[b], wgg_ref[...], bgg_ref[...], same_le, same, levels, c)
        s = [s_scr[b, p * LANES:(p + 1) * LANES, :] for p in pairs]
        for i in range(n_sub):
            rows = slice(i * c, (i + 1) * c)
            ws = [_dot(jnp.concatenate([qm[2 * p][rows], qm[2 * p + 1][rows]], axis=0), s[p]) for p in pairs]
            padded = [jnp.concatenate([zeros] * i + [v[h][rows]] + [zeros] * (n_sub - 1 - i), axis=0)
                      for h in heads]
            upd = [_dot(k_end_t[h // 2][(h % 2) * GLA_DK:(h % 2 + 1) * GLA_DK, :], padded[h]) for h in heads]
            yield
            for h in heads:
                o_ref[b, rows, h * GLA_DV:(h + 1) * GLA_DV] = (
                    o_intra[h][rows] + ws[h // 2][(h % 2) * c:(h % 2 + 1) * c]).astype(o_ref.dtype)
            s = [s[p] * jnp.exp(bend_t[p][:, i * c:i * c + 1])
                 + jnp.concatenate([upd[2 * p], upd[2 * p + 1]], axis=0) for p in pairs]
        for p in pairs:
            s_scr[b, p * LANES:(p + 1) * LANES, :] = s[p]

    def per_step(i, carry):
        _interleave([one_batch(i * BATCH_INTERLEAVE + j) for j in range(BATCH_INTERLEAVE)])
        return carry

    lax.fori_loop(0, nb // BATCH_INTERLEAVE, per_step, None)

    @pl.when(t == pl.num_programs(0) - 1)
    def _():
        sfin_ref[...] = s_scr[...]


def _gla_sample_kernel(qkv_ref, small_ref, s0_ref, wgg_ref, bgg_ref, o_ref, sfin_ref):
    ns, c, _ = qkv_ref.shape
    r = ns * c
    heads = range(GLA_HEADS)
    same_le, _, same = _block_masks(r, c)
    levels = _gla_att_levels(r, c)
    q = qkv_ref[:, :, 0:GLA_KEY].reshape(r, GLA_KEY)
    k = qkv_ref[:, :, GLA_KEY:2 * GLA_KEY].reshape(r, GLA_KEY)
    v = [qkv_ref[:, :, 2 * GLA_KEY + h * GLA_DV:2 * GLA_KEY + (h + 1) * GLA_DV].reshape(r, GLA_DV)
         for h in heads]
    (qm, o_intra, k_end_t, bend_t), = _interleave([_gla_prepass(
        q, k, v, small_ref[...].reshape(r, LANES), wgg_ref[...], bgg_ref[...], same_le, same, levels, c)])
    seq_of_row = lax.broadcasted_iota(jnp.int32, (r, GLA_DV), 0) // c
    for p in range(GLA_PAIRS):
        ps = slice(p * LANES, (p + 1) * LANES)
        inter = [[], []]
        for s in range(ns):
            rows = slice(s * c, (s + 1) * c)
            ws = _dot(jnp.concatenate([qm[2 * p][rows], qm[2 * p + 1][rows]], axis=0), s0_ref[s, ps, :])
            inter[0].append(ws[:c])
            inter[1].append(ws[c:])
        for hh in range(2):
            h = 2 * p + hh
            o = o_intra[h] + jnp.concatenate(inter[hh], axis=0)
            o_ref[:, h * GLA_DV:(h + 1) * GLA_DV] = o.astype(o_ref.dtype)
        for s in range(ns):
            upd = [_dot(k_end_t[p][hh * GLA_DK:(hh + 1) * GLA_DK, :],
                        jnp.where(seq_of_row == s, v[2 * p + hh], 0.0)) for hh in range(2)]
            sfin_ref[s, ps, :] = (s0_ref[s, ps, :] * jnp.exp(bend_t[p][:, s * c:s * c + 1])
                                  + jnp.concatenate(upd, axis=0))


def _gla_prompt(qkv3, small3, s0, wgg, bgg, c):
    bsz, t_len, _ = qkv3.shape
    assert bsz % BATCH_INTERLEAVE == 0 and t_len % GROUP_ROWS == 0
    rows = GLA_HEADS * GLA_DK
    r = GROUP_ROWS
    full3 = lambda t: (0, 0, 0)
    const2 = lambda t: (0, 0)
    return pl.pallas_call(
        functools.partial(_gla_prompt_kernel, c=c),
        grid=(t_len // r,),
        in_specs=[
            pl.BlockSpec((bsz, r, GLA_QKV), lambda t: (0, t, 0)),
            pl.BlockSpec((bsz, r, LANES), lambda t: (0, t, 0)),
            pl.BlockSpec((bsz, rows, GLA_DV), full3),
            pl.BlockSpec((LANES, GLA_KEY), const2),
            pl.BlockSpec((1, GLA_KEY), const2),
        ],
        out_specs=[
            pl.BlockSpec((bsz, r, GLA_VAL), lambda t: (0, t, 0)),
            pl.BlockSpec((bsz, rows, GLA_DV), full3),
        ],
        out_shape=[
            jax.ShapeDtypeStruct((bsz, t_len, GLA_VAL), BF16),
            jax.ShapeDtypeStruct((bsz, rows, GLA_DV), F32),
        ],
        scratch_shapes=[pltpu.VMEM((bsz, rows, GLA_DV), F32)],
        compiler_params=pltpu.CompilerParams(dimension_semantics=("arbitrary",),
                                             vmem_limit_bytes=VMEM_MIB["gla_prompt"] * 2 ** 20),
        name="gla_prompt",
    )(qkv3, small3, s0, wgg, bgg)


def _gla_sample(qkv3, small3, s0, wgg, bgg):
    bsz, c, _ = qkv3.shape
    rows = GLA_HEADS * GLA_DK
    ns = GROUP_ROWS // c
    grp = lambda g: (g, 0, 0)
    const2 = lambda g: (0, 0)
    return pl.pallas_call(
        _gla_sample_kernel,
        grid=(bsz // ns,),
        in_specs=[
            pl.BlockSpec((ns, c, GLA_QKV), grp),
            pl.BlockSpec((ns, c, LANES), grp),
            pl.BlockSpec((ns, rows, GLA_DV), grp),
            pl.BlockSpec((LANES, GLA_KEY), const2),
            pl.BlockSpec((1, GLA_KEY), const2),
        ],
        out_specs=[
            pl.BlockSpec((ns * c, GLA_VAL), lambda g: (g, 0)),
            pl.BlockSpec((ns, rows, GLA_DV), grp),
        ],
        out_shape=[
            jax.ShapeDtypeStruct((bsz * c, GLA_VAL), BF16),
            jax.ShapeDtypeStruct((bsz, rows, GLA_DV), F32),
        ],
        compiler_params=pltpu.CompilerParams(dimension_semantics=("arbitrary",),
                                             vmem_limit_bytes=VMEM_MIB["gla_sample"] * 2 ** 20),
        name="gla_sample",
    )(qkv3, small3, s0, wgg, bgg)


def _block_unit_lower_inverse(a_list, block):
    r = a_list[0].shape[0]
    row = lax.broadcasted_iota(jnp.int32, (r, r), 0)
    col = lax.broadcasted_iota(jnp.int32, (r, r), 1)
    in_block = lambda n: (row // n) == (col // n)
    base = min(block, INVERSE_BASE_BLOCK)
    a_base = a_list if base == block else [jnp.where(in_block(base), a, 0.0) for a in a_list]
    xs = [jnp.where(row == col, 1.0, 0.0) - a for a in a_base]
    ps = [_dot(a, a) for a in a_base]
    n = 2
    while n < base:
        yield
        last = 2 * n >= base
        ms = [_dot(x if last else jnp.concatenate([p.astype(BF16), x.astype(BF16)], axis=0), p)
              for p, x in zip(ps, xs)]
        ps = [m[:r] for m in ms]
        xs = [x + m[-r:] for x, m in zip(xs, ms)]
        n *= 2
    n = base
    while n < block:
        yield
        between = in_block(2 * n) & jnp.logical_not(in_block(n))
        ts = [_dot(jnp.where(between, a, 0.0), x) for a, x in zip(a_list, xs)]
        yield
        xs = [x - _dot(x, t) for x, t in zip(xs, ts)]
        n *= 2
    return xs


def _gdn_decays(sm, alog_c, dtb_c, same_lt, same):
    r = sm.shape[0]
    heads = range(GDN_HEADS)
    ab = sm.T[SM_A:SM_A + 2 * GDN_HEADS, :]
    g8 = -jnp.exp(alog_c) * _softplus(ab + dtb_c)
    beta_c = _sigmoid(ab).T
    g1, g2 = _split2(g8)
    same_ge = same & jnp.logical_not(same_lt)
    sel = jnp.concatenate([jnp.where(same_ge, 1.0, 0.0).astype(BF16),
                           jnp.where(same, 1.0, 0.0).astype(BF16)], axis=1)
    sums = jnp.dot(jnp.concatenate([g1, g2], axis=0), sel, preferred_element_type=F32)
    sums = sums[:2 * GDN_HEADS] + sums[2 * GDN_HEADS:]
    dec_rows = sums[:, :r]
    dec_c = dec_rows.T
    dend_c = sums[:, r:].T
    return ([dec_c[:, h:h + 1] for h in heads], [dec_rows[h:h + 1, :] for h in heads],
            [dend_c[:, h:h + 1] for h in heads], [beta_c[:, GDN_HEADS + h:GDN_HEADS + h + 1] for h in heads])


def _gdn_head_cols(h):
    return [slice(part * GDN_KEY + h * GDN_DK, part * GDN_KEY + (h + 1) * GDN_DK) for part in range(3)]


def _gdn_prepass(qkv_of_head, sm, alog_c, dtb_c, same_le, same_lt, same, block, heads):
    dcol, drow, dend, beta = ([x[h] for h in heads] for x in _gdn_decays(sm, alog_c, dtb_c, same_lt, same))
    idx = range(len(heads))
    gamma = [jnp.where(same_le, jnp.exp(jnp.where(same_le, dcol[i] - drow[i], 0.0)), 0.0) for i in idx]
    q, k, v = (list(x) for x in zip(*[qkv_of_head(h) for h in heads]))
    r = q[0].shape[0]
    q = [x * lax.rsqrt(jnp.sum(x * x, axis=-1, keepdims=True) + NORM_EPS) * (GDN_DK ** -0.5) for x in q]
    k = [x * lax.rsqrt(jnp.sum(x * x, axis=-1, keepdims=True) + NORM_EPS) for x in k]
    yield
    kb = [k[i] * beta[i] for i in idx]
    vb = [v[i] * beta[i] for i in idx]
    kq = [_dot_nt(jnp.concatenate([kb[i], q[i]], axis=0), k[i]) for i in idx]
    yield
    a_mat = [jnp.where(same_lt, kq[i][:r] * gamma[i], 0.0) for i in idx]
    qk = [(kq[i][r:] * gamma[i]).astype(BF16) for i in idx]
    t_inv = yield from _block_unit_lower_inverse(a_mat, block)
    edec = [jnp.exp(dcol[i]) for i in idx]
    uw = [_dot(t_inv[i], jnp.concatenate([vb[i], kb[i] * edec[i]], axis=1)) for i in idx]
    q_e = [(q[i] * edec[i]).astype(BF16) for i in idx]
    k_end_t = [(k[i] * jnp.exp(dend[i] - dcol[i])).T.astype(BF16) for i in idx]
    yield
    u = [x[:, :GDN_DV] for x in uw]
    w = [x[:, GDN_DV:].astype(BF16) for x in uw]
    return u, w, q_e, qk, k_end_t, dend


def _conv_silu(win, cw_ref):
    conv = win(0) * cw_ref[CONV_WIDTH - 1:CONV_WIDTH, :]
    for j in range(1, CONV_WIDTH):
        conv = conv + win(j) * cw_ref[CONV_WIDTH - 1 - j:CONV_WIDTH - j, :]
    return _silu(conv)

def _gdn_prompt_kernel(qkv_ref, small_ref, cs_ref, s0_ref, cw_ref, alog_ref, dtb_ref,
                       o_ref, sfin_ref, cnew_ref, xbuf, prev, s_scr, *, c):
    t = pl.program_id(0)
    nb, r, _ = qkv_ref.shape
    keep = CONV_WIDTH - 1
    base = SUBLANES

    @pl.when(t == 0)
    def _():
        s_scr[...] = s0_ref[...]
        prev[:, base - keep:base, :] = cs_ref[...]

    same_le, same_lt, same = _block_masks(r, c)
    zeros = jnp.zeros((c, GDN_DV), F32)
    n_sub = r // c

    def one_batch(b, slot, heads, first):
        xb = xbuf.at[slot]
        if first:
            xb[base - keep:base, :] = prev[b, base - keep:base, :]
            xb[base:base + r, :] = qkv_ref[b]
            prev[b, base - keep:base, :] = xb[base + r - keep:base + r, :]

        def qkv_of_head(h):
            return [_conv_silu(lambda j: xb[base - j:base - j + r, cols], cw_ref.at[:, cols])
                    for cols in _gdn_head_cols(h)]

        u, w, q_e, qk, k_end_t, dend = yield from _gdn_prepass(
            qkv_of_head, small_ref[b], alog_ref[...], dtb_ref[...], same_le, same_lt, same, c, heads)
        idx = range(len(heads))
        hs = [slice(h * GDN_DK, (h + 1) * GDN_DK) for h in heads]
        s = [s_scr[b, hs[i], :] for i in idx]
        for ci in range(n_sub):
            rows = slice(ci * c, (ci + 1) * c)
            ws = [_dot(jnp.concatenate([w[i][rows], q_e[i][rows]], axis=0), s[i]) for i in idx]
            yield
            v_new = [u[i][rows] - ws[i][:c] for i in idx]
            padded = [jnp.concatenate([zeros] * ci + [v_new[i]] + [zeros] * (n_sub - 1 - ci), axis=0)
                      for i in idx]
            upd = [_dot(jnp.concatenate([qk[i][rows], k_end_t[i]], axis=0), padded[i]) for i in idx]
            yield
            for i in idx:
                o_ref[b, rows, hs[i]] = (ws[i][c:] + upd[i][:c]).astype(o_ref.dtype)
            s = [s[i] * jnp.exp(dend[i][ci * c:ci * c + 1, :]) + upd[i][c:] for i in idx]
        for i in idx:
            s_scr[b, hs[i], :] = s[i]

    def per_step(step, carry):
        for g in range(0, GDN_HEADS, GDN_HEAD_GROUP):
            heads = tuple(range(g, g + GDN_HEAD_GROUP))
            _interleave([one_batch(step * GDN_BATCH_INTERLEAVE + j, j, heads, g == 0)
                         for j in range(GDN_BATCH_INTERLEAVE)])
        return carry

    lax.fori_loop(0, nb // GDN_BATCH_INTERLEAVE, per_step, None)

    @pl.when(t == pl.num_programs(0) - 1)
    def _():
        sfin_ref[...] = s_scr[...]
        cnew_ref[...] = prev[:, base - keep:base, :]


def _gdn_sample_kernel(qkv_ref, small_ref, cs_ref, s0_ref, cw_ref, alog_ref, dtb_ref,
                       o_ref, sfin_ref, cnew_ref, xbuf):
    ns, c, _ = qkv_ref.shape
    r = ns * c
    keep = CONV_WIDTH - 1
    base = SUBLANES
    xbuf[:, base - keep:base, :] = cs_ref[...]
    xbuf[:, base:base + c, :] = qkv_ref[...]
    conv = _conv_silu(lambda j: xbuf[:, base - j:base - j + c, :], cw_ref).reshape(r, CONV_CH)
    cnew_ref[...] = xbuf[:, base + c - keep:base + c, :]

    same_le, same_lt, same = _block_masks(r, c)
    seq_of_row = lax.broadcasted_iota(jnp.int32, (r, GDN_DV), 0) // c
    (u, w, q_e, qk, k_end_t, dend), = _interleave([_gdn_prepass(
        lambda h: [conv[:, cols] for cols in _gdn_head_cols(h)], small_ref[...].reshape(r, LANES),
        alog_ref[...], dtb_ref[...], same_le, same_lt, same, c, tuple(range(GDN_HEADS)))])
    for h in range(GDN_HEADS):
        hs = slice(h * GDN_DK, (h + 1) * GDN_DK)
        v_parts, o_parts = [], []
        for s in range(ns):
            rows = slice(s * c, (s + 1) * c)
            ws = _dot(jnp.concatenate([w[h][rows], q_e[h][rows]], axis=0), s0_ref[s, hs, :])
            v_parts.append(u[h][rows] - ws[:c])
            o_parts.append(ws[c:])
        v_new = jnp.concatenate(v_parts, axis=0)
        o = jnp.concatenate(o_parts, axis=0) + _dot(qk[h], v_new)
        o_ref[:, h * GDN_DV:(h + 1) * GDN_DV] = o.astype(o_ref.dtype)
        for s in range(ns):
            upd = _dot(k_end_t[h], jnp.where(seq_of_row == s, v_new, 0.0))
            sfin_ref[s, hs, :] = s0_ref[s, hs, :] * jnp.exp(dend[h][s * c:s * c + 1, :]) + upd


def _gdn_prompt(qkv3, small3, conv_state, s0, conv_w, alog_v, dtb_v, c):
    bsz, t_len, _ = qkv3.shape
    assert bsz % GDN_BATCH_INTERLEAVE == 0 and t_len % GROUP_ROWS == 0 and GDN_HEADS % GDN_HEAD_GROUP == 0
    rows = GDN_HEADS * GDN_DK
    keep = CONV_WIDTH - 1
    r = GROUP_ROWS
    full3 = lambda t: (0, 0, 0)
    const2 = lambda t: (0, 0)
    return pl.pallas_call(
        functools.partial(_gdn_prompt_kernel, c=c),
        grid=(t_len // r,),
        in_specs=[
            pl.BlockSpec((bsz, r, CONV_CH), lambda t: (0, t, 0)),
            pl.BlockSpec((bsz, r, LANES), lambda t: (0, t, 0)),
            pl.BlockSpec((bsz, keep, CONV_CH), full3),
            pl.BlockSpec((bsz, rows, GDN_DV), full3),
            pl.BlockSpec((CONV_WIDTH, CONV_CH), const2),
            pl.BlockSpec((2 * GDN_HEADS, 1), const2),
            pl.BlockSpec((2 * GDN_HEADS, 1), const2),
        ],
        out_specs=[
            pl.BlockSpec((bsz, r, GDN_VAL), lambda t: (0, t, 0)),
            pl.BlockSpec((bsz, rows, GDN_DV), full3),
            pl.BlockSpec((bsz, keep, CONV_CH), full3),
        ],
        out_shape=[
            jax.ShapeDtypeStruct((bsz, t_len, GDN_VAL), BF16),
            jax.ShapeDtypeStruct((bsz, rows, GDN_DV), F32),
            jax.ShapeDtypeStruct((bsz, keep, CONV_CH), F32),
        ],
        scratch_shapes=[pltpu.VMEM((GDN_BATCH_INTERLEAVE, SUBLANES + r, CONV_CH), F32),
                        pltpu.VMEM((bsz, SUBLANES, CONV_CH), F32),
                        pltpu.VMEM((bsz, rows, GDN_DV), F32)],
        compiler_params=pltpu.CompilerParams(dimension_semantics=("arbitrary",),
                                             vmem_limit_bytes=VMEM_MIB["gdn_prompt"] * 2 ** 20),
        name="gdn_prompt",
    )(qkv3, small3, conv_state, s0, conv_w, alog_v, dtb_v)


def _gdn_sample(qkv3, small3, conv_state, s0, conv_w, alog_v, dtb_v):
    bsz, c, _ = qkv3.shape
    rows = GDN_HEADS * GDN_DK
    keep = CONV_WIDTH - 1
    ns = GROUP_ROWS // c
    grp = lambda g: (g, 0, 0)
    const2 = lambda g: (0, 0)
    return pl.pallas_call(
        _gdn_sample_kernel,
        grid=(bsz // ns,),
        in_specs=[
            pl.BlockSpec((ns, c, CONV_CH), grp),
            pl.BlockSpec((ns, c, LANES), grp),
            pl.BlockSpec((ns, keep, CONV_CH), grp),
            pl.BlockSpec((ns, rows, GDN_DV), grp),
            pl.BlockSpec((CONV_WIDTH, CONV_CH), const2),
            pl.BlockSpec((2 * GDN_HEADS, 1), const2),
            pl.BlockSpec((2 * GDN_HEADS, 1), const2),
        ],
        out_specs=[
            pl.BlockSpec((ns * c, GDN_VAL), lambda g: (g, 0)),
            pl.BlockSpec((ns, rows, GDN_DV), grp),
            pl.BlockSpec((ns, keep, CONV_CH), grp),
        ],
        out_shape=[
            jax.ShapeDtypeStruct((bsz * c, GDN_VAL), BF16),
            jax.ShapeDtypeStruct((bsz, rows, GDN_DV), F32),
            jax.ShapeDtypeStruct((bsz, keep, CONV_CH), F32),
        ],
        scratch_shapes=[pltpu.VMEM((ns, 2 * SUBLANES, CONV_CH), F32)],
        compiler_params=pltpu.CompilerParams(dimension_semantics=("arbitrary",),
                                             vmem_limit_bytes=VMEM_MIB["gdn_sample"] * 2 ** 20),
        name="gdn_sample",
    )(qkv3, small3, conv_state, s0, conv_w, alog_v, dtb_v)


def _head_norm_gate(o, silu_z, w):
    parts = []
    for h in range(o.shape[-1] // LANES):
        oh = o[:, h * LANES:(h + 1) * LANES]
        parts.append(oh * lax.rsqrt(jnp.mean(oh * oh, axis=-1, keepdims=True) + NORM_EPS) * w)
    return jnp.concatenate(parts, axis=-1) * silu_z


def _out_kernel(oa_ref, ob_ref, ga_ref, gb_ref, za_ref, zb_ref, x_ref, p_ref, anw_ref, bnw_ref,
                wua_ref, wub_ref, wout_ref, wpg_ref, wp_ref, fnw_ref, y_ref):
    f32 = lambda ref: ref[...].astype(F32)
    ya = _dot(_head_norm_gate(f32(oa_ref), f32(za_ref), anw_ref[...]), wua_ref[...])
    yb = _dot(_head_norm_gate(f32(ob_ref), f32(zb_ref), bnw_ref[...]), wub_ref[...])
    merged = f32(ga_ref) * ya + f32(gb_ref) * yb
    h1 = x_ref[...] + _dot(merged, wout_ref[...])
    h2 = h1 + _sigmoid(_dot(h1, wpg_ref[...])) * _dot(p_ref[...], wp_ref[...])
    y_ref[...] = h2 * lax.rsqrt(jnp.mean(h2 * h2, axis=-1, keepdims=True) + NORM_EPS) * fnw_ref[...]


def _out_stage(o_a, o_b, gz, x2d, p2d, anw, bnw, wua, wub, wout, wpg, wp, fnw):
    n = x2d.shape[0]
    tm = min(ROW_TILE, n)
    const = lambda i: (0, 0)
    return pl.pallas_call(
        _out_kernel,
        grid=(n // tm,),
        in_specs=[
            pl.BlockSpec((tm, GLA_VAL), lambda i: (i, 0)),
            pl.BlockSpec((tm, GDN_VAL), lambda i: (i, 0)),
            pl.BlockSpec((tm, D_MODEL), lambda i: (i, GZ_GATE_A // D_MODEL)),
            pl.BlockSpec((tm, D_MODEL), lambda i: (i, GZ_GATE_B // D_MODEL)),
            pl.BlockSpec((tm, GLA_VAL), lambda i: (i, GZ_Z_A // GLA_VAL)),
            pl.BlockSpec((tm, GDN_VAL), lambda i: (i, GZ_Z_B // GDN_VAL)),
            pl.BlockSpec((tm, D_MODEL), lambda i: (i, 0)),
            pl.BlockSpec((tm, PLE_DIM), lambda i: (i, 0)),
            pl.BlockSpec((1, GLA_DV), const),
            pl.BlockSpec((1, GDN_DV), const),
            pl.BlockSpec((GLA_VAL, D_MODEL), const),
            pl.BlockSpec((GDN_VAL, D_MODEL), const),
            pl.BlockSpec((D_MODEL, D_MODEL), const),
            pl.BlockSpec((D_MODEL, D_MODEL), const),
            pl.BlockSpec((PLE_DIM, D_MODEL), const),
            pl.BlockSpec((1, D_MODEL), const),
        ],
        out_specs=pl.BlockSpec((tm, D_MODEL), lambda i: (i, 0)),
        out_shape=jax.ShapeDtypeStruct((n, D_MODEL), F32),
        compiler_params=pltpu.CompilerParams(dimension_semantics=("arbitrary",),
                                             vmem_limit_bytes=VMEM_MIB["out_stage"] * 2 ** 20),
        name="out_stage",
    )(o_a, o_b, gz, gz, gz, gz, x2d, p2d, anw, bnw, wua, wub, wout, wpg, wp, fnw)


def _in_offsets():
    offs = [0]
    for s in IN_SPLITS:
        offs.append(offs[-1] + s)
    return offs


def _regroup_kernel(wt_ref, o_ref):
    offs = _in_offsets()
    (q_a, _, _, g_a, z_a, qkv_b, a_b, _, z_b, gate_a, gate_b, end) = offs
    piece = lambda lo, hi: wt_ref[lo:hi, :].astype(BF16)
    o_ref[P_GLA_QKV:P_GLA_QKV + GLA_QKV, :] = piece(q_a, g_a)
    o_ref[P_QKV_B:P_QKV_B + CONV_CH, :] = piece(qkv_b, a_b)
    small = jnp.concatenate([wt_ref[g_a:z_a, :], wt_ref[a_b:z_b, :],
                             jnp.zeros((LANES - (z_a - g_a) - (z_b - a_b), wt_ref.shape[1]), F32)], axis=0)
    o_ref[P_SMALL:P_SMALL + LANES, :] = small.astype(BF16)
    o_ref[P_GATES + GZ_GATE_A:P_GATES + GZ_GATE_A + D_MODEL, :] = piece(gate_a, gate_b)
    o_ref[P_GATES + GZ_GATE_B:P_GATES + GZ_GATE_B + D_MODEL, :] = piece(gate_b, end)
    o_ref[P_GATES + GZ_Z_A:P_GATES + GZ_Z_A + GLA_VAL, :] = piece(z_a, qkv_b)
    o_ref[P_GATES + GZ_Z_B:P_GATES + GZ_Z_B + GDN_VAL, :] = piece(z_b, gate_a)


def _regroup_w_in(w_in_t):
    cols = 256
    return pl.pallas_call(
        _regroup_kernel,
        grid=(D_MODEL // cols,),
        in_specs=[pl.BlockSpec((None, w_in_t.shape[1], cols), lambda i: (0, 0, i))],
        out_specs=pl.BlockSpec((P_COLS, cols), lambda i: (0, i)),
        out_shape=jax.ShapeDtypeStruct((P_COLS, D_MODEL), BF16),
        compiler_params=pltpu.CompilerParams(dimension_semantics=("arbitrary",),
                                             vmem_limit_bytes=VMEM_MIB["regroup_w_in"] * 2 ** 20),
        name="regroup_w_in",
    )(w_in_t)


CAST_STEPS = 4


def _cast_kernel(*refs):
    n = len(refs) // 2
    for w_ref, o_ref in zip(refs[:n], refs[n:]):
        o_ref[...] = w_ref[...].astype(o_ref.dtype)


def _cast_weights(ws):
    rows = [w.shape[1] // CAST_STEPS for w in ws]
    return pl.pallas_call(
        _cast_kernel,
        grid=(CAST_STEPS,),
        in_specs=[pl.BlockSpec((None, r, w.shape[2]), lambda i: (0, i, 0)) for w, r in zip(ws, rows)],
        out_specs=[pl.BlockSpec((r, w.shape[2]), lambda i: (i, 0)) for w, r in zip(ws, rows)],
        out_shape=[jax.ShapeDtypeStruct(w.shape[1:], BF16) for w in ws],
        compiler_params=pltpu.CompilerParams(dimension_semantics=("arbitrary",),
                                             vmem_limit_bytes=VMEM_MIB["cast_weights"] * 2 ** 20),
        name="cast_weights",
    )(*ws)


def _head_param_col(v):
    return jnp.zeros((2 * GDN_HEADS, 1), F32).at[:GDN_HEADS, 0].set(v.astype(F32))


def _trunk(x, p, s_gla, s_gdn, conv_state, wts):
    bsz, t_len, _ = x.shape
    c = min(PROMPT_CHUNK, t_len)
    n = bsz * t_len
    x2d = x.reshape(n, D_MODEL)
    p2d = p.reshape(n, PLE_DIM)
    gla_in, gdn_in, small, gz = _inproj(x2d, wts["norm_w"], wts["w_in_r"])
    gla_in3 = gla_in.reshape(bsz, t_len, GLA_QKV)
    gdn_in3 = gdn_in.reshape(bsz, t_len, CONV_CH)
    small3 = small.reshape(bsz, t_len, LANES)
    s_gla2 = s_gla.reshape(bsz, GLA_HEADS * GLA_DK, GLA_DV)
    s_gdn2 = s_gdn.reshape(bsz, GDN_HEADS * GDN_DK, GDN_DV)
    if t_len % GROUP_ROWS == 0:
        o_a, gla_fin = _gla_prompt(gla_in3, small3, s_gla2, wts["wgg"], wts["bgg"], c)
        o_b, gdn_fin, conv_new = _gdn_prompt(gdn_in3, small3, conv_state, s_gdn2, wts["conv_w"],
                                             wts["alog_v"], wts["dtb_v"], c)
    else:
        assert GROUP_ROWS % t_len == 0 and bsz % (GROUP_ROWS // t_len) == 0 and t_len >= CONV_WIDTH - 1
        o_a, gla_fin = _gla_sample(gla_in3, small3, s_gla2, wts["wgg"], wts["bgg"])
        o_b, gdn_fin, conv_new = _gdn_sample(gdn_in3, small3, conv_state, s_gdn2, wts["conv_w"],
                                             wts["alog_v"], wts["dtb_v"])
    y = _out_stage(o_a.reshape(n, GLA_VAL), o_b.reshape(n, GDN_VAL), gz, x2d, p2d, wts["anw"], wts["bnw"],
                   wts["wua"], wts["wub"], wts["wout"], wts["wpg"], wts["wp"], wts["fnw"])
    return (y.reshape(bsz, t_len, D_MODEL),
            gla_fin.reshape(1, bsz, GLA_HEADS, GLA_DK, GLA_DV),
            gdn_fin.reshape(1, bsz, GDN_HEADS, GDN_DK, GDN_DV),
            conv_new.reshape(1, bsz, CONV_WIDTH - 1, CONV_CH))


def kernel(x_prompt, x_sample, state_gla, state_gdn, state_conv, p_prompt, p_sample, norm_w, w_in,
           w_gla_gate, b_gla_gate, gla_norm_w, conv_w, gdn_a_log, gdn_dt_bias, gdn_norm_w,
           w_up_gla, w_up_gdn, w_out, w_ple_gate, w_ple, final_norm_w):
    wgg = jnp.zeros((LANES, GLA_KEY), F32).at[SM_G:SM_G + GLA_GATE_RANK].set(w_gla_gate[0]).astype(BF16)
    wts = {
        "norm_w": norm_w[0].reshape(1, D_MODEL),
        "w_in_r": _regroup_w_in(jnp.swapaxes(w_in, 1, 2)),
        "wgg": wgg,
        "bgg": b_gla_gate[0].reshape(1, GLA_KEY),
        "conv_w": conv_w[0],
        "alog_v": _head_param_col(gdn_a_log[0]),
        "dtb_v": _head_param_col(gdn_dt_bias[0]),
        "anw": gla_norm_w[0].reshape(1, GLA_DV),
        "bnw": gdn_norm_w[0].reshape(1, GDN_DV),
        "fnw": final_norm_w.reshape(1, D_MODEL),
    }
    wts["wua"], wts["wub"], wts["wout"], wts["wpg"], wts["wp"] = _cast_weights(
        [w_up_gla, w_up_gdn, w_out, w_ple_gate, w_ple])
    bsz = x_prompt.shape[0]
    dt = x_prompt.dtype
    y_p, gla_p, gdn_p, conv_p = _trunk(
        x_prompt, p_prompt[0],
        jnp.zeros((bsz, GLA_HEADS, GLA_DK, GLA_DV), dt), jnp.zeros((bsz, GDN_HEADS, GDN_DK, GDN_DV), dt),
        jnp.zeros((bsz, CONV_WIDTH - 1, CONV_CH), dt), wts)
    y_s, gla_s, gdn_s, conv_s = _trunk(x_sample, p_sample[0], state_gla[0], state_gdn[0], state_conv[0], wts)
    return (y_p, y_s, gla_p, gdn_p, conv_p, gla_s, gdn_s, conv_s)
```

```python
import functools

import jax
import jax.numpy as jnp
from jax import lax
from jax.experimental import pallas as pl
from jax.experimental.pallas import tpu as pltpu

F32 = jnp.float32
BF16 = jnp.bfloat16

D_MODEL = 1024
PLE_DIM = 256
NORM_EPS = 1e-6
GLA_HEADS = 4
GLA_DK = 64
GLA_DV = 128
GLA_KEY = GLA_HEADS * GLA_DK
GLA_VAL = GLA_HEADS * GLA_DV
GLA_GATE_RANK = 16
GLA_GATE_TEMP = 16.0
GDN_HEADS = 4
GDN_DK = 128
GDN_DV = 128
GDN_KEY = GDN_HEADS * GDN_DK
GDN_VAL = GDN_HEADS * GDN_DV
CONV_WIDTH = 4
CONV_CH = 2 * GDN_KEY + GDN_VAL
IN_SPLITS = (GLA_KEY, GLA_KEY, GLA_VAL, GLA_GATE_RANK, GLA_VAL, CONV_CH, GDN_HEADS, GDN_HEADS,
             GDN_VAL, D_MODEL, D_MODEL)

LANES = 128
SUBLANES = 8

GLA_QKV = 2 * GLA_KEY + GLA_VAL
P_GLA_QKV = 0
P_QKV_B = P_GLA_QKV + GLA_QKV
P_SMALL = P_QKV_B + CONV_CH
P_GATES = P_SMALL + LANES
GZ_COLS = 2 * D_MODEL + GLA_VAL + GDN_VAL
P_COLS = P_GATES + GZ_COLS
GZ_GATE_A = 0
GZ_GATE_B = D_MODEL
GZ_Z_A = 2 * D_MODEL
GZ_Z_B = 2 * D_MODEL + GLA_VAL
SM_G = 0
SM_A = GLA_GATE_RANK
SM_B = GLA_GATE_RANK + GDN_HEADS

PROMPT_CHUNK = 64
GROUP_ROWS = 128
BATCH_INTERLEAVE = 8
GDN_BATCH_INTERLEAVE = 8
GDN_HEAD_GROUP = 4
INVERSE_BASE_BLOCK = 16
ROW_TILE = 512
VMEM_MIB = {"regroup_w_in": 40, "inproj": 40, "gla_prompt": 40, "gla_sample": 40, "gdn_prompt": 48,
            "gdn_sample": 40, "out_stage": 40}


def _dot(a, b):
    return jnp.dot(a.astype(BF16), b.astype(BF16), preferred_element_type=F32)


def _dot_nt(a, b):
    return lax.dot_general(a.astype(BF16), b.astype(BF16), (((1,), (1,)), ((), ())),
                           preferred_element_type=F32)


def _split2(x):
    h1 = x.astype(BF16)
    return h1, (x - h1.astype(F32)).astype(BF16)


def _cumsum_rows(tri, x):
    x1, x2 = _split2(x)
    d = functools.partial(jnp.dot, preferred_element_type=F32)
    return d(tri, x1) + d(tri, x2)


def _softplus(x):
    return jnp.maximum(x, 0.0) + jnp.log(1.0 + jnp.exp(-jnp.abs(x)))


def _sigmoid(x):
    return 0.5 * jnp.tanh(0.5 * x) + 0.5


def _silu(x):
    half = 0.5 * x
    return half * jnp.tanh(half) + half


def _interleave(emitters):
    results = [None] * len(emitters)
    live = list(range(len(emitters)))
    while live:
        for i in list(live):
            try:
                next(emitters[i])
            except StopIteration as stop:
                results[i] = stop.value
                live.remove(i)
    return results


def _block_masks(r, block):
    row = lax.broadcasted_iota(jnp.int32, (r, r), 0)
    col = lax.broadcasted_iota(jnp.int32, (r, r), 1)
    same = (row // block) == (col // block)
    return same & (row >= col), same & (row > col), same


INPROJ_COL_STEP = 512


def _inproj_kernel(x_ref, nw_ref, w_ref, gla_ref, gdn_ref, small_ref, gz_ref):
    x = x_ref[...]
    xn = x * lax.rsqrt(jnp.mean(x * x, axis=-1, keepdims=True) + NORM_EPS) * nw_ref[...]
    xb = xn.astype(BF16)

    def tasks(o_ref, w0, width, act=None, o0=0):
        out = []
        for c0 in range(0, width, INPROJ_COL_STEP):
            c1 = min(c0 + INPROJ_COL_STEP, width)

            def task(c0=c0, c1=c1):
                res = _dot_nt(xb, w_ref[w0 + c0:w0 + c1, :])
                o_ref[:, o0 + c0:o0 + c1] = (res if act is None else act(res)).astype(o_ref.dtype)
            out.append(task)
        return out

    plain = tasks(gla_ref, P_GLA_QKV, GLA_QKV) + tasks(gdn_ref, P_QKV_B, CONV_CH) + tasks(small_ref, P_SMALL, LANES)
    activated = (tasks(gz_ref, P_GATES + GZ_GATE_A, 2 * D_MODEL, act=_sigmoid, o0=GZ_GATE_A)
                 + tasks(gz_ref, P_GATES + GZ_Z_A, GLA_VAL + GDN_VAL, act=_silu, o0=GZ_Z_A))
    for i in range(max(len(plain), len(activated))):
        for group in (activated, plain):
            if i < len(group):
                group[i]()


def _inproj(x2d, norm_w, w_in_r):
    n = x2d.shape[0]
    tm = min(ROW_TILE, n)
    rows = lambda i: (i, 0)
    return pl.pallas_call(
        _inproj_kernel,
        grid=(n // tm,),
        in_specs=[
            pl.BlockSpec((tm, D_MODEL), rows),
            pl.BlockSpec((1, D_MODEL), lambda i: (0, 0)),
            pl.BlockSpec((P_COLS, D_MODEL), lambda i: (0, 0), pipeline_mode=pl.Buffered(1)),
        ],
        out_specs=[
            pl.BlockSpec((tm, GLA_QKV), rows),
            pl.BlockSpec((tm, CONV_CH), rows),
            pl.BlockSpec((tm, LANES), rows),
            pl.BlockSpec((tm, GZ_COLS), rows),
        ],
        out_shape=[
            jax.ShapeDtypeStruct((n, GLA_QKV), F32),
            jax.ShapeDtypeStruct((n, CONV_CH), F32),
            jax.ShapeDtypeStruct((n, LANES), F32),
            jax.ShapeDtypeStruct((n, GZ_COLS), BF16),
        ],
        compiler_params=pltpu.CompilerParams(dimension_semantics=("arbitrary",),
                                             vmem_limit_bytes=VMEM_MIB["inproj"] * 2 ** 20),
        name="inproj",
    )(x2d, norm_w, w_in_r)


GLA_PAIRS = GLA_HEADS // 2
GLA_SUB_BLOCK = 16


def _gla_att_levels(r, block):
    row = lax.broadcasted_iota(jnp.int32, (r, r), 0)
    col = lax.broadcasted_iota(jnp.int32, (r, r), 1)
    sub = min(block, GLA_SUB_BLOCK)
    levels = [(None, ((row // sub) == (col // sub)) & (row >= col))]
    half = sub
    while half < block:
        levels.append((half, ((row // (2 * half)) == (col // (2 * half)))
                       & ((row // half) % 2 == 1) & ((col // half) % 2 == 0)))
        half *= 2
    return levels


def _rows_at(x, n, offset):
    return jnp.concatenate([jnp.broadcast_to(x[i + offset:i + offset + 1, :], (n, x.shape[1]))
                            for i in range(0, x.shape[0], n)], axis=0)


def _gla_prepass(q, k, v, sm, wgg, bgg, same_le, same, levels, block):
    r = q.shape[0]
    heads = range(GLA_HEADS)
    pairs = range(GLA_PAIRS)
    pre = _dot(sm, wgg) + bgg
    yield
    gk = (jnp.minimum(pre, 0.0) - jnp.log(1.0 + jnp.exp(-jnp.abs(pre)))) * (1.0 / GLA_GATE_TEMP)
    sums = _cumsum_rows(jnp.concatenate([jnp.where(same_le, 1.0, 0.0).astype(BF16),
                                         jnp.where(same, 1.0, 0.0).astype(BF16)], axis=0), gk)
    yield
    bcum = sums[:r]
    bend = sums[r:]
    bex = bcum - gk
    scale = GLA_DK ** -0.5
    q_e = q * jnp.exp(bcum) * scale
    lane = lax.broadcasted_iota(jnp.int32, (r, LANES), 1)
    in_head = [lane < GLA_DK, lane >= GLA_DK]
    pl_ = [slice(p * LANES, (p + 1) * LANES) for p in pairs]
    head_only = lambda x, h: jnp.where(in_head[h % 2], x[:, pl_[h // 2]], 0.0)
    qm = [head_only(q_e, h) for h in heads]
    sub = min(block, GLA_SUB_BLOCK)
    start = _rows_at(bex, sub, 0)
    groups = {None: (q * jnp.exp(bcum - start) * scale, [(levels[0][1], k * jnp.exp(start - bcum))])}
    for half, mask in levels[1:]:
        key = None if half == sub else half
        if key not in groups:
            groups[key] = (q * jnp.exp(bcum - _rows_at(bex, half, 0)) * scale, [])
        groups[key][1].append((mask, k * jnp.exp(_rows_at(bcum, half, half - 1) - bcum)))
    att = [jnp.zeros((r, r), F32)] * GLA_HEADS
    for q_l, parts in groups.values():
        for h in heads:
            keys = [k_l[:, pl_[h // 2]].astype(BF16) for _, k_l in parts]
            prod = _dot_nt(head_only(q_l, h), keys[0] if len(keys) == 1 else jnp.concatenate(keys, axis=0))
            for i, (mask, _) in enumerate(parts):
                att[h] = jnp.where(mask, prod[:, i * r:(i + 1) * r], att[h])
    yield
    o_intra = [_dot(att[h], v[h]) for h in heads]
    k_end = k * jnp.exp(bend - bcum)
    k_end_t = [k_end[:, pl_[p]].T for p in pairs]
    bend_t = [bend[:, pl_[p]].T for p in pairs]
    yield
    return qm, o_intra, k_end_t, bend_t


def _gla_prompt_kernel(qkv_ref, small_ref, s0_ref, wgg_ref, bgg_ref, o_ref, sfin_ref, s_scr, *, c):
    t = pl.program_id(0)
    nb, r, _ = qkv_ref.shape

    @pl.when(t == 0)
    def _():
        s_scr[...] = s0_ref[...]

    same_le, _, same = _block_masks(r, c)
    levels = _gla_att_levels(r, c)
    zeros = jnp.zeros((c, GLA_DV), F32)
    n_sub = r // c
    heads = range(GLA_HEADS)
    pairs = range(GLA_PAIRS)

    def one_batch(b):
        q = qkv_ref[b, :, 0:GLA_KEY]
        k = qkv_ref[b, :, GLA_KEY:2 * GLA_KEY]
        v = [qkv_ref[b, :, 2 * GLA_KEY + h * GLA_DV:2 * GLA_KEY + (h + 1) * GLA_DV] for h in heads]
        qm, o_intra, k_end_t, bend_t = yield from _gla_prepass(
            q, k, v, small_ref[b], wgg_ref[...], bgg_ref[...], same_le, same, levels, c)
        s = [s_scr[b, p * LANES:(p + 1) * LANES, :] for p in pairs]
        for i in range(n_sub):
            rows = slice(i * c, (i + 1) * c)
            ws = [_dot(jnp.concatenate([qm[2 * p][rows], qm[2 * p + 1][rows]], axis=0), s[p]) for p in pairs]
            padded = [jnp.concatenate([zeros] * i + [v[h][rows]] + [zeros] * (n_sub - 1 - i), axis=0)
                      for h in heads]
            upd = [_dot(k_end_t[h // 2][(h % 2) * GLA_DK:(h % 2 + 1) * GLA_DK, :], padded[h]) for h in heads]
            yield
            for h in heads:
                o_ref[b, rows, h * GLA_DV:(h + 1) * GLA_DV] = (
                    o_intra[h][rows] + ws[h // 2][(h % 2) * c:(h % 2 + 1) * c]).astype(o_ref.dtype)
            s = [s[p] * jnp.exp(bend_t[p][:, i * c:i * c + 1])
                 + jnp.concatenate([upd[2 * p], upd[2 * p + 1]], axis=0) for p in pairs]
        for p in pairs:
            s_scr[b, p * LANES:(p + 1) * LANES, :] = s[p]

    def per_step(i, carry):
        _interleave([one_batch(i * BATCH_INTERLEAVE + j) for j in range(BATCH_INTERLEAVE)])
        return carry

    lax.fori_loop(0, nb // BATCH_INTERLEAVE, per_step, None)

    @pl.when(t == pl.num_programs(0) - 1)
    def _():
        sfin_ref[...] = s_scr[...]


def _gla_sample_kernel(qkv_ref, small_ref, s0_ref, wgg_ref, bgg_ref, o_ref, sfin_ref):
    ns, c, _ = qkv_ref.shape
    r = ns * c
    heads = range(GLA_HEADS)
    same_le, _, same = _block_masks(r, c)
    levels = _gla_att_levels(r, c)
    q = qkv_ref[:, :, 0:GLA_KEY].reshape(r, GLA_KEY)
    k = qkv_ref[:, :, GLA_KEY:2 * GLA_KEY].reshape(r, GLA_KEY)
    v = [qkv_ref[:, :, 2 * GLA_KEY + h * GLA_DV:2 * GLA_KEY + (h + 1) * GLA_DV].reshape(r, GLA_DV)
         for h in heads]
    (qm, o_intra, k_end_t, bend_t), = _interleave([_gla_prepass(
        q, k, v, small_ref[...].reshape(r, LANES), wgg_ref[...], bgg_ref[...], same_le, same, levels, c)])
    seq_of_row = lax.broadcasted_iota(jnp.int32, (r, GLA_DV), 0) // c
    for p in range(GLA_PAIRS):
        ps = slice(p * LANES, (p + 1) * LANES)
        inter = [[], []]
        for s in range(ns):
            rows = slice(s * c, (s + 1) * c)
            ws = _dot(jnp.concatenate([qm[2 * p][rows], qm[2 * p + 1][rows]], axis=0), s0_ref[s, ps, :])
            inter[0].append(ws[:c])
            inter[1].append(ws[c:])
        for hh in range(2):
            h = 2 * p + hh
            o = o_intra[h] + jnp.concatenate(inter[hh], axis=0)
            o_ref[:, h * GLA_DV:(h + 1) * GLA_DV] = o.astype(o_ref.dtype)
        for s in range(ns):
            upd = [_dot(k_end_t[p][hh * GLA_DK:(hh + 1) * GLA_DK, :],
                        jnp.where(seq_of_row == s, v[2 * p + hh], 0.0)) for hh in range(2)]
            sfin_ref[s, ps, :] = (s0_ref[s, ps, :] * jnp.exp(bend_t[p][:, s * c:s * c + 1])
                                  + jnp.concatenate(upd, axis=0))


def _gla_prompt(qkv3, small3, s0, wgg, bgg, c):
    bsz, t_len, _ = qkv3.shape
    assert bsz % BATCH_INTERLEAVE == 0 and t_len % GROUP_ROWS == 0
    rows = GLA_HEADS * GLA_DK
    r = GROUP_ROWS
    full3 = lambda t: (0, 0, 0)
    const2 = lambda t: (0, 0)
    return pl.pallas_call(
        functools.partial(_gla_prompt_kernel, c=c),
        grid=(t_len // r,),
        in_specs=[
            pl.BlockSpec((bsz, r, GLA_QKV), lambda t: (0, t, 0)),
            pl.BlockSpec((bsz, r, LANES), lambda t: (0, t, 0)),
            pl.BlockSpec((bsz, rows, GLA_DV), full3),
            pl.BlockSpec((LANES, GLA_KEY), const2),
            pl.BlockSpec((1, GLA_KEY), const2),
        ],
        out_specs=[
            pl.BlockSpec((bsz, r, GLA_VAL), lambda t: (0, t, 0)),
            pl.BlockSpec((bsz, rows, GLA_DV), full3),
        ],
        out_shape=[
            jax.ShapeDtypeStruct((bsz, t_len, GLA_VAL), BF16),
            jax.ShapeDtypeStruct((bsz, rows, GLA_DV), F32),
        ],
        scratch_shapes=[pltpu.VMEM((bsz, rows, GLA_DV), F32)],
        compiler_params=pltpu.CompilerParams(dimension_semantics=("arbitrary",),
                                             vmem_limit_bytes=VMEM_MIB["gla_prompt"] * 2 ** 20),
        name="gla_prompt",
    )(qkv3, small3, s0, wgg, bgg)


def _gla_sample(qkv3, small3, s0, wgg, bgg):
    bsz, c, _ = qkv3.shape
    rows = GLA_HEADS * GLA_DK
    ns = GROUP_ROWS // c
    grp = lambda g: (g, 0, 0)
    const2 = lambda g: (0, 0)
    return pl.pallas_call(
        _gla_sample_kernel,
        grid=(bsz // ns,),
        in_specs=[
            pl.BlockSpec((ns, c, GLA_QKV), grp),
            pl.BlockSpec((ns, c, LANES), grp),
            pl.BlockSpec((ns, rows, GLA_DV), grp),
            pl.BlockSpec((LANES, GLA_KEY), const2),
            pl.BlockSpec((1, GLA_KEY), const2),
        ],
        out_specs=[
            pl.BlockSpec((ns * c, GLA_VAL), lambda g: (g, 0)),
            pl.BlockSpec((ns, rows, GLA_DV), grp),
        ],
        out_shape=[
            jax.ShapeDtypeStruct((bsz * c, GLA_VAL), BF16),
            jax.ShapeDtypeStruct((bsz, rows, GLA_DV), F32),
        ],
        compiler_params=pltpu.CompilerParams(dimension_semantics=("arbitrary",),
                                             vmem_limit_bytes=VMEM_MIB["gla_sample"] * 2 ** 20),
        name="gla_sample",
    )(qkv3, small3, s0, wgg, bgg)


def _block_unit_lower_inverse(a_list, block):
    r = a_list[0].shape[0]
    row = lax.broadcasted_iota(jnp.int32, (r, r), 0)
    col = lax.broadcasted_iota(jnp.int32, (r, r), 1)
    in_block = lambda n: (row // n) == (col // n)
    base = min(block, INVERSE_BASE_BLOCK)
    a_base = a_list if base == block else [jnp.where(in_block(base), a, 0.0) for a in a_list]
    xs = [jnp.where(row == col, 1.0, 0.0) - a for a in a_base]
    ps = [_dot(a, a) for a in a_base]
    n = 2
    while n < base:
        yield
        last = 2 * n >= base
        ms = [_dot(x if last else jnp.concatenate([p.astype(BF16), x.astype(BF16)], axis=0), p)
              for p, x in zip(ps, xs)]
        ps = [m[:r] for m in ms]
        xs = [x + m[-r:] for x, m in zip(xs, ms)]
        n *= 2
    n = base
    while n < block:
        yield
        between = in_block(2 * n) & jnp.logical_not(in_block(n))
        ts = [_dot(jnp.where(between, a, 0.0), x) for a, x in zip(a_list, xs)]
        yield
        xs = [x - _dot(x, t) for x, t in zip(xs, ts)]
        n *= 2
    return xs


def _gdn_decays(sm, alog_c, dtb_c, same_lt, same):
    r = sm.shape[0]
    heads = range(GDN_HEADS)
    ab = sm.T[SM_A:SM_A + 2 * GDN_HEADS, :]
    g8 = -jnp.exp(alog_c) * _softplus(ab + dtb_c)
    beta_c = _sigmoid(ab).T
    g1, g2 = _split2(g8)
    same_ge = same & jnp.logical_not(same_lt)
    sel = jnp.concatenate([jnp.where(same_ge, 1.0, 0.0).astype(BF16),
                           jnp.where(same, 1.0, 0.0).astype(BF16)], axis=1)
    sums = jnp.dot(jnp.concatenate([g1, g2], axis=0), sel, preferred_element_type=F32)
    sums = sums[:2 * GDN_HEADS] + sums[2 * GDN_HEADS:]
    dec_rows = sums[:, :r]
    dec_c = dec_rows.T
    dend_c = sums[:, r:].T
    return ([dec_c[:, h:h + 1] for h in heads], [dec_rows[h:h + 1, :] for h in heads],
            [dend_c[:, h:h + 1] for h in heads], [beta_c[:, GDN_HEADS + h:GDN_HEADS + h + 1] for h in heads])


def _gdn_head_cols(h):
    return [slice(part * GDN_KEY + h * GDN_DK, part * GDN_KEY + (h + 1) * GDN_DK) for part in range(3)]


def _gdn_prepass(qkv_of_head, sm, alog_c, dtb_c, same_le, same_lt, same, block, heads):
    dcol, drow, dend, beta = ([x[h] for h in heads] for x in _gdn_decays(sm, alog_c, dtb_c, same_lt, same))
    idx = range(len(heads))
    gamma = [jnp.where(same_le, jnp.exp(jnp.where(same_le, dcol[i] - drow[i], 0.0)), 0.0) for i in idx]
    q, k, v = (list(x) for x in zip(*[qkv_of_head(h) for h in heads]))
    r = q[0].shape[0]
    q = [x * lax.rsqrt(jnp.sum(x * x, axis=-1, keepdims=True) + NORM_EPS) * (GDN_DK ** -0.5) for x in q]
    k = [x * lax.rsqrt(jnp.sum(x * x, axis=-1, keepdims=True) + NORM_EPS) for x in k]
    yield
    kb = [k[i] * beta[i] for i in idx]
    vb = [v[i] * beta[i] for i in idx]
    kq = [_dot_nt(jnp.concatenate([kb[i], q[i]], axis=0), k[i]) for i in idx]
    yield
    a_mat = [jnp.where(same_lt, kq[i][:r] * gamma[i], 0.0) for i in idx]
    qk = [(kq[i][r:] * gamma[i]).astype(BF16) for i in idx]
    t_inv = yield from _block_unit_lower_inverse(a_mat, block)
    edec = [jnp.exp(dcol[i]) for i in idx]
    uw = [_dot(t_inv[i], jnp.concatenate([vb[i], kb[i] * edec[i]], axis=1)) for i in idx]
    q_e = [(q[i] * edec[i]).astype(BF16) for i in idx]
    k_end_t = [(k[i] * jnp.exp(dend[i] - dcol[i])).T.astype(BF16) for i in idx]
    yield
    u = [x[:, :GDN_DV] for x in uw]
    w = [x[:, GDN_DV:].astype(BF16) for x in uw]
    return u, w, q_e, qk, k_end_t, dend


def _conv_silu(win, cw_ref):
    conv = win(0) * cw_ref[CONV_WIDTH - 1:CONV_WIDTH, :]
    for j in range(1, CONV_WIDTH):
        conv = conv + win(j) * cw_ref[CONV_WIDTH - 1 - j:CONV_WIDTH - j, :]
    return _silu(conv)

def _gdn_prompt_kernel(qkv_ref, small_ref, cs_ref, s0_ref, cw_ref, alog_ref, dtb_ref,
                       o_ref, sfin_ref, cnew_ref, xbuf, prev, s_scr, *, c):
    t = pl.program_id(0)
    nb, r, _ = qkv_ref.shape
    keep = CONV_WIDTH - 1
    base = SUBLANES

    @pl.when(t == 0)
    def _():
        s_scr[...] = s0_ref[...]
        prev[:, base - keep:base, :] = cs_ref[...]

    same_le, same_lt, same = _block_masks(r, c)
    zeros = jnp.zeros((c, GDN_DV), F32)
    n_sub = r // c

    def one_batch(b, slot, heads, first):
        xb = xbuf.at[slot]
        if first:
            xb[base - keep:base, :] = prev[b, base - keep:base, :]
            xb[base:base + r, :] = qkv_ref[b]
            prev[b, base - keep:base, :] = xb[base + r - keep:base + r, :]

        def qkv_of_head(h):
            return [_conv_silu(lambda j: xb[base - j:base - j + r, cols], cw_ref.at[:, cols])
                    for cols in _gdn_head_cols(h)]

        u, w, q_e, qk, k_end_t, dend = yield from _gdn_prepass(
            qkv_of_head, small_ref[b], alog_ref[...], dtb_ref[...], same_le, same_lt, same, c, heads)
        idx = range(len(heads))
        hs = [slice(h * GDN_DK, (h + 1) * GDN_DK) for h in heads]
        s = [s_scr[b, hs[i], :] for i in idx]
        for ci in range(n_sub):
            rows = slice(ci * c, (ci + 1) * c)
            ws = [_dot(jnp.concatenate([w[i][rows], q_e[i][rows]], axis=0), s[i]) for i in idx]
            yield
            v_new = [u[i][rows] - ws[i][:c] for i in idx]
            padded = [jnp.concatenate([zeros] * ci + [v_new[i]] + [zeros] * (n_sub - 1 - ci), axis=0)
                      for i in idx]
            upd = [_dot(jnp.concatenate([qk[i][rows], k_end_t[i]], axis=0), padded[i]) for i in idx]
            yield
            for i in idx:
                o_ref[b, rows, hs[i]] = (ws[i][c:] + upd[i][:c]).astype(o_ref.dtype)
            s = [s[i] * jnp.exp(dend[i][ci * c:ci * c + 1, :]) + upd[i][c:] for i in idx]
        for i in idx:
            s_scr[b, hs[i], :] = s[i]

    def per_step(step, carry):
        for g in range(0, GDN_HEADS, GDN_HEAD_GROUP):
            heads = tuple(range(g, g + GDN_HEAD_GROUP))
            _interleave([one_batch(step * GDN_BATCH_INTERLEAVE + j, j, heads, g == 0)
                         for j in range(GDN_BATCH_INTERLEAVE)])
        return carry

    lax.fori_loop(0, nb // GDN_BATCH_INTERLEAVE, per_step, None)

    @pl.when(t == pl.num_programs(0) - 1)
    def _():
        sfin_ref[...] = s_scr[...]
        cnew_ref[...] = prev[:, base - keep:base, :]


def _gdn_sample_kernel(qkv_ref, small_ref, cs_ref, s0_ref, cw_ref, alog_ref, dtb_ref,
                       o_ref, sfin_ref, cnew_ref, xbuf):
    ns, c, _ = qkv_ref.shape
    r = ns * c
    keep = CONV_WIDTH - 1
    base = SUBLANES
    xbuf[:, base - keep:base, :] = cs_ref[...]
    xbuf[:, base:base + c, :] = qkv_ref[...]
    conv = _conv_silu(lambda j: xbuf[:, base - j:base - j + c, :], cw_ref).reshape(r, CONV_CH)
    cnew_ref[...] = xbuf[:, base + c - keep:base + c, :]

    same_le, same_lt, same = _block_masks(r, c)
    seq_of_row = lax.broadcasted_iota(jnp.int32, (r, GDN_DV), 0) // c
    (u, w, q_e, qk, k_end_t, dend), = _interleave([_gdn_prepass(
        lambda h: [conv[:, cols] for cols in _gdn_head_cols(h)], small_ref[...].reshape(r, LANES),
        alog_ref[...], dtb_ref[...], same_le, same_lt, same, c, tuple(range(GDN_HEADS)))])
    for h in range(GDN_HEADS):
        hs = slice(h * GDN_DK, (h + 1) * GDN_DK)
        v_parts, o_parts = [], []
        for s in range(ns):
            rows = slice(s * c, (s + 1) * c)
            ws = _dot(jnp.concatenate([w[h][rows], q_e[h][rows]], axis=0), s0_ref[s, hs, :])
            v_parts.append(u[h][rows] - ws[:c])
            o_parts.append(ws[c:])
        v_new = jnp.concatenate(v_parts, axis=0)
        o = jnp.concatenate(o_parts, axis=0) + _dot(qk[h], v_new)
        o_ref[:, h * GDN_DV:(h + 1) * GDN_DV] = o.astype(o_ref.dtype)
        for s in range(ns):
            upd = _dot(k_end_t[h], jnp.where(seq_of_row == s, v_new, 0.0))
            sfin_ref[s, hs, :] = s0_ref[s, hs, :] * jnp.exp(dend[h][s * c:s * c + 1, :]) + upd


def _gdn_prompt(qkv3, small3, conv_state, s0, conv_w, alog_v, dtb_v, c):
    bsz, t_len, _ = qkv3.shape
    assert bsz % GDN_BATCH_INTERLEAVE == 0 and t_len % GROUP_ROWS == 0 and GDN_HEADS % GDN_HEAD_GROUP == 0
    rows = GDN_HEADS * GDN_DK
    keep = CONV_WIDTH - 1
    r = GROUP_ROWS
    full3 = lambda t: (0, 0, 0)
    const2 = lambda t: (0, 0)
    return pl.pallas_call(
        functools.partial(_gdn_prompt_kernel, c=c),
        grid=(t_len // r,),
        in_specs=[
            pl.BlockSpec((bsz, r, CONV_CH), lambda t: (0, t, 0)),
            pl.BlockSpec((bsz, r, LANES), lambda t: (0, t, 0)),
            pl.BlockSpec((bsz, keep, CONV_CH), full3),
            pl.BlockSpec((bsz, rows, GDN_DV), full3),
            pl.BlockSpec((CONV_WIDTH, CONV_CH), const2),
            pl.BlockSpec((2 * GDN_HEADS, 1), const2),
            pl.BlockSpec((2 * GDN_HEADS, 1), const2),
        ],
        out_specs=[
            pl.BlockSpec((bsz, r, GDN_VAL), lambda t: (0, t, 0)),
            pl.BlockSpec((bsz, rows, GDN_DV), full3),
            pl.BlockSpec((bsz, keep, CONV_CH), full3),
        ],
        out_shape=[
            jax.ShapeDtypeStruct((bsz, t_len, GDN_VAL), BF16),
            jax.ShapeDtypeStruct((bsz, rows, GDN_DV), F32),
            jax.ShapeDtypeStruct((bsz, keep, CONV_CH), F32),
        ],
        scratch_shapes=[pltpu.VMEM((GDN_BATCH_INTERLEAVE, SUBLANES + r, CONV_CH), F32),
                        pltpu.VMEM((bsz, SUBLANES, CONV_CH), F32),
                        pltpu.VMEM((bsz, rows, GDN_DV), F32)],
        compiler_params=pltpu.CompilerParams(dimension_semantics=("arbitrary",),
                                             vmem_limit_bytes=VMEM_MIB["gdn_prompt"] * 2 ** 20),
        name="gdn_prompt",
    )(qkv3, small3, conv_state, s0, conv_w, alog_v, dtb_v)


def _gdn_sample(qkv3, small3, conv_state, s0, conv_w, alog_v, dtb_v):
    bsz, c, _ = qkv3.shape
    rows = GDN_HEADS * GDN_DK
    keep = CONV_WIDTH - 1
    ns = GROUP_ROWS // c
    grp = lambda g: (g, 0, 0)
    const2 = lambda g: (0, 0)
    return pl.pallas_call(
        _gdn_sample_kernel,
        grid=(bsz // ns,),
        in_specs=[
            pl.BlockSpec((ns, c, CONV_CH), grp),
            pl.BlockSpec((ns, c, LANES), grp),
            pl.BlockSpec((ns, keep, CONV_CH), grp),
            pl.BlockSpec((ns, rows, GDN_DV), grp),
            pl.BlockSpec((CONV_WIDTH, CONV_CH), const2),
            pl.BlockSpec((2 * GDN_HEADS, 1), const2),
            pl.BlockSpec((2 * GDN_HEADS, 1), const2),
        ],
        out_specs=[
            pl.BlockSpec((ns * c, GDN_VAL), lambda g: (g, 0)),
            pl.BlockSpec((ns, rows, GDN_DV), grp),
            pl.BlockSpec((ns, keep, CONV_CH), grp),
        ],
        out_shape=[
            jax.ShapeDtypeStruct((bsz * c, GDN_VAL), BF16),
            jax.ShapeDtypeStruct((bsz, rows, GDN_DV), F32),
            jax.ShapeDtypeStruct((bsz, keep, CONV_CH), F32),
        ],
        scratch_shapes=[pltpu.VMEM((ns, 2 * SUBLANES, CONV_CH), F32)],
        compiler_params=pltpu.CompilerParams(dimension_semantics=("arbitrary",),
                                             vmem_limit_bytes=VMEM_MIB["gdn_sample"] * 2 ** 20),
        name="gdn_sample",
    )(qkv3, small3, conv_state, s0, conv_w, alog_v, dtb_v)


def _head_norm_gate(o, silu_z, w):
    parts = []
    for h in range(o.shape[-1] // LANES):
        oh = o[:, h * LANES:(h + 1) * LANES]
        parts.append(oh * lax.rsqrt(jnp.mean(oh * oh, axis=-1, keepdims=True) + NORM_EPS) * w)
    return jnp.concatenate(parts, axis=-1) * silu_z


def _out_kernel(oa_ref, ob_ref, ga_ref, gb_ref, za_ref, zb_ref, x_ref, p_ref, anw_ref, bnw_ref,
                wua_ref, wub_ref, wout_ref, wpg_ref, wp_ref, fnw_ref, y_ref):
    f32 = lambda ref: ref[...].astype(F32)
    ya = _dot(_head_norm_gate(f32(oa_ref), f32(za_ref), anw_ref[...]), wua_ref[...])
    yb = _dot(_head_norm_gate(f32(ob_ref), f32(zb_ref), bnw_ref[...]), wub_ref[...])
    merged = f32(ga_ref) * ya + f32(gb_ref) * yb
    h1 = x_ref[...] + _dot(merged, wout_ref[...])
    h2 = h1 + _sigmoid(_dot(h1, wpg_ref[...])) * _dot(p_ref[...], wp_ref[...])
    y_ref[...] = h2 * lax.rsqrt(jnp.mean(h2 * h2, axis=-1, keepdims=True) + NORM_EPS) * fnw_ref[...]


def _out_stage(o_a, o_b, gz, x2d, p2d, anw, bnw, wua, wub, wout, wpg, wp, fnw):
    n = x2d.shape[0]
    tm = min(ROW_TILE, n)
    const = lambda i: (0, 0)
    return pl.pallas_call(
        _out_kernel,
        grid=(n // tm,),
        in_specs=[
            pl.BlockSpec((tm, GLA_VAL), lambda i: (i, 0)),
            pl.BlockSpec((tm, GDN_VAL), lambda i: (i, 0)),
            pl.BlockSpec((tm, D_MODEL), lambda i: (i, GZ_GATE_A // D_MODEL)),
            pl.BlockSpec((tm, D_MODEL), lambda i: (i, GZ_GATE_B // D_MODEL)),
            pl.BlockSpec((tm, GLA_VAL), lambda i: (i, GZ_Z_A // GLA_VAL)),
            pl.BlockSpec((tm, GDN_VAL), lambda i: (i, GZ_Z_B // GDN_VAL)),
            pl.BlockSpec((tm, D_MODEL), lambda i: (i, 0)),
            pl.BlockSpec((tm, PLE_DIM), lambda i: (i, 0)),
            pl.BlockSpec((1, GLA_DV), const),
            pl.BlockSpec((1, GDN_DV), const),
            pl.BlockSpec((GLA_VAL, D_MODEL), const),
            pl.BlockSpec((GDN_VAL, D_MODEL), const),
            pl.BlockSpec((D_MODEL, D_MODEL), const),
            pl.BlockSpec((D_MODEL, D_MODEL), const),
            pl.BlockSpec((PLE_DIM, D_MODEL), const),
            pl.BlockSpec((1, D_MODEL), const),
        ],
        out_specs=pl.BlockSpec((tm, D_MODEL), lambda i: (i, 0)),
        out_shape=jax.ShapeDtypeStruct((n, D_MODEL), F32),
        compiler_params=pltpu.CompilerParams(dimension_semantics=("arbitrary",),
                                             vmem_limit_bytes=VMEM_MIB["out_stage"] * 2 ** 20),
        name="out_stage",
    )(o_a, o_b, gz, gz, gz, gz, x2d, p2d, anw, bnw, wua, wub, wout, wpg, wp, fnw)


def _in_offsets():
    offs = [0]
    for s in IN_SPLITS:
        offs.append(offs[-1] + s)
    return offs


def _regroup_kernel(wt_ref, o_ref):
    offs = _in_offsets()
    (q_a, _, _, g_a, z_a, qkv_b, a_b, _, z_b, gate_a, gate_b, end) = offs
    piece = lambda lo, hi: wt_ref[lo:hi, :].astype(BF16)
    o_ref[P_GLA_QKV:P_GLA_QKV + GLA_QKV, :] = piece(q_a, g_a)
    o_ref[P_QKV_B:P_QKV_B + CONV_CH, :] = piece(qkv_b, a_b)
    small = jnp.concatenate([wt_ref[g_a:z_a, :], wt_ref[a_b:z_b, :],
                             jnp.zeros((LANES - (z_a - g_a) - (z_b - a_b), wt_ref.shape[1]), F32)], axis=0)
    o_ref[P_SMALL:P_SMALL + LANES, :] = small.astype(BF16)
    o_ref[P_GATES + GZ_GATE_A:P_GATES + GZ_GATE_A + D_MODEL, :] = piece(gate_a, gate_b)
    o_ref[P_GATES + GZ_GATE_B:P_GATES + GZ_GATE_B + D_MODEL, :] = piece(gate_b, end)
    o_ref[P_GATES + GZ_Z_A:P_GATES + GZ_Z_A + GLA_VAL, :] = piece(z_a, qkv_b)
    o_ref[P_GATES + GZ_Z_B:P_GATES + GZ_Z_B + GDN_VAL, :] = piece(z_b, gate_a)


def _regroup_w_in(w_in_t):
    cols = 256
    return pl.pallas_call(
        _regroup_kernel,
        grid=(D_MODEL // cols,),
        in_specs=[pl.BlockSpec((None, w_in_t.shape[1], cols), lambda i: (0, 0, i))],
        out_specs=pl.BlockSpec((P_COLS, cols), lambda i: (0, i)),
        out_shape=jax.ShapeDtypeStruct((P_COLS, D_MODEL), BF16),
        compiler_params=pltpu.CompilerParams(dimension_semantics=("arbitrary",),
                                             vmem_limit_bytes=VMEM_MIB["regroup_w_in"] * 2 ** 20),
        name="regroup_w_in",
    )(w_in_t)


def _head_param_col(v):
    return jnp.zeros((2 * GDN_HEADS, 1), F32).at[:GDN_HEADS, 0].set(v.astype(F32))


def _trunk(x, p, s_gla, s_gdn, conv_state, wts):
    bsz, t_len, _ = x.shape
    c = min(PROMPT_CHUNK, t_len)
    n = bsz * t_len
    x2d = x.reshape(n, D_MODEL)
    p2d = p.reshape(n, PLE_DIM)
    gla_in, gdn_in, small, gz = _inproj(x2d, wts["norm_w"], wts["w_in_r"])
    gla_in3 = gla_in.reshape(bsz, t_len, GLA_QKV)
    gdn_in3 = gdn_in.reshape(bsz, t_len, CONV_CH)
    small3 = small.reshape(bsz, t_len, LANES)
    s_gla2 = s_gla.reshape(bsz, GLA_HEADS * GLA_DK, GLA_DV)
    s_gdn2 = s_gdn.reshape(bsz, GDN_HEADS * GDN_DK, GDN_DV)
    if t_len % GROUP_ROWS == 0:
        o_a, gla_fin = _gla_prompt(gla_in3, small3, s_gla2, wts["wgg"], wts["bgg"], c)
        o_b, gdn_fin, conv_new = _gdn_prompt(gdn_in3, small3, conv_state, s_gdn2, wts["conv_w"],
                                             wts["alog_v"], wts["dtb_v"], c)
    else:
        assert GROUP_ROWS % t_len == 0 and bsz % (GROUP_ROWS // t_len) == 0 and t_len >= CONV_WIDTH - 1
        o_a, gla_fin = _gla_sample(gla_in3, small3, s_gla2, wts["wgg"], wts["bgg"])
        o_b, gdn_fin, conv_new = _gdn_sample(gdn_in3, small3, conv_state, s_gdn2, wts["conv_w"],
                                             wts["alog_v"], wts["dtb_v"])
    y = _out_stage(o_a.reshape(n, GLA_VAL), o_b.reshape(n, GDN_VAL), gz, x2d, p2d, wts["anw"], wts["bnw"],
                   wts["wua"], wts["wub"], wts["wout"], wts["wpg"], wts["wp"], wts["fnw"])
    return (y.reshape(bsz, t_len, D_MODEL),
            gla_fin.reshape(1, bsz, GLA_HEADS, GLA_DK, GLA_DV),
            gdn_fin.reshape(1, bsz, GDN_HEADS, GDN_DK, GDN_DV),
            conv_new.reshape(1, bsz, CONV_WIDTH - 1, CONV_CH))


def kernel(x_prompt, x_sample, state_gla, state_gdn, state_conv, p_prompt, p_sample, norm_w, w_in,
           w_gla_gate, b_gla_gate, gla_norm_w, conv_w, gdn_a_log, gdn_dt_bias, gdn_norm_w,
           w_up_gla, w_up_gdn, w_out, w_ple_gate, w_ple, final_norm_w):
    wgg = jnp.zeros((LANES, GLA_KEY), F32).at[SM_G:SM_G + GLA_GATE_RANK].set(w_gla_gate[0]).astype(BF16)
    wts = {
        "norm_w": norm_w[0].reshape(1, D_MODEL),
        "w_in_r": _regroup_w_in(jnp.swapaxes(w_in, 1, 2)),
        "wgg": wgg,
        "bgg": b_gla_gate[0].reshape(1, GLA_KEY),
        "conv_w": conv_w[0],
        "alog_v": _head_param_col(gdn_a_log[0]),
        "dtb_v": _head_param_col(gdn_dt_bias[0]),
        "anw": gla_norm_w[0].reshape(1, GLA_DV),
        "bnw": gdn_norm_w[0].reshape(1, GDN_DV),
        "wua": w_up_gla[0].astype(BF16),
        "wub": w_up_gdn[0].astype(BF16),
        "wout": w_out[0].astype(BF16),
        "wpg": w_ple_gate[0].astype(BF16),
        "wp": w_ple[0].astype(BF16),
        "fnw": final_norm_w.reshape(1, D_MODEL),
    }
    bsz = x_prompt.shape[0]
    dt = x_prompt.dtype
    y_p, gla_p, gdn_p, conv_p = _trunk(
        x_prompt, p_prompt[0],
        jnp.zeros((bsz, GLA_HEADS, GLA_DK, GLA_DV), dt), jnp.zeros((bsz, GDN_HEADS, GDN_DK, GDN_DV), dt),
        jnp.zeros((bsz, CONV_WIDTH - 1, CONV_CH), dt), wts)
    y_s, gla_s, gdn_s, conv_s = _trunk(x_sample, p_sample[0], state_gla[0], state_gdn[0], state_conv[0], wts)
    return (y_p, y_s, gla_p, gdn_p, conv_p, gla_s, gdn_s, conv_s)
```
